```python
import math
import jax, jax.numpy as jnp
from jax import lax
import numpy as np

D_MODEL = 2048
BATCH = 4
SEQ = 2048
DEPTH = 1

MIX_WIDTH = D_MODEL
ATTN_WIDTH = MIX_WIDTH // 2
SSM_WIDTH = MIX_WIDTH - ATTN_WIDTH
ATTN_HEAD_DIM = 64
ATTN_VALUE_DIM = 2 * ATTN_HEAD_DIM
ATTN_HEADS = ATTN_WIDTH // ATTN_VALUE_DIM
SSM_GROUP = 16
SSM_GROUPS = SSM_WIDTH // SSM_GROUP
SSM_STATE = 64
D_FF = ((8 * D_MODEL // 3 + 255) // 256) * 256
IN_COLS = 3 * ATTN_WIDTH + SSM_WIDTH
QBLOCK = 128
NORM_EPS = 1e-6
DT_MIN = 1e-3
DT_MAX = 1e-1

kernel_name = "hybrid_diffattn_s5_macaron_encoder"


def rms_norm(x, g):
    xf = x.astype(jnp.float32)
    y = xf * lax.rsqrt(jnp.mean(xf * xf, axis=-1, keepdims=True) + NORM_EPS)
    return (y * g.astype(jnp.float32)).astype(x.dtype)


def swiglu(x, w_gate, w_up, w_down):
    return (jax.nn.silu(x @ w_gate) * (x @ w_up)) @ w_down


def alibi_slopes():
    h = np.arange(ATTN_HEADS) + 1
    return jnp.asarray(2.0 ** (-8.0 * h / ATTN_HEADS), dtype=jnp.float32)


def diff_attention(q, k, v, lam, slopes):
    B, S, H, _, dh = q.shape
    nblk = S // QBLOCK
    qb = (q * (dh ** -0.5)).reshape(B, nblk, QBLOCK, H, 2, dh).transpose(1, 0, 2, 3, 4, 5)
    starts = jnp.arange(nblk, dtype=jnp.int32) * QBLOCK
    kpos = jnp.arange(S, dtype=jnp.int32)

    def block(args):
        qblk, start = args
        s = jnp.einsum('bqhcd,bkhcd->bhcqk', qblk, k).astype(jnp.float32)
        qpos = start + jnp.arange(QBLOCK, dtype=jnp.int32)
        dist = jnp.abs(qpos[:, None] - kpos[None, :]).astype(jnp.float32)
        s = s - slopes[None, :, None, None, None] * dist[None, None, None]
        p = jax.nn.softmax(s, axis=-1)
        w = p[:, :, 0] - lam * p[:, :, 1]
        return jnp.einsum('bhqk,bkhe->bqhe', w.astype(v.dtype), v)

    out = lax.map(block, (qb, starts))
    return out.transpose(1, 0, 2, 3, 4).reshape(B, S, H, v.shape[-1])


def _complex_affine_combine(e1, e2):
    a1r, a1i, x1r, x1i = e1
    a2r, a2i, x2r, x2i = e2
    return (a2r * a1r - a2i * a1i,
            a2r * a1i + a2i * a1r,
            a2r * x1r - a2i * x1i + x2r,
            a2r * x1i + a2i * x1r + x2i)


def s5_direction(u, lam_re, lam_im, log_dt, b_re, b_im, c_re, c_im, reverse):
    f32 = jnp.float32
    lam_re = lam_re.astype(f32); lam_im = lam_im.astype(f32)
    dt = jnp.exp(log_dt.astype(f32))[:, None]
    mag = jnp.exp(lam_re * dt)
    a_re = mag * jnp.cos(lam_im * dt)
    a_im = mag * jnp.sin(lam_im * dt)
    den = lam_re * lam_re + lam_im * lam_im
    nr = a_re - 1.0
    f_re = (nr * lam_re + a_im * lam_im) / den
    f_im = (a_im * lam_re - nr * lam_im) / den
    b_re = b_re.astype(f32); b_im = b_im.astype(f32)
    bb_re = f_re[..., None] * b_re - f_im[..., None] * b_im
    bb_im = f_re[..., None] * b_im + f_im[..., None] * b_re
    bu_re = jnp.einsum('bsgp,gnp->bsgn', u, bb_re)
    bu_im = jnp.einsum('bsgp,gnp->bsgn', u, bb_im)
    ar = jnp.broadcast_to(a_re, bu_re.shape)
    ai = jnp.broadcast_to(a_im, bu_re.shape)
    _, _, h_re, h_im = lax.associative_scan(
        _complex_affine_combine, (ar, ai, bu_re, bu_im), reverse=reverse, axis=1)
    return (jnp.einsum('gpn,bsgn->bsgp', c_re.astype(f32), h_re)
            - jnp.einsum('gpn,bsgn->bsgp', c_im.astype(f32), h_im))


def s5_mixer(u, lam_re, lam_im, log_dt, b_re, b_im, c_re, c_im, d, w_glu, b_glu, out_g):
    B, S, _ = u.shape
    uf = u.astype(jnp.float32).reshape(B, S, SSM_GROUPS, SSM_GROUP)
    y = (s5_direction(uf, lam_re[0], lam_im[0], log_dt[0], b_re[0], b_im[0], c_re[0], c_im[0], False)
         + s5_direction(uf, lam_re[1], lam_im[1], log_dt[1], b_re[1], b_im[1], c_re[1], c_im[1], True)
         + d.astype(jnp.float32) * uf)
    y = y.reshape(B, S, SSM_WIDTH).astype(u.dtype)
    g = jax.nn.gelu(y)
    g = g * jax.nn.sigmoid(g @ w_glu + b_glu)
    return rms_norm(g, out_g)


def setup_inputs(seed: int = 0) -> dict:
    key = jax.random.key(seed)
    ks = iter(jax.random.split(key, 40))
    f32 = jnp.float32

    def nrm(shape, scale):
        return jax.random.normal(next(ks), shape, f32) * scale

    def gain(shape):
        return 1.0 + nrm(shape, 0.02)

    L, G, N, P = DEPTH, SSM_GROUPS, SSM_STATE, SSM_GROUP
    n_idx = jnp.arange(N, dtype=f32)
    return {
        "x": nrm((BATCH, SEQ, D_MODEL), 1.0),
        "ff1_pre_g": gain((L, D_MODEL)),
        "ff1_w_gate": nrm((L, D_MODEL, D_FF), D_MODEL ** -0.5),
        "ff1_w_up": nrm((L, D_MODEL, D_FF), D_MODEL ** -0.5),
        "ff1_w_down": nrm((L, D_FF, D_MODEL), D_FF ** -0.5),
        "ff1_post_g": gain((L, D_MODEL)),
        "mix_pre_g": gain((L, D_MODEL)),
        "w_in": nrm((L, D_MODEL, IN_COLS), D_MODEL ** -0.5),
        "lam_q1": nrm((L, ATTN_HEAD_DIM), 0.1),
        "lam_k1": nrm((L, ATTN_HEAD_DIM), 0.1),
        "lam_q2": nrm((L, ATTN_HEAD_DIM), 0.1),
        "lam_k2": nrm((L, ATTN_HEAD_DIM), 0.1),
        "attn_head_g": gain((L, ATTN_VALUE_DIM)),
        "ssm_lam_re": -0.5 + nrm((L, 2, G, N), 0.01),
        "ssm_lam_im": jnp.pi * n_idx + nrm((L, 2, G, N), 0.01),
        "ssm_log_dt": jax.random.uniform(next(ks), (L, 2, G), f32, math.log(DT_MIN), math.log(DT_MAX)),
        "ssm_b_re": nrm((L, 2, G, N, P), P ** -0.5),
        "ssm_b_im": nrm((L, 2, G, N, P), P ** -0.5),
        "ssm_c_re": nrm((L, 2, G, P, N), N ** -0.5),
        "ssm_c_im": nrm((L, 2, G, P, N), N ** -0.5),
        "ssm_d": nrm((L, G, P), 0.5),
        "ssm_w_glu": nrm((L, SSM_WIDTH, SSM_WIDTH), SSM_WIDTH ** -0.5),
        "ssm_b_glu": nrm((L, SSM_WIDTH), 0.01),
        "ssm_out_g": gain((L, SSM_WIDTH)),
        "w_out": nrm((L, MIX_WIDTH, D_MODEL), MIX_WIDTH ** -0.5),
        "mix_post_g": gain((L, D_MODEL)),
        "ff2_pre_g": gain((L, D_MODEL)),
        "ff2_w_gate": nrm((L, D_MODEL, D_FF), D_MODEL ** -0.5),
        "ff2_w_up": nrm((L, D_MODEL, D_FF), D_MODEL ** -0.5),
        "ff2_w_down": nrm((L, D_FF, D_MODEL), D_FF ** -0.5),
        "ff2_post_g": gain((L, D_MODEL)),
    }


def reference(x, ff1_pre_g, ff1_w_gate, ff1_w_up, ff1_w_down, ff1_post_g,
              mix_pre_g, w_in, lam_q1, lam_k1, lam_q2, lam_k2, attn_head_g,
              ssm_lam_re, ssm_lam_im, ssm_log_dt, ssm_b_re, ssm_b_im, ssm_c_re, ssm_c_im,
              ssm_d, ssm_w_glu, ssm_b_glu, ssm_out_g, w_out, mix_post_g,
              ff2_pre_g, ff2_w_gate, ff2_w_up, ff2_w_down, ff2_post_g):
    B, S, _ = x.shape
    slopes = alibi_slopes()
    f32 = jnp.float32
    for l in range(DEPTH):
        h = rms_norm(x, ff1_pre_g[l])
        x = x + 0.5 * rms_norm(swiglu(h, ff1_w_gate[l], ff1_w_up[l], ff1_w_down[l]), ff1_post_g[l])

        h = rms_norm(x, mix_pre_g[l])
        proj = h @ w_in[l]
        q, k, v, u = jnp.split(proj, [ATTN_WIDTH, 2 * ATTN_WIDTH, 3 * ATTN_WIDTH], axis=-1)
        q = q.reshape(B, S, ATTN_HEADS, 2, ATTN_HEAD_DIM)
        k = k.reshape(B, S, ATTN_HEADS, 2, ATTN_HEAD_DIM)
        v = v.reshape(B, S, ATTN_HEADS, ATTN_VALUE_DIM)
        lam_init = 0.8 - 0.6 * math.exp(-0.3 * l)
        lam = (jnp.exp(jnp.sum(lam_q1[l].astype(f32) * lam_k1[l].astype(f32)))
               - jnp.exp(jnp.sum(lam_q2[l].astype(f32) * lam_k2[l].astype(f32))) + lam_init)
        a = diff_attention(q, k, v, lam, slopes)
        a = (rms_norm(a, attn_head_g[l]) * (1.0 - lam_init)).reshape(B, S, ATTN_WIDTH)
        s = s5_mixer(u, ssm_lam_re[l], ssm_lam_im[l], ssm_log_dt[l], ssm_b_re[l], ssm_b_im[l],
                     ssm_c_re[l], ssm_c_im[l], ssm_d[l], ssm_w_glu[l], ssm_b_glu[l], ssm_out_g[l])
        mixed = jnp.concatenate([a, s], axis=-1) @ w_out[l]
        x = x + rms_norm(mixed, mix_post_g[l])

        h = rms_norm(x, ff2_pre_g[l])
        x = x + 0.5 * rms_norm(swiglu(h, ff2_w_gate[l], ff2_w_up[l], ff2_w_down[l]), ff2_post_g[l])
    return x
```

```python
import functools
import math

import jax
import jax.numpy as jnp
from jax import lax
from jax.experimental import pallas as pl
from jax.experimental.pallas import tpu as pltpu

F32 = jnp.float32
BF16 = jnp.bfloat16

NORM_EPS = 1e-6
ATTN_HEADS = 8
ATTN_HEAD_DIM = 64
ATTN_VALUE_DIM = 2 * ATTN_HEAD_DIM
SSM_GROUP = 16
SSM_STATE = 64
CHUNK = 16
GROUP_LANES = CHUNK * SSM_GROUP
SUBLANES = 8

VMEM_LIMIT_BYTES = 56 * 1024 * 1024


def _rms_scale(x):
    return lax.rsqrt(jnp.mean(x * x, axis=-1, keepdims=True) + NORM_EPS)


def _sigmoid(x):
    return 1.0 / (1.0 + jnp.exp(-x))


def _ffn_kernel(x_ref, pre_g_ref, wg_ref, wu_ref, wd_ref, post_g_ref, o_ref, h_ref):
    j = pl.program_id(1)
    last = pl.num_programs(1) - 1

    @pl.when(j == 0)
    def _():
        x = x_ref[...]
        h_ref[...] = (x * _rms_scale(x) * pre_g_ref[...]).astype(BF16)

    h = h_ref[...]
    gate = jnp.dot(h, wg_ref[...], preferred_element_type=F32)
    up = jnp.dot(h, wu_ref[...], preferred_element_type=F32)
    act = (gate * _sigmoid(gate) * up).astype(BF16)
    part = jnp.dot(act, wd_ref[...], preferred_element_type=F32)

    @pl.when(j == 0)
    def _():
        o_ref[...] = part

    @pl.when(j > 0)
    def _():
        o_ref[...] += part

    @pl.when(j == last)
    def _():
        acc = o_ref[...]
        o_ref[...] = x_ref[...] + 0.5 * (acc * _rms_scale(acc) * post_g_ref[...])


def _ffn(x, pre_g, w_gate, w_up, w_down, post_g, *, tm=512, tf=512):
    t, d = x.shape
    f = w_gate.shape[1]
    assert t % tm == 0 and f % tf == 0
    return pl.pallas_call(
        _ffn_kernel,
        grid=(t // tm, f // tf),
        in_specs=[
            pl.BlockSpec((tm, d), lambda i, j: (i, 0)),
            pl.BlockSpec((1, d), lambda i, j: (0, 0)),
            pl.BlockSpec((d, tf), lambda i, j: (0, j)),
            pl.BlockSpec((d, tf), lambda i, j: (0, j)),
            pl.BlockSpec((tf, d), lambda i, j: (j, 0)),
            pl.BlockSpec((1, d), lambda i, j: (0, 0)),
        ],
        out_specs=pl.BlockSpec((tm, d), lambda i, j: (i, 0)),
        out_shape=jax.ShapeDtypeStruct((t, d), F32),
        scratch_shapes=[pltpu.VMEM((tm, d), BF16)],
        compiler_params=pltpu.CompilerParams(
            dimension_semantics=("parallel", "arbitrary"),
            vmem_limit_bytes=VMEM_LIMIT_BYTES),
        name="ffn",
    )(x, pre_g.reshape(1, d), w_gate, w_up, w_down, post_g.reshape(1, d))


def _in_proj_kernel(x_ref, g_ref, w_ref, qkv_ref, u_ref, h_ref, *, n_qkv):
    j = pl.program_id(1)

    @pl.when(j == 0)
    def _():
        x = x_ref[...]
        h_ref[...] = (x * _rms_scale(x) * g_ref[...]).astype(BF16)

    r = jnp.dot(h_ref[...], w_ref[...], preferred_element_type=F32)

    @pl.when(j < n_qkv)
    def _():
        qkv_ref[...] = r.astype(BF16)

    @pl.when(j == n_qkv)
    def _():
        u_ref[...] = r


def _in_proj(x, g, w_in, *, ssm_width, tm=1024):
    t, d = x.shape
    n = w_in.shape[1]
    tn = ssm_width
    n_qkv = (n - ssm_width) // tn
    assert t % tm == 0 and n == (n_qkv + 1) * tn
    return pl.pallas_call(
        functools.partial(_in_proj_kernel, n_qkv=n_qkv),
        grid=(t // tm, n_qkv + 1),
        in_specs=[
            pl.BlockSpec((tm, d), lambda i, j: (i, 0)),
            pl.BlockSpec((1, d), lambda i, j: (0, 0)),
            pl.BlockSpec((d, tn), lambda i, j: (0, j)),
        ],
        out_specs=[
            pl.BlockSpec((tm, tn), lambda i, j: (i, jnp.minimum(j, n_qkv - 1))),
            pl.BlockSpec((tm, tn), lambda i, j: (i, 0)),
        ],
        out_shape=[
            jax.ShapeDtypeStruct((t, n - ssm_width), BF16),
            jax.ShapeDtypeStruct((t, ssm_width), F32),
        ],
        scratch_shapes=[pltpu.VMEM((tm, d), BF16)],
        compiler_params=pltpu.CompilerParams(
            dimension_semantics=("parallel", "arbitrary"),
            vmem_limit_bytes=VMEM_LIMIT_BYTES),
        name="in_proj",
    )(x, g.reshape(1, d), w_in)


def _attn_kernel(slopes_ref, lq1_ref, lk1_ref, lq2_ref, lk2_ref, hg_ref,
                 q_ref, k_ref, v_ref, o_ref, bias_ref, *, tq, lam_init):
    h = pl.program_id(0)
    qi = pl.program_id(1)
    b = pl.program_id(2)
    s_len = k_ref.shape[0]

    @pl.when(b == 0)
    def _():
        qpos = qi * tq + lax.broadcasted_iota(jnp.int32, (tq, s_len), 0)
        kpos = lax.broadcasted_iota(jnp.int32, (tq, s_len), 1)
        bias_ref[...] = slopes_ref[h] * jnp.abs(qpos - kpos).astype(F32)

    lam = (jnp.exp(jnp.sum(lq1_ref[...] * lk1_ref[...], axis=-1, keepdims=True))
           - jnp.exp(jnp.sum(lq2_ref[...] * lk2_ref[...], axis=-1, keepdims=True))
           + lam_init)

    q = q_ref[...] * jnp.asarray(ATTN_HEAD_DIM ** -0.5, BF16)
    k = k_ref[...]
    lane = lax.broadcasted_iota(jnp.int32, q.shape, 1)
    bias = bias_ref[...]
    w = None
    for c in range(2):
        in_map = (lane >= c * ATTN_HEAD_DIM) & (lane < (c + 1) * ATTN_HEAD_DIM)
        qc = jnp.where(in_map, q, jnp.zeros_like(q))
        s = lax.dot_general(qc, k, (((1,), (1,)), ((), ())), preferred_element_type=F32) - bias
        p = jnp.exp(s - jnp.max(s, axis=-1, keepdims=True))
        inv = 1.0 / jnp.sum(p, axis=-1, keepdims=True)
        if c == 0:
            w = p * inv
        else:
            w = w - p * (lam * inv)
    o = jnp.dot(w.astype(BF16), v_ref[...], preferred_element_type=F32)
    o_ref[...] = (o * _rms_scale(o) * hg_ref[...] * (1.0 - lam_init)).astype(o_ref.dtype)


def _attention(qkv, slopes, lq1, lk1, lq2, lk2, head_g, *, batch, seq, lam_init, tq=512):
    t = qkv.shape[0]
    e = ATTN_VALUE_DIM
    nq = seq // tq
    nh = ATTN_HEADS
    vec = lambda a: a.reshape(1, -1).astype(F32)
    small = lambda n: pl.BlockSpec((1, n), lambda h, qi, b: (0, 0))
    return pl.pallas_call(
        functools.partial(_attn_kernel, tq=tq, lam_init=lam_init),
        grid=(nh, nq, batch),
        in_specs=[
            pl.BlockSpec(memory_space=pltpu.SMEM),
            small(ATTN_HEAD_DIM), small(ATTN_HEAD_DIM), small(ATTN_HEAD_DIM), small(ATTN_HEAD_DIM),
            small(e),
            pl.BlockSpec((tq, e), lambda h, qi, b: (b * nq + qi, h)),
            pl.BlockSpec((seq, e), lambda h, qi, b: (b, nh + h)),
            pl.BlockSpec((seq, e), lambda h, qi, b: (b, 2 * nh + h)),
        ],
        out_specs=pl.BlockSpec((tq, e), lambda h, qi, b: (b * nq + qi, h)),
        out_shape=jax.ShapeDtypeStruct((t, nh * e), BF16),
        scratch_shapes=[pltpu.VMEM((tq, seq), F32)],
        compiler_params=pltpu.CompilerParams(
            dimension_semantics=("parallel", "parallel", "arbitrary"),
            vmem_limit_bytes=VMEM_LIMIT_BYTES),
        name="diff_attention",
    )(slopes, vec(lq1), vec(lk1), vec(lq2), vec(lk2), vec(head_g), qkv, qkv, qkv)


def _discretize(lam_re, lam_im, log_dt):
    dt = jnp.exp(log_dt)
    mag = jnp.exp(lam_re * dt)
    a_re = mag * jnp.cos(lam_im * dt)
    a_im = mag * jnp.sin(lam_im * dt)
    den = lam_re * lam_re + lam_im * lam_im
    nr = a_re - 1.0
    f_re = (nr * lam_re + a_im * lam_im) / den
    f_im = (a_im * lam_re - nr * lam_im) / den
    return dt, a_re, a_im, f_re, f_im


def _power(lam_re, lam_im, dt, e):
    mag = jnp.exp(lam_re * dt * e)
    ang = lam_im * dt * e
    return mag * jnp.cos(ang), mag * jnp.sin(ang)


def _shift_lanes(x, s, left):
    if s == 0:
        return x
    n = x.shape[-1]
    lane = lax.broadcasted_iota(jnp.int32, x.shape, 1)
    if left:
        return jnp.where(lane < n - s, pltpu.roll(x, n - s, axis=1), 0.0)
    return jnp.where(lane >= s, pltpu.roll(x, s, axis=1), 0.0)


def _s5_prep_kernel(lre_c_ref, lim_c_ref, ldt_c_ref, bre_c_ref, bim_c_ref, ctre_ref, ctim_ref,
                    lre_r_ref, lim_r_ref, ldt_r_ref, btre_r_ref, btim_r_ref,
                    w1_ref, cm_ref, al_ref):
    n, p, l = SSM_STATE, SSM_GROUP, CHUNK
    lag = (lax.broadcasted_iota(jnp.int32, (1, GROUP_LANES), 1) // p).astype(F32)

    t_rows = None
    cm_re, cm_im = [], []
    for d in range(2):
        lr, li, ldt = lre_c_ref[d, 0], lim_c_ref[d, 0], ldt_c_ref[d, 0]
        dt, a_re, a_im, f_re, f_im = _discretize(lr, li, ldt)
        b_re, b_im = bre_c_ref[d, 0], bim_c_ref[d, 0]
        bb_re = f_re * b_re - f_im * b_im
        bb_im = f_re * b_im + f_im * b_re
        e0 = lag if d == 0 else (l - 1.0) - lag
        pw_re, pw_im = _power(lr, li, dt, e0)
        ct_re, ct_im = ctre_ref[d, 0], ctim_ref[d, 0]
        ac_re = pw_re * ct_re - pw_im * ct_im
        ac_im = pw_re * ct_im + pw_im * ct_re
        contract0 = (((0,), (0,)), ((), ()))
        taps = (lax.dot_general(bb_re, ac_re, contract0, precision=lax.Precision.HIGHEST,
                                preferred_element_type=F32)
                - lax.dot_general(bb_im, ac_im, contract0, precision=lax.Precision.HIGHEST,
                                  preferred_element_type=F32))
        rows = [_shift_lanes(taps, p * (tp if d == 0 else l - 1 - tp), left=(d == 1))
                for tp in range(l)]
        rows = jnp.concatenate(rows, axis=0)
        t_rows = rows if t_rows is None else t_rows + rows
        cm_re.append(a_re * ac_re - a_im * ac_im)
        cm_im.append(-(a_re * ac_im + a_im * ac_re))
    w1_ref[0, :, 0:GROUP_LANES] = t_rows.astype(BF16)
    cm_ref[0] = jnp.concatenate(cm_re + cm_im, axis=0).astype(BF16)

    lr, li, ldt = lre_r_ref[0], lim_r_ref[0], ldt_r_ref[0]
    dt, a_re, a_im, f_re, f_im = _discretize(lr, li, ldt)
    bt_re, bt_im = btre_r_ref[0], btim_r_ref[0]
    bb_re = f_re * bt_re - f_im * bt_im
    bb_im = f_re * bt_im + f_im * bt_re
    tau = lax.broadcasted_iota(jnp.int32, (l, 2 * n), 0)
    lane = lax.broadcasted_iota(jnp.int32, (l, 2 * n), 1)
    e = jnp.where(lane < n, l - 1 - tau, tau).astype(F32)
    pw_re, pw_im = _power(lr, li, dt, e)
    bm_re = pw_re[:, None, :] * bb_re[None, :, :] - pw_im[:, None, :] * bb_im[None, :, :]
    bm_im = pw_re[:, None, :] * bb_im[None, :, :] + pw_im[:, None, :] * bb_re[None, :, :]
    w1_ref[0, :, GROUP_LANES:GROUP_LANES + 2 * n] = bm_re.reshape(l * p, 2 * n).astype(BF16)
    w1_ref[0, :, GROUP_LANES + 2 * n:] = bm_im.reshape(l * p, 2 * n).astype(BF16)
    al_re, al_im = _power(lr, li, dt, jnp.full((1, 2 * n), float(l), F32))
    al_ref[0] = jnp.concatenate([al_re, al_im], axis=0)


def _s5_prep(lam_re, lam_im, log_dt, b_re, b_im, c_re, c_im):
    _, g, n = lam_re.shape
    p = b_re.shape[-1]
    col = lambda a: a.reshape(2, g, n, 1)
    ldt_c = jnp.broadcast_to(log_dt[:, :, None, None], (2, g, n, 1))
    ct = lambda c: jnp.tile(jnp.swapaxes(c, 2, 3), (1, 1, 1, CHUNK))
    row = lambda a: jnp.concatenate([a[0], a[1]], axis=-1)[:, None, :]
    ldt_r = row(jnp.broadcast_to(log_dt[:, :, None], (2, g, n)))
    bt = lambda b: jnp.concatenate([jnp.swapaxes(b[0], 1, 2), jnp.swapaxes(b[1], 1, 2)], axis=-1)

    spec_c = lambda last: pl.BlockSpec((2, 1, n, last), lambda i: (0, i, 0, 0))
    spec_r = lambda rows: pl.BlockSpec((1, rows, 2 * n), lambda i: (i, 0, 0))
    return pl.pallas_call(
        _s5_prep_kernel,
        grid=(g,),
        in_specs=[spec_c(1), spec_c(1), spec_c(1), spec_c(p), spec_c(p),
                  spec_c(GROUP_LANES), spec_c(GROUP_LANES),
                  spec_r(1), spec_r(1), spec_r(1), spec_r(p), spec_r(p)],
        out_specs=[
            pl.BlockSpec((1, GROUP_LANES, GROUP_LANES + 4 * n), lambda i: (i, 0, 0)),
            pl.BlockSpec((1, 4 * n, GROUP_LANES), lambda i: (i, 0, 0)),
            pl.BlockSpec((1, 2, 2 * n), lambda i: (i, 0, 0)),
        ],
        out_shape=[
            jax.ShapeDtypeStruct((g, GROUP_LANES, GROUP_LANES + 4 * n), BF16),
            jax.ShapeDtypeStruct((g, 4 * n, GROUP_LANES), BF16),
            jax.ShapeDtypeStruct((g, 2, 2 * n), F32),
        ],
        compiler_params=pltpu.CompilerParams(dimension_semantics=("parallel",)),
        name="s5_prep",
    )(col(lam_re), col(lam_im), ldt_c, b_re, b_im, ct(c_re), ct(c_im),
      row(lam_re), row(lam_im), ldt_r, bt(b_re), bt(b_im))


def _s5_chunk_kernel(u_ref, w1_ref, cm_ref, al_ref, d_ref, y_ref, s_ref, hf_ref, h_ref,
                     *, nb, nc, gb):
    gl = GROUP_LANES
    n2 = 2 * SSM_STATE
    for g in range(gb):
        sl = slice(g * gl, (g + 1) * gl)
        u = u_ref[:, sl]
        r = jnp.dot(u.astype(BF16), w1_ref[g], preferred_element_type=F32)
        y_ref[:, sl] = r[:, :gl] + d_ref[:, sl] * u
        s_ref[:, sl] = r[:, gl:]

    a_re = [al_ref[g, 0:1, :] for g in range(gb)]
    a_im = [al_ref[g, 1:2, :] for g in range(gb)]
    is_fwd = lax.broadcasted_iota(jnp.int32, (SUBLANES, n2), 1) < SSM_STATE

    def step(h, s):
        out = []
        for g in range(gb):
            hr, hi = h[:, g * gl:g * gl + n2], h[:, g * gl + n2:(g + 1) * gl]
            sr, si = s[:, g * gl:g * gl + n2], s[:, g * gl + n2:(g + 1) * gl]
            out.append(a_re[g] * hr - a_im[g] * hi + sr)
            out.append(a_re[g] * hi + a_im[g] * hr + si)
        return jnp.concatenate(out, axis=1)

    cpt = SUBLANES // nb
    n_tiles = nc // cpt
    tile_rows = lambda t: pl.ds(pl.multiple_of(t * SUBLANES, SUBLANES), SUBLANES)

    def fwd_body(t, h):
        s = s_ref[tile_rows(t), :]
        before = []
        for k in range(cpt):
            before.append(h)
            h = step(h, s[k * nb:(k + 1) * nb])
        hf_ref[tile_rows(t), :] = jnp.concatenate(before, axis=0)
        return h

    lax.fori_loop(0, n_tiles, fwd_body, jnp.zeros((nb, gb * gl), F32))

    def bwd_body(i, h):
        t = n_tiles - 1 - i
        s = s_ref[tile_rows(t), :]
        before = [None] * cpt
        for k in reversed(range(cpt)):
            before[k] = h
            h = step(h, s[k * nb:(k + 1) * nb])
        hb = jnp.concatenate(before, axis=0)
        hf = hf_ref[tile_rows(t), :]
        merged = [jnp.where(is_fwd, hf[:, k * n2:(k + 1) * n2], hb[:, k * n2:(k + 1) * n2])
                  for k in range(2 * gb)]
        h_ref[tile_rows(t), :] = jnp.concatenate(merged, axis=1)
        return h

    lax.fori_loop(0, n_tiles, bwd_body, jnp.zeros((nb, gb * gl), F32))

    for g in range(gb):
        sl = slice(g * gl, (g + 1) * gl)
        y_ref[:, sl] += jnp.dot(h_ref[:, sl].astype(BF16), cm_ref[g], preferred_element_type=F32)


def _s5_chunks(u_t, w1, cm, al, d_t, *, nb, nc, gb=4):
    rows, lanes = u_t.shape
    g = lanes // GROUP_LANES
    assert rows == nb * nc and g % gb == 0
    blk = pl.BlockSpec((rows, gb * GROUP_LANES), lambda i: (0, i))
    return pl.pallas_call(
        functools.partial(_s5_chunk_kernel, nb=nb, nc=nc, gb=gb),
        grid=(g // gb,),
        in_specs=[
            blk,
            pl.BlockSpec((gb,) + w1.shape[1:], lambda i: (i, 0, 0)),
            pl.BlockSpec((gb,) + cm.shape[1:], lambda i: (i, 0, 0)),
            pl.BlockSpec((gb,) + al.shape[1:], lambda i: (i, 0, 0)),
            pl.BlockSpec((1, gb * GROUP_LANES), lambda i: (0, i)),
        ],
        out_specs=blk,
        out_shape=jax.ShapeDtypeStruct((rows, lanes), F32),
        scratch_shapes=[pltpu.VMEM((rows, gb * GROUP_LANES), F32)] * 3,
        compiler_params=pltpu.CompilerParams(
            dimension_semantics=("parallel",), vmem_limit_bytes=VMEM_LIMIT_BYTES),
        name="s5_chunks",
    )(u_t, w1, cm, al, d_t)


def _mix_out_kernel(x_ref, a_ref, y_ref, wglu_ref, bglu_ref, og_ref, wo_a_ref, wo_s_ref, pg_ref,
                    o_ref):
    y = y_ref[...]
    g = y * (0.5 * (1.0 + jnp.tanh(math.sqrt(2.0 / math.pi) * (y + 0.044715 * (y * y * y)))))
    z = jnp.dot(g.astype(BF16), wglu_ref[...], preferred_element_type=F32) + bglu_ref[...]
    s = g * _sigmoid(z)
    s = s * _rms_scale(s) * og_ref[...]
    mixed = (jnp.dot(a_ref[...], wo_a_ref[...], preferred_element_type=F32)
             + jnp.dot(s.astype(BF16), wo_s_ref[...], preferred_element_type=F32))
    o_ref[...] = x_ref[...] + mixed * _rms_scale(mixed) * pg_ref[...]


def _mix_out(x, a, y, w_glu, b_glu, out_g, w_out, post_g, *, tm=512):
    t, d = x.shape
    wa = a.shape[1]
    ws = y.shape[1]
    const = lambda shape: pl.BlockSpec(shape, lambda i: (0, 0))
    return pl.pallas_call(
        _mix_out_kernel,
        grid=(t // tm,),
        in_specs=[
            pl.BlockSpec((tm, d), lambda i: (i, 0)),
            pl.BlockSpec((tm, wa), lambda i: (i, 0)),
            pl.BlockSpec((tm, ws), lambda i: (i, 0)),
            const((ws, ws)), const((1, ws)), const((1, ws)),
            pl.BlockSpec((wa, d), lambda i: (0, 0)),
            pl.BlockSpec((ws, d), lambda i: (wa // ws, 0)),
            const((1, d)),
        ],
        out_specs=pl.BlockSpec((tm, d), lambda i: (i, 0)),
        out_shape=jax.ShapeDtypeStruct((t, d), F32),
        compiler_params=pltpu.CompilerParams(
            dimension_semantics=("parallel",), vmem_limit_bytes=VMEM_LIMIT_BYTES),
        name="mix_out",
    )(x, a, y, w_glu, b_glu.reshape(1, ws), out_g.reshape(1, ws), w_out, w_out,
      post_g.reshape(1, d))


def kernel(x, ff1_pre_g, ff1_w_gate, ff1_w_up, ff1_w_down, ff1_post_g, mix_pre_g, w_in, lam_q1, lam_k1, lam_q2, lam_k2, attn_head_g, ssm_lam_re, ssm_lam_im, ssm_log_dt, ssm_b_re, ssm_b_im, ssm_c_re, ssm_c_im, ssm_d, ssm_w_glu, ssm_b_glu, ssm_out_g, w_out, mix_post_g, ff2_pre_g, ff2_w_gate, ff2_w_up, ff2_w_down, ff2_post_g):
    batch, seq, d_model = x.shape
    depth = w_in.shape[0]
    ssm_width = ssm_w_glu.shape[-1]
    n_groups = ssm_width // SSM_GROUP
    n_chunks = seq // CHUNK
    slopes = jnp.asarray([2.0 ** (-8.0 * (i + 1) / ATTN_HEADS) for i in range(ATTN_HEADS)], F32)
    bf = lambda w: w.astype(BF16)

    xt = x.reshape(batch * seq, d_model)
    for l in range(depth):
        xt = _ffn(xt, ff1_pre_g[l], bf(ff1_w_gate[l]), bf(ff1_w_up[l]), bf(ff1_w_down[l]),
                  ff1_post_g[l])

        qkv, u = _in_proj(xt, mix_pre_g[l], bf(w_in[l]), ssm_width=ssm_width)
        lam_init = 0.8 - 0.6 * math.exp(-0.3 * l)
        a = _attention(qkv, slopes, lam_q1[l], lam_k1[l], lam_q2[l], lam_k2[l], attn_head_g[l],
                       batch=batch, seq=seq, lam_init=lam_init)

        w1, cm, al = _s5_prep(ssm_lam_re[l], ssm_lam_im[l], ssm_log_dt[l], ssm_b_re[l],
                              ssm_b_im[l], ssm_c_re[l], ssm_c_im[l])
        u_t = (u.reshape(batch, n_chunks, CHUNK, n_groups, SSM_GROUP)
               .transpose(1, 0, 3, 2, 4).reshape(n_chunks * batch, n_groups * GROUP_LANES))
        d_t = jnp.tile(ssm_d[l][:, None, :], (1, CHUNK, 1)).reshape(1, n_groups * GROUP_LANES)
        y_t = _s5_chunks(u_t, w1, cm, al, d_t, nb=batch, nc=n_chunks)
        y = (y_t.reshape(n_chunks, batch, n_groups, CHUNK, SSM_GROUP)
             .transpose(1, 0, 3, 2, 4).reshape(batch * seq, ssm_width))

        xt = _mix_out(xt, a, y, bf(ssm_w_glu[l]), ssm_b_glu[l], ssm_out_g[l], bf(w_out[l]),
                      mix_post_g[l])

        xt = _ffn(xt, ff2_pre_g[l], bf(ff2_w_gate[l]), bf(ff2_w_up[l]), bf(ff2_w_down[l]),
                  ff2_post_g[l])
    return xt.reshape(batch, seq, d_model)
```

```python
import functools
import math

import jax
import jax.numpy as jnp
from jax import lax
from jax.experimental import pallas as pl
from jax.experimental.pallas import tpu as pltpu

F32 = jnp.float32
BF16 = jnp.bfloat16

NORM_EPS = 1e-6
ATTN_HEADS = 8
ATTN_HEAD_DIM = 64
ATTN_VALUE_DIM = 2 * ATTN_HEAD_DIM
SSM_GROUP = 16
SSM_STATE = 64
CHUNK = 16
GROUP_LANES = CHUNK * SSM_GROUP
SUBLANES = 8
LANES = 128
GROUPS_PER_BLOCK = LANES // SSM_GROUP

VMEM_LIMIT_BYTES = 56 * 1024 * 1024


def _rms_scale(x):
    return lax.rsqrt(jnp.mean(x * x, axis=-1, keepdims=True) + NORM_EPS)


def _sigmoid(x):
    return 1.0 / (1.0 + jnp.exp(-x))


def _ffn_kernel(x_ref, pre_g_ref, wg_ref, wu_ref, wd_ref, post_g_ref, o_ref, h_ref):
    j = pl.program_id(1)
    last = pl.num_programs(1) - 1

    @pl.when(j == 0)
    def _():
        x = x_ref[...]
        h_ref[...] = (x * _rms_scale(x) * pre_g_ref[...]).astype(BF16)

    h = h_ref[...]
    gate = jnp.dot(h, wg_ref[...], preferred_element_type=F32)
    up = jnp.dot(h, wu_ref[...], preferred_element_type=F32)
    act = (gate * _sigmoid(gate) * up).astype(BF16)
    part = jnp.dot(act, wd_ref[...], preferred_element_type=F32)

    @pl.when(j == 0)
    def _():
        o_ref[...] = part

    @pl.when(j > 0)
    def _():
        o_ref[...] += part

    @pl.when(j == last)
    def _():
        acc = o_ref[...]
        o_ref[...] = x_ref[...] + 0.5 * (acc * _rms_scale(acc) * post_g_ref[...])


def _ffn(x, pre_g, w_gate, w_up, w_down, post_g, *, tm=512, tf=512):
    t, d = x.shape
    f = w_gate.shape[1]
    assert t % tm == 0 and f % tf == 0
    return pl.pallas_call(
        _ffn_kernel,
        grid=(t // tm, f // tf),
        in_specs=[
            pl.BlockSpec((tm, d), lambda i, j: (i, 0)),
            pl.BlockSpec((1, d), lambda i, j: (0, 0)),
            pl.BlockSpec((d, tf), lambda i, j: (0, j)),
            pl.BlockSpec((d, tf), lambda i, j: (0, j)),
            pl.BlockSpec((tf, d), lambda i, j: (j, 0)),
            pl.BlockSpec((1, d), lambda i, j: (0, 0)),
        ],
        out_specs=pl.BlockSpec((tm, d), lambda i, j: (i, 0)),
        out_shape=jax.ShapeDtypeStruct((t, d), F32),
        scratch_shapes=[pltpu.VMEM((tm, d), BF16)],
        compiler_params=pltpu.CompilerParams(
            dimension_semantics=("parallel", "arbitrary"),
            vmem_limit_bytes=VMEM_LIMIT_BYTES),
        name="ffn",
    )(x, pre_g.reshape(1, d), w_gate, w_up, w_down, post_g.reshape(1, d))


def _in_proj_kernel(x_ref, g_ref, w_ref, qkv_ref, u_ref, h_ref, *, n_qkv):
    j = pl.program_id(1)

    @pl.when(j == 0)
    def _():
        x = x_ref[...]
        h_ref[...] = (x * _rms_scale(x) * g_ref[...]).astype(BF16)

    r = jnp.dot(h_ref[...], w_ref[...], preferred_element_type=F32)

    @pl.when(j < n_qkv)
    def _():
        qkv_ref[...] = r.astype(BF16)

    @pl.when(j == n_qkv)
    def _():
        u_ref[...] = r


def _in_proj(x, g, w_in, *, ssm_width, tm=1024):
    t, d = x.shape
    n = w_in.shape[1]
    tn = ssm_width
    n_qkv = (n - ssm_width) // tn
    assert t % tm == 0 and n == (n_qkv + 1) * tn
    return pl.pallas_call(
        functools.partial(_in_proj_kernel, n_qkv=n_qkv),
        grid=(t // tm, n_qkv + 1),
        in_specs=[
            pl.BlockSpec((tm, d), lambda i, j: (i, 0)),
            pl.BlockSpec((1, d), lambda i, j: (0, 0)),
            pl.BlockSpec((d, tn), lambda i, j: (0, j)),
        ],
        out_specs=[
            pl.BlockSpec((tm, tn), lambda i, j: (i, jnp.minimum(j, n_qkv - 1))),
            pl.BlockSpec((tm, tn), lambda i, j: (i, 0)),
        ],
        out_shape=[
            jax.ShapeDtypeStruct((t, n - ssm_width), BF16),
            jax.ShapeDtypeStruct((t, ssm_width), F32),
        ],
        scratch_shapes=[pltpu.VMEM((tm, d), BF16)],
        compiler_params=pltpu.CompilerParams(
            dimension_semantics=("parallel", "arbitrary"),
            vmem_limit_bytes=VMEM_LIMIT_BYTES),
        name="in_proj",
    )(x, g.reshape(1, d), w_in)


def _attn_kernel(slopes_ref, lq1_ref, lk1_ref, lq2_ref, lk2_ref, hg_ref,
                 q_ref, k_ref, v_ref, o_ref, bias_ref, *, tq, lam_init):
    h = pl.program_id(0)
    qi = pl.program_id(1)
    b = pl.program_id(2)
    s_len = k_ref.shape[0]

    @pl.when(b == 0)
    def _():
        qpos = qi * tq + lax.broadcasted_iota(jnp.int32, (tq, s_len), 0)
        kpos = lax.broadcasted_iota(jnp.int32, (tq, s_len), 1)
        bias_ref[...] = slopes_ref[h] * jnp.abs(qpos - kpos).astype(F32)

    lam = (jnp.exp(jnp.sum(lq1_ref[...] * lk1_ref[...], axis=-1, keepdims=True))
           - jnp.exp(jnp.sum(lq2_ref[...] * lk2_ref[...], axis=-1, keepdims=True))
           + lam_init)

    q = q_ref[...] * jnp.asarray(ATTN_HEAD_DIM ** -0.5, BF16)
    k = k_ref[...]
    lane = lax.broadcasted_iota(jnp.int32, q.shape, 1)
    bias = bias_ref[...]
    w = None
    for c in range(2):
        in_map = (lane >= c * ATTN_HEAD_DIM) & (lane < (c + 1) * ATTN_HEAD_DIM)
        qc = jnp.where(in_map, q, jnp.zeros_like(q))
        s = lax.dot_general(qc, k, (((1,), (1,)), ((), ())), preferred_element_type=F32) - bias
        p = jnp.exp(s - jnp.max(s, axis=-1, keepdims=True))
        inv = 1.0 / jnp.sum(p, axis=-1, keepdims=True)
        if c == 0:
            w = p * inv
        else:
            w = w - p * (lam * inv)
    o = jnp.dot(w.astype(BF16), v_ref[...], preferred_element_type=F32)
    o_ref[...] = (o * _rms_scale(o) * hg_ref[...] * (1.0 - lam_init)).astype(o_ref.dtype)


def _attention(qkv, slopes, lq1, lk1, lq2, lk2, head_g, *, batch, seq, lam_init, tq=512):
    t = qkv.shape[0]
    e = ATTN_VALUE_DIM
    nq = seq // tq
    nh = ATTN_HEADS
    vec = lambda a: a.reshape(1, -1).astype(F32)
    small = lambda n: pl.BlockSpec((1, n), lambda h, qi, b: (0, 0))
    return pl.pallas_call(
        functools.partial(_attn_kernel, tq=tq, lam_init=lam_init),
        grid=(nh, nq, batch),
        in_specs=[
            pl.BlockSpec(memory_space=pltpu.SMEM),
            small(ATTN_HEAD_DIM), small(ATTN_HEAD_DIM), small(ATTN_HEAD_DIM), small(ATTN_HEAD_DIM),
            small(e),
            pl.BlockSpec((tq, e), lambda h, qi, b: (b * nq + qi, h)),
            pl.BlockSpec((seq, e), lambda h, qi, b: (b, nh + h)),
            pl.BlockSpec((seq, e), lambda h, qi, b: (b, 2 * nh + h)),
        ],
        out_specs=pl.BlockSpec((tq, e), lambda h, qi, b: (b * nq + qi, h)),
        out_shape=jax.ShapeDtypeStruct((t, nh * e), BF16),
        scratch_shapes=[pltpu.VMEM((tq, seq), F32)],
        compiler_params=pltpu.CompilerParams(
            dimension_semantics=("parallel", "parallel", "arbitrary"),
            vmem_limit_bytes=VMEM_LIMIT_BYTES),
        name="diff_attention",
    )(slopes, vec(lq1), vec(lk1), vec(lq2), vec(lk2), vec(head_g), qkv, qkv, qkv)


def _discretize(lam_re, lam_im, log_dt):
    dt = jnp.exp(log_dt)
    mag = jnp.exp(lam_re * dt)
    a_re = mag * jnp.cos(lam_im * dt)
    a_im = mag * jnp.sin(lam_im * dt)
    den = lam_re * lam_re + lam_im * lam_im
    nr = a_re - 1.0
    f_re = (nr * lam_re + a_im * lam_im) / den
    f_im = (a_im * lam_re - nr * lam_im) / den
    return dt, a_re, a_im, f_re, f_im


def _power(lam_re, lam_im, dt, e):
    mag = jnp.exp(lam_re * dt * e)
    ang = lam_im * dt * e
    return mag * jnp.cos(ang), mag * jnp.sin(ang)


def _shift_lanes(x, s, left):
    if s == 0:
        return x
    n = x.shape[-1]
    lane = lax.broadcasted_iota(jnp.int32, x.shape, 1)
    if left:
        return jnp.where(lane < n - s, pltpu.roll(x, n - s, axis=1), 0.0)
    return jnp.where(lane >= s, pltpu.roll(x, s, axis=1), 0.0)


def _s5_prep_kernel(lre_c_ref, lim_c_ref, ldt_c_ref, bre_c_ref, bim_c_ref, ctre_ref, ctim_ref,
                    lre_r_ref, lim_r_ref, ldt_r_ref, btre_r_ref, btim_r_ref,
                    w1_ref, cm_ref, al_ref):
    n, p, l = SSM_STATE, SSM_GROUP, CHUNK
    lag = (lax.broadcasted_iota(jnp.int32, (1, GROUP_LANES), 1) // p).astype(F32)

    t_rows = None
    cm_re, cm_im = [], []
    for d in range(2):
        lr, li, ldt = lre_c_ref[d, 0], lim_c_ref[d, 0], ldt_c_ref[d, 0]
        dt, a_re, a_im, f_re, f_im = _discretize(lr, li, ldt)
        b_re, b_im = bre_c_ref[d, 0], bim_c_ref[d, 0]
        bb_re = f_re * b_re - f_im * b_im
        bb_im = f_re * b_im + f_im * b_re
        e0 = lag if d == 0 else (l - 1.0) - lag
        pw_re, pw_im = _power(lr, li, dt, e0)
        ct_re, ct_im = ctre_ref[d, 0], ctim_ref[d, 0]
        ac_re = pw_re * ct_re - pw_im * ct_im
        ac_im = pw_re * ct_im + pw_im * ct_re
        contract0 = (((0,), (0,)), ((), ()))
        taps = (lax.dot_general(bb_re, ac_re, contract0, precision=lax.Precision.HIGHEST,
                                preferred_element_type=F32)
                - lax.dot_general(bb_im, ac_im, contract0, precision=lax.Precision.HIGHEST,
                                  preferred_element_type=F32))
        rows = [_shift_lanes(taps, p * (tp if d == 0 else l - 1 - tp), left=(d == 1))
                for tp in range(l)]
        rows = jnp.concatenate(rows, axis=0)
        t_rows = rows if t_rows is None else t_rows + rows
        cm_re.append(a_re * ac_re - a_im * ac_im)
        cm_im.append(-(a_re * ac_im + a_im * ac_re))
    w1_ref[0, :, 0:GROUP_LANES] = t_rows.astype(BF16)
    cm_ref[0] = jnp.concatenate(cm_re + cm_im, axis=0).astype(BF16)

    lr, li, ldt = lre_r_ref[0], lim_r_ref[0], ldt_r_ref[0]
    dt, a_re, a_im, f_re, f_im = _discretize(lr, li, ldt)
    bt_re, bt_im = btre_r_ref[0], btim_r_ref[0]
    bb_re = f_re * bt_re - f_im * bt_im
    bb_im = f_re * bt_im + f_im * bt_re
    tau = lax.broadcasted_iota(jnp.int32, (l, 2 * n), 0)
    lane = lax.broadcasted_iota(jnp.int32, (l, 2 * n), 1)
    e = jnp.where(lane < n, l - 1 - tau, tau).astype(F32)
    pw_re, pw_im = _power(lr, li, dt, e)
    bm_re = pw_re[:, None, :] * bb_re[None, :, :] - pw_im[:, None, :] * bb_im[None, :, :]
    bm_im = pw_re[:, None, :] * bb_im[None, :, :] + pw_im[:, None, :] * bb_re[None, :, :]
    w1_ref[0, :, GROUP_LANES:GROUP_LANES + 2 * n] = bm_re.reshape(l * p, 2 * n).astype(BF16)
    w1_ref[0, :, GROUP_LANES + 2 * n:] = bm_im.reshape(l * p, 2 * n).astype(BF16)
    al_re, al_im = _power(lr, li, dt, jnp.full((1, 2 * n), float(l), F32))
    al_ref[0] = jnp.concatenate([al_re, al_im], axis=0)


def _s5_prep(lam_re, lam_im, log_dt, b_re, b_im, c_re, c_im):
    _, g, n = lam_re.shape
    p = b_re.shape[-1]
    col = lambda a: a.reshape(2, g, n, 1)
    ldt_c = jnp.broadcast_to(log_dt[:, :, None, None], (2, g, n, 1))
    ct = lambda c: jnp.tile(jnp.swapaxes(c, 2, 3), (1, 1, 1, CHUNK))
    row = lambda a: jnp.concatenate([a[0], a[1]], axis=-1)[:, None, :]
    ldt_r = row(jnp.broadcast_to(log_dt[:, :, None], (2, g, n)))
    bt = lambda b: jnp.concatenate([jnp.swapaxes(b[0], 1, 2), jnp.swapaxes(b[1], 1, 2)], axis=-1)

    spec_c = lambda last: pl.BlockSpec((2, 1, n, last), lambda i: (0, i, 0, 0))
    spec_r = lambda rows: pl.BlockSpec((1, rows, 2 * n), lambda i: (i, 0, 0))
    return pl.pallas_call(
        _s5_prep_kernel,
        grid=(g,),
        in_specs=[spec_c(1), spec_c(1), spec_c(1), spec_c(p), spec_c(p),
                  spec_c(GROUP_LANES), spec_c(GROUP_LANES),
                  spec_r(1), spec_r(1), spec_r(1), spec_r(p), spec_r(p)],
        out_specs=[
            pl.BlockSpec((1, GROUP_LANES, GROUP_LANES + 4 * n), lambda i: (i, 0, 0)),
            pl.BlockSpec((1, 4 * n, GROUP_LANES), lambda i: (i, 0, 0)),
            pl.BlockSpec((1, 2, 2 * n), lambda i: (i, 0, 0)),
        ],
        out_shape=[
            jax.ShapeDtypeStruct((g, GROUP_LANES, GROUP_LANES + 4 * n), BF16),
            jax.ShapeDtypeStruct((g, 4 * n, GROUP_LANES), BF16),
            jax.ShapeDtypeStruct((g, 2, 2 * n), F32),
        ],
        compiler_params=pltpu.CompilerParams(dimension_semantics=("parallel",)),
        name="s5_prep",
    )(col(lam_re), col(lam_im), ldt_c, b_re, b_im, ct(c_re), ct(c_im),
      row(lam_re), row(lam_im), ldt_r, bt(b_re), bt(b_im))


def _swap_pieces(v):
    piece = lax.broadcasted_iota(jnp.int32, v[0].shape, 1) // SSM_GROUP
    v = list(v)
    for d in (4, 2, 1):
        hi = (piece & d) != 0
        for a in range(8):
            if a & d:
                continue
            va, vb = v[a], v[a + d]
            v[a] = jnp.where(hi, pltpu.roll(vb, d * SSM_GROUP, axis=1), va)
            v[a + d] = jnp.where(hi, vb, pltpu.roll(va, LANES - d * SSM_GROUP, axis=1))
    return v


def _s5_chunk_kernel(u_ref, w1_ref, cm_ref, al_ref, d_ref, y_ref, x_ref, s_ref, hf_ref, h_ref,
                     *, nb, nc):
    gl = GROUP_LANES
    gb = GROUPS_PER_BLOCK
    n2 = 2 * SSM_STATE
    half = CHUNK // 2
    ctile = SUBLANES * CHUNK

    def token_rows(ct, h, j):
        return pl.ds(ct * ctile + h * half + j, SUBLANES, stride=CHUNK)

    def chunk_rows(ct, b):
        return pl.ds(ct * SUBLANES * nb + b, SUBLANES, stride=nb)

    def gather_body(ct, carry):
        for b in range(nb):
            for h in range(2):
                v = [u_ref[b, token_rows(ct, h, j), :] for j in range(half)]
                w = _swap_pieces(v)
                for g in range(gb):
                    x_ref[2 * g + h, chunk_rows(ct, b), :] = w[g]
        return carry

    lax.fori_loop(0, nc // SUBLANES, gather_body, 0)

    for g in range(gb):
        x = jnp.concatenate([x_ref[2 * g], x_ref[2 * g + 1]], axis=1)
        r = jnp.dot(x.astype(BF16), w1_ref[g], preferred_element_type=F32)
        x_ref[2 * g] = r[:, :LANES]
        x_ref[2 * g + 1] = r[:, LANES:gl]
        s_ref[:, g * gl:(g + 1) * gl] = r[:, gl:]

    a_re = [al_ref[g, 0:1, :] for g in range(gb)]
    a_im = [al_ref[g, 1:2, :] for g in range(gb)]
    is_fwd = lax.broadcasted_iota(jnp.int32, (SUBLANES, n2), 1) < SSM_STATE

    def step(h, s):
        out = []
        for g in range(gb):
            hr, hi = h[:, g * gl:g * gl + n2], h[:, g * gl + n2:(g + 1) * gl]
            sr, si = s[:, g * gl:g * gl + n2], s[:, g * gl + n2:(g + 1) * gl]
            out.append(a_re[g] * hr - a_im[g] * hi + sr)
            out.append(a_re[g] * hi + a_im[g] * hr + si)
        return jnp.concatenate(out, axis=1)

    cpt = SUBLANES // nb
    n_tiles = nc // cpt
    tile_rows = lambda t: pl.ds(pl.multiple_of(t * SUBLANES, SUBLANES), SUBLANES)

    def fwd_body(t, h):
        s = s_ref[tile_rows(t), :]
        before = []
        for k in range(cpt):
            before.append(h)
            h = step(h, s[k * nb:(k + 1) * nb])
        hf_ref[tile_rows(t), :] = jnp.concatenate(before, axis=0)
        return h

    lax.fori_loop(0, n_tiles, fwd_body, jnp.zeros((nb, gb * gl), F32))

    def bwd_body(i, h):
        t = n_tiles - 1 - i
        s = s_ref[tile_rows(t), :]
        before = [None] * cpt
        for k in reversed(range(cpt)):
            before[k] = h
            h = step(h, s[k * nb:(k + 1) * nb])
        hb = jnp.concatenate(before, axis=0)
        hf = hf_ref[tile_rows(t), :]
        merged = [jnp.where(is_fwd, hf[:, k * n2:(k + 1) * n2], hb[:, k * n2:(k + 1) * n2])
                  for k in range(2 * gb)]
        h_ref[tile_rows(t), :] = jnp.concatenate(merged, axis=1)
        return h

    lax.fori_loop(0, n_tiles, bwd_body, jnp.zeros((nb, gb * gl), F32))

    for g in range(gb):
        r = jnp.dot(h_ref[:, g * gl:(g + 1) * gl].astype(BF16), cm_ref[g],
                    preferred_element_type=F32)
        x_ref[2 * g] += r[:, :LANES]
        x_ref[2 * g + 1] += r[:, LANES:]

    def scatter_body(ct, carry):
        for b in range(nb):
            for h in range(2):
                w = [x_ref[2 * g + h, chunk_rows(ct, b), :] for g in range(gb)]
                v = _swap_pieces(w)
                for j in range(half):
                    rows = token_rows(ct, h, j)
                    y_ref[b, rows, :] = v[j] + d_ref[...] * u_ref[b, rows, :]
        return carry

    lax.fori_loop(0, nc // SUBLANES, scatter_body, 0)


def _s5_chunks(u, w1, cm, al, d):
    nb, seq, width = u.shape
    nc = seq // CHUNK
    rows = nb * nc
    gb = GROUPS_PER_BLOCK
    assert SUBLANES % nb == 0 and nc % SUBLANES == 0 and width % LANES == 0
    blk = pl.BlockSpec((nb, seq, LANES), lambda i: (0, 0, i))
    return pl.pallas_call(
        functools.partial(_s5_chunk_kernel, nb=nb, nc=nc),
        grid=(width // LANES,),
        in_specs=[
            blk,
            pl.BlockSpec((gb,) + w1.shape[1:], lambda i: (i, 0, 0)),
            pl.BlockSpec((gb,) + cm.shape[1:], lambda i: (i, 0, 0)),
            pl.BlockSpec((gb,) + al.shape[1:], lambda i: (i, 0, 0)),
            pl.BlockSpec((1, LANES), lambda i: (0, i)),
        ],
        out_specs=blk,
        out_shape=jax.ShapeDtypeStruct(u.shape, F32),
        scratch_shapes=[pltpu.VMEM((2 * gb, rows, LANES), F32)]
        + [pltpu.VMEM((rows, gb * GROUP_LANES), F32)] * 3,
        compiler_params=pltpu.CompilerParams(
            dimension_semantics=("parallel",), vmem_limit_bytes=VMEM_LIMIT_BYTES),
        name="s5_chunks",
    )(u, w1, cm, al, d.reshape(1, width))


def _mix_out_kernel(x_ref, a_ref, y_ref, wglu_ref, bglu_ref, og_ref, wo_a_ref, wo_s_ref, pg_ref,
                    o_ref):
    y = y_ref[...]
    g = y * (0.5 * (1.0 + jnp.tanh(math.sqrt(2.0 / math.pi) * (y + 0.044715 * (y * y * y)))))
    z = jnp.dot(g.astype(BF16), wglu_ref[...], preferred_element_type=F32) + bglu_ref[...]
    s = g * _sigmoid(z)
    s = s * _rms_scale(s) * og_ref[...]
    mixed = (jnp.dot(a_ref[...], wo_a_ref[...], preferred_element_type=F32)
             + jnp.dot(s.astype(BF16), wo_s_ref[...], preferred_element_type=F32))
    o_ref[...] = x_ref[...] + mixed * _rms_scale(mixed) * pg_ref[...]


def _mix_out(x, a, y, w_glu, b_glu, out_g, w_out, post_g, *, tm=512):
    t, d = x.shape
    wa = a.shape[1]
    ws = y.shape[1]
    const = lambda shape: pl.BlockSpec(shape, lambda i: (0, 0))
    return pl.pallas_call(
        _mix_out_kernel,
        grid=(t // tm,),
        in_specs=[
            pl.BlockSpec((tm, d), lambda i: (i, 0)),
            pl.BlockSpec((tm, wa), lambda i: (i, 0)),
            pl.BlockSpec((tm, ws), lambda i: (i, 0)),
            const((ws, ws)), const((1, ws)), const((1, ws)),
            pl.BlockSpec((wa, d), lambda i: (0, 0)),
            pl.BlockSpec((ws, d), lambda i: (wa // ws, 0)),
            const((1, d)),
        ],
        out_specs=pl.BlockSpec((tm, d), lambda i: (i, 0)),
        out_shape=jax.ShapeDtypeStruct((t, d), F32),
        compiler_params=pltpu.CompilerParams(
            dimension_semantics=("parallel",), vmem_limit_bytes=VMEM_LIMIT_BYTES),
        name="mix_out",
    )(x, a, y, w_glu, b_glu.reshape(1, ws), out_g.reshape(1, ws), w_out, w_out,
      post_g.reshape(1, d))


def kernel(x, ff1_pre_g, ff1_w_gate, ff1_w_up, ff1_w_down, ff1_post_g, mix_pre_g, w_in, lam_q1, lam_k1, lam_q2, lam_k2, attn_head_g, ssm_lam_re, ssm_lam_im, ssm_log_dt, ssm_b_re, ssm_b_im, ssm_c_re, ssm_c_im, ssm_d, ssm_w_glu, ssm_b_glu, ssm_out_g, w_out, mix_post_g, ff2_pre_g, ff2_w_gate, ff2_w_up, ff2_w_down, ff2_post_g):
    batch, seq, d_model = x.shape
    depth = w_in.shape[0]
    ssm_width = ssm_w_glu.shape[-1]
    slopes = jnp.asarray([2.0 ** (-8.0 * (i + 1) / ATTN_HEADS) for i in range(ATTN_HEADS)], F32)
    bf = lambda w: w.astype(BF16)

    xt = x.reshape(batch * seq, d_model)
    for l in range(depth):
        xt = _ffn(xt, ff1_pre_g[l], bf(ff1_w_gate[l]), bf(ff1_w_up[l]), bf(ff1_w_down[l]),
                  ff1_post_g[l])

        qkv, u = _in_proj(xt, mix_pre_g[l], bf(w_in[l]), ssm_width=ssm_width)
        lam_init = 0.8 - 0.6 * math.exp(-0.3 * l)
        a = _attention(qkv, slopes, lam_q1[l], lam_k1[l], lam_q2[l], lam_k2[l], attn_head_g[l],
                       batch=batch, seq=seq, lam_init=lam_init)

        w1, cm, al = _s5_prep(ssm_lam_re[l], ssm_lam_im[l], ssm_log_dt[l], ssm_b_re[l],
                              ssm_b_im[l], ssm_c_re[l], ssm_c_im[l])
        y = _s5_chunks(u.reshape(batch, seq, ssm_width), w1, cm, al, ssm_d[l])
        y = y.reshape(batch * seq, ssm_width)

        xt = _mix_out(xt, a, y, bf(ssm_w_glu[l]), ssm_b_glu[l], ssm_out_g[l], bf(w_out[l]),
                      mix_post_g[l])

        xt = _ffn(xt, ff2_pre_g[l], bf(ff2_w_gate[l]), bf(ff2_w_up[l]), bf(ff2_w_down[l]),
                  ff2_post_g[l])
    return xt.reshape(batch, seq, d_model)
```

```python
import functools
import math

import jax
import jax.numpy as jnp
from jax import lax
from jax.experimental import pallas as pl
from jax.experimental.pallas import tpu as pltpu

F32 = jnp.float32
BF16 = jnp.bfloat16

NORM_EPS = 1e-6
ATTN_HEADS = 8
ATTN_HEAD_DIM = 64
ATTN_VALUE_DIM = 2 * ATTN_HEAD_DIM
KEY_CHUNK = 512
SSM_GROUP = 16
SSM_STATE = 64
CHUNK = 16
GROUP_LANES = CHUNK * SSM_GROUP
SUBLANES = 8
LANES = 128
GROUPS_PER_BLOCK = LANES // SSM_GROUP

VMEM_LIMIT_BYTES = 56 * 1024 * 1024


def _rms_scale(x):
    return lax.rsqrt(jnp.mean(x * x, axis=-1, keepdims=True) + NORM_EPS)


def _sigmoid(x):
    return 1.0 / (1.0 + jnp.exp(-x))


def _ffn_kernel(x_ref, pre_g_ref, wg_ref, wu_ref, wd_ref, post_g_ref, o_ref, h_ref):
    j = pl.program_id(1)
    last = pl.num_programs(1) - 1

    @pl.when(j == 0)
    def _():
        x = x_ref[...]
        h_ref[...] = (x * _rms_scale(x) * pre_g_ref[...]).astype(BF16)
        o_ref[...] = jnp.zeros_like(o_ref)

    h = h_ref[...]
    gate = jnp.dot(h, wg_ref[...], preferred_element_type=F32)
    up = jnp.dot(h, wu_ref[...], preferred_element_type=F32)
    act = (gate * _sigmoid(gate) * up).astype(BF16)
    o_ref[...] += jnp.dot(act, wd_ref[...], preferred_element_type=F32)

    @pl.when(j == last)
    def _():
        acc = o_ref[...]
        o_ref[...] = x_ref[...] + 0.5 * (acc * _rms_scale(acc) * post_g_ref[...])


def _ffn(x, pre_g, w_gate, w_up, w_down, post_g, *, tm=512, tf=512):
    t, d = x.shape
    f = w_gate.shape[1]
    assert t % tm == 0 and f % tf == 0
    return pl.pallas_call(
        _ffn_kernel,
        grid=(t // tm, f // tf),
        in_specs=[
            pl.BlockSpec((tm, d), lambda i, j: (i, 0)),
            pl.BlockSpec((1, d), lambda i, j: (0, 0)),
            pl.BlockSpec((d, tf), lambda i, j: (0, j)),
            pl.BlockSpec((d, tf), lambda i, j: (0, j)),
            pl.BlockSpec((tf, d), lambda i, j: (j, 0)),
            pl.BlockSpec((1, d), lambda i, j: (0, 0)),
        ],
        out_specs=pl.BlockSpec((tm, d), lambda i, j: (i, 0)),
        out_shape=jax.ShapeDtypeStruct((t, d), F32),
        scratch_shapes=[pltpu.VMEM((tm, d), BF16)],
        compiler_params=pltpu.CompilerParams(
            dimension_semantics=("parallel", "arbitrary"),
            vmem_limit_bytes=VMEM_LIMIT_BYTES),
        name="ffn",
    )(x, pre_g.reshape(1, d), w_gate, w_up, w_down, post_g.reshape(1, d))


def _in_proj_kernel(x_ref, g_ref, w_ref, qkv_ref, u_ref, h_ref, *, n_qkv):
    j = pl.program_id(1)

    @pl.when(j == 0)
    def _():
        x = x_ref[...]
        h_ref[...] = (x * _rms_scale(x) * g_ref[...]).astype(BF16)

    r = jnp.dot(h_ref[...], w_ref[...], preferred_element_type=F32)

    @pl.when(j < n_qkv)
    def _():
        qkv_ref[...] = r.astype(BF16)

    @pl.when(j == n_qkv)
    def _():
        u_ref[...] = r


def _in_proj(x, g, w_in, *, ssm_width, tm=1024):
    t, d = x.shape
    n = w_in.shape[1]
    tn = ssm_width
    n_qkv = (n - ssm_width) // tn
    assert t % tm == 0 and n == (n_qkv + 1) * tn
    return pl.pallas_call(
        functools.partial(_in_proj_kernel, n_qkv=n_qkv),
        grid=(t // tm, n_qkv + 1),
        in_specs=[
            pl.BlockSpec((tm, d), lambda i, j: (i, 0)),
            pl.BlockSpec((1, d), lambda i, j: (0, 0)),
            pl.BlockSpec((d, tn), lambda i, j: (0, j)),
        ],
        out_specs=[
            pl.BlockSpec((tm, tn), lambda i, j: (i, jnp.minimum(j, n_qkv - 1))),
            pl.BlockSpec((tm, tn), lambda i, j: (i, 0)),
        ],
        out_shape=[
            jax.ShapeDtypeStruct((t, n - ssm_width), BF16),
            jax.ShapeDtypeStruct((t, ssm_width), F32),
        ],
        scratch_shapes=[pltpu.VMEM((tm, d), BF16)],
        compiler_params=pltpu.CompilerParams(
            dimension_semantics=("parallel", "arbitrary"),
            vmem_limit_bytes=VMEM_LIMIT_BYTES),
        name="in_proj",
    )(x, g.reshape(1, d), w_in)


def _attn_kernel(slopes_ref, lq1_ref, lk1_ref, lq2_ref, lk2_ref, hg_ref,
                 q_ref, k_ref, v_ref, o_ref, bias_ref, s_ref, p_ref, *, tq, lam_init):
    h = pl.program_id(0)
    qi = pl.program_id(1)
    b = pl.program_id(2)
    s_len = k_ref.shape[0]

    @pl.when(b == 0)
    def _():
        qpos = qi * tq + lax.broadcasted_iota(jnp.int32, (tq, s_len), 0)
        kpos = lax.broadcasted_iota(jnp.int32, (tq, s_len), 1)
        bias_ref[...] = slopes_ref[h] * jnp.abs(qpos - kpos).astype(F32)

    lam = (jnp.exp(jnp.sum(lq1_ref[...] * lk1_ref[...], axis=-1, keepdims=True))
           - jnp.exp(jnp.sum(lq2_ref[...] * lk2_ref[...], axis=-1, keepdims=True))
           + lam_init)

    q = q_ref[...] * jnp.asarray(ATTN_HEAD_DIM ** -0.5, BF16)
    k = k_ref[...]
    v = v_ref[...]
    v_ext = jnp.concatenate([v, jnp.ones_like(v)], axis=1)
    lane = lax.broadcasted_iota(jnp.int32, q.shape, 1)
    e = v.shape[1]
    for c in range(2):
        in_map = (lane >= c * ATTN_HEAD_DIM) & (lane < (c + 1) * ATTN_HEAD_DIM)
        qc = jnp.where(in_map, q, jnp.zeros_like(q))
        for j in range(0, s_len, KEY_CHUNK):
            cols = slice(j, j + KEY_CHUNK)
            s_ref[c, :, cols] = lax.dot_general(
                qc, k[cols], (((1,), (1,)), ((), ())),
                preferred_element_type=F32) - bias_ref[:, cols]
    for c in range(2):
        s = s_ref[c]
        p_ref[c] = jnp.exp(s - jnp.max(s, axis=-1, keepdims=True)).astype(BF16)
    outs = []
    for c in range(2):
        pv = jnp.dot(p_ref[c], v_ext, preferred_element_type=F32)
        outs.append(pv[:, :e] / pv[:, e:])
    o = outs[0] - lam * outs[1]
    o_ref[...] = (o * _rms_scale(o) * hg_ref[...] * (1.0 - lam_init)).astype(o_ref.dtype)


def _attention(qkv, slopes, lq1, lk1, lq2, lk2, head_g, *, batch, seq, lam_init, tq=512):
    t = qkv.shape[0]
    e = ATTN_VALUE_DIM
    nq = seq // tq
    nh = ATTN_HEADS
    vec = lambda a: a.reshape(1, -1).astype(F32)
    small = lambda n: pl.BlockSpec((1, n), lambda h, qi, b: (0, 0))
    return pl.pallas_call(
        functools.partial(_attn_kernel, tq=tq, lam_init=lam_init),
        grid=(nh, nq, batch),
        in_specs=[
            pl.BlockSpec(memory_space=pltpu.SMEM),
            small(ATTN_HEAD_DIM), small(ATTN_HEAD_DIM), small(ATTN_HEAD_DIM), small(ATTN_HEAD_DIM),
            small(e),
            pl.BlockSpec((tq, e), lambda h, qi, b: (b * nq + qi, h)),
            pl.BlockSpec((seq, e), lambda h, qi, b: (b, nh + h)),
            pl.BlockSpec((seq, e), lambda h, qi, b: (b, 2 * nh + h)),
        ],
        out_specs=pl.BlockSpec((tq, e), lambda h, qi, b: (b * nq + qi, h)),
        out_shape=jax.ShapeDtypeStruct((t, nh * e), BF16),
        scratch_shapes=[pltpu.VMEM((tq, seq), F32), pltpu.VMEM((2, tq, seq), F32),
                        pltpu.VMEM((2, tq, seq), BF16)],
        compiler_params=pltpu.CompilerParams(
            dimension_semantics=("parallel", "parallel", "arbitrary"),
            vmem_limit_bytes=VMEM_LIMIT_BYTES),
        name="diff_attention",
    )(slopes, vec(lq1), vec(lk1), vec(lq2), vec(lk2), vec(head_g), qkv, qkv, qkv)


def _discretize(lam_re, lam_im, log_dt):
    dt = jnp.exp(log_dt)
    mag = jnp.exp(lam_re * dt)
    a_re = mag * jnp.cos(lam_im * dt)
    a_im = mag * jnp.sin(lam_im * dt)
    den = lam_re * lam_re + lam_im * lam_im
    nr = a_re - 1.0
    f_re = (nr * lam_re + a_im * lam_im) / den
    f_im = (a_im * lam_re - nr * lam_im) / den
    return dt, a_re, a_im, f_re, f_im


def _power(lam_re, lam_im, dt, e):
    mag = jnp.exp(lam_re * dt * e)
    ang = lam_im * dt * e
    return mag * jnp.cos(ang), mag * jnp.sin(ang)


def _shift_lanes(x, s, left):
    if s == 0:
        return x
    n = x.shape[-1]
    lane = lax.broadcasted_iota(jnp.int32, x.shape, 1)
    if left:
        return jnp.where(lane < n - s, pltpu.roll(x, n - s, axis=1), 0.0)
    return jnp.where(lane >= s, pltpu.roll(x, s, axis=1), 0.0)


def _s5_prep_kernel(lre_c_ref, lim_c_ref, ldt_c_ref, bre_c_ref, bim_c_ref, ctre_ref, ctim_ref,
                    lre_r_ref, lim_r_ref, ldt_r_ref, btre_r_ref, btim_r_ref,
                    w1_ref, cm_ref, al_ref):
    n, p, l = SSM_STATE, SSM_GROUP, CHUNK
    lag = (lax.broadcasted_iota(jnp.int32, (1, GROUP_LANES), 1) // p).astype(F32)

    t_rows = None
    cm_re, cm_im = [], []
    for d in range(2):
        lr, li, ldt = lre_c_ref[d, 0], lim_c_ref[d, 0], ldt_c_ref[d, 0]
        dt, a_re, a_im, f_re, f_im = _discretize(lr, li, ldt)
        b_re, b_im = bre_c_ref[d, 0], bim_c_ref[d, 0]
        bb_re = f_re * b_re - f_im * b_im
        bb_im = f_re * b_im + f_im * b_re
        e0 = lag if d == 0 else (l - 1.0) - lag
        pw_re, pw_im = _power(lr, li, dt, e0)
        ct_re, ct_im = ctre_ref[d, 0], ctim_ref[d, 0]
        ac_re = pw_re * ct_re - pw_im * ct_im
        ac_im = pw_re * ct_im + pw_im * ct_re
        contract0 = (((0,), (0,)), ((), ()))
        taps = (lax.dot_general(bb_re, ac_re, contract0, precision=lax.Precision.HIGHEST,
                                preferred_element_type=F32)
                - lax.dot_general(bb_im, ac_im, contract0, precision=lax.Precision.HIGHEST,
                                  preferred_element_type=F32))
        rows = [_shift_lanes(taps, p * (tp if d == 0 else l - 1 - tp), left=(d == 1))
                for tp in range(l)]
        rows = jnp.concatenate(rows, axis=0)
        t_rows = rows if t_rows is None else t_rows + rows
        cm_re.append(a_re * ac_re - a_im * ac_im)
        cm_im.append(-(a_re * ac_im + a_im * ac_re))
    w1_ref[0, :, 0:GROUP_LANES] = t_rows.astype(BF16)
    cm_ref[0] = jnp.concatenate(cm_re + cm_im, axis=0).astype(BF16)

    lr, li, ldt = lre_r_ref[0], lim_r_ref[0], ldt_r_ref[0]
    dt, a_re, a_im, f_re, f_im = _discretize(lr, li, ldt)
    bt_re, bt_im = btre_r_ref[0], btim_r_ref[0]
    bb_re = f_re * bt_re - f_im * bt_im
    bb_im = f_re * bt_im + f_im * bt_re
    tau = lax.broadcasted_iota(jnp.int32, (l, 2 * n), 0)
    lane = lax.broadcasted_iota(jnp.int32, (l, 2 * n), 1)
    e = jnp.where(lane < n, l - 1 - tau, tau).astype(F32)
    pw_re, pw_im = _power(lr, li, dt, e)
    bm_re = pw_re[:, None, :] * bb_re[None, :, :] - pw_im[:, None, :] * bb_im[None, :, :]
    bm_im = pw_re[:, None, :] * bb_im[None, :, :] + pw_im[:, None, :] * bb_re[None, :, :]
    w1_ref[0, :, GROUP_LANES:GROUP_LANES + 2 * n] = bm_re.reshape(l * p, 2 * n).astype(BF16)
    w1_ref[0, :, GROUP_LANES + 2 * n:] = bm_im.reshape(l * p, 2 * n).astype(BF16)
    al_re, al_im = _power(lr, li, dt, jnp.full((1, 2 * n), float(l), F32))
    al_ref[0] = jnp.concatenate([al_re, al_im], axis=0)


def _s5_prep(lam_re, lam_im, log_dt, b_re, b_im, c_re, c_im):
    _, g, n = lam_re.shape
    p = b_re.shape[-1]
    col = lambda a: a.reshape(2, g, n, 1)
    ldt_c = jnp.broadcast_to(log_dt[:, :, None, None], (2, g, n, 1))
    ct = lambda c: jnp.tile(jnp.swapaxes(c, 2, 3), (1, 1, 1, CHUNK))
    row = lambda a: jnp.concatenate([a[0], a[1]], axis=-1)[:, None, :]
    ldt_r = row(jnp.broadcast_to(log_dt[:, :, None], (2, g, n)))
    bt = lambda b: jnp.concatenate([jnp.swapaxes(b[0], 1, 2), jnp.swapaxes(b[1], 1, 2)], axis=-1)

    spec_c = lambda last: pl.BlockSpec((2, 1, n, last), lambda i: (0, i, 0, 0))
    spec_r = lambda rows: pl.BlockSpec((1, rows, 2 * n), lambda i: (i, 0, 0))
    return pl.pallas_call(
        _s5_prep_kernel,
        grid=(g,),
        in_specs=[spec_c(1), spec_c(1), spec_c(1), spec_c(p), spec_c(p),
                  spec_c(GROUP_LANES), spec_c(GROUP_LANES),
                  spec_r(1), spec_r(1), spec_r(1), spec_r(p), spec_r(p)],
        out_specs=[
            pl.BlockSpec((1, GROUP_LANES, GROUP_LANES + 4 * n), lambda i: (i, 0, 0)),
            pl.BlockSpec((1, 4 * n, GROUP_LANES), lambda i: (i, 0, 0)),
            pl.BlockSpec((1, 2, 2 * n), lambda i: (i, 0, 0)),
        ],
        out_shape=[
            jax.ShapeDtypeStruct((g, GROUP_LANES, GROUP_LANES + 4 * n), BF16),
            jax.ShapeDtypeStruct((g, 4 * n, GROUP_LANES), BF16),
            jax.ShapeDtypeStruct((g, 2, 2 * n), F32),
        ],
        compiler_params=pltpu.CompilerParams(dimension_semantics=("parallel",)),
        name="s5_prep",
    )(col(lam_re), col(lam_im), ldt_c, b_re, b_im, ct(c_re), ct(c_im),
      row(lam_re), row(lam_im), ldt_r, bt(b_re), bt(b_im))


def _swap_pieces(v):
    piece = lax.broadcasted_iota(jnp.int32, v[0].shape, 1) // SSM_GROUP
    v = list(v)
    for d in (4, 2, 1):
        hi = (piece & d) != 0
        for a in range(8):
            if a & d:
                continue
            va, vb = v[a], v[a + d]
            v[a] = jnp.where(hi, pltpu.roll(vb, d * SSM_GROUP, axis=1), va)
            v[a + d] = jnp.where(hi, vb, pltpu.roll(va, LANES - d * SSM_GROUP, axis=1))
    return v


def _s5_chunk_kernel(u_ref, w1_ref, cm_ref, al_ref, d_ref, y_ref, x_ref, s_ref, hf_ref, h_ref,
                     *, nb, nc):
    gl = GROUP_LANES
    gb = GROUPS_PER_BLOCK
    n2 = 2 * SSM_STATE
    half = CHUNK // 2
    ctile = SUBLANES * CHUNK

    def token_rows(ct, h, j):
        return pl.ds(ct * ctile + h * half + j, SUBLANES, stride=CHUNK)

    def chunk_rows(ct, b):
        return pl.ds(ct * SUBLANES * nb + b, SUBLANES, stride=nb)

    def gather_body(ct, carry):
        for b in range(nb):
            for h in range(2):
                v = [u_ref[b, token_rows(ct, h, j), :] for j in range(half)]
                w = _swap_pieces(v)
                for g in range(gb):
                    x_ref[2 * g + h, chunk_rows(ct, b), :] = w[g]
        return carry

    lax.fori_loop(0, nc // SUBLANES, gather_body, 0)

    for g in range(gb):
        x = jnp.concatenate([x_ref[2 * g], x_ref[2 * g + 1]], axis=1)
        r = jnp.dot(x.astype(BF16), w1_ref[g], preferred_element_type=F32)
        x_ref[2 * g] = r[:, :LANES]
        x_ref[2 * g + 1] = r[:, LANES:gl]
        s_ref[:, g * gl:(g + 1) * gl] = r[:, gl:]

    a_re = [al_ref[g, 0:1, :] for g in range(gb)]
    a_im = [al_ref[g, 1:2, :] for g in range(gb)]
    is_fwd = lax.broadcasted_iota(jnp.int32, (SUBLANES, n2), 1) < SSM_STATE

    def step(h, s):
        out = []
        for g in range(gb):
            hr, hi = h[:, g * gl:g * gl + n2], h[:, g * gl + n2:(g + 1) * gl]
            sr, si = s[:, g * gl:g * gl + n2], s[:, g * gl + n2:(g + 1) * gl]
            out.append(a_re[g] * hr - a_im[g] * hi + sr)
            out.append(a_re[g] * hi + a_im[g] * hr + si)
        return jnp.concatenate(out, axis=1)

    cpt = SUBLANES // nb
    n_tiles = nc // cpt
    tile_rows = lambda t: pl.ds(pl.multiple_of(t * SUBLANES, SUBLANES), SUBLANES)

    def fwd_body(t, h):
        s = s_ref[tile_rows(t), :]
        before = []
        for k in range(cpt):
            before.append(h)
            h = step(h, s[k * nb:(k + 1) * nb])
        hf_ref[tile_rows(t), :] = jnp.concatenate(before, axis=0)
        return h

    lax.fori_loop(0, n_tiles, fwd_body, jnp.zeros((nb, gb * gl), F32))

    def bwd_body(i, h):
        t = n_tiles - 1 - i
        s = s_ref[tile_rows(t), :]
        before = [None] * cpt
        for k in reversed(range(cpt)):
            before[k] = h
            h = step(h, s[k * nb:(k + 1) * nb])
        hb = jnp.concatenate(before, axis=0)
        hf = hf_ref[tile_rows(t), :]
        merged = [jnp.where(is_fwd, hf[:, k * n2:(k + 1) * n2], hb[:, k * n2:(k + 1) * n2])
                  for k in range(2 * gb)]
        h_ref[tile_rows(t), :] = jnp.concatenate(merged, axis=1)
        return h

    lax.fori_loop(0, n_tiles, bwd_body, jnp.zeros((nb, gb * gl), F32))

    for g in range(gb):
        r = jnp.dot(h_ref[:, g * gl:(g + 1) * gl].astype(BF16), cm_ref[g],
                    preferred_element_type=F32)
        x_ref[2 * g] += r[:, :LANES]
        x_ref[2 * g + 1] += r[:, LANES:]

    def scatter_body(ct, carry):
        for b in range(nb):
            for h in range(2):
                w = [x_ref[2 * g + h, chunk_rows(ct, b), :] for g in range(gb)]
                v = _swap_pieces(w)
                for j in range(half):
                    rows = token_rows(ct, h, j)
                    y_ref[b, rows, :] = v[j] + d_ref[...] * u_ref[b, rows, :]
        return carry

    lax.fori_loop(0, nc // SUBLANES, scatter_body, 0)


def _s5_chunks(u, w1, cm, al, d):
    nb, seq, width = u.shape
    nc = seq // CHUNK
    rows = nb * nc
    gb = GROUPS_PER_BLOCK
    assert SUBLANES % nb == 0 and nc % SUBLANES == 0 and width % LANES == 0
    blk = pl.BlockSpec((nb, seq, LANES), lambda i: (0, 0, i))
    return pl.pallas_call(
        functools.partial(_s5_chunk_kernel, nb=nb, nc=nc),
        grid=(width // LANES,),
        in_specs=[
            blk,
            pl.BlockSpec((gb,) + w1.shape[1:], lambda i: (i, 0, 0)),
            pl.BlockSpec((gb,) + cm.shape[1:], lambda i: (i, 0, 0)),
            pl.BlockSpec((gb,) + al.shape[1:], lambda i: (i, 0, 0)),
            pl.BlockSpec((1, LANES), lambda i: (0, i)),
        ],
        out_specs=blk,
        out_shape=jax.ShapeDtypeStruct(u.shape, F32),
        scratch_shapes=[pltpu.VMEM((2 * gb, rows, LANES), F32)]
        + [pltpu.VMEM((rows, gb * GROUP_LANES), F32)] * 3,
        compiler_params=pltpu.CompilerParams(
            dimension_semantics=("parallel",), vmem_limit_bytes=VMEM_LIMIT_BYTES),
        name="s5_chunks",
    )(u, w1, cm, al, d.reshape(1, width))


def _mix_out_kernel(x_ref, a_ref, y_ref, wglu_ref, bglu_ref, og_ref, wo_a_ref, wo_s_ref, pg_ref,
                    o_ref):
    y = y_ref[...]
    g = y * (0.5 * (1.0 + jnp.tanh(math.sqrt(2.0 / math.pi) * (y + 0.044715 * (y * y * y)))))
    z = jnp.dot(g.astype(BF16), wglu_ref[...], preferred_element_type=F32) + bglu_ref[...]
    s = g * _sigmoid(z)
    s = s * _rms_scale(s) * og_ref[...]
    mixed = (jnp.dot(a_ref[...], wo_a_ref[...], preferred_element_type=F32)
             + jnp.dot(s.astype(BF16), wo_s_ref[...], preferred_element_type=F32))
    o_ref[...] = x_ref[...] + mixed * _rms_scale(mixed) * pg_ref[...]


def _mix_out(x, a, y, w_glu, b_glu, out_g, w_out, post_g, *, tm=512):
    t, d = x.shape
    wa = a.shape[1]
    ws = y.shape[1]
    const = lambda shape: pl.BlockSpec(shape, lambda i: (0, 0))
    return pl.pallas_call(
        _mix_out_kernel,
        grid=(t // tm,),
        in_specs=[
            pl.BlockSpec((tm, d), lambda i: (i, 0)),
            pl.BlockSpec((tm, wa), lambda i: (i, 0)),
            pl.BlockSpec((tm, ws), lambda i: (i, 0)),
            const((ws, ws)), const((1, ws)), const((1, ws)),
            pl.BlockSpec((wa, d), lambda i: (0, 0)),
            pl.BlockSpec((ws, d), lambda i: (wa // ws, 0)),
            const((1, d)),
        ],
        out_specs=pl.BlockSpec((tm, d), lambda i: (i, 0)),
        out_shape=jax.ShapeDtypeStruct((t, d), F32),
        compiler_params=pltpu.CompilerParams(
            dimension_semantics=("parallel",), vmem_limit_bytes=VMEM_LIMIT_BYTES),
        name="mix_out",
    )(x, a, y, w_glu, b_glu.reshape(1, ws), out_g.reshape(1, ws), w_out, w_out,
      post_g.reshape(1, d))


def kernel(x, ff1_pre_g, ff1_w_gate, ff1_w_up, ff1_w_down, ff1_post_g, mix_pre_g, w_in, lam_q1, lam_k1, lam_q2, lam_k2, attn_head_g, ssm_lam_re, ssm_lam_im, ssm_log_dt, ssm_b_re, ssm_b_im, ssm_c_re, ssm_c_im, ssm_d, ssm_w_glu, ssm_b_glu, ssm_out_g, w_out, mix_post_g, ff2_pre_g, ff2_w_gate, ff2_w_up, ff2_w_down, ff2_post_g):
    batch, seq, d_model = x.shape
    depth = w_in.shape[0]
    ssm_width = ssm_w_glu.shape[-1]
    slopes = jnp.asarray([2.0 ** (-8.0 * (i + 1) / ATTN_HEADS) for i in range(ATTN_HEADS)], F32)
    bf = lambda w: w.astype(BF16)

    xt = x.reshape(batch * seq, d_model)
    for l in range(depth):
        xt = _ffn(xt, ff1_pre_g[l], bf(ff1_w_gate[l]), bf(ff1_w_up[l]), bf(ff1_w_down[l]),
                  ff1_post_g[l])

        qkv, u = _in_proj(xt, mix_pre_g[l], bf(w_in[l]), ssm_width=ssm_width)
        lam_init = 0.8 - 0.6 * math.exp(-0.3 * l)
        a = _attention(qkv, slopes, lam_q1[l], lam_k1[l], lam_q2[l], lam_k2[l], attn_head_g[l],
                       batch=batch, seq=seq, lam_init=lam_init)

        w1, cm, al = _s5_prep(ssm_lam_re[l], ssm_lam_im[l], ssm_log_dt[l], ssm_b_re[l],
                              ssm_b_im[l], ssm_c_re[l], ssm_c_im[l])
        y = _s5_chunks(u.reshape(batch, seq, ssm_width), w1, cm, al, ssm_d[l])
        y = y.reshape(batch * seq, ssm_width)

        xt = _mix_out(xt, a, y, bf(ssm_w_glu[l]), ssm_b_glu[l], ssm_out_g[l], bf(w_out[l]),
                      mix_post_g[l])

        xt = _ffn(xt, ff2_pre_g[l], bf(ff2_w_gate[l]), bf(ff2_w_up[l]), bf(ff2_w_down[l]),
                  ff2_post_g[l])
    return xt.reshape(batch, seq, d_model)
```

```python
import functools
import math

import jax
import jax.numpy as jnp
from jax import lax
from jax.experimental import pallas as pl
from jax.experimental.pallas import tpu as pltpu

F32 = jnp.float32
BF16 = jnp.bfloat16

NORM_EPS = 1e-6
ATTN_HEADS = 8
ATTN_HEAD_DIM = 64
ATTN_VALUE_DIM = 2 * ATTN_HEAD_DIM
KEY_CHUNK = 512
SSM_GROUP = 16
SSM_STATE = 64
CHUNK = 16
GROUP_LANES = CHUNK * SSM_GROUP
SUBLANES = 8
LANES = 128
GROUPS_PER_BLOCK = LANES // SSM_GROUP

VMEM_LIMIT_BYTES = 56 * 1024 * 1024


def _rms_scale(x):
    return lax.rsqrt(jnp.mean(x * x, axis=-1, keepdims=True) + NORM_EPS)


def _sigmoid(x):
    return 1.0 / (1.0 + jnp.exp(-x))


def _ffn_kernel(x_ref, pre_g_ref, wg_ref, wu_ref, wd_ref, post_g_ref, o_ref, h_ref):
    j = pl.program_id(1)
    last = pl.num_programs(1) - 1

    @pl.when(j == 0)
    def _():
        x = x_ref[...]
        h_ref[...] = (x * _rms_scale(x) * pre_g_ref[...]).astype(BF16)
        o_ref[...] = jnp.zeros_like(o_ref)

    h = h_ref[...]
    gate = jnp.dot(h, wg_ref[...].astype(BF16), preferred_element_type=F32)
    up = jnp.dot(h, wu_ref[...].astype(BF16), preferred_element_type=F32)
    act = (gate * _sigmoid(gate) * up).astype(BF16)
    o_ref[...] += jnp.dot(act, wd_ref[...].astype(BF16), preferred_element_type=F32)

    @pl.when(j == last)
    def _():
        acc = o_ref[...]
        o_ref[...] = x_ref[...] + 0.5 * (acc * _rms_scale(acc) * post_g_ref[...])


def _ffn(x, pre_g, w_gate, w_up, w_down, post_g, *, tm=1024, tf=256):
    t, d = x.shape
    f = w_gate.shape[1]
    assert t % tm == 0 and f % tf == 0
    return pl.pallas_call(
        _ffn_kernel,
        grid=(t // tm, f // tf),
        in_specs=[
            pl.BlockSpec((tm, d), lambda i, j: (i, 0), pipeline_mode=pl.Buffered(1)),
            pl.BlockSpec((1, d), lambda i, j: (0, 0)),
            pl.BlockSpec((d, tf), lambda i, j: (0, j)),
            pl.BlockSpec((d, tf), lambda i, j: (0, j)),
            pl.BlockSpec((tf, d), lambda i, j: (j, 0)),
            pl.BlockSpec((1, d), lambda i, j: (0, 0)),
        ],
        out_specs=pl.BlockSpec((tm, d), lambda i, j: (i, 0)),
        out_shape=jax.ShapeDtypeStruct((t, d), F32),
        scratch_shapes=[pltpu.VMEM((tm, d), BF16)],
        compiler_params=pltpu.CompilerParams(
            dimension_semantics=("parallel", "arbitrary"),
            vmem_limit_bytes=VMEM_LIMIT_BYTES),
        name="ffn",
    )(x, pre_g.reshape(1, d), w_gate, w_up, w_down, post_g.reshape(1, d))


def _in_proj_kernel(x_ref, g_ref, w_ref, qkv_ref, u_ref, h_ref, *, n_qkv):
    j = pl.program_id(1)

    @pl.when(j == 0)
    def _():
        x = x_ref[...]
        h_ref[...] = (x * _rms_scale(x) * g_ref[...]).astype(BF16)

    r = jnp.dot(h_ref[...], w_ref[...], preferred_element_type=F32)

    @pl.when(j < n_qkv)
    def _():
        qkv_ref[...] = r.astype(BF16)

    @pl.when(j == n_qkv)
    def _():
        u_ref[...] = r


def _in_proj(x, g, w_in, *, ssm_width, tm=1024):
    t, d = x.shape
    n = w_in.shape[1]
    tn = ssm_width
    n_qkv = (n - ssm_width) // tn
    assert t % tm == 0 and n == (n_qkv + 1) * tn
    return pl.pallas_call(
        functools.partial(_in_proj_kernel, n_qkv=n_qkv),
        grid=(t // tm, n_qkv + 1),
        in_specs=[
            pl.BlockSpec((tm, d), lambda i, j: (i, 0)),
            pl.BlockSpec((1, d), lambda i, j: (0, 0)),
            pl.BlockSpec((d, tn), lambda i, j: (0, j)),
        ],
        out_specs=[
            pl.BlockSpec((tm, tn), lambda i, j: (i, jnp.minimum(j, n_qkv - 1))),
            pl.BlockSpec((tm, tn), lambda i, j: (i, 0)),
        ],
        out_shape=[
            jax.ShapeDtypeStruct((t, n - ssm_width), BF16),
            jax.ShapeDtypeStruct((t, ssm_width), F32),
        ],
        scratch_shapes=[pltpu.VMEM((tm, d), BF16)],
        compiler_params=pltpu.CompilerParams(
            dimension_semantics=("parallel", "arbitrary"),
            vmem_limit_bytes=VMEM_LIMIT_BYTES),
        name="in_proj",
    )(x, g.reshape(1, d), w_in)


def _attn_kernel(slopes_ref, lq1_ref, lk1_ref, lq2_ref, lk2_ref, hg_ref,
                 q_ref, k_ref, v_ref, o_ref, bias_ref, s_ref, p_ref, *, tq, lam_init):
    h = pl.program_id(0)
    qi = pl.program_id(1)
    b = pl.program_id(2)
    s_len = k_ref.shape[0]

    @pl.when(b == 0)
    def _():
        qpos = qi * tq + lax.broadcasted_iota(jnp.int32, (tq, s_len), 0)
        kpos = lax.broadcasted_iota(jnp.int32, (tq, s_len), 1)
        bias_ref[...] = slopes_ref[h] * jnp.abs(qpos - kpos).astype(F32)

    lam = (jnp.exp(jnp.sum(lq1_ref[...] * lk1_ref[...], axis=-1, keepdims=True))
           - jnp.exp(jnp.sum(lq2_ref[...] * lk2_ref[...], axis=-1, keepdims=True))
           + lam_init)

    q = q_ref[...] * jnp.asarray(ATTN_HEAD_DIM ** -0.5, BF16)
    k = k_ref[...]
    v = v_ref[...]
    v_ext = jnp.concatenate([v, jnp.ones_like(v)], axis=1)
    lane = lax.broadcasted_iota(jnp.int32, q.shape, 1)
    e = v.shape[1]
    for c in range(2):
        in_map = (lane >= c * ATTN_HEAD_DIM) & (lane < (c + 1) * ATTN_HEAD_DIM)
        qc = jnp.where(in_map, q, jnp.zeros_like(q))
        for j in range(0, s_len, KEY_CHUNK):
            cols = slice(j, j + KEY_CHUNK)
            s_ref[c, :, cols] = lax.dot_general(
                qc, k[cols], (((1,), (1,)), ((), ())),
                preferred_element_type=F32) - bias_ref[:, cols]
    for c in range(2):
        s = s_ref[c]
        p_ref[c] = jnp.exp(s - jnp.max(s, axis=-1, keepdims=True)).astype(BF16)
    outs = []
    for c in range(2):
        pv = jnp.dot(p_ref[c], v_ext, preferred_element_type=F32)
        outs.append(pv[:, :e] / pv[:, e:])
    o = outs[0] - lam * outs[1]
    o_ref[...] = (o * _rms_scale(o) * hg_ref[...] * (1.0 - lam_init)).astype(o_ref.dtype)


def _attention(qkv, slopes, lq1, lk1, lq2, lk2, head_g, *, batch, seq, lam_init, tq=512):
    t = qkv.shape[0]
    e = ATTN_VALUE_DIM
    nq = seq // tq
    nh = ATTN_HEADS
    vec = lambda a: a.reshape(1, -1).astype(F32)
    small = lambda n: pl.BlockSpec((1, n), lambda h, qi, b: (0, 0))
    return pl.pallas_call(
        functools.partial(_attn_kernel, tq=tq, lam_init=lam_init),
        grid=(nh, nq, batch),
        in_specs=[
            pl.BlockSpec(memory_space=pltpu.SMEM),
            small(ATTN_HEAD_DIM), small(ATTN_HEAD_DIM), small(ATTN_HEAD_DIM), small(ATTN_HEAD_DIM),
            small(e),
            pl.BlockSpec((tq, e), lambda h, qi, b: (b * nq + qi, h)),
            pl.BlockSpec((seq, e), lambda h, qi, b: (b, nh + h)),
            pl.BlockSpec((seq, e), lambda h, qi, b: (b, 2 * nh + h)),
        ],
        out_specs=pl.BlockSpec((tq, e), lambda h, qi, b: (b * nq + qi, h)),
        out_shape=jax.ShapeDtypeStruct((t, nh * e), BF16),
        scratch_shapes=[pltpu.VMEM((tq, seq), F32), pltpu.VMEM((2, tq, seq), F32),
                        pltpu.VMEM((2, tq, seq), BF16)],
        compiler_params=pltpu.CompilerParams(
            dimension_semantics=("parallel", "parallel", "arbitrary"),
            vmem_limit_bytes=VMEM_LIMIT_BYTES),
        name="diff_attention",
    )(slopes, vec(lq1), vec(lk1), vec(lq2), vec(lk2), vec(head_g), qkv, qkv, qkv)


def _discretize(lam_re, lam_im, log_dt):
    dt = jnp.exp(log_dt)
    mag = jnp.exp(lam_re * dt)
    a_re = mag * jnp.cos(lam_im * dt)
    a_im = mag * jnp.sin(lam_im * dt)
    den = lam_re * lam_re + lam_im * lam_im
    nr = a_re - 1.0
    f_re = (nr * lam_re + a_im * lam_im) / den
    f_im = (a_im * lam_re - nr * lam_im) / den
    return dt, a_re, a_im, f_re, f_im


def _power(lam_re, lam_im, dt, e):
    mag = jnp.exp(lam_re * dt * e)
    ang = lam_im * dt * e
    return mag * jnp.cos(ang), mag * jnp.sin(ang)


def _shift_lanes(x, s, left):
    if s == 0:
        return x
    n = x.shape[-1]
    lane = lax.broadcasted_iota(jnp.int32, x.shape, 1)
    if left:
        return jnp.where(lane < n - s, pltpu.roll(x, n - s, axis=1), 0.0)
    return jnp.where(lane >= s, pltpu.roll(x, s, axis=1), 0.0)


def _s5_prep_kernel(lre_c_ref, lim_c_ref, ldt_c_ref, bre_c_ref, bim_c_ref, ctre_ref, ctim_ref,
                    lre_r_ref, lim_r_ref, ldt_r_ref, btre_r_ref, btim_r_ref,
                    w1_ref, cm_ref, al_ref):
    n, p, l = SSM_STATE, SSM_GROUP, CHUNK
    lag = (lax.broadcasted_iota(jnp.int32, (1, GROUP_LANES), 1) // p).astype(F32)

    t_rows = None
    cm_re, cm_im = [], []
    for d in range(2):
        lr, li, ldt = lre_c_ref[d, 0], lim_c_ref[d, 0], ldt_c_ref[d, 0]
        dt, a_re, a_im, f_re, f_im = _discretize(lr, li, ldt)
        b_re, b_im = bre_c_ref[d, 0], bim_c_ref[d, 0]
        bb_re = f_re * b_re - f_im * b_im
        bb_im = f_re * b_im + f_im * b_re
        e0 = lag if d == 0 else (l - 1.0) - lag
        pw_re, pw_im = _power(lr, li, dt, e0)
        ct_re, ct_im = ctre_ref[d, 0], ctim_ref[d, 0]
        ac_re = pw_re * ct_re - pw_im * ct_im
        ac_im = pw_re * ct_im + pw_im * ct_re
        contract0 = (((0,), (0,)), ((), ()))
        taps = (lax.dot_general(bb_re, ac_re, contract0, precision=lax.Precision.HIGHEST,
                                preferred_element_type=F32)
                - lax.dot_general(bb_im, ac_im, contract0, precision=lax.Precision.HIGHEST,
                                  preferred_element_type=F32))
        rows = [_shift_lanes(taps, p * (tp if d == 0 else l - 1 - tp), left=(d == 1))
                for tp in range(l)]
        rows = jnp.concatenate(rows, axis=0)
        t_rows = rows if t_rows is None else t_rows + rows
        cm_re.append(a_re * ac_re - a_im * ac_im)
        cm_im.append(-(a_re * ac_im + a_im * ac_re))
    w1_ref[0, :, 0:GROUP_LANES] = t_rows.astype(BF16)
    cm_ref[0] = jnp.concatenate(cm_re + cm_im, axis=0).astype(BF16)

    lr, li, ldt = lre_r_ref[0], lim_r_ref[0], ldt_r_ref[0]
    dt, a_re, a_im, f_re, f_im = _discretize(lr, li, ldt)
    bt_re, bt_im = btre_r_ref[0], btim_r_ref[0]
    bb_re = f_re * bt_re - f_im * bt_im
    bb_im = f_re * bt_im + f_im * bt_re
    tau = lax.broadcasted_iota(jnp.int32, (l, 2 * n), 0)
    lane = lax.broadcasted_iota(jnp.int32, (l, 2 * n), 1)
    e = jnp.where(lane < n, l - 1 - tau, tau).astype(F32)
    pw_re, pw_im = _power(lr, li, dt, e)
    bm_re = pw_re[:, None, :] * bb_re[None, :, :] - pw_im[:, None, :] * bb_im[None, :, :]
    bm_im = pw_re[:, None, :] * bb_im[None, :, :] + pw_im[:, None, :] * bb_re[None, :, :]
    w1_ref[0, :, GROUP_LANES:GROUP_LANES + 2 * n] = bm_re.reshape(l * p, 2 * n).astype(BF16)
    w1_ref[0, :, GROUP_LANES + 2 * n:] = bm_im.reshape(l * p, 2 * n).astype(BF16)
    al_re, al_im = _power(lr, li, dt, jnp.full((1, 2 * n), float(l), F32))
    al_ref[0] = jnp.concatenate([al_re, al_im], axis=0)


def _s5_prep(lam_re, lam_im, log_dt, b_re, b_im, c_re, c_im):
    _, g, n = lam_re.shape
    p = b_re.shape[-1]
    col = lambda a: a.reshape(2, g, n, 1)
    ldt_c = jnp.broadcast_to(log_dt[:, :, None, None], (2, g, n, 1))
    ct = lambda c: jnp.tile(jnp.swapaxes(c, 2, 3), (1, 1, 1, CHUNK))
    row = lambda a: jnp.concatenate([a[0], a[1]], axis=-1)[:, None, :]
    ldt_r = row(jnp.broadcast_to(log_dt[:, :, None], (2, g, n)))
    bt = lambda b: jnp.concatenate([jnp.swapaxes(b[0], 1, 2), jnp.swapaxes(b[1], 1, 2)], axis=-1)

    spec_c = lambda last: pl.BlockSpec((2, 1, n, last), lambda i: (0, i, 0, 0))
    spec_r = lambda rows: pl.BlockSpec((1, rows, 2 * n), lambda i: (i, 0, 0))
    return pl.pallas_call(
        _s5_prep_kernel,
        grid=(g,),
        in_specs=[spec_c(1), spec_c(1), spec_c(1), spec_c(p), spec_c(p),
                  spec_c(GROUP_LANES), spec_c(GROUP_LANES),
                  spec_r(1), spec_r(1), spec_r(1), spec_r(p), spec_r(p)],
        out_specs=[
            pl.BlockSpec((1, GROUP_LANES, GROUP_LANES + 4 * n), lambda i: (i, 0, 0)),
            pl.BlockSpec((1, 4 * n, GROUP_LANES), lambda i: (i, 0, 0)),
            pl.BlockSpec((1, 2, 2 * n), lambda i: (i, 0, 0)),
        ],
        out_shape=[
            jax.ShapeDtypeStruct((g, GROUP_LANES, GROUP_LANES + 4 * n), BF16),
            jax.ShapeDtypeStruct((g, 4 * n, GROUP_LANES), BF16),
            jax.ShapeDtypeStruct((g, 2, 2 * n), F32),
        ],
        compiler_params=pltpu.CompilerParams(dimension_semantics=("parallel",)),
        name="s5_prep",
    )(col(lam_re), col(lam_im), ldt_c, b_re, b_im, ct(c_re), ct(c_im),
      row(lam_re), row(lam_im), ldt_r, bt(b_re), bt(b_im))


def _swap_pieces(v):
    piece = lax.broadcasted_iota(jnp.int32, v[0].shape, 1) // SSM_GROUP
    v = list(v)
    for d in (4, 2, 1):
        hi = (piece & d) != 0
        for a in range(8):
            if a & d:
                continue
            va, vb = v[a], v[a + d]
            v[a] = jnp.where(hi, pltpu.roll(vb, d * SSM_GROUP, axis=1), va)
            v[a + d] = jnp.where(hi, vb, pltpu.roll(va, LANES - d * SSM_GROUP, axis=1))
    return v


def _s5_chunk_kernel(u_ref, w1_ref, cm_ref, al_ref, d_ref, y_ref, x_ref, s_ref, hf_ref, h_ref,
                     *, nb, nc):
    gl = GROUP_LANES
    gb = GROUPS_PER_BLOCK
    n2 = 2 * SSM_STATE
    half = CHUNK // 2
    ctile = SUBLANES * CHUNK

    def token_rows(ct, h, j):
        return pl.ds(ct * ctile + h * half + j, SUBLANES, stride=CHUNK)

    def chunk_rows(ct, b):
        return pl.ds(ct * SUBLANES * nb + b, SUBLANES, stride=nb)

    def gather_body(ct, carry):
        for b in range(nb):
            for h in range(2):
                v = [u_ref[b, token_rows(ct, h, j), :] for j in range(half)]
                w = _swap_pieces(v)
                for g in range(gb):
                    x_ref[2 * g + h, chunk_rows(ct, b), :] = w[g]
        return carry

    lax.fori_loop(0, nc // SUBLANES, gather_body, 0)

    for g in range(gb):
        x = jnp.concatenate([x_ref[2 * g], x_ref[2 * g + 1]], axis=1)
        r = jnp.dot(x.astype(BF16), w1_ref[g], preferred_element_type=F32)
        x_ref[2 * g] = r[:, :LANES]
        x_ref[2 * g + 1] = r[:, LANES:gl]
        s_ref[:, g * gl:(g + 1) * gl] = r[:, gl:]

    a_re = [al_ref[g, 0:1, :] for g in range(gb)]
    a_im = [al_ref[g, 1:2, :] for g in range(gb)]
    is_fwd = lax.broadcasted_iota(jnp.int32, (SUBLANES, n2), 1) < SSM_STATE

    def step(h, s):
        out = []
        for g in range(gb):
            hr, hi = h[:, g * gl:g * gl + n2], h[:, g * gl + n2:(g + 1) * gl]
            sr, si = s[:, g * gl:g * gl + n2], s[:, g * gl + n2:(g + 1) * gl]
            out.append(a_re[g] * hr - a_im[g] * hi + sr)
            out.append(a_re[g] * hi + a_im[g] * hr + si)
        return jnp.concatenate(out, axis=1)

    cpt = SUBLANES // nb
    n_tiles = nc // cpt
    tile_rows = lambda t: pl.ds(pl.multiple_of(t * SUBLANES, SUBLANES), SUBLANES)

    def fwd_body(t, h):
        s = s_ref[tile_rows(t), :]
        before = []
        for k in range(cpt):
            before.append(h)
            h = step(h, s[k * nb:(k + 1) * nb])
        hf_ref[tile_rows(t), :] = jnp.concatenate(before, axis=0)
        return h

    lax.fori_loop(0, n_tiles, fwd_body, jnp.zeros((nb, gb * gl), F32))

    def bwd_body(i, h):
        t = n_tiles - 1 - i
        s = s_ref[tile_rows(t), :]
        before = [None] * cpt
        for k in reversed(range(cpt)):
            before[k] = h
            h = step(h, s[k * nb:(k + 1) * nb])
        hb = jnp.concatenate(before, axis=0)
        hf = hf_ref[tile_rows(t), :]
        merged = [jnp.where(is_fwd, hf[:, k * n2:(k + 1) * n2], hb[:, k * n2:(k + 1) * n2])
                  for k in range(2 * gb)]
        h_ref[tile_rows(t), :] = jnp.concatenate(merged, axis=1)
        return h

    lax.fori_loop(0, n_tiles, bwd_body, jnp.zeros((nb, gb * gl), F32))

    for g in range(gb):
        r = jnp.dot(h_ref[:, g * gl:(g + 1) * gl].astype(BF16), cm_ref[g],
                    preferred_element_type=F32)
        x_ref[2 * g] += r[:, :LANES]
        x_ref[2 * g + 1] += r[:, LANES:]

    def scatter_body(ct, carry):
        for b in range(nb):
            for h in range(2):
                w = [x_ref[2 * g + h, chunk_rows(ct, b), :] for g in range(gb)]
                v = _swap_pieces(w)
                for j in range(half):
                    rows = token_rows(ct, h, j)
                    y_ref[b, rows, :] = v[j] + d_ref[...] * u_ref[b, rows, :]
        return carry

    lax.fori_loop(0, nc // SUBLANES, scatter_body, 0)


def _s5_chunks(u, w1, cm, al, d):
    nb, seq, width = u.shape
    nc = seq // CHUNK
    rows = nb * nc
    gb = GROUPS_PER_BLOCK
    assert SUBLANES % nb == 0 and nc % SUBLANES == 0 and width % LANES == 0
    blk = pl.BlockSpec((nb, seq, LANES), lambda i: (0, 0, i))
    return pl.pallas_call(
        functools.partial(_s5_chunk_kernel, nb=nb, nc=nc),
        grid=(width // LANES,),
        in_specs=[
            blk,
            pl.BlockSpec((gb,) + w1.shape[1:], lambda i: (i, 0, 0)),
            pl.BlockSpec((gb,) + cm.shape[1:], lambda i: (i, 0, 0)),
            pl.BlockSpec((gb,) + al.shape[1:], lambda i: (i, 0, 0)),
            pl.BlockSpec((1, LANES), lambda i: (0, i)),
        ],
        out_specs=blk,
        out_shape=jax.ShapeDtypeStruct(u.shape, F32),
        scratch_shapes=[pltpu.VMEM((2 * gb, rows, LANES), F32)]
        + [pltpu.VMEM((rows, gb * GROUP_LANES), F32)] * 3,
        compiler_params=pltpu.CompilerParams(
            dimension_semantics=("parallel",), vmem_limit_bytes=VMEM_LIMIT_BYTES),
        name="s5_chunks",
    )(u, w1, cm, al, d.reshape(1, width))


def _mix_out_kernel(x_ref, a_ref, y_ref, wglu_ref, bglu_ref, og_ref, wo_a_ref, wo_s_ref, pg_ref,
                    o_ref):
    y = y_ref[...]
    g = y * (0.5 * (1.0 + jnp.tanh(math.sqrt(2.0 / math.pi) * (y + 0.044715 * (y * y * y)))))
    z = jnp.dot(g.astype(BF16), wglu_ref[...], preferred_element_type=F32) + bglu_ref[...]
    s = g * _sigmoid(z)
    s = s * _rms_scale(s) * og_ref[...]
    mixed = (jnp.dot(a_ref[...], wo_a_ref[...], preferred_element_type=F32)
             + jnp.dot(s.astype(BF16), wo_s_ref[...], preferred_element_type=F32))
    o_ref[...] = x_ref[...] + mixed * _rms_scale(mixed) * pg_ref[...]


def _mix_out(x, a, y, w_glu, b_glu, out_g, w_out, post_g, *, tm=512):
    t, d = x.shape
    wa = a.shape[1]
    ws = y.shape[1]
    const = lambda shape: pl.BlockSpec(shape, lambda i: (0, 0))
    return pl.pallas_call(
        _mix_out_kernel,
        grid=(t // tm,),
        in_specs=[
            pl.BlockSpec((tm, d), lambda i: (i, 0)),
            pl.BlockSpec((tm, wa), lambda i: (i, 0)),
            pl.BlockSpec((tm, ws), lambda i: (i, 0)),
            const((ws, ws)), const((1, ws)), const((1, ws)),
            pl.BlockSpec((wa, d), lambda i: (0, 0)),
            pl.BlockSpec((ws, d), lambda i: (wa // ws, 0)),
            const((1, d)),
        ],
        out_specs=pl.BlockSpec((tm, d), lambda i: (i, 0)),
        out_shape=jax.ShapeDtypeStruct((t, d), F32),
        compiler_params=pltpu.CompilerParams(
            dimension_semantics=("parallel",), vmem_limit_bytes=VMEM_LIMIT_BYTES),
        name="mix_out",
    )(x, a, y, w_glu, b_glu.reshape(1, ws), out_g.reshape(1, ws), w_out, w_out,
      post_g.reshape(1, d))


def kernel(x, ff1_pre_g, ff1_w_gate, ff1_w_up, ff1_w_down, ff1_post_g, mix_pre_g, w_in, lam_q1, lam_k1, lam_q2, lam_k2, attn_head_g, ssm_lam_re, ssm_lam_im, ssm_log_dt, ssm_b_re, ssm_b_im, ssm_c_re, ssm_c_im, ssm_d, ssm_w_glu, ssm_b_glu, ssm_out_g, w_out, mix_post_g, ff2_pre_g, ff2_w_gate, ff2_w_up, ff2_w_down, ff2_post_g):
    batch, seq, d_model = x.shape
    depth = w_in.shape[0]
    ssm_width = ssm_w_glu.shape[-1]
    slopes = jnp.asarray([2.0 ** (-8.0 * (i + 1) / ATTN_HEADS) for i in range(ATTN_HEADS)], F32)
    bf = lambda w: w.astype(BF16)

    xt = x.reshape(batch * seq, d_model)
    for l in range(depth):
        xt = _ffn(xt, ff1_pre_g[l], ff1_w_gate[l], ff1_w_up[l], ff1_w_down[l], ff1_post_g[l])

        qkv, u = _in_proj(xt, mix_pre_g[l], bf(w_in[l]), ssm_width=ssm_width)
        lam_init = 0.8 - 0.6 * math.exp(-0.3 * l)
        a = _attention(qkv, slopes, lam_q1[l], lam_k1[l], lam_q2[l], lam_k2[l], attn_head_g[l],
                       batch=batch, seq=seq, lam_init=lam_init)

        w1, cm, al = _s5_prep(ssm_lam_re[l], ssm_lam_im[l], ssm_log_dt[l], ssm_b_re[l],
                              ssm_b_im[l], ssm_c_re[l], ssm_c_im[l])
        y = _s5_chunks(u.reshape(batch, seq, ssm_width), w1, cm, al, ssm_d[l])
        y = y.reshape(batch * seq, ssm_width)

        xt = _mix_out(xt, a, y, bf(ssm_w_glu[l]), ssm_b_glu[l], ssm_out_g[l], bf(w_out[l]),
                      mix_post_g[l])

        xt = _ffn(xt, ff2_pre_g[l], ff2_w_gate[l], ff2_w_up[l], ff2_w_down[l], ff2_post_g[l])
    return xt.reshape(batch, seq, d_model)
```

```python
import functools
import math

import jax
import jax.numpy as jnp
from jax import lax
from jax.experimental import pallas as pl
from jax.experimental.pallas import tpu as pltpu

F32 = jnp.float32
BF16 = jnp.bfloat16

NORM_EPS = 1e-6
ATTN_HEADS = 8
ATTN_HEAD_DIM = 64
ATTN_VALUE_DIM = 2 * ATTN_HEAD_DIM
KEY_CHUNK = 512
SSM_GROUP = 16
SSM_STATE = 64
CHUNK = 16
GROUP_LANES = CHUNK * SSM_GROUP
SUBLANES = 8
LANES = 128
GROUPS_PER_BLOCK = LANES // SSM_GROUP
PREP_GROUPS_PER_STEP = 4

VMEM_LIMIT_BYTES = 56 * 1024 * 1024


def _rms_scale(x):
    return lax.rsqrt(jnp.mean(x * x, axis=-1, keepdims=True) + NORM_EPS)


def _sigmoid(x):
    return 1.0 / (1.0 + jnp.exp(-x))


def _ffn_kernel(x_ref, pre_g_ref, wg_ref, wu_ref, wd_ref, post_g_ref, o_ref, h_ref):
    j = pl.program_id(1)
    last = pl.num_programs(1) - 1

    @pl.when(j == 0)
    def _():
        x = x_ref[...]
        h_ref[...] = (x * _rms_scale(x) * pre_g_ref[...]).astype(BF16)
        o_ref[...] = jnp.zeros_like(o_ref)

    h = h_ref[...]
    gate = jnp.dot(h, wg_ref[...].astype(BF16), preferred_element_type=F32)
    up = jnp.dot(h, wu_ref[...].astype(BF16), preferred_element_type=F32)
    act = (gate * _sigmoid(gate) * up).astype(BF16)
    o_ref[...] += jnp.dot(act, wd_ref[...].astype(BF16), preferred_element_type=F32)

    @pl.when(j == last)
    def _():
        acc = o_ref[...]
        o_ref[...] = x_ref[...] + 0.5 * (acc * _rms_scale(acc) * post_g_ref[...])


def _ffn(x, pre_g, w_gate, w_up, w_down, post_g, *, tm=1024, tf=256):
    t, d = x.shape
    f = w_gate.shape[1]
    assert t % tm == 0 and f % tf == 0
    return pl.pallas_call(
        _ffn_kernel,
        grid=(t // tm, f // tf),
        in_specs=[
            pl.BlockSpec((tm, d), lambda i, j: (i, 0), pipeline_mode=pl.Buffered(1)),
            pl.BlockSpec((1, d), lambda i, j: (0, 0)),
            pl.BlockSpec((d, tf), lambda i, j: (0, j)),
            pl.BlockSpec((d, tf), lambda i, j: (0, j)),
            pl.BlockSpec((tf, d), lambda i, j: (j, 0)),
            pl.BlockSpec((1, d), lambda i, j: (0, 0)),
        ],
        out_specs=pl.BlockSpec((tm, d), lambda i, j: (i, 0)),
        out_shape=jax.ShapeDtypeStruct((t, d), F32),
        scratch_shapes=[pltpu.VMEM((tm, d), BF16)],
        compiler_params=pltpu.CompilerParams(
            dimension_semantics=("parallel", "arbitrary"),
            vmem_limit_bytes=VMEM_LIMIT_BYTES),
        name="ffn",
    )(x, pre_g.reshape(1, d), w_gate, w_up, w_down, post_g.reshape(1, d))


def _in_proj_kernel(x_ref, g_ref, w_ref, qkv_ref, u_ref, h_ref, *, n_qkv):
    j = pl.program_id(1)

    @pl.when(j == 0)
    def _():
        x = x_ref[...]
        h_ref[...] = (x * _rms_scale(x) * g_ref[...]).astype(BF16)

    def project():
        return jnp.dot(h_ref[...], w_ref[...].astype(BF16), preferred_element_type=F32)

    @pl.when(j < n_qkv)
    def _():
        qkv_ref[...] = project().astype(BF16)

    @pl.when(j == n_qkv)
    def _():
        u_ref[...] = project()


def _in_proj(x, g, w_in, *, ssm_width, tm=1024):
    t, d = x.shape
    n = w_in.shape[1]
    tn = ssm_width
    n_qkv = (n - ssm_width) // tn
    assert t % tm == 0 and n == (n_qkv + 1) * tn
    return pl.pallas_call(
        functools.partial(_in_proj_kernel, n_qkv=n_qkv),
        grid=(t // tm, n_qkv + 1),
        in_specs=[
            pl.BlockSpec((tm, d), lambda i, j: (i, 0), pipeline_mode=pl.Buffered(1)),
            pl.BlockSpec((1, d), lambda i, j: (0, 0)),
            pl.BlockSpec((d, tn), lambda i, j: (0, j)),
        ],
        out_specs=[
            pl.BlockSpec((tm, tn), lambda i, j: (i, jnp.minimum(j, n_qkv - 1))),
            pl.BlockSpec((tm, tn), lambda i, j: (i, 0)),
        ],
        out_shape=[
            jax.ShapeDtypeStruct((t, n - ssm_width), BF16),
            jax.ShapeDtypeStruct((t, ssm_width), F32),
        ],
        scratch_shapes=[pltpu.VMEM((tm, d), BF16)],
        compiler_params=pltpu.CompilerParams(
            dimension_semantics=("parallel", "arbitrary"),
            vmem_limit_bytes=VMEM_LIMIT_BYTES),
        name="in_proj",
    )(x, g.reshape(1, d), w_in)


def _attn_kernel(slopes_ref, lq1_ref, lk1_ref, lq2_ref, lk2_ref, hg_ref,
                 q_ref, k_ref, v_ref, o_ref, bias_ref, s_ref, p_ref, *, tq, lam_init):
    h = pl.program_id(0)
    qi = pl.program_id(1)
    b = pl.program_id(2)
    s_len = k_ref.shape[0]

    @pl.when(b == 0)
    def _():
        qpos = qi * tq + lax.broadcasted_iota(jnp.int32, (tq, s_len), 0)
        kpos = lax.broadcasted_iota(jnp.int32, (tq, s_len), 1)
        bias_ref[...] = slopes_ref[h] * jnp.abs(qpos - kpos).astype(F32)

    lam = (jnp.exp(jnp.sum(lq1_ref[...] * lk1_ref[...], axis=-1, keepdims=True))
           - jnp.exp(jnp.sum(lq2_ref[...] * lk2_ref[...], axis=-1, keepdims=True))
           + lam_init)

    q = q_ref[...] * jnp.asarray(ATTN_HEAD_DIM ** -0.5, BF16)
    k = k_ref[...]
    v = v_ref[...]
    v_ext = jnp.concatenate([v, jnp.ones_like(v)], axis=1)
    lane = lax.broadcasted_iota(jnp.int32, q.shape, 1)
    e = v.shape[1]
    for c in range(2):
        in_map = (lane >= c * ATTN_HEAD_DIM) & (lane < (c + 1) * ATTN_HEAD_DIM)
        qc = jnp.where(in_map, q, jnp.zeros_like(q))
        for j in range(0, s_len, KEY_CHUNK):
            cols = slice(j, j + KEY_CHUNK)
            s_ref[c, :, cols] = lax.dot_general(
                qc, k[cols], (((1,), (1,)), ((), ())),
                preferred_element_type=F32) - bias_ref[:, cols]
    for c in range(2):
        s = s_ref[c]
        p_ref[c] = jnp.exp(s - jnp.max(s, axis=-1, keepdims=True)).astype(BF16)
    outs = []
    for c in range(2):
        pv = jnp.dot(p_ref[c], v_ext, preferred_element_type=F32)
        outs.append(pv[:, :e] / pv[:, e:])
    o = outs[0] - lam * outs[1]
    o_ref[...] = (o * _rms_scale(o) * hg_ref[...] * (1.0 - lam_init)).astype(o_ref.dtype)


def _attention(qkv, slopes, lq1, lk1, lq2, lk2, head_g, *, batch, seq, lam_init, tq=512):
    t = qkv.shape[0]
    e = ATTN_VALUE_DIM
    nq = seq // tq
    nh = ATTN_HEADS
    vec = lambda a: a.reshape(1, -1).astype(F32)
    small = lambda n: pl.BlockSpec((1, n), lambda h, qi, b: (0, 0))
    return pl.pallas_call(
        functools.partial(_attn_kernel, tq=tq, lam_init=lam_init),
        grid=(nh, nq, batch),
        in_specs=[
            pl.BlockSpec(memory_space=pltpu.SMEM),
            small(ATTN_HEAD_DIM), small(ATTN_HEAD_DIM), small(ATTN_HEAD_DIM), small(ATTN_HEAD_DIM),
            small(e),
            pl.BlockSpec((tq, e), lambda h, qi, b: (b * nq + qi, h)),
            pl.BlockSpec((seq, e), lambda h, qi, b: (b, nh + h)),
            pl.BlockSpec((seq, e), lambda h, qi, b: (b, 2 * nh + h)),
        ],
        out_specs=pl.BlockSpec((tq, e), lambda h, qi, b: (b * nq + qi, h)),
        out_shape=jax.ShapeDtypeStruct((t, nh * e), BF16),
        scratch_shapes=[pltpu.VMEM((tq, seq), F32), pltpu.VMEM((2, tq, seq), F32),
                        pltpu.VMEM((2, tq, seq), BF16)],
        compiler_params=pltpu.CompilerParams(
            dimension_semantics=("parallel", "parallel", "arbitrary"),
            vmem_limit_bytes=VMEM_LIMIT_BYTES),
        name="diff_attention",
    )(slopes, vec(lq1), vec(lk1), vec(lq2), vec(lk2), vec(head_g), qkv, qkv, qkv)


def _discretize(lam_re, lam_im, log_dt):
    dt = jnp.exp(log_dt)
    mag = jnp.exp(lam_re * dt)
    a_re = mag * jnp.cos(lam_im * dt)
    a_im = mag * jnp.sin(lam_im * dt)
    den = lam_re * lam_re + lam_im * lam_im
    nr = a_re - 1.0
    f_re = (nr * lam_re + a_im * lam_im) / den
    f_im = (a_im * lam_re - nr * lam_im) / den
    return a_re, a_im, f_re, f_im


def _cmul(ar, ai, br, bi):
    return ar * br - ai * bi, ar * bi + ai * br


def _split_bf16(x, parts):
    out = []
    for k in range(parts):
        piece = x.astype(BF16)
        out.append(piece)
        if k + 1 < parts:
            x = x - piece.astype(F32)
    return out


def _copy_dot(x, onehot, dims):
    return sum(lax.dot_general(piece, onehot, dims, preferred_element_type=F32)
               for piece in _split_bf16(x, 3))


def _dot_3pass(a, b, dims):
    a_hi, a_lo = _split_bf16(a, 2)
    b_hi, b_lo = _split_bf16(b, 2)
    dot = lambda x, y: lax.dot_general(x, y, dims, preferred_element_type=F32)
    return dot(a_hi, b_hi) + (dot(a_hi, b_lo) + dot(a_lo, b_hi))


def _int_power(a_re, a_im, e, nbits):
    shape = jnp.broadcast_shapes(a_re.shape, e.shape)
    p_re, p_im = jnp.ones(shape, F32), jnp.zeros(shape, F32)
    s_re, s_im = a_re, a_im
    for k in range(nbits):
        bit = (e & (1 << k)) != 0
        m_re, m_im = _cmul(p_re, p_im, s_re, s_im)
        p_re, p_im = jnp.where(bit, m_re, p_re), jnp.where(bit, m_im, p_im)
        if k + 1 < nbits:
            s_re, s_im = _cmul(s_re, s_im, s_re, s_im)
    return p_re, p_im


def _shift_lanes(x, s, left):
    if s == 0:
        return x
    n = x.shape[-1]
    lane = lax.broadcasted_iota(jnp.int32, x.shape, 1)
    if left:
        return jnp.where(lane < n - s, pltpu.roll(x, n - s, axis=1), 0.0)
    return jnp.where(lane >= s, pltpu.roll(x, s, axis=1), 0.0)


def _s5_prep_kernel(lre_ref, lim_ref, ldt_ref, bre_c_ref, bim_c_ref, ctre_ref, ctim_ref,
                    btre_r_ref, btim_r_ref, w1_ref, cm_ref, al_ref):
    for i in range(w1_ref.shape[0]):
        _s5_prep_group(i, lre_ref, lim_ref, ldt_ref, bre_c_ref, bim_c_ref, ctre_ref, ctim_ref,
                       btre_r_ref, btim_r_ref, w1_ref, cm_ref, al_ref)


def _s5_prep_group(i, lre_ref, lim_ref, ldt_ref, bre_c_ref, bim_c_ref, ctre_ref, ctim_ref,
                   btre_r_ref, btim_r_ref, w1_ref, cm_ref, al_ref):
    n, p, l = SSM_STATE, SSM_GROUP, CHUNK
    nbits = l.bit_length() - 1
    contract0 = (((0,), (0,)), ((), ()))
    contract_mm = (((1,), (0,)), ((), ()))

    a_re, a_im, f_re, f_im = _discretize(lre_ref[i], lim_ref[i], ldt_ref[i])
    bt_re, bt_im = btre_r_ref[i], btim_r_ref[i]
    bb_re, bb_im = _cmul(f_re, f_im, bt_re, bt_im)
    tau = lax.broadcasted_iota(jnp.int32, (l, 2 * n), 0)
    lane = lax.broadcasted_iota(jnp.int32, (l, 2 * n), 1)
    pw_re, pw_im = _int_power(a_re, a_im, jnp.where(lane < n, l - 1 - tau, tau), nbits)
    bm_re = pw_re[:, None, :] * bb_re[None, :, :] - pw_im[:, None, :] * bb_im[None, :, :]
    bm_im = pw_re[:, None, :] * bb_im[None, :, :] + pw_im[:, None, :] * bb_re[None, :, :]
    w1_ref[i, :, GROUP_LANES:GROUP_LANES + 2 * n] = bm_re.reshape(l * p, 2 * n).astype(BF16)
    w1_ref[i, :, GROUP_LANES + 2 * n:] = bm_im.reshape(l * p, 2 * n).astype(BF16)
    al_re, al_im = a_re, a_im
    for _ in range(nbits):
        al_re, al_im = _cmul(al_re, al_im, al_re, al_im)
    al_ref[i] = jnp.concatenate([al_re, al_im], axis=0)

    quant = jnp.concatenate([a_re, a_im, f_re, f_im, jnp.zeros((SUBLANES - 4, 2 * n), F32)], axis=0)
    quant_t = quant.T

    lanes_p = lax.broadcasted_iota(jnp.int32, (p, GROUP_LANES), 1)
    rows_p = lax.broadcasted_iota(jnp.int32, (p, GROUP_LANES), 0)
    tile_p = (lanes_p % p == rows_p).astype(BF16)
    tile_lag = ((l - 1) - lanes_p // p == rows_p).astype(BF16)
    pwt_re = _copy_dot(pw_re, tile_lag, contract0)
    pwt_im = _copy_dot(pw_im, tile_lag, contract0)
    t_rows = None
    cm_re, cm_im = [], []
    for d in range(2):
        states = slice(d * n, (d + 1) * n)
        col = lambda k: quant_t[states, k:k + 1]
        a_re_c, a_im_c, f_re_c, f_im_c = col(0), col(1), col(2), col(3)
        bb_re, bb_im = _cmul(f_re_c, f_im_c, bre_c_ref[d, i], bim_c_ref[d, i])
        ct_re = _copy_dot(ctre_ref[d, i], tile_p, contract_mm)
        ct_im = _copy_dot(ctim_ref[d, i], tile_p, contract_mm)
        ac_re, ac_im = _cmul(pwt_re[states], pwt_im[states], ct_re, ct_im)
        taps = (_dot_3pass(bb_re, ac_re, contract0)
                - _dot_3pass(bb_im, ac_im, contract0))
        rows = [_shift_lanes(taps, p * (tp if d == 0 else l - 1 - tp), left=(d == 1))
                for tp in range(l)]
        rows = jnp.concatenate(rows, axis=0)
        t_rows = rows if t_rows is None else t_rows + rows
        r_re, r_im = _cmul(a_re_c, a_im_c, ac_re, ac_im)
        cm_re.append(r_re)
        cm_im.append(-r_im)
    w1_ref[i, :, 0:GROUP_LANES] = t_rows.astype(BF16)
    cm_ref[i] = jnp.concatenate(cm_re + cm_im, axis=0).astype(BF16)


def _s5_prep(lam_re, lam_im, log_dt, b_re, b_im, c_re, c_im):
    _, g, n = lam_re.shape
    p = b_re.shape[-1]
    row = lambda a: jnp.concatenate([a[0], a[1]], axis=-1)[:, None, :]
    ldt_r = row(jnp.broadcast_to(log_dt[:, :, None], (2, g, n)))
    bt = lambda b: jnp.concatenate([jnp.swapaxes(b[0], 1, 2), jnp.swapaxes(b[1], 1, 2)], axis=-1)
    ct = lambda c: jnp.swapaxes(c, 2, 3)

    gb = PREP_GROUPS_PER_STEP
    assert g % gb == 0
    spec_c = pl.BlockSpec((2, gb, n, p), lambda i: (0, i, 0, 0))
    spec_r = lambda rows: pl.BlockSpec((gb, rows, 2 * n), lambda i: (i, 0, 0))
    return pl.pallas_call(
        _s5_prep_kernel,
        grid=(g // gb,),
        in_specs=[spec_r(1), spec_r(1), spec_r(1), spec_c, spec_c, spec_c, spec_c,
                  spec_r(p), spec_r(p)],
        out_specs=[
            pl.BlockSpec((gb, GROUP_LANES, GROUP_LANES + 4 * n), lambda i: (i, 0, 0)),
            pl.BlockSpec((gb, 4 * n, GROUP_LANES), lambda i: (i, 0, 0)),
            pl.BlockSpec((gb, 2, 2 * n), lambda i: (i, 0, 0)),
        ],
        out_shape=[
            jax.ShapeDtypeStruct((g, GROUP_LANES, GROUP_LANES + 4 * n), BF16),
            jax.ShapeDtypeStruct((g, 4 * n, GROUP_LANES), BF16),
            jax.ShapeDtypeStruct((g, 2, 2 * n), F32),
        ],
        compiler_params=pltpu.CompilerParams(dimension_semantics=("parallel",)),
        name="s5_prep",
    )(row(lam_re), row(lam_im), ldt_r, b_re, b_im, ct(c_re), ct(c_im), bt(b_re), bt(b_im))


def _swap_pieces(v):
    piece = lax.broadcasted_iota(jnp.int32, v[0].shape, 1) // SSM_GROUP
    v = list(v)
    for d in (4, 2, 1):
        hi = (piece & d) != 0
        for a in range(8):
            if a & d:
                continue
            va, vb = v[a], v[a + d]
            v[a] = jnp.where(hi, pltpu.roll(vb, d * SSM_GROUP, axis=1), va)
            v[a + d] = jnp.where(hi, vb, pltpu.roll(va, LANES - d * SSM_GROUP, axis=1))
    return v


def _s5_chunk_kernel(u_ref, w1_ref, cm_ref, al_ref, d_ref, y_ref, x_ref, s_ref, hf_ref, h_ref,
                     *, nb, nc):
    gl = GROUP_LANES
    gb = GROUPS_PER_BLOCK
    n2 = 2 * SSM_STATE
    half = CHUNK // 2
    ctile = SUBLANES * CHUNK

    def token_rows(ct, h, j):
        return pl.ds(ct * ctile + h * half + j, SUBLANES, stride=CHUNK)

    def chunk_rows(ct, b):
        return pl.ds(ct * SUBLANES * nb + b, SUBLANES, stride=nb)

    def gather_body(ct, carry):
        for b in range(nb):
            for h in range(2):
                v = [u_ref[b, token_rows(ct, h, j), :] for j in range(half)]
                w = _swap_pieces(v)
                for g in range(gb):
                    x_ref[2 * g + h, chunk_rows(ct, b), :] = w[g]
        return carry

    lax.fori_loop(0, nc // SUBLANES, gather_body, 0)

    for g in range(gb):
        x = jnp.concatenate([x_ref[2 * g], x_ref[2 * g + 1]], axis=1)
        r = jnp.dot(x.astype(BF16), w1_ref[g], preferred_element_type=F32)
        x_ref[2 * g] = r[:, :LANES]
        x_ref[2 * g + 1] = r[:, LANES:gl]
        s_ref[:, g * gl:(g + 1) * gl] = r[:, gl:]

    a_re = [al_ref[g, 0:1, :] for g in range(gb)]
    a_im = [al_ref[g, 1:2, :] for g in range(gb)]
    is_fwd = lax.broadcasted_iota(jnp.int32, (SUBLANES, n2), 1) < SSM_STATE

    def step(h, s):
        out = []
        for g in range(gb):
            hr, hi = h[:, g * gl:g * gl + n2], h[:, g * gl + n2:(g + 1) * gl]
            sr, si = s[:, g * gl:g * gl + n2], s[:, g * gl + n2:(g + 1) * gl]
            out.append(a_re[g] * hr - a_im[g] * hi + sr)
            out.append(a_re[g] * hi + a_im[g] * hr + si)
        return jnp.concatenate(out, axis=1)

    cpt = SUBLANES // nb
    n_tiles = nc // cpt
    tile_rows = lambda t: pl.ds(pl.multiple_of(t * SUBLANES, SUBLANES), SUBLANES)

    def fwd_body(t, h):
        s = s_ref[tile_rows(t), :]
        before = []
        for k in range(cpt):
            before.append(h)
            h = step(h, s[k * nb:(k + 1) * nb])
        hf_ref[tile_rows(t), :] = jnp.concatenate(before, axis=0)
        return h

    lax.fori_loop(0, n_tiles, fwd_body, jnp.zeros((nb, gb * gl), F32))

    def bwd_body(i, h):
        t = n_tiles - 1 - i
        s = s_ref[tile_rows(t), :]
        before = [None] * cpt
        for k in reversed(range(cpt)):
            before[k] = h
            h = step(h, s[k * nb:(k + 1) * nb])
        hb = jnp.concatenate(before, axis=0)
        hf = hf_ref[tile_rows(t), :]
        merged = [jnp.where(is_fwd, hf[:, k * n2:(k + 1) * n2], hb[:, k * n2:(k + 1) * n2])
                  for k in range(2 * gb)]
        h_ref[tile_rows(t), :] = jnp.concatenate(merged, axis=1)
        return h

    lax.fori_loop(0, n_tiles, bwd_body, jnp.zeros((nb, gb * gl), F32))

    for g in range(gb):
        r = jnp.dot(h_ref[:, g * gl:(g + 1) * gl].astype(BF16), cm_ref[g],
                    preferred_element_type=F32)
        x_ref[2 * g] += r[:, :LANES]
        x_ref[2 * g + 1] += r[:, LANES:]

    def scatter_body(ct, carry):
        for b in range(nb):
            for h in range(2):
                w = [x_ref[2 * g + h, chunk_rows(ct, b), :] for g in range(gb)]
                v = _swap_pieces(w)
                for j in range(half):
                    rows = token_rows(ct, h, j)
                    y_ref[b, rows, :] = v[j] + d_ref[...] * u_ref[b, rows, :]
        return carry

    lax.fori_loop(0, nc // SUBLANES, scatter_body, 0)


def _s5_chunks(u, w1, cm, al, d):
    nb, seq, width = u.shape
    nc = seq // CHUNK
    rows = nb * nc
    gb = GROUPS_PER_BLOCK
    assert SUBLANES % nb == 0 and nc % SUBLANES == 0 and width % LANES == 0
    blk = pl.BlockSpec((nb, seq, LANES), lambda i: (0, 0, i))
    return pl.pallas_call(
        functools.partial(_s5_chunk_kernel, nb=nb, nc=nc),
        grid=(width // LANES,),
        in_specs=[
            blk,
            pl.BlockSpec((gb,) + w1.shape[1:], lambda i: (i, 0, 0)),
            pl.BlockSpec((gb,) + cm.shape[1:], lambda i: (i, 0, 0)),
            pl.BlockSpec((gb,) + al.shape[1:], lambda i: (i, 0, 0)),
            pl.BlockSpec((1, LANES), lambda i: (0, i)),
        ],
        out_specs=blk,
        out_shape=jax.ShapeDtypeStruct(u.shape, F32),
        scratch_shapes=[pltpu.VMEM((2 * gb, rows, LANES), F32)]
        + [pltpu.VMEM((rows, gb * GROUP_LANES), F32)] * 3,
        compiler_params=pltpu.CompilerParams(
            dimension_semantics=("parallel",), vmem_limit_bytes=VMEM_LIMIT_BYTES),
        name="s5_chunks",
    )(u, w1, cm, al, d.reshape(1, width))


def _mix_out_kernel(x_ref, a_ref, y_ref, wglu_ref, bglu_ref, og_ref, wo_a_ref, wo_s_ref, pg_ref,
                    o_ref):
    y = y_ref[...]
    g = y * (0.5 * (1.0 + jnp.tanh(math.sqrt(2.0 / math.pi) * (y + 0.044715 * (y * y * y)))))
    z = jnp.dot(g.astype(BF16), wglu_ref[...], preferred_element_type=F32) + bglu_ref[...]
    s = g * _sigmoid(z)
    s = s * _rms_scale(s) * og_ref[...]
    mixed = (jnp.dot(a_ref[...], wo_a_ref[...], preferred_element_type=F32)
             + jnp.dot(s.astype(BF16), wo_s_ref[...], preferred_element_type=F32))
    o_ref[...] = x_ref[...] + mixed * _rms_scale(mixed) * pg_ref[...]


def _mix_out(x, a, y, w_glu, b_glu, out_g, w_out, post_g, *, tm=512):
    t, d = x.shape
    wa = a.shape[1]
    ws = y.shape[1]
    const = lambda shape: pl.BlockSpec(shape, lambda i: (0, 0))
    return pl.pallas_call(
        _mix_out_kernel,
        grid=(t // tm,),
        in_specs=[
            pl.BlockSpec((tm, d), lambda i: (i, 0)),
            pl.BlockSpec((tm, wa), lambda i: (i, 0)),
            pl.BlockSpec((tm, ws), lambda i: (i, 0)),
            const((ws, ws)), const((1, ws)), const((1, ws)),
            pl.BlockSpec((wa, d), lambda i: (0, 0)),
            pl.BlockSpec((ws, d), lambda i: (wa // ws, 0)),
            const((1, d)),
        ],
        out_specs=pl.BlockSpec((tm, d), lambda i: (i, 0)),
        out_shape=jax.ShapeDtypeStruct((t, d), F32),
        compiler_params=pltpu.CompilerParams(
            dimension_semantics=("parallel",), vmem_limit_bytes=VMEM_LIMIT_BYTES),
        name="mix_out",
    )(x, a, y, w_glu, b_glu.reshape(1, ws), out_g.reshape(1, ws), w_out, w_out,
      post_g.reshape(1, d))


def kernel(x, ff1_pre_g, ff1_w_gate, ff1_w_up, ff1_w_down, ff1_post_g, mix_pre_g, w_in, lam_q1, lam_k1, lam_q2, lam_k2, attn_head_g, ssm_lam_re, ssm_lam_im, ssm_log_dt, ssm_b_re, ssm_b_im, ssm_c_re, ssm_c_im, ssm_d, ssm_w_glu, ssm_b_glu, ssm_out_g, w_out, mix_post_g, ff2_pre_g, ff2_w_gate, ff2_w_up, ff2_w_down, ff2_post_g):
    batch, seq, d_model = x.shape
    depth = w_in.shape[0]
    ssm_width = ssm_w_glu.shape[-1]
    slopes = jnp.asarray([2.0 ** (-8.0 * (i + 1) / ATTN_HEADS) for i in range(ATTN_HEADS)], F32)
    bf = lambda w: w.astype(BF16)

    xt = x.reshape(batch * seq, d_model)
    for l in range(depth):
        xt = _ffn(xt, ff1_pre_g[l], ff1_w_gate[l], ff1_w_up[l], ff1_w_down[l], ff1_post_g[l])

        qkv, u = _in_proj(xt, mix_pre_g[l], w_in[l], ssm_width=ssm_width)
        lam_init = 0.8 - 0.6 * math.exp(-0.3 * l)
        a = _attention(qkv, slopes, lam_q1[l], lam_k1[l], lam_q2[l], lam_k2[l], attn_head_g[l],
                       batch=batch, seq=seq, lam_init=lam_init)

        w1, cm, al = _s5_prep(ssm_lam_re[l], ssm_lam_im[l], ssm_log_dt[l], ssm_b_re[l],
                              ssm_b_im[l], ssm_c_re[l], ssm_c_im[l])
        y = _s5_chunks(u.reshape(batch, seq, ssm_width), w1, cm, al, ssm_d[l])
        y = y.reshape(batch * seq, ssm_width)

        xt = _mix_out(xt, a, y, bf(ssm_w_glu[l]), ssm_b_glu[l], ssm_out_g[l], bf(w_out[l]),
                      mix_post_g[l])

        xt = _ffn(xt, ff2_pre_g[l], ff2_w_gate[l], ff2_w_up[l], ff2_w_down[l], ff2_post_g[l])
    return xt.reshape(batch, seq, d_model)
```

```python
import functools
import math

import jax
import jax.numpy as jnp
from jax import lax
from jax.experimental import pallas as pl
from jax.experimental.pallas import tpu as pltpu

F32 = jnp.float32
BF16 = jnp.bfloat16

NORM_EPS = 1e-6
ATTN_HEADS = 8
ATTN_HEAD_DIM = 64
ATTN_VALUE_DIM = 2 * ATTN_HEAD_DIM
KEY_CHUNK = 512
SSM_GROUP = 16
SSM_STATE = 64
CHUNK = 16
GROUP_LANES = CHUNK * SSM_GROUP
SUBLANES = 8
LANES = 128
GROUPS_PER_BLOCK = LANES // SSM_GROUP
PREP_GROUPS_PER_STEP = 4

VMEM_LIMIT_BYTES = 56 * 1024 * 1024


def _rms_scale(x):
    return lax.rsqrt(jnp.mean(x * x, axis=-1, keepdims=True) + NORM_EPS)


def _sigmoid(x):
    return 1.0 / (1.0 + jnp.exp(-x))


def _ffn_kernel(x_ref, pre_g_ref, wg_ref, wu_ref, wd_ref, post_g_ref, o_ref, h_ref):
    j = pl.program_id(1)
    last = pl.num_programs(1) - 1

    @pl.when(j == 0)
    def _():
        x = x_ref[...]
        h_ref[...] = (x * _rms_scale(x) * pre_g_ref[...]).astype(BF16)
        o_ref[...] = jnp.zeros_like(o_ref)

    h = h_ref[...]
    gate = jnp.dot(h, wg_ref[...].astype(BF16), preferred_element_type=F32)
    up = jnp.dot(h, wu_ref[...].astype(BF16), preferred_element_type=F32)
    act = (gate * _sigmoid(gate) * up).astype(BF16)
    o_ref[...] += jnp.dot(act, wd_ref[...].astype(BF16), preferred_element_type=F32)

    @pl.when(j == last)
    def _():
        acc = o_ref[...]
        o_ref[...] = x_ref[...] + 0.5 * (acc * _rms_scale(acc) * post_g_ref[...])


def _ffn(x, pre_g, w_gate, w_up, w_down, post_g, *, tm=1024, tf=256):
    t, d = x.shape
    f = w_gate.shape[1]
    assert t % tm == 0 and f % tf == 0
    return pl.pallas_call(
        _ffn_kernel,
        grid=(t // tm, f // tf),
        in_specs=[
            pl.BlockSpec((tm, d), lambda i, j: (i, 0), pipeline_mode=pl.Buffered(1)),
            pl.BlockSpec((1, d), lambda i, j: (0, 0)),
            pl.BlockSpec((d, tf), lambda i, j: (0, j)),
            pl.BlockSpec((d, tf), lambda i, j: (0, j)),
            pl.BlockSpec((tf, d), lambda i, j: (j, 0)),
            pl.BlockSpec((1, d), lambda i, j: (0, 0)),
        ],
        out_specs=pl.BlockSpec((tm, d), lambda i, j: (i, 0)),
        out_shape=jax.ShapeDtypeStruct((t, d), F32),
        scratch_shapes=[pltpu.VMEM((tm, d), BF16)],
        compiler_params=pltpu.CompilerParams(
            dimension_semantics=("parallel", "arbitrary"),
            vmem_limit_bytes=VMEM_LIMIT_BYTES),
        name="ffn",
    )(x, pre_g.reshape(1, d), w_gate, w_up, w_down, post_g.reshape(1, d))


def _in_proj_kernel(x_ref, g_ref, w_ref, qkv_ref, u_ref, h_ref, *, n_qkv):
    j = pl.program_id(1)

    @pl.when(j == 0)
    def _():
        x = x_ref[...]
        h_ref[...] = (x * _rms_scale(x) * g_ref[...]).astype(BF16)

    def project():
        return jnp.dot(h_ref[...], w_ref[...].astype(BF16), preferred_element_type=F32)

    @pl.when(j < n_qkv)
    def _():
        qkv_ref[...] = project().astype(BF16)

    @pl.when(j == n_qkv)
    def _():
        u_ref[...] = project()


def _in_proj(x, g, w_in, *, ssm_width, tm=1024):
    t, d = x.shape
    n = w_in.shape[1]
    tn = ssm_width
    n_qkv = (n - ssm_width) // tn
    assert t % tm == 0 and n == (n_qkv + 1) * tn
    return pl.pallas_call(
        functools.partial(_in_proj_kernel, n_qkv=n_qkv),
        grid=(t // tm, n_qkv + 1),
        in_specs=[
            pl.BlockSpec((tm, d), lambda i, j: (i, 0)),
            pl.BlockSpec((1, d), lambda i, j: (0, 0)),
            pl.BlockSpec((d, tn), lambda i, j: (0, j)),
        ],
        out_specs=[
            pl.BlockSpec((tm, tn), lambda i, j: (i, jnp.minimum(j, n_qkv - 1))),
            pl.BlockSpec((tm, tn), lambda i, j: (i, 0)),
        ],
        out_shape=[
            jax.ShapeDtypeStruct((t, n - ssm_width), BF16),
            jax.ShapeDtypeStruct((t, ssm_width), F32),
        ],
        scratch_shapes=[pltpu.VMEM((tm, d), BF16)],
        compiler_params=pltpu.CompilerParams(
            dimension_semantics=("parallel", "arbitrary"),
            vmem_limit_bytes=VMEM_LIMIT_BYTES),
        name="in_proj",
    )(x, g.reshape(1, d), w_in)


def _attn_kernel(slopes_ref, lq1_ref, lk1_ref, lq2_ref, lk2_ref, hg_ref,
                 q_ref, k_ref, v_ref, o_ref, bias_ref, s_ref, p_ref, *, tq, lam_init):
    h = pl.program_id(0)
    qi = pl.program_id(1)
    b = pl.program_id(2)
    s_len = k_ref.shape[0]

    @pl.when(b == 0)
    def _():
        qpos = qi * tq + lax.broadcasted_iota(jnp.int32, (tq, s_len), 0)
        kpos = lax.broadcasted_iota(jnp.int32, (tq, s_len), 1)
        bias_ref[...] = slopes_ref[h] * jnp.abs(qpos - kpos).astype(F32)

    lam = (jnp.exp(jnp.sum(lq1_ref[...] * lk1_ref[...], axis=-1, keepdims=True))
           - jnp.exp(jnp.sum(lq2_ref[...] * lk2_ref[...], axis=-1, keepdims=True))
           + lam_init)

    q = q_ref[...] * jnp.asarray(ATTN_HEAD_DIM ** -0.5, BF16)
    k = k_ref[...]
    v = v_ref[...]
    v_ext = jnp.concatenate([v, jnp.ones_like(v)], axis=1)
    lane = lax.broadcasted_iota(jnp.int32, q.shape, 1)
    e = v.shape[1]
    for c in range(2):
        in_map = (lane >= c * ATTN_HEAD_DIM) & (lane < (c + 1) * ATTN_HEAD_DIM)
        qc = jnp.where(in_map, q, jnp.zeros_like(q))
        for j in range(0, s_len, KEY_CHUNK):
            cols = slice(j, j + KEY_CHUNK)
            s_ref[c, :, cols] = lax.dot_general(
                qc, k[cols], (((1,), (1,)), ((), ())),
                preferred_element_type=F32) - bias_ref[:, cols]
    for c in range(2):
        s = s_ref[c]
        p_ref[c] = jnp.exp(s - jnp.max(s, axis=-1, keepdims=True)).astype(BF16)
    outs = []
    for c in range(2):
        pv = jnp.dot(p_ref[c], v_ext, preferred_element_type=F32)
        outs.append(pv[:, :e] / pv[:, e:])
    o = outs[0] - lam * outs[1]
    o_ref[...] = (o * _rms_scale(o) * hg_ref[...] * (1.0 - lam_init)).astype(o_ref.dtype)


def _attention(qkv, slopes, lq1, lk1, lq2, lk2, head_g, *, batch, seq, lam_init, tq=512):
    t = qkv.shape[0]
    e = ATTN_VALUE_DIM
    nq = seq // tq
    nh = ATTN_HEADS
    vec = lambda a: a.reshape(1, -1).astype(F32)
    small = lambda n: pl.BlockSpec((1, n), lambda h, qi, b: (0, 0))
    return pl.pallas_call(
        functools.partial(_attn_kernel, tq=tq, lam_init=lam_init),
        grid=(nh, nq, batch),
        in_specs=[
            pl.BlockSpec(memory_space=pltpu.SMEM),
            small(ATTN_HEAD_DIM), small(ATTN_HEAD_DIM), small(ATTN_HEAD_DIM), small(ATTN_HEAD_DIM),
            small(e),
            pl.BlockSpec((tq, e), lambda h, qi, b: (b * nq + qi, h)),
            pl.BlockSpec((seq, e), lambda h, qi, b: (b, nh + h)),
            pl.BlockSpec((seq, e), lambda h, qi, b: (b, 2 * nh + h)),
        ],
        out_specs=pl.BlockSpec((tq, e), lambda h, qi, b: (b * nq + qi, h)),
        out_shape=jax.ShapeDtypeStruct((t, nh * e), BF16),
        scratch_shapes=[pltpu.VMEM((tq, seq), F32), pltpu.VMEM((2, tq, seq), F32),
                        pltpu.VMEM((2, tq, seq), BF16)],
        compiler_params=pltpu.CompilerParams(
            dimension_semantics=("parallel", "parallel", "arbitrary"),
            vmem_limit_bytes=VMEM_LIMIT_BYTES),
        name="diff_attention",
    )(slopes, vec(lq1), vec(lk1), vec(lq2), vec(lk2), vec(head_g), qkv, qkv, qkv)


def _discretize(lam_re, lam_im, log_dt):
    dt = jnp.exp(log_dt)
    mag = jnp.exp(lam_re * dt)
    a_re = mag * jnp.cos(lam_im * dt)
    a_im = mag * jnp.sin(lam_im * dt)
    den = lam_re * lam_re + lam_im * lam_im
    nr = a_re - 1.0
    f_re = (nr * lam_re + a_im * lam_im) / den
    f_im = (a_im * lam_re - nr * lam_im) / den
    return a_re, a_im, f_re, f_im


def _cmul(ar, ai, br, bi):
    return ar * br - ai * bi, ar * bi + ai * br


def _split_bf16(x, parts):
    out = []
    for k in range(parts):
        piece = x.astype(BF16)
        out.append(piece)
        if k + 1 < parts:
            x = x - piece.astype(F32)
    return out


def _copy_dot(x, onehot, dims):
    return sum(lax.dot_general(piece, onehot, dims, preferred_element_type=F32)
               for piece in _split_bf16(x, 3))


def _dot_3pass(a, b, dims):
    a_hi, a_lo = _split_bf16(a, 2)
    b_hi, b_lo = _split_bf16(b, 2)
    dot = lambda x, y: lax.dot_general(x, y, dims, preferred_element_type=F32)
    return dot(a_hi, b_hi) + (dot(a_hi, b_lo) + dot(a_lo, b_hi))


def _int_power(a_re, a_im, e, nbits):
    shape = jnp.broadcast_shapes(a_re.shape, e.shape)
    p_re, p_im = jnp.ones(shape, F32), jnp.zeros(shape, F32)
    s_re, s_im = a_re, a_im
    for k in range(nbits):
        bit = (e & (1 << k)) != 0
        m_re, m_im = _cmul(p_re, p_im, s_re, s_im)
        p_re, p_im = jnp.where(bit, m_re, p_re), jnp.where(bit, m_im, p_im)
        if k + 1 < nbits:
            s_re, s_im = _cmul(s_re, s_im, s_re, s_im)
    return p_re, p_im


def _shift_lanes(x, s, left):
    if s == 0:
        return x
    n = x.shape[-1]
    lane = lax.broadcasted_iota(jnp.int32, x.shape, 1)
    if left:
        return jnp.where(lane < n - s, pltpu.roll(x, n - s, axis=1), 0.0)
    return jnp.where(lane >= s, pltpu.roll(x, s, axis=1), 0.0)


def _s5_prep_kernel(lre_ref, lim_ref, ldt_ref, bre_c_ref, bim_c_ref, ctre_ref, ctim_ref,
                    btre_r_ref, btim_r_ref, w1_ref, cm_ref, al_ref):
    for i in range(w1_ref.shape[0]):
        _s5_prep_group(i, lre_ref, lim_ref, ldt_ref, bre_c_ref, bim_c_ref, ctre_ref, ctim_ref,
                       btre_r_ref, btim_r_ref, w1_ref, cm_ref, al_ref)


def _s5_prep_group(i, lre_ref, lim_ref, ldt_ref, bre_c_ref, bim_c_ref, ctre_ref, ctim_ref,
                   btre_r_ref, btim_r_ref, w1_ref, cm_ref, al_ref):
    n, p, l = SSM_STATE, SSM_GROUP, CHUNK
    nbits = l.bit_length() - 1
    contract0 = (((0,), (0,)), ((), ()))
    contract_mm = (((1,), (0,)), ((), ()))

    a_re, a_im, f_re, f_im = _discretize(lre_ref[i], lim_ref[i], ldt_ref[i])
    bt_re, bt_im = btre_r_ref[i], btim_r_ref[i]
    bb_re, bb_im = _cmul(f_re, f_im, bt_re, bt_im)
    tau = lax.broadcasted_iota(jnp.int32, (l, 2 * n), 0)
    lane = lax.broadcasted_iota(jnp.int32, (l, 2 * n), 1)
    pw_re, pw_im = _int_power(a_re, a_im, jnp.where(lane < n, l - 1 - tau, tau), nbits)
    bm_re = pw_re[:, None, :] * bb_re[None, :, :] - pw_im[:, None, :] * bb_im[None, :, :]
    bm_im = pw_re[:, None, :] * bb_im[None, :, :] + pw_im[:, None, :] * bb_re[None, :, :]
    w1_ref[i, :, GROUP_LANES:GROUP_LANES + 2 * n] = bm_re.reshape(l * p, 2 * n).astype(BF16)
    w1_ref[i, :, GROUP_LANES + 2 * n:] = bm_im.reshape(l * p, 2 * n).astype(BF16)
    al_re, al_im = a_re, a_im
    for _ in range(nbits):
        al_re, al_im = _cmul(al_re, al_im, al_re, al_im)
    al_ref[i] = jnp.concatenate([al_re, al_im], axis=0)

    quant = jnp.concatenate([a_re, a_im, f_re, f_im, jnp.zeros((SUBLANES - 4, 2 * n), F32)], axis=0)
    quant_t = quant.T

    lanes_p = lax.broadcasted_iota(jnp.int32, (p, GROUP_LANES), 1)
    rows_p = lax.broadcasted_iota(jnp.int32, (p, GROUP_LANES), 0)
    tile_p = (lanes_p % p == rows_p).astype(BF16)
    tile_lag = ((l - 1) - lanes_p // p == rows_p).astype(BF16)
    pwt_re = _copy_dot(pw_re, tile_lag, contract0)
    pwt_im = _copy_dot(pw_im, tile_lag, contract0)
    t_rows = None
    cm_re, cm_im = [], []
    for d in range(2):
        states = slice(d * n, (d + 1) * n)
        col = lambda k: quant_t[states, k:k + 1]
        a_re_c, a_im_c, f_re_c, f_im_c = col(0), col(1), col(2), col(3)
        bb_re, bb_im = _cmul(f_re_c, f_im_c, bre_c_ref[d, i], bim_c_ref[d, i])
        ct_re = _copy_dot(ctre_ref[d, i], tile_p, contract_mm)
        ct_im = _copy_dot(ctim_ref[d, i], tile_p, contract_mm)
        ac_re, ac_im = _cmul(pwt_re[states], pwt_im[states], ct_re, ct_im)
        taps = (_dot_3pass(bb_re, ac_re, contract0)
                - _dot_3pass(bb_im, ac_im, contract0))
        rows = [_shift_lanes(taps, p * (tp if d == 0 else l - 1 - tp), left=(d == 1))
                for tp in range(l)]
        rows = jnp.concatenate(rows, axis=0)
        t_rows = rows if t_rows is None else t_rows + rows
        r_re, r_im = _cmul(a_re_c, a_im_c, ac_re, ac_im)
        cm_re.append(r_re)
        cm_im.append(-r_im)
    w1_ref[i, :, 0:GROUP_LANES] = t_rows.astype(BF16)
    cm_ref[i] = jnp.concatenate(cm_re + cm_im, axis=0).astype(BF16)


def _s5_prep(lam_re, lam_im, log_dt, b_re, b_im, c_re, c_im):
    _, g, n = lam_re.shape
    p = b_re.shape[-1]
    row = lambda a: jnp.concatenate([a[0], a[1]], axis=-1)[:, None, :]
    ldt_r = row(jnp.broadcast_to(log_dt[:, :, None], (2, g, n)))
    bt = lambda b: jnp.concatenate([jnp.swapaxes(b[0], 1, 2), jnp.swapaxes(b[1], 1, 2)], axis=-1)
    ct = lambda c: jnp.swapaxes(c, 2, 3)

    gb = PREP_GROUPS_PER_STEP
    assert g % gb == 0
    spec_c = pl.BlockSpec((2, gb, n, p), lambda i: (0, i, 0, 0))
    spec_r = lambda rows: pl.BlockSpec((gb, rows, 2 * n), lambda i: (i, 0, 0))
    return pl.pallas_call(
        _s5_prep_kernel,
        grid=(g // gb,),
        in_specs=[spec_r(1), spec_r(1), spec_r(1), spec_c, spec_c, spec_c, spec_c,
                  spec_r(p), spec_r(p)],
        out_specs=[
            pl.BlockSpec((gb, GROUP_LANES, GROUP_LANES + 4 * n), lambda i: (i, 0, 0)),
            pl.BlockSpec((gb, 4 * n, GROUP_LANES), lambda i: (i, 0, 0)),
            pl.BlockSpec((gb, 2, 2 * n), lambda i: (i, 0, 0)),
        ],
        out_shape=[
            jax.ShapeDtypeStruct((g, GROUP_LANES, GROUP_LANES + 4 * n), BF16),
            jax.ShapeDtypeStruct((g, 4 * n, GROUP_LANES), BF16),
            jax.ShapeDtypeStruct((g, 2, 2 * n), F32),
        ],
        compiler_params=pltpu.CompilerParams(dimension_semantics=("parallel",)),
        name="s5_prep",
    )(row(lam_re), row(lam_im), ldt_r, b_re, b_im, ct(c_re), ct(c_im), bt(b_re), bt(b_im))


def _swap_pieces(v):
    piece = lax.broadcasted_iota(jnp.int32, v[0].shape, 1) // SSM_GROUP
    v = list(v)
    for d in (4, 2, 1):
        hi = (piece & d) != 0
        for a in range(8):
            if a & d:
                continue
            va, vb = v[a], v[a + d]
            v[a] = jnp.where(hi, pltpu.roll(vb, d * SSM_GROUP, axis=1), va)
            v[a + d] = jnp.where(hi, vb, pltpu.roll(va, LANES - d * SSM_GROUP, axis=1))
    return v


def _s5_chunk_kernel(u_ref, w1_ref, cm_ref, al_ref, d_ref, y_ref, x_ref, s_ref, hf_ref, h_ref,
                     *, nb, nc):
    gl = GROUP_LANES
    gb = GROUPS_PER_BLOCK
    n2 = 2 * SSM_STATE
    half = CHUNK // 2
    ctile = SUBLANES * CHUNK

    def token_rows(ct, h, j):
        return pl.ds(ct * ctile + h * half + j, SUBLANES, stride=CHUNK)

    def chunk_rows(ct, b):
        return pl.ds(ct * SUBLANES * nb + b, SUBLANES, stride=nb)

    def gather_body(ct, carry):
        for b in range(nb):
            for h in range(2):
                v = [u_ref[b, token_rows(ct, h, j), :] for j in range(half)]
                w = _swap_pieces(v)
                for g in range(gb):
                    x_ref[2 * g + h, chunk_rows(ct, b), :] = w[g]
        return carry

    lax.fori_loop(0, nc // SUBLANES, gather_body, 0)

    for g in range(gb):
        x = jnp.concatenate([x_ref[2 * g], x_ref[2 * g + 1]], axis=1)
        r = jnp.dot(x.astype(BF16), w1_ref[g], preferred_element_type=F32)
        x_ref[2 * g] = r[:, :LANES]
        x_ref[2 * g + 1] = r[:, LANES:gl]
        s_ref[:, g * gl:(g + 1) * gl] = r[:, gl:]

    a_re = [al_ref[g, 0:1, :] for g in range(gb)]
    a_im = [al_ref[g, 1:2, :] for g in range(gb)]
    is_fwd = lax.broadcasted_iota(jnp.int32, (SUBLANES, n2), 1) < SSM_STATE

    def step(h, s):
        out = []
        for g in range(gb):
            hr, hi = h[:, g * gl:g * gl + n2], h[:, g * gl + n2:(g + 1) * gl]
            sr, si = s[:, g * gl:g * gl + n2], s[:, g * gl + n2:(g + 1) * gl]
            out.append(a_re[g] * hr - a_im[g] * hi + sr)
            out.append(a_re[g] * hi + a_im[g] * hr + si)
        return jnp.concatenate(out, axis=1)

    cpt = SUBLANES // nb
    n_tiles = nc // cpt
    tile_rows = lambda t: pl.ds(pl.multiple_of(t * SUBLANES, SUBLANES), SUBLANES)

    def fwd_body(t, h):
        s = s_ref[tile_rows(t), :]
        before = []
        for k in range(cpt):
            before.append(h)
            h = step(h, s[k * nb:(k + 1) * nb])
        hf_ref[tile_rows(t), :] = jnp.concatenate(before, axis=0)
        return h

    lax.fori_loop(0, n_tiles, fwd_body, jnp.zeros((nb, gb * gl), F32))

    def bwd_body(i, h):
        t = n_tiles - 1 - i
        s = s_ref[tile_rows(t), :]
        before = [None] * cpt
        for k in reversed(range(cpt)):
            before[k] = h
            h = step(h, s[k * nb:(k + 1) * nb])
        hb = jnp.concatenate(before, axis=0)
        hf = hf_ref[tile_rows(t), :]
        merged = [jnp.where(is_fwd, hf[:, k * n2:(k + 1) * n2], hb[:, k * n2:(k + 1) * n2])
                  for k in range(2 * gb)]
        h_ref[tile_rows(t), :] = jnp.concatenate(merged, axis=1)
        return h

    lax.fori_loop(0, n_tiles, bwd_body, jnp.zeros((nb, gb * gl), F32))

    for g in range(gb):
        r = jnp.dot(h_ref[:, g * gl:(g + 1) * gl].astype(BF16), cm_ref[g],
                    preferred_element_type=F32)
        x_ref[2 * g] += r[:, :LANES]
        x_ref[2 * g + 1] += r[:, LANES:]

    def scatter_body(ct, carry):
        for b in range(nb):
            for h in range(2):
                w = [x_ref[2 * g + h, chunk_rows(ct, b), :] for g in range(gb)]
                v = _swap_pieces(w)
                for j in range(half):
                    rows = token_rows(ct, h, j)
                    y_ref[b, rows, :] = v[j] + d_ref[...] * u_ref[b, rows, :]
        return carry

    lax.fori_loop(0, nc // SUBLANES, scatter_body, 0)


def _s5_chunks(u, w1, cm, al, d):
    nb, seq, width = u.shape
    nc = seq // CHUNK
    rows = nb * nc
    gb = GROUPS_PER_BLOCK
    assert SUBLANES % nb == 0 and nc % SUBLANES == 0 and width % LANES == 0
    blk = pl.BlockSpec((nb, seq, LANES), lambda i: (0, 0, i))
    return pl.pallas_call(
        functools.partial(_s5_chunk_kernel, nb=nb, nc=nc),
        grid=(width // LANES,),
        in_specs=[
            blk,
            pl.BlockSpec((gb,) + w1.shape[1:], lambda i: (i, 0, 0)),
            pl.BlockSpec((gb,) + cm.shape[1:], lambda i: (i, 0, 0)),
            pl.BlockSpec((gb,) + al.shape[1:], lambda i: (i, 0, 0)),
            pl.BlockSpec((1, LANES), lambda i: (0, i)),
        ],
        out_specs=blk,
        out_shape=jax.ShapeDtypeStruct(u.shape, F32),
        scratch_shapes=[pltpu.VMEM((2 * gb, rows, LANES), F32)]
        + [pltpu.VMEM((rows, gb * GROUP_LANES), F32)] * 3,
        compiler_params=pltpu.CompilerParams(
            dimension_semantics=("parallel",), vmem_limit_bytes=VMEM_LIMIT_BYTES),
        name="s5_chunks",
    )(u, w1, cm, al, d.reshape(1, width))


def _mix_out_kernel(x_ref, a_ref, y_ref, wglu_ref, bglu_ref, og_ref, wo_a_ref, wo_s_ref, pg_ref,
                    o_ref):
    y = y_ref[...]
    g = y * (0.5 * (1.0 + jnp.tanh(math.sqrt(2.0 / math.pi) * (y + 0.044715 * (y * y * y)))))
    z = jnp.dot(g.astype(BF16), wglu_ref[...], preferred_element_type=F32) + bglu_ref[...]
    s = g * _sigmoid(z)
    s = s * _rms_scale(s) * og_ref[...]
    mixed = (jnp.dot(a_ref[...], wo_a_ref[...], preferred_element_type=F32)
             + jnp.dot(s.astype(BF16), wo_s_ref[...], preferred_element_type=F32))
    o_ref[...] = x_ref[...] + mixed * _rms_scale(mixed) * pg_ref[...]


def _mix_out(x, a, y, w_glu, b_glu, out_g, w_out, post_g, *, tm=512):
    t, d = x.shape
    wa = a.shape[1]
    ws = y.shape[1]
    const = lambda shape: pl.BlockSpec(shape, lambda i: (0, 0))
    return pl.pallas_call(
        _mix_out_kernel,
        grid=(t // tm,),
        in_specs=[
            pl.BlockSpec((tm, d), lambda i: (i, 0)),
            pl.BlockSpec((tm, wa), lambda i: (i, 0)),
            pl.BlockSpec((tm, ws), lambda i: (i, 0)),
            const((ws, ws)), const((1, ws)), const((1, ws)),
            pl.BlockSpec((wa, d), lambda i: (0, 0)),
            pl.BlockSpec((ws, d), lambda i: (wa // ws, 0)),
            const((1, d)),
        ],
        out_specs=pl.BlockSpec((tm, d), lambda i: (i, 0)),
        out_shape=jax.ShapeDtypeStruct((t, d), F32),
        compiler_params=pltpu.CompilerParams(
            dimension_semantics=("parallel",), vmem_limit_bytes=VMEM_LIMIT_BYTES),
        name="mix_out",
    )(x, a, y, w_glu, b_glu.reshape(1, ws), out_g.reshape(1, ws), w_out, w_out,
      post_g.reshape(1, d))


def kernel(x, ff1_pre_g, ff1_w_gate, ff1_w_up, ff1_w_down, ff1_post_g, mix_pre_g, w_in, lam_q1, lam_k1, lam_q2, lam_k2, attn_head_g, ssm_lam_re, ssm_lam_im, ssm_log_dt, ssm_b_re, ssm_b_im, ssm_c_re, ssm_c_im, ssm_d, ssm_w_glu, ssm_b_glu, ssm_out_g, w_out, mix_post_g, ff2_pre_g, ff2_w_gate, ff2_w_up, ff2_w_down, ff2_post_g):
    batch, seq, d_model = x.shape
    depth = w_in.shape[0]
    ssm_width = ssm_w_glu.shape[-1]
    slopes = jnp.asarray([2.0 ** (-8.0 * (i + 1) / ATTN_HEADS) for i in range(ATTN_HEADS)], F32)
    bf = lambda w: w.astype(BF16)

    xt = x.reshape(batch * seq, d_model)
    for l in range(depth):
        xt = _ffn(xt, ff1_pre_g[l], bf(ff1_w_gate[l]), bf(ff1_w_up[l]), bf(ff1_w_down[l]),
                  ff1_post_g[l], tf=512)

        qkv, u = _in_proj(xt, mix_pre_g[l], bf(w_in[l]), ssm_width=ssm_width)
        lam_init = 0.8 - 0.6 * math.exp(-0.3 * l)
        a = _attention(qkv, slopes, lam_q1[l], lam_k1[l], lam_q2[l], lam_k2[l], attn_head_g[l],
                       batch=batch, seq=seq, lam_init=lam_init)

        w1, cm, al = _s5_prep(ssm_lam_re[l], ssm_lam_im[l], ssm_log_dt[l], ssm_b_re[l],
                              ssm_b_im[l], ssm_c_re[l], ssm_c_im[l])
        y = _s5_chunks(u.reshape(batch, seq, ssm_width), w1, cm, al, ssm_d[l])
        y = y.reshape(batch * seq, ssm_width)

        xt = _mix_out(xt, a, y, bf(ssm_w_glu[l]), ssm_b_glu[l], ssm_out_g[l], bf(w_out[l]),
                      mix_post_g[l])

        xt = _ffn(xt, ff2_pre_g[l], ff2_w_gate[l], ff2_w_up[l], ff2_w_down[l], ff2_post_g[l])
    return xt.reshape(batch, seq, d_model)
```

```python
import functools
import math

import jax
import jax.numpy as jnp
from jax import lax
from jax.experimental import pallas as pl
from jax.experimental.pallas import tpu as pltpu

F32 = jnp.float32
BF16 = jnp.bfloat16

NORM_EPS = 1e-6
ATTN_HEADS = 8
ATTN_HEAD_DIM = 64
ATTN_VALUE_DIM = 2 * ATTN_HEAD_DIM
KEY_CHUNK = 512
SSM_GROUP = 16
SSM_STATE = 64
CHUNK = 16
GROUP_LANES = CHUNK * SSM_GROUP
SUBLANES = 8
LANES = 128
BF16_ROWS = 16
GROUPS_PER_BLOCK = LANES // SSM_GROUP
PREP_GROUPS_PER_STEP = 4

VMEM_LIMIT_BYTES = 56 * 1024 * 1024


def _rms_scale(x):
    return lax.rsqrt(jnp.mean(x * x, axis=-1, keepdims=True) + NORM_EPS)


def _sigmoid(x):
    return 1.0 / (1.0 + jnp.exp(-x))


class _CastJobs:
    def __init__(self, arrays, n_steps, flat_index):
        self.shapes = [a.shape for a in arrays]
        self.flat_index = flat_index
        self.views, self.slabs = [], []
        for a in arrays:
            n = 1 << (n_steps.bit_length() - 1)
            while a.size % (n * BF16_ROWS * LANES):
                n //= 2
            cols = a.size // (n * BF16_ROWS)
            self.views.append(a.reshape(n * BF16_ROWS, cols))
            self.slabs.append((BF16_ROWS, cols, n))

    def _spec(self, slab):
        r, c, n = slab
        return pl.BlockSpec((r, c), lambda *g: (jnp.minimum(self.flat_index(*g), n - 1), 0))

    @property
    def in_specs(self):
        return [self._spec(s) for s in self.slabs]

    out_specs = in_specs

    @property
    def out_shapes(self):
        return [jax.ShapeDtypeStruct(v.shape, BF16) for v in self.views]

    def __len__(self):
        return len(self.views)

    def results(self, outs):
        return [o.reshape(s) for o, s in zip(outs, self.shapes)]


def _run_cast_jobs(in_refs, out_refs):
    for i_ref, o_ref in zip(in_refs, out_refs):
        o_ref[...] = i_ref[...].astype(BF16)


def _ffn_kernel(*refs, n_jobs):
    x_ref, pre_g_ref, wg_ref, wu_ref, wd_ref, post_g_ref = refs[:6]
    job_in = refs[6:6 + n_jobs]
    o_ref = refs[6 + n_jobs]
    job_out = refs[7 + n_jobs:7 + 2 * n_jobs]
    h_ref = refs[7 + 2 * n_jobs]
    j = pl.program_id(1)
    last = pl.num_programs(1) - 1
    _run_cast_jobs(job_in, job_out)

    @pl.when(j == 0)
    def _():
        x = x_ref[...]
        h_ref[...] = (x * _rms_scale(x) * pre_g_ref[...]).astype(BF16)
        o_ref[...] = jnp.zeros_like(o_ref)

    h = h_ref[...]
    gate = jnp.dot(h, wg_ref[...].astype(BF16), preferred_element_type=F32)
    up = jnp.dot(h, wu_ref[...].astype(BF16), preferred_element_type=F32)
    act = (gate * _sigmoid(gate) * up).astype(BF16)
    o_ref[...] += jnp.dot(act, wd_ref[...].astype(BF16), preferred_element_type=F32)

    @pl.when(j == last)
    def _():
        acc = o_ref[...]
        o_ref[...] = x_ref[...] + 0.5 * (acc * _rms_scale(acc) * post_g_ref[...])


def _ffn(x, pre_g, w_gate, w_up, w_down, post_g, *, cast=(), tm=512, tf=512):
    t, d = x.shape
    f = w_gate.shape[1]
    assert t % tm == 0 and f % tf == 0
    nj = f // tf
    jobs = _CastJobs(cast, (t // tm) * nj, lambda i, j: i * nj + j)
    outs = pl.pallas_call(
        functools.partial(_ffn_kernel, n_jobs=len(jobs)),
        grid=(t // tm, nj),
        in_specs=[
            pl.BlockSpec((tm, d), lambda i, j: (i, 0)),
            pl.BlockSpec((1, d), lambda i, j: (0, 0)),
            pl.BlockSpec((d, tf), lambda i, j: (0, j)),
            pl.BlockSpec((d, tf), lambda i, j: (0, j)),
            pl.BlockSpec((tf, d), lambda i, j: (j, 0)),
            pl.BlockSpec((1, d), lambda i, j: (0, 0)),
        ] + jobs.in_specs,
        out_specs=[pl.BlockSpec((tm, d), lambda i, j: (i, 0))] + jobs.out_specs,
        out_shape=[jax.ShapeDtypeStruct((t, d), F32)] + jobs.out_shapes,
        scratch_shapes=[pltpu.VMEM((tm, d), BF16)],
        compiler_params=pltpu.CompilerParams(
            dimension_semantics=("arbitrary", "arbitrary"),
            vmem_limit_bytes=VMEM_LIMIT_BYTES),
        name="ffn",
    )(x, pre_g.reshape(1, d), w_gate, w_up, w_down, post_g.reshape(1, d), *jobs.views)
    return outs[0], jobs.results(outs[1:])


def _in_proj_kernel(x_ref, g_ref, w_ref, qkv_ref, u_ref, h_ref, *, n_qkv):
    j = pl.program_id(1)

    @pl.when(j == 0)
    def _():
        x = x_ref[...]
        h_ref[...] = (x * _rms_scale(x) * g_ref[...]).astype(BF16)

    def project():
        return jnp.dot(h_ref[...], w_ref[...].astype(BF16), preferred_element_type=F32)

    @pl.when(j < n_qkv)
    def _():
        qkv_ref[...] = project().astype(BF16)

    @pl.when(j == n_qkv)
    def _():
        u_ref[...] = project()


def _in_proj(x, g, w_in, *, ssm_width, tm=1024):
    t, d = x.shape
    n = w_in.shape[1]
    tn = ssm_width
    n_qkv = (n - ssm_width) // tn
    assert t % tm == 0 and n == (n_qkv + 1) * tn
    return pl.pallas_call(
        functools.partial(_in_proj_kernel, n_qkv=n_qkv),
        grid=(t // tm, n_qkv + 1),
        in_specs=[
            pl.BlockSpec((tm, d), lambda i, j: (i, 0)),
            pl.BlockSpec((1, d), lambda i, j: (0, 0)),
            pl.BlockSpec((d, tn), lambda i, j: (0, j)),
        ],
        out_specs=[
            pl.BlockSpec((tm, tn), lambda i, j: (i, jnp.minimum(j, n_qkv - 1))),
            pl.BlockSpec((tm, tn), lambda i, j: (i, 0)),
        ],
        out_shape=[
            jax.ShapeDtypeStruct((t, n - ssm_width), BF16),
            jax.ShapeDtypeStruct((t, ssm_width), F32),
        ],
        scratch_shapes=[pltpu.VMEM((tm, d), BF16)],
        compiler_params=pltpu.CompilerParams(
            dimension_semantics=("parallel", "arbitrary"),
            vmem_limit_bytes=VMEM_LIMIT_BYTES),
        name="in_proj",
    )(x, g.reshape(1, d), w_in)


def _attn_kernel(*refs, tq, lam_init, n_jobs):
    (slopes_ref, lq1_ref, lk1_ref, lq2_ref, lk2_ref, hg_ref, q_ref, k_ref, v_ref) = refs[:9]
    job_in = refs[9:9 + n_jobs]
    o_ref = refs[9 + n_jobs]
    job_out = refs[10 + n_jobs:10 + 2 * n_jobs]
    bias_ref, s_ref, p_ref = refs[10 + 2 * n_jobs:]
    h = pl.program_id(0)
    qi = pl.program_id(1)
    b = pl.program_id(2)
    s_len = k_ref.shape[0]
    _run_cast_jobs(job_in, job_out)

    @pl.when(b == 0)
    def _():
        qpos = qi * tq + lax.broadcasted_iota(jnp.int32, (tq, s_len), 0)
        kpos = lax.broadcasted_iota(jnp.int32, (tq, s_len), 1)
        bias_ref[...] = slopes_ref[h] * jnp.abs(qpos - kpos).astype(F32)

    lam = (jnp.exp(jnp.sum(lq1_ref[...] * lk1_ref[...], axis=-1, keepdims=True))
           - jnp.exp(jnp.sum(lq2_ref[...] * lk2_ref[...], axis=-1, keepdims=True))
           + lam_init)

    q = q_ref[...] * jnp.asarray(ATTN_HEAD_DIM ** -0.5, BF16)
    k = k_ref[...]
    v = v_ref[...]
    v_ext = jnp.concatenate([v, jnp.ones_like(v)], axis=1)
    lane = lax.broadcasted_iota(jnp.int32, q.shape, 1)
    e = v.shape[1]
    for c in range(2):
        in_map = (lane >= c * ATTN_HEAD_DIM) & (lane < (c + 1) * ATTN_HEAD_DIM)
        qc = jnp.where(in_map, q, jnp.zeros_like(q))
        for j in range(0, s_len, KEY_CHUNK):
            cols = slice(j, j + KEY_CHUNK)
            s_ref[c, :, cols] = lax.dot_general(
                qc, k[cols], (((1,), (1,)), ((), ())),
                preferred_element_type=F32) - bias_ref[:, cols]
    for c in range(2):
        s = s_ref[c]
        p_ref[c] = jnp.exp(s - jnp.max(s, axis=-1, keepdims=True)).astype(BF16)
    outs = []
    for c in range(2):
        pv = jnp.dot(p_ref[c], v_ext, preferred_element_type=F32)
        outs.append(pv[:, :e] / pv[:, e:])
    o = outs[0] - lam * outs[1]
    o_ref[...] = (o * _rms_scale(o) * hg_ref[...] * (1.0 - lam_init)).astype(o_ref.dtype)


def _attention(qkv, slopes, lq1, lk1, lq2, lk2, head_g, *, batch, seq, lam_init, cast=(), tq=512):
    t = qkv.shape[0]
    e = ATTN_VALUE_DIM
    nq = seq // tq
    nh = ATTN_HEADS
    vec = lambda a: a.reshape(1, -1).astype(F32)
    small = lambda n: pl.BlockSpec((1, n), lambda h, qi, b: (0, 0))
    jobs = _CastJobs(cast, nh * nq * batch, lambda h, qi, b: (h * nq + qi) * batch + b)
    outs = pl.pallas_call(
        functools.partial(_attn_kernel, tq=tq, lam_init=lam_init, n_jobs=len(jobs)),
        grid=(nh, nq, batch),
        in_specs=[
            pl.BlockSpec(memory_space=pltpu.SMEM),
            small(ATTN_HEAD_DIM), small(ATTN_HEAD_DIM), small(ATTN_HEAD_DIM), small(ATTN_HEAD_DIM),
            small(e),
            pl.BlockSpec((tq, e), lambda h, qi, b: (b * nq + qi, h)),
            pl.BlockSpec((seq, e), lambda h, qi, b: (b, nh + h)),
            pl.BlockSpec((seq, e), lambda h, qi, b: (b, 2 * nh + h)),
        ] + jobs.in_specs,
        out_specs=[pl.BlockSpec((tq, e), lambda h, qi, b: (b * nq + qi, h))] + jobs.out_specs,
        out_shape=[jax.ShapeDtypeStruct((t, nh * e), BF16)] + jobs.out_shapes,
        scratch_shapes=[pltpu.VMEM((tq, seq), F32), pltpu.VMEM((2, tq, seq), F32),
                        pltpu.VMEM((2, tq, seq), BF16)],
        compiler_params=pltpu.CompilerParams(
            dimension_semantics=("arbitrary", "arbitrary", "arbitrary"),
            vmem_limit_bytes=VMEM_LIMIT_BYTES),
        name="diff_attention",
    )(slopes, vec(lq1), vec(lk1), vec(lq2), vec(lk2), vec(head_g), qkv, qkv, qkv, *jobs.views)
    return outs[0], jobs.results(outs[1:])


def _discretize(lam_re, lam_im, log_dt):
    dt = jnp.exp(log_dt)
    mag = jnp.exp(lam_re * dt)
    a_re = mag * jnp.cos(lam_im * dt)
    a_im = mag * jnp.sin(lam_im * dt)
    den = lam_re * lam_re + lam_im * lam_im
    nr = a_re - 1.0
    f_re = (nr * lam_re + a_im * lam_im) / den
    f_im = (a_im * lam_re - nr * lam_im) / den
    return a_re, a_im, f_re, f_im


def _cmul(ar, ai, br, bi):
    return ar * br - ai * bi, ar * bi + ai * br


def _split_bf16(x, parts):
    out = []
    for k in range(parts):
        piece = x.astype(BF16)
        out.append(piece)
        if k + 1 < parts:
            x = x - piece.astype(F32)
    return out


def _copy_dot(x, onehot, dims):
    return sum(lax.dot_general(piece, onehot, dims, preferred_element_type=F32)
               for piece in _split_bf16(x, 3))


def _dot_3pass(a, b, dims):
    a_hi, a_lo = _split_bf16(a, 2)
    b_hi, b_lo = _split_bf16(b, 2)
    dot = lambda x, y: lax.dot_general(x, y, dims, preferred_element_type=F32)
    return dot(a_hi, b_hi) + (dot(a_hi, b_lo) + dot(a_lo, b_hi))


def _int_power(a_re, a_im, e, nbits):
    shape = jnp.broadcast_shapes(a_re.shape, e.shape)
    p_re, p_im = jnp.ones(shape, F32), jnp.zeros(shape, F32)
    s_re, s_im = a_re, a_im
    for k in range(nbits):
        bit = (e & (1 << k)) != 0
        m_re, m_im = _cmul(p_re, p_im, s_re, s_im)
        p_re, p_im = jnp.where(bit, m_re, p_re), jnp.where(bit, m_im, p_im)
        if k + 1 < nbits:
            s_re, s_im = _cmul(s_re, s_im, s_re, s_im)
    return p_re, p_im


def _shift_lanes(x, s, left):
    if s == 0:
        return x
    n = x.shape[-1]
    lane = lax.broadcasted_iota(jnp.int32, x.shape, 1)
    if left:
        return jnp.where(lane < n - s, pltpu.roll(x, n - s, axis=1), 0.0)
    return jnp.where(lane >= s, pltpu.roll(x, s, axis=1), 0.0)


def _s5_prep_kernel(lre_ref, lim_ref, ldt_ref, bre_c_ref, bim_c_ref, ctre_ref, ctim_ref,
                    btre_r_ref, btim_r_ref, w1_ref, cm_ref, al_ref):
    for i in range(w1_ref.shape[0]):
        _s5_prep_group(i, lre_ref, lim_ref, ldt_ref, bre_c_ref, bim_c_ref, ctre_ref, ctim_ref,
                       btre_r_ref, btim_r_ref, w1_ref, cm_ref, al_ref)


def _s5_prep_group(i, lre_ref, lim_ref, ldt_ref, bre_c_ref, bim_c_ref, ctre_ref, ctim_ref,
                   btre_r_ref, btim_r_ref, w1_ref, cm_ref, al_ref):
    n, p, l = SSM_STATE, SSM_GROUP, CHUNK
    nbits = l.bit_length() - 1
    contract0 = (((0,), (0,)), ((), ()))
    contract_mm = (((1,), (0,)), ((), ()))

    a_re, a_im, f_re, f_im = _discretize(lre_ref[i], lim_ref[i], ldt_ref[i])
    bt_re, bt_im = btre_r_ref[i], btim_r_ref[i]
    bb_re, bb_im = _cmul(f_re, f_im, bt_re, bt_im)
    tau = lax.broadcasted_iota(jnp.int32, (l, 2 * n), 0)
    lane = lax.broadcasted_iota(jnp.int32, (l, 2 * n), 1)
    pw_re, pw_im = _int_power(a_re, a_im, jnp.where(lane < n, l - 1 - tau, tau), nbits)
    bm_re = pw_re[:, None, :] * bb_re[None, :, :] - pw_im[:, None, :] * bb_im[None, :, :]
    bm_im = pw_re[:, None, :] * bb_im[None, :, :] + pw_im[:, None, :] * bb_re[None, :, :]
    w1_ref[i, :, GROUP_LANES:GROUP_LANES + 2 * n] = bm_re.reshape(l * p, 2 * n).astype(BF16)
    w1_ref[i, :, GROUP_LANES + 2 * n:] = bm_im.reshape(l * p, 2 * n).astype(BF16)
    al_re, al_im = a_re, a_im
    for _ in range(nbits):
        al_re, al_im = _cmul(al_re, al_im, al_re, al_im)
    al_ref[i] = jnp.concatenate([al_re, al_im], axis=0)

    quant = jnp.concatenate([a_re, a_im, f_re, f_im, jnp.zeros((SUBLANES - 4, 2 * n), F32)], axis=0)
    quant_t = quant.T

    lanes_p = lax.broadcasted_iota(jnp.int32, (p, GROUP_LANES), 1)
    rows_p = lax.broadcasted_iota(jnp.int32, (p, GROUP_LANES), 0)
    tile_p = (lanes_p % p == rows_p).astype(BF16)
    tile_lag = ((l - 1) - lanes_p // p == rows_p).astype(BF16)
    pwt_re = _copy_dot(pw_re, tile_lag, contract0)
    pwt_im = _copy_dot(pw_im, tile_lag, contract0)
    t_rows = None
    cm_re, cm_im = [], []
    for d in range(2):
        states = slice(d * n, (d + 1) * n)
        col = lambda k: quant_t[states, k:k + 1]
        a_re_c, a_im_c, f_re_c, f_im_c = col(0), col(1), col(2), col(3)
        bb_re, bb_im = _cmul(f_re_c, f_im_c, bre_c_ref[d, i], bim_c_ref[d, i])
        ct_re = _copy_dot(ctre_ref[d, i], tile_p, contract_mm)
        ct_im = _copy_dot(ctim_ref[d, i], tile_p, contract_mm)
        ac_re, ac_im = _cmul(pwt_re[states], pwt_im[states], ct_re, ct_im)
        taps = (_dot_3pass(bb_re, ac_re, contract0)
                - _dot_3pass(bb_im, ac_im, contract0))
        rows = [_shift_lanes(taps, p * (tp if d == 0 else l - 1 - tp), left=(d == 1))
                for tp in range(l)]
        rows = jnp.concatenate(rows, axis=0)
        t_rows = rows if t_rows is None else t_rows + rows
        r_re, r_im = _cmul(a_re_c, a_im_c, ac_re, ac_im)
        cm_re.append(r_re)
        cm_im.append(-r_im)
    w1_ref[i, :, 0:GROUP_LANES] = t_rows.astype(BF16)
    cm_ref[i] = jnp.concatenate(cm_re + cm_im, axis=0).astype(BF16)


def _s5_prep(lam_re, lam_im, log_dt, b_re, b_im, c_re, c_im):
    _, g, n = lam_re.shape
    p = b_re.shape[-1]
    row = lambda a: jnp.concatenate([a[0], a[1]], axis=-1)[:, None, :]
    ldt_r = row(jnp.broadcast_to(log_dt[:, :, None], (2, g, n)))
    bt = lambda b: jnp.concatenate([jnp.swapaxes(b[0], 1, 2), jnp.swapaxes(b[1], 1, 2)], axis=-1)
    ct = lambda c: jnp.swapaxes(c, 2, 3)

    gb = PREP_GROUPS_PER_STEP
    assert g % gb == 0
    spec_c = pl.BlockSpec((2, gb, n, p), lambda i: (0, i, 0, 0))
    spec_r = lambda rows: pl.BlockSpec((gb, rows, 2 * n), lambda i: (i, 0, 0))
    return pl.pallas_call(
        _s5_prep_kernel,
        grid=(g // gb,),
        in_specs=[spec_r(1), spec_r(1), spec_r(1), spec_c, spec_c, spec_c, spec_c,
                  spec_r(p), spec_r(p)],
        out_specs=[
            pl.BlockSpec((gb, GROUP_LANES, GROUP_LANES + 4 * n), lambda i: (i, 0, 0)),
            pl.BlockSpec((gb, 4 * n, GROUP_LANES), lambda i: (i, 0, 0)),
            pl.BlockSpec((gb, 2, 2 * n), lambda i: (i, 0, 0)),
        ],
        out_shape=[
            jax.ShapeDtypeStruct((g, GROUP_LANES, GROUP_LANES + 4 * n), BF16),
            jax.ShapeDtypeStruct((g, 4 * n, GROUP_LANES), BF16),
            jax.ShapeDtypeStruct((g, 2, 2 * n), F32),
        ],
        compiler_params=pltpu.CompilerParams(dimension_semantics=("parallel",)),
        name="s5_prep",
    )(row(lam_re), row(lam_im), ldt_r, b_re, b_im, ct(c_re), ct(c_im), bt(b_re), bt(b_im))


def _swap_pieces(v):
    piece = lax.broadcasted_iota(jnp.int32, v[0].shape, 1) // SSM_GROUP
    v = list(v)
    for d in (4, 2, 1):
        hi = (piece & d) != 0
        for a in range(8):
            if a & d:
                continue
            va, vb = v[a], v[a + d]
            v[a] = jnp.where(hi, pltpu.roll(vb, d * SSM_GROUP, axis=1), va)
            v[a + d] = jnp.where(hi, vb, pltpu.roll(va, LANES - d * SSM_GROUP, axis=1))
    return v


def _s5_chunk_kernel(u_ref, w1_ref, cm_ref, al_ref, d_ref, y_ref, x_ref, s_ref, hf_ref, h_ref,
                     *, nb, nc):
    gl = GROUP_LANES
    gb = GROUPS_PER_BLOCK
    n2 = 2 * SSM_STATE
    half = CHUNK // 2
    ctile = SUBLANES * CHUNK

    def token_rows(ct, h, j):
        return pl.ds(ct * ctile + h * half + j, SUBLANES, stride=CHUNK)

    def chunk_rows(ct, b):
        return pl.ds(ct * SUBLANES * nb + b, SUBLANES, stride=nb)

    def gather_body(ct, carry):
        for b in range(nb):
            for h in range(2):
                v = [u_ref[b, token_rows(ct, h, j), :] for j in range(half)]
                w = _swap_pieces(v)
                for g in range(gb):
                    x_ref[2 * g + h, chunk_rows(ct, b), :] = w[g]
        return carry

    lax.fori_loop(0, nc // SUBLANES, gather_body, 0)

    for g in range(gb):
        x = jnp.concatenate([x_ref[2 * g], x_ref[2 * g + 1]], axis=1)
        r = jnp.dot(x.astype(BF16), w1_ref[g], preferred_element_type=F32)
        x_ref[2 * g] = r[:, :LANES]
        x_ref[2 * g + 1] = r[:, LANES:gl]
        s_ref[:, g * gl:(g + 1) * gl] = r[:, gl:]

    a_re = [al_ref[g, 0:1, :] for g in range(gb)]
    a_im = [al_ref[g, 1:2, :] for g in range(gb)]
    is_fwd = lax.broadcasted_iota(jnp.int32, (SUBLANES, n2), 1) < SSM_STATE

    def step(h, s):
        out = []
        for g in range(gb):
            hr, hi = h[:, g * gl:g * gl + n2], h[:, g * gl + n2:(g + 1) * gl]
            sr, si = s[:, g * gl:g * gl + n2], s[:, g * gl + n2:(g + 1) * gl]
            out.append(a_re[g] * hr - a_im[g] * hi + sr)
            out.append(a_re[g] * hi + a_im[g] * hr + si)
        return jnp.concatenate(out, axis=1)

    cpt = SUBLANES // nb
    n_tiles = nc // cpt
    tile_rows = lambda t: pl.ds(pl.multiple_of(t * SUBLANES, SUBLANES), SUBLANES)

    def fwd_body(t, h):
        s = s_ref[tile_rows(t), :]
        before = []
        for k in range(cpt):
            before.append(h)
            h = step(h, s[k * nb:(k + 1) * nb])
        hf_ref[tile_rows(t), :] = jnp.concatenate(before, axis=0)
        return h

    lax.fori_loop(0, n_tiles, fwd_body, jnp.zeros((nb, gb * gl), F32))

    def bwd_body(i, h):
        t = n_tiles - 1 - i
        s = s_ref[tile_rows(t), :]
        before = [None] * cpt
        for k in reversed(range(cpt)):
            before[k] = h
            h = step(h, s[k * nb:(k + 1) * nb])
        hb = jnp.concatenate(before, axis=0)
        hf = hf_ref[tile_rows(t), :]
        merged = [jnp.where(is_fwd, hf[:, k * n2:(k + 1) * n2], hb[:, k * n2:(k + 1) * n2])
                  for k in range(2 * gb)]
        h_ref[tile_rows(t), :] = jnp.concatenate(merged, axis=1)
        return h

    lax.fori_loop(0, n_tiles, bwd_body, jnp.zeros((nb, gb * gl), F32))

    for g in range(gb):
        r = jnp.dot(h_ref[:, g * gl:(g + 1) * gl].astype(BF16), cm_ref[g],
                    preferred_element_type=F32)
        x_ref[2 * g] += r[:, :LANES]
        x_ref[2 * g + 1] += r[:, LANES:]

    def scatter_body(ct, carry):
        for b in range(nb):
            for h in range(2):
                w = [x_ref[2 * g + h, chunk_rows(ct, b), :] for g in range(gb)]
                v = _swap_pieces(w)
                for j in range(half):
                    rows = token_rows(ct, h, j)
                    y_ref[b, rows, :] = v[j] + d_ref[...] * u_ref[b, rows, :]
        return carry

    lax.fori_loop(0, nc // SUBLANES, scatter_body, 0)


def _s5_chunks(u, w1, cm, al, d):
    nb, seq, width = u.shape
    nc = seq // CHUNK
    rows = nb * nc
    gb = GROUPS_PER_BLOCK
    assert SUBLANES % nb == 0 and nc % SUBLANES == 0 and width % LANES == 0
    blk = pl.BlockSpec((nb, seq, LANES), lambda i: (0, 0, i))
    return pl.pallas_call(
        functools.partial(_s5_chunk_kernel, nb=nb, nc=nc),
        grid=(width // LANES,),
        in_specs=[
            blk,
            pl.BlockSpec((gb,) + w1.shape[1:], lambda i: (i, 0, 0)),
            pl.BlockSpec((gb,) + cm.shape[1:], lambda i: (i, 0, 0)),
            pl.BlockSpec((gb,) + al.shape[1:], lambda i: (i, 0, 0)),
            pl.BlockSpec((1, LANES), lambda i: (0, i)),
        ],
        out_specs=blk,
        out_shape=jax.ShapeDtypeStruct(u.shape, F32),
        scratch_shapes=[pltpu.VMEM((2 * gb, rows, LANES), F32)]
        + [pltpu.VMEM((rows, gb * GROUP_LANES), F32)] * 3,
        compiler_params=pltpu.CompilerParams(
            dimension_semantics=("parallel",), vmem_limit_bytes=VMEM_LIMIT_BYTES),
        name="s5_chunks",
    )(u, w1, cm, al, d.reshape(1, width))


def _mix_out_kernel(x_ref, a_ref, y_ref, wglu_ref, bglu_ref, og_ref, wo_a_ref, wo_s_ref, pg_ref,
                    o_ref):
    y = y_ref[...]
    g = y * (0.5 * (1.0 + jnp.tanh(math.sqrt(2.0 / math.pi) * (y + 0.044715 * (y * y * y)))))
    z = jnp.dot(g.astype(BF16), wglu_ref[...], preferred_element_type=F32) + bglu_ref[...]
    s = g * _sigmoid(z)
    s = s * _rms_scale(s) * og_ref[...]
    mixed = (jnp.dot(a_ref[...], wo_a_ref[...], preferred_element_type=F32)
             + jnp.dot(s.astype(BF16), wo_s_ref[...], preferred_element_type=F32))
    o_ref[...] = x_ref[...] + mixed * _rms_scale(mixed) * pg_ref[...]


def _mix_out(x, a, y, w_glu, b_glu, out_g, w_out, post_g, *, tm=512):
    t, d = x.shape
    wa = a.shape[1]
    ws = y.shape[1]
    const = lambda shape: pl.BlockSpec(shape, lambda i: (0, 0))
    return pl.pallas_call(
        _mix_out_kernel,
        grid=(t // tm,),
        in_specs=[
            pl.BlockSpec((tm, d), lambda i: (i, 0)),
            pl.BlockSpec((tm, wa), lambda i: (i, 0)),
            pl.BlockSpec((tm, ws), lambda i: (i, 0)),
            const((ws, ws)), const((1, ws)), const((1, ws)),
            pl.BlockSpec((wa, d), lambda i: (0, 0)),
            pl.BlockSpec((ws, d), lambda i: (wa // ws, 0)),
            const((1, d)),
        ],
        out_specs=pl.BlockSpec((tm, d), lambda i: (i, 0)),
        out_shape=jax.ShapeDtypeStruct((t, d), F32),
        compiler_params=pltpu.CompilerParams(
            dimension_semantics=("parallel",), vmem_limit_bytes=VMEM_LIMIT_BYTES),
        name="mix_out",
    )(x, a, y, w_glu, b_glu.reshape(1, ws), out_g.reshape(1, ws), w_out, w_out,
      post_g.reshape(1, d))


def kernel(x, ff1_pre_g, ff1_w_gate, ff1_w_up, ff1_w_down, ff1_post_g, mix_pre_g, w_in, lam_q1, lam_k1, lam_q2, lam_k2, attn_head_g, ssm_lam_re, ssm_lam_im, ssm_log_dt, ssm_b_re, ssm_b_im, ssm_c_re, ssm_c_im, ssm_d, ssm_w_glu, ssm_b_glu, ssm_out_g, w_out, mix_post_g, ff2_pre_g, ff2_w_gate, ff2_w_up, ff2_w_down, ff2_post_g):
    batch, seq, d_model = x.shape
    depth = w_in.shape[0]
    ssm_width = ssm_w_glu.shape[-1]
    slopes = jnp.asarray([2.0 ** (-8.0 * (i + 1) / ATTN_HEADS) for i in range(ATTN_HEADS)], F32)
    bf = lambda w: w.astype(BF16)

    xt = x.reshape(batch * seq, d_model)
    for l in range(depth):
        xt, (w_in_bf, w_glu_bf, w_out_bf) = _ffn(
            xt, ff1_pre_g[l], bf(ff1_w_gate[l]), bf(ff1_w_up[l]), bf(ff1_w_down[l]),
            ff1_post_g[l], cast=(w_in[l], ssm_w_glu[l], w_out[l]))

        qkv, u = _in_proj(xt, mix_pre_g[l], w_in_bf, ssm_width=ssm_width)
        lam_init = 0.8 - 0.6 * math.exp(-0.3 * l)
        a, ff2_w = _attention(qkv, slopes, lam_q1[l], lam_k1[l], lam_q2[l], lam_k2[l],
                              attn_head_g[l], batch=batch, seq=seq, lam_init=lam_init,
                              cast=(ff2_w_gate[l], ff2_w_up[l], ff2_w_down[l]))

        w1, cm, al = _s5_prep(ssm_lam_re[l], ssm_lam_im[l], ssm_log_dt[l], ssm_b_re[l],
                              ssm_b_im[l], ssm_c_re[l], ssm_c_im[l])
        y = _s5_chunks(u.reshape(batch, seq, ssm_width), w1, cm, al, ssm_d[l])
        y = y.reshape(batch * seq, ssm_width)

        xt = _mix_out(xt, a, y, w_glu_bf, ssm_b_glu[l], ssm_out_g[l], w_out_bf, mix_post_g[l])

        xt, _ = _ffn(xt, ff2_pre_g[l], *ff2_w, ff2_post_g[l])
    return xt.reshape(batch, seq, d_model)
```

```python
import functools
import math

import jax
import jax.numpy as jnp
from jax import lax
from jax.experimental import pallas as pl
from jax.experimental.pallas import tpu as pltpu

F32 = jnp.float32
BF16 = jnp.bfloat16

NORM_EPS = 1e-6
ATTN_HEADS = 8
ATTN_HEAD_DIM = 64
ATTN_VALUE_DIM = 2 * ATTN_HEAD_DIM
KEY_CHUNK = 512
SSM_GROUP = 16
SSM_STATE = 64
CHUNK = 16
GROUP_LANES = CHUNK * SSM_GROUP
SUBLANES = 8
LANES = 128
BF16_ROWS = 16
GROUPS_PER_BLOCK = LANES // SSM_GROUP
PREP_GROUPS_PER_STEP = 4

VMEM_LIMIT_BYTES = 56 * 1024 * 1024


def _rms_scale(x):
    return lax.rsqrt(jnp.mean(x * x, axis=-1, keepdims=True) + NORM_EPS)


def _sigmoid(x):
    return 1.0 / (1.0 + jnp.exp(-x))


class _CastJobs:
    def __init__(self, arrays, n_steps, flat_index):
        self.flat_index = flat_index
        self.views, self.slabs = list(arrays), []
        for a in arrays:
            rows, cols = a.shape
            assert rows % BF16_ROWS == 0
            n = max(k for k in range(1, n_steps + 1) if (rows // BF16_ROWS) % k == 0)
            self.slabs.append((rows // n, cols, n))

    def _spec(self, slab):
        r, c, n = slab
        return pl.BlockSpec((r, c), lambda *g: (jnp.minimum(self.flat_index(*g), n - 1), 0))

    @property
    def in_specs(self):
        return [self._spec(s) for s in self.slabs]

    out_specs = in_specs

    @property
    def out_shapes(self):
        return [jax.ShapeDtypeStruct(v.shape, BF16) for v in self.views]

    def __len__(self):
        return len(self.views)


def _run_cast_jobs(in_refs, out_refs):
    for i_ref, o_ref in zip(in_refs, out_refs):
        o_ref[...] = i_ref[...].astype(BF16)


def _ffn_kernel(*refs, n_jobs):
    x_ref, pre_g_ref, wg_ref, wu_ref, wd_ref, post_g_ref = refs[:6]
    job_in = refs[6:6 + n_jobs]
    o_ref = refs[6 + n_jobs]
    job_out = refs[7 + n_jobs:7 + 2 * n_jobs]
    h_ref = refs[7 + 2 * n_jobs]
    j = pl.program_id(1)
    last = pl.num_programs(1) - 1

    @pl.when(j == 0)
    def _():
        x = x_ref[...]
        h_ref[...] = (x * _rms_scale(x) * pre_g_ref[...]).astype(BF16)
        o_ref[...] = jnp.zeros_like(o_ref)

    _run_cast_jobs(job_in, job_out)
    h = h_ref[...]
    gate = jnp.dot(h, wg_ref[...].astype(BF16), preferred_element_type=F32)
    up = jnp.dot(h, wu_ref[...].astype(BF16), preferred_element_type=F32)
    act = (gate * _sigmoid(gate) * up).astype(BF16)
    o_ref[...] += jnp.dot(act, wd_ref[...].astype(BF16), preferred_element_type=F32)

    @pl.when(j == last)
    def _():
        acc = o_ref[...]
        o_ref[...] = x_ref[...] + 0.5 * (acc * _rms_scale(acc) * post_g_ref[...])


def _ffn(x, pre_g, w_gate, w_up, w_down, post_g, *, cast=(), tm=512, tf=512,
         single_buffer_x=False):
    t, d = x.shape
    f = w_gate.shape[1]
    assert t % tm == 0 and f % tf == 0
    nj = f // tf
    jobs = _CastJobs(cast, (t // tm) * nj, lambda i, j: i * nj + j)
    x_mode = dict(pipeline_mode=pl.Buffered(1)) if single_buffer_x else {}
    outs = pl.pallas_call(
        functools.partial(_ffn_kernel, n_jobs=len(jobs)),
        grid=(t // tm, nj),
        in_specs=[
            pl.BlockSpec((tm, d), lambda i, j: (i, 0), **x_mode),
            pl.BlockSpec((1, d), lambda i, j: (0, 0)),
            pl.BlockSpec((d, tf), lambda i, j: (0, j)),
            pl.BlockSpec((d, tf), lambda i, j: (0, j)),
            pl.BlockSpec((tf, d), lambda i, j: (j, 0)),
            pl.BlockSpec((1, d), lambda i, j: (0, 0)),
        ] + jobs.in_specs,
        out_specs=[pl.BlockSpec((tm, d), lambda i, j: (i, 0))] + jobs.out_specs,
        out_shape=[jax.ShapeDtypeStruct((t, d), F32)] + jobs.out_shapes,
        scratch_shapes=[pltpu.VMEM((tm, d), BF16)],
        compiler_params=pltpu.CompilerParams(
            dimension_semantics=("arbitrary", "arbitrary"),
            vmem_limit_bytes=VMEM_LIMIT_BYTES),
        name="ffn",
    )(x, pre_g.reshape(1, d), w_gate, w_up, w_down, post_g.reshape(1, d), *jobs.views)
    return outs[0], list(outs[1:])


def _in_proj_kernel(x_ref, g_ref, w_ref, qkv_ref, u_ref, h_ref, *, n_qkv):
    j = pl.program_id(1)

    @pl.when(j == 0)
    def _():
        x = x_ref[...]
        h_ref[...] = (x * _rms_scale(x) * g_ref[...]).astype(BF16)

    def project():
        return jnp.dot(h_ref[...], w_ref[...].astype(BF16), preferred_element_type=F32)

    @pl.when(j < n_qkv)
    def _():
        qkv_ref[...] = project().astype(BF16)

    @pl.when(j == n_qkv)
    def _():
        u_ref[...] = project()


def _in_proj(x, g, w_in, *, ssm_width, tm=1024):
    t, d = x.shape
    n = w_in.shape[1]
    tn = ssm_width
    n_qkv = (n - ssm_width) // tn
    assert t % tm == 0 and n == (n_qkv + 1) * tn
    return pl.pallas_call(
        functools.partial(_in_proj_kernel, n_qkv=n_qkv),
        grid=(t // tm, n_qkv + 1),
        in_specs=[
            pl.BlockSpec((tm, d), lambda i, j: (i, 0)),
            pl.BlockSpec((1, d), lambda i, j: (0, 0)),
            pl.BlockSpec((d, tn), lambda i, j: (0, j)),
        ],
        out_specs=[
            pl.BlockSpec((tm, tn), lambda i, j: (i, jnp.minimum(j, n_qkv - 1))),
            pl.BlockSpec((tm, tn), lambda i, j: (i, 0)),
        ],
        out_shape=[
            jax.ShapeDtypeStruct((t, n - ssm_width), BF16),
            jax.ShapeDtypeStruct((t, ssm_width), F32),
        ],
        scratch_shapes=[pltpu.VMEM((tm, d), BF16)],
        compiler_params=pltpu.CompilerParams(
            dimension_semantics=("parallel", "arbitrary"),
            vmem_limit_bytes=VMEM_LIMIT_BYTES),
        name="in_proj",
    )(x, g.reshape(1, d), w_in)


def _attn_kernel(*refs, tq, lam_init, n_jobs):
    (slopes_ref, lq1_ref, lk1_ref, lq2_ref, lk2_ref, hg_ref, q_ref, k_ref, v_ref) = refs[:9]
    job_in = refs[9:9 + n_jobs]
    o_ref = refs[9 + n_jobs]
    job_out = refs[10 + n_jobs:10 + 2 * n_jobs]
    bias_ref, s_ref, p_ref = refs[10 + 2 * n_jobs:]
    h = pl.program_id(0)
    qi = pl.program_id(1)
    b = pl.program_id(2)
    s_len = k_ref.shape[0]

    @pl.when(b == 0)
    def _():
        qpos = qi * tq + lax.broadcasted_iota(jnp.int32, (tq, s_len), 0)
        kpos = lax.broadcasted_iota(jnp.int32, (tq, s_len), 1)
        bias_ref[...] = slopes_ref[h] * jnp.abs(qpos - kpos).astype(F32)

    _run_cast_jobs(job_in, job_out)
    lam = (jnp.exp(jnp.sum(lq1_ref[...] * lk1_ref[...], axis=-1, keepdims=True))
           - jnp.exp(jnp.sum(lq2_ref[...] * lk2_ref[...], axis=-1, keepdims=True))
           + lam_init)

    q = q_ref[...] * jnp.asarray(ATTN_HEAD_DIM ** -0.5, BF16)
    k = k_ref[...]
    v = v_ref[...]
    v_ext = jnp.concatenate([v, jnp.ones_like(v)], axis=1)
    lane = lax.broadcasted_iota(jnp.int32, q.shape, 1)
    e = v.shape[1]
    for c in range(2):
        in_map = (lane >= c * ATTN_HEAD_DIM) & (lane < (c + 1) * ATTN_HEAD_DIM)
        qc = jnp.where(in_map, q, jnp.zeros_like(q))
        for j in range(0, s_len, KEY_CHUNK):
            cols = slice(j, j + KEY_CHUNK)
            s_ref[c, :, cols] = lax.dot_general(
                qc, k[cols], (((1,), (1,)), ((), ())),
                preferred_element_type=F32) - bias_ref[:, cols]
    for c in range(2):
        s = s_ref[c]
        p_ref[c] = jnp.exp(s - jnp.max(s, axis=-1, keepdims=True)).astype(BF16)
    outs = []
    for c in range(2):
        pv = jnp.dot(p_ref[c], v_ext, preferred_element_type=F32)
        outs.append(pv[:, :e] / pv[:, e:])
    o = outs[0] - lam * outs[1]
    o_ref[...] = (o * _rms_scale(o) * hg_ref[...] * (1.0 - lam_init)).astype(o_ref.dtype)


def _attention(qkv, slopes, lq1, lk1, lq2, lk2, head_g, *, batch, seq, lam_init, cast=(), tq=512):
    t = qkv.shape[0]
    e = ATTN_VALUE_DIM
    nq = seq // tq
    nh = ATTN_HEADS
    vec = lambda a: a.reshape(1, -1).astype(F32)
    small = lambda n: pl.BlockSpec((1, n), lambda h, qi, b: (0, 0))
    jobs = _CastJobs(cast, nh * nq * batch, lambda h, qi, b: (h * nq + qi) * batch + b)
    outs = pl.pallas_call(
        functools.partial(_attn_kernel, tq=tq, lam_init=lam_init, n_jobs=len(jobs)),
        grid=(nh, nq, batch),
        in_specs=[
            pl.BlockSpec(memory_space=pltpu.SMEM),
            small(ATTN_HEAD_DIM), small(ATTN_HEAD_DIM), small(ATTN_HEAD_DIM), small(ATTN_HEAD_DIM),
            small(e),
            pl.BlockSpec((tq, e), lambda h, qi, b: (b * nq + qi, h)),
            pl.BlockSpec((seq, e), lambda h, qi, b: (b, nh + h)),
            pl.BlockSpec((seq, e), lambda h, qi, b: (b, 2 * nh + h)),
        ] + jobs.in_specs,
        out_specs=[pl.BlockSpec((tq, e), lambda h, qi, b: (b * nq + qi, h))] + jobs.out_specs,
        out_shape=[jax.ShapeDtypeStruct((t, nh * e), BF16)] + jobs.out_shapes,
        scratch_shapes=[pltpu.VMEM((tq, seq), F32), pltpu.VMEM((2, tq, seq), F32),
                        pltpu.VMEM((2, tq, seq), BF16)],
        compiler_params=pltpu.CompilerParams(
            dimension_semantics=("arbitrary", "arbitrary", "arbitrary"),
            vmem_limit_bytes=VMEM_LIMIT_BYTES),
        name="diff_attention",
    )(slopes, vec(lq1), vec(lk1), vec(lq2), vec(lk2), vec(head_g), qkv, qkv, qkv, *jobs.views)
    return outs[0], list(outs[1:])


def _discretize(lam_re, lam_im, log_dt):
    dt = jnp.exp(log_dt)
    mag = jnp.exp(lam_re * dt)
    a_re = mag * jnp.cos(lam_im * dt)
    a_im = mag * jnp.sin(lam_im * dt)
    den = lam_re * lam_re + lam_im * lam_im
    nr = a_re - 1.0
    f_re = (nr * lam_re + a_im * lam_im) / den
    f_im = (a_im * lam_re - nr * lam_im) / den
    return a_re, a_im, f_re, f_im


def _cmul(ar, ai, br, bi):
    return ar * br - ai * bi, ar * bi + ai * br


def _split_bf16(x, parts):
    out = []
    for k in range(parts):
        piece = x.astype(BF16)
        out.append(piece)
        if k + 1 < parts:
            x = x - piece.astype(F32)
    return out


def _copy_dot(x, onehot, dims):
    return sum(lax.dot_general(piece, onehot, dims, preferred_element_type=F32)
               for piece in _split_bf16(x, 3))


def _dot_3pass(a, b, dims):
    a_hi, a_lo = _split_bf16(a, 2)
    b_hi, b_lo = _split_bf16(b, 2)
    dot = lambda x, y: lax.dot_general(x, y, dims, preferred_element_type=F32)
    return dot(a_hi, b_hi) + (dot(a_hi, b_lo) + dot(a_lo, b_hi))


def _int_power(a_re, a_im, e, nbits):
    shape = jnp.broadcast_shapes(a_re.shape, e.shape)
    p_re, p_im = jnp.ones(shape, F32), jnp.zeros(shape, F32)
    s_re, s_im = a_re, a_im
    for k in range(nbits):
        bit = (e & (1 << k)) != 0
        m_re, m_im = _cmul(p_re, p_im, s_re, s_im)
        p_re, p_im = jnp.where(bit, m_re, p_re), jnp.where(bit, m_im, p_im)
        if k + 1 < nbits:
            s_re, s_im = _cmul(s_re, s_im, s_re, s_im)
    return p_re, p_im


def _shift_lanes(x, s, left):
    if s == 0:
        return x
    n = x.shape[-1]
    lane = lax.broadcasted_iota(jnp.int32, x.shape, 1)
    if left:
        return jnp.where(lane < n - s, pltpu.roll(x, n - s, axis=1), 0.0)
    return jnp.where(lane >= s, pltpu.roll(x, s, axis=1), 0.0)


def _s5_prep_kernel(lre_ref, lim_ref, ldt_ref, bre_c_ref, bim_c_ref, ctre_ref, ctim_ref,
                    btre_r_ref, btim_r_ref, w1_ref, cm_ref, al_ref):
    for i in range(w1_ref.shape[0]):
        _s5_prep_group(i, lre_ref, lim_ref, ldt_ref, bre_c_ref, bim_c_ref, ctre_ref, ctim_ref,
                       btre_r_ref, btim_r_ref, w1_ref, cm_ref, al_ref)


def _s5_prep_group(i, lre_ref, lim_ref, ldt_ref, bre_c_ref, bim_c_ref, ctre_ref, ctim_ref,
                   btre_r_ref, btim_r_ref, w1_ref, cm_ref, al_ref):
    n, p, l = SSM_STATE, SSM_GROUP, CHUNK
    nbits = l.bit_length() - 1
    contract0 = (((0,), (0,)), ((), ()))
    contract_mm = (((1,), (0,)), ((), ()))

    a_re, a_im, f_re, f_im = _discretize(lre_ref[i], lim_ref[i], ldt_ref[i])
    bt_re, bt_im = btre_r_ref[i], btim_r_ref[i]
    bb_re, bb_im = _cmul(f_re, f_im, bt_re, bt_im)
    tau = lax.broadcasted_iota(jnp.int32, (l, 2 * n), 0)
    lane = lax.broadcasted_iota(jnp.int32, (l, 2 * n), 1)
    pw_re, pw_im = _int_power(a_re, a_im, jnp.where(lane < n, l - 1 - tau, tau), nbits)
    bm_re = pw_re[:, None, :] * bb_re[None, :, :] - pw_im[:, None, :] * bb_im[None, :, :]
    bm_im = pw_re[:, None, :] * bb_im[None, :, :] + pw_im[:, None, :] * bb_re[None, :, :]
    w1_ref[i, :, GROUP_LANES:GROUP_LANES + 2 * n] = bm_re.reshape(l * p, 2 * n).astype(BF16)
    w1_ref[i, :, GROUP_LANES + 2 * n:] = bm_im.reshape(l * p, 2 * n).astype(BF16)
    al_re, al_im = a_re, a_im
    for _ in range(nbits):
        al_re, al_im = _cmul(al_re, al_im, al_re, al_im)
    al_ref[i] = jnp.concatenate([al_re, al_im], axis=0)

    quant = jnp.concatenate([a_re, a_im, f_re, f_im, jnp.zeros((SUBLANES - 4, 2 * n), F32)], axis=0)
    quant_t = quant.T

    lanes_p = lax.broadcasted_iota(jnp.int32, (p, GROUP_LANES), 1)
    rows_p = lax.broadcasted_iota(jnp.int32, (p, GROUP_LANES), 0)
    tile_p = (lanes_p % p == rows_p).astype(BF16)
    tile_lag = ((l - 1) - lanes_p // p == rows_p).astype(BF16)
    pwt_re = _copy_dot(pw_re, tile_lag, contract0)
    pwt_im = _copy_dot(pw_im, tile_lag, contract0)
    t_rows = None
    cm_re, cm_im = [], []
    for d in range(2):
        states = slice(d * n, (d + 1) * n)
        col = lambda k: quant_t[states, k:k + 1]
        a_re_c, a_im_c, f_re_c, f_im_c = col(0), col(1), col(2), col(3)
        bb_re, bb_im = _cmul(f_re_c, f_im_c, bre_c_ref[d, i], bim_c_ref[d, i])
        ct_re = _copy_dot(ctre_ref[d, i], tile_p, contract_mm)
        ct_im = _copy_dot(ctim_ref[d, i], tile_p, contract_mm)
        ac_re, ac_im = _cmul(pwt_re[states], pwt_im[states], ct_re, ct_im)
        taps = (_dot_3pass(bb_re, ac_re, contract0)
                - _dot_3pass(bb_im, ac_im, contract0))
        rows = [_shift_lanes(taps, p * (tp if d == 0 else l - 1 - tp), left=(d == 1))
                for tp in range(l)]
        rows = jnp.concatenate(rows, axis=0)
        t_rows = rows if t_rows is None else t_rows + rows
        r_re, r_im = _cmul(a_re_c, a_im_c, ac_re, ac_im)
        cm_re.append(r_re)
        cm_im.append(-r_im)
    w1_ref[i, :, 0:GROUP_LANES] = t_rows.astype(BF16)
    cm_ref[i] = jnp.concatenate(cm_re + cm_im, axis=0).astype(BF16)


def _s5_prep(lam_re, lam_im, log_dt, b_re, b_im, c_re, c_im):
    _, g, n = lam_re.shape
    p = b_re.shape[-1]
    row = lambda a: jnp.concatenate([a[0], a[1]], axis=-1)[:, None, :]
    ldt_r = row(jnp.broadcast_to(log_dt[:, :, None], (2, g, n)))
    bt = lambda b: jnp.concatenate([jnp.swapaxes(b[0], 1, 2), jnp.swapaxes(b[1], 1, 2)], axis=-1)
    ct = lambda c: jnp.swapaxes(c, 2, 3)

    gb = PREP_GROUPS_PER_STEP
    assert g % gb == 0
    spec_c = pl.BlockSpec((2, gb, n, p), lambda i: (0, i, 0, 0))
    spec_r = lambda rows: pl.BlockSpec((gb, rows, 2 * n), lambda i: (i, 0, 0))
    return pl.pallas_call(
        _s5_prep_kernel,
        grid=(g // gb,),
        in_specs=[spec_r(1), spec_r(1), spec_r(1), spec_c, spec_c, spec_c, spec_c,
                  spec_r(p), spec_r(p)],
        out_specs=[
            pl.BlockSpec((gb, GROUP_LANES, GROUP_LANES + 4 * n), lambda i: (i, 0, 0)),
            pl.BlockSpec((gb, 4 * n, GROUP_LANES), lambda i: (i, 0, 0)),
            pl.BlockSpec((gb, 2, 2 * n), lambda i: (i, 0, 0)),
        ],
        out_shape=[
            jax.ShapeDtypeStruct((g, GROUP_LANES, GROUP_LANES + 4 * n), BF16),
            jax.ShapeDtypeStruct((g, 4 * n, GROUP_LANES), BF16),
            jax.ShapeDtypeStruct((g, 2, 2 * n), F32),
        ],
        compiler_params=pltpu.CompilerParams(dimension_semantics=("parallel",)),
        name="s5_prep",
    )(row(lam_re), row(lam_im), ldt_r, b_re, b_im, ct(c_re), ct(c_im), bt(b_re), bt(b_im))


def _swap_pieces(v):
    piece = lax.broadcasted_iota(jnp.int32, v[0].shape, 1) // SSM_GROUP
    v = list(v)
    for d in (4, 2, 1):
        hi = (piece & d) != 0
        for a in range(8):
            if a & d:
                continue
            va, vb = v[a], v[a + d]
            v[a] = jnp.where(hi, pltpu.roll(vb, d * SSM_GROUP, axis=1), va)
            v[a + d] = jnp.where(hi, vb, pltpu.roll(va, LANES - d * SSM_GROUP, axis=1))
    return v


def _s5_chunk_kernel(u_ref, w1_ref, cm_ref, al_ref, d_ref, y_ref, x_ref, s_ref, hf_ref, h_ref,
                     *, nb, nc):
    gl = GROUP_LANES
    gb = GROUPS_PER_BLOCK
    n2 = 2 * SSM_STATE
    half = CHUNK // 2
    ctile = SUBLANES * CHUNK

    def token_rows(ct, h, j):
        return pl.ds(ct * ctile + h * half + j, SUBLANES, stride=CHUNK)

    def chunk_rows(ct, b):
        return pl.ds(ct * SUBLANES * nb + b, SUBLANES, stride=nb)

    def gather_body(ct, carry):
        for b in range(nb):
            for h in range(2):
                v = [u_ref[b, token_rows(ct, h, j), :] for j in range(half)]
                w = _swap_pieces(v)
                for g in range(gb):
                    x_ref[2 * g + h, chunk_rows(ct, b), :] = w[g]
        return carry

    lax.fori_loop(0, nc // SUBLANES, gather_body, 0)

    for g in range(gb):
        x = jnp.concatenate([x_ref[2 * g], x_ref[2 * g + 1]], axis=1)
        r = jnp.dot(x.astype(BF16), w1_ref[g], preferred_element_type=F32)
        x_ref[2 * g] = r[:, :LANES]
        x_ref[2 * g + 1] = r[:, LANES:gl]
        s_ref[:, g * gl:(g + 1) * gl] = r[:, gl:]

    a_re = [al_ref[g, 0:1, :] for g in range(gb)]
    a_im = [al_ref[g, 1:2, :] for g in range(gb)]
    is_fwd = lax.broadcasted_iota(jnp.int32, (SUBLANES, n2), 1) < SSM_STATE

    def step(h, s):
        out = []
        for g in range(gb):
            hr, hi = h[:, g * gl:g * gl + n2], h[:, g * gl + n2:(g + 1) * gl]
            sr, si = s[:, g * gl:g * gl + n2], s[:, g * gl + n2:(g + 1) * gl]
            out.append(a_re[g] * hr - a_im[g] * hi + sr)
            out.append(a_re[g] * hi + a_im[g] * hr + si)
        return jnp.concatenate(out, axis=1)

    cpt = SUBLANES // nb
    n_tiles = nc // cpt
    tile_rows = lambda t: pl.ds(pl.multiple_of(t * SUBLANES, SUBLANES), SUBLANES)

    def fwd_body(t, h):
        s = s_ref[tile_rows(t), :]
        before = []
        for k in range(cpt):
            before.append(h)
            h = step(h, s[k * nb:(k + 1) * nb])
        hf_ref[tile_rows(t), :] = jnp.concatenate(before, axis=0)
        return h

    lax.fori_loop(0, n_tiles, fwd_body, jnp.zeros((nb, gb * gl), F32))

    def bwd_body(i, h):
        t = n_tiles - 1 - i
        s = s_ref[tile_rows(t), :]
        before = [None] * cpt
        for k in reversed(range(cpt)):
            before[k] = h
            h = step(h, s[k * nb:(k + 1) * nb])
        hb = jnp.concatenate(before, axis=0)
        hf = hf_ref[tile_rows(t), :]
        merged = [jnp.where(is_fwd, hf[:, k * n2:(k + 1) * n2], hb[:, k * n2:(k + 1) * n2])
                  for k in range(2 * gb)]
        h_ref[tile_rows(t), :] = jnp.concatenate(merged, axis=1)
        return h

    lax.fori_loop(0, n_tiles, bwd_body, jnp.zeros((nb, gb * gl), F32))

    for g in range(gb):
        r = jnp.dot(h_ref[:, g * gl:(g + 1) * gl].astype(BF16), cm_ref[g],
                    preferred_element_type=F32)
        x_ref[2 * g] += r[:, :LANES]
        x_ref[2 * g + 1] += r[:, LANES:]

    def scatter_body(ct, carry):
        for b in range(nb):
            for h in range(2):
                w = [x_ref[2 * g + h, chunk_rows(ct, b), :] for g in range(gb)]
                v = _swap_pieces(w)
                for j in range(half):
                    rows = token_rows(ct, h, j)
                    y_ref[b, rows, :] = v[j] + d_ref[...] * u_ref[b, rows, :]
        return carry

    lax.fori_loop(0, nc // SUBLANES, scatter_body, 0)


def _s5_chunks(u, w1, cm, al, d):
    nb, seq, width = u.shape
    nc = seq // CHUNK
    rows = nb * nc
    gb = GROUPS_PER_BLOCK
    assert SUBLANES % nb == 0 and nc % SUBLANES == 0 and width % LANES == 0
    blk = pl.BlockSpec((nb, seq, LANES), lambda i: (0, 0, i))
    return pl.pallas_call(
        functools.partial(_s5_chunk_kernel, nb=nb, nc=nc),
        grid=(width // LANES,),
        in_specs=[
            blk,
            pl.BlockSpec((gb,) + w1.shape[1:], lambda i: (i, 0, 0)),
            pl.BlockSpec((gb,) + cm.shape[1:], lambda i: (i, 0, 0)),
            pl.BlockSpec((gb,) + al.shape[1:], lambda i: (i, 0, 0)),
            pl.BlockSpec((1, LANES), lambda i: (0, i)),
        ],
        out_specs=blk,
        out_shape=jax.ShapeDtypeStruct(u.shape, F32),
        scratch_shapes=[pltpu.VMEM((2 * gb, rows, LANES), F32)]
        + [pltpu.VMEM((rows, gb * GROUP_LANES), F32)] * 3,
        compiler_params=pltpu.CompilerParams(
            dimension_semantics=("parallel",), vmem_limit_bytes=VMEM_LIMIT_BYTES),
        name="s5_chunks",
    )(u, w1, cm, al, d.reshape(1, width))


def _mix_out_kernel(x_ref, a_ref, y_ref, wglu_ref, bglu_ref, og_ref, wo_a_ref, wo_s_ref, pg_ref,
                    o_ref):
    y = y_ref[...]
    g = y * (0.5 * (1.0 + jnp.tanh(math.sqrt(2.0 / math.pi) * (y + 0.044715 * (y * y * y)))))
    z = jnp.dot(g.astype(BF16), wglu_ref[...], preferred_element_type=F32) + bglu_ref[...]
    s = g * _sigmoid(z)
    s = s * _rms_scale(s) * og_ref[...]
    mixed = (jnp.dot(a_ref[...], wo_a_ref[...], preferred_element_type=F32)
             + jnp.dot(s.astype(BF16), wo_s_ref[...], preferred_element_type=F32))
    o_ref[...] = x_ref[...] + mixed * _rms_scale(mixed) * pg_ref[...]


def _mix_out(x, a, y, w_glu, b_glu, out_g, w_out, post_g, *, tm=512):
    t, d = x.shape
    wa = a.shape[1]
    ws = y.shape[1]
    const = lambda shape: pl.BlockSpec(shape, lambda i: (0, 0))
    return pl.pallas_call(
        _mix_out_kernel,
        grid=(t // tm,),
        in_specs=[
            pl.BlockSpec((tm, d), lambda i: (i, 0)),
            pl.BlockSpec((tm, wa), lambda i: (i, 0)),
            pl.BlockSpec((tm, ws), lambda i: (i, 0)),
            const((ws, ws)), const((1, ws)), const((1, ws)),
            pl.BlockSpec((wa, d), lambda i: (0, 0)),
            pl.BlockSpec((ws, d), lambda i: (wa // ws, 0)),
            const((1, d)),
        ],
        out_specs=pl.BlockSpec((tm, d), lambda i: (i, 0)),
        out_shape=jax.ShapeDtypeStruct((t, d), F32),
        compiler_params=pltpu.CompilerParams(
            dimension_semantics=("parallel",), vmem_limit_bytes=VMEM_LIMIT_BYTES),
        name="mix_out",
    )(x, a, y, w_glu, b_glu.reshape(1, ws), out_g.reshape(1, ws), w_out, w_out,
      post_g.reshape(1, d))


def kernel(x, ff1_pre_g, ff1_w_gate, ff1_w_up, ff1_w_down, ff1_post_g, mix_pre_g, w_in, lam_q1, lam_k1, lam_q2, lam_k2, attn_head_g, ssm_lam_re, ssm_lam_im, ssm_log_dt, ssm_b_re, ssm_b_im, ssm_c_re, ssm_c_im, ssm_d, ssm_w_glu, ssm_b_glu, ssm_out_g, w_out, mix_post_g, ff2_pre_g, ff2_w_gate, ff2_w_up, ff2_w_down, ff2_post_g):
    batch, seq, d_model = x.shape
    depth = w_in.shape[0]
    ssm_width = ssm_w_glu.shape[-1]
    slopes = jnp.asarray([2.0 ** (-8.0 * (i + 1) / ATTN_HEADS) for i in range(ATTN_HEADS)], F32)
    bf = lambda w: w.astype(BF16)

    xt = x.reshape(batch * seq, d_model)
    for l in range(depth):
        xt, (w_in_bf, w_glu_bf, w_out_bf) = _ffn(
            xt, ff1_pre_g[l], ff1_w_gate[l], ff1_w_up[l], ff1_w_down[l], ff1_post_g[l],
            cast=(w_in[l], ssm_w_glu[l], w_out[l]), tm=1024, tf=256, single_buffer_x=True)

        qkv, u = _in_proj(xt, mix_pre_g[l], w_in_bf, ssm_width=ssm_width)
        lam_init = 0.8 - 0.6 * math.exp(-0.3 * l)
        a, ff2_w = _attention(qkv, slopes, lam_q1[l], lam_k1[l], lam_q2[l], lam_k2[l],
                              attn_head_g[l], batch=batch, seq=seq, lam_init=lam_init,
                              cast=(ff2_w_gate[l], ff2_w_up[l], ff2_w_down[l]))

        w1, cm, al = _s5_prep(ssm_lam_re[l], ssm_lam_im[l], ssm_log_dt[l], ssm_b_re[l],
                              ssm_b_im[l], ssm_c_re[l], ssm_c_im[l])
        y = _s5_chunks(u.reshape(batch, seq, ssm_width), w1, cm, al, ssm_d[l])
        y = y.reshape(batch * seq, ssm_width)

        xt = _mix_out(xt, a, y, w_glu_bf, ssm_b_glu[l], ssm_out_g[l], w_out_bf, mix_post_g[l])

        xt, _ = _ffn(xt, ff2_pre_g[l], *ff2_w, ff2_post_g[l])
    return xt.reshape(batch, seq, d_model)
```

```python
import functools
import math

import jax
import jax.numpy as jnp
from jax import lax
from jax.experimental import pallas as pl
from jax.experimental.pallas import tpu as pltpu

F32 = jnp.float32
BF16 = jnp.bfloat16

NORM_EPS = 1e-6
ATTN_HEADS = 8
ATTN_HEAD_DIM = 64
ATTN_VALUE_DIM = 2 * ATTN_HEAD_DIM
POS_SPLIT = 16
SSM_GROUP = 16
SSM_STATE = 64
CHUNK = 16
GROUP_LANES = CHUNK * SSM_GROUP
SUBLANES = 8
LANES = 128
BF16_ROWS = 16
GROUPS_PER_BLOCK = LANES // SSM_GROUP
PREP_GROUPS_PER_STEP = 4

VMEM_LIMIT_BYTES = 56 * 1024 * 1024


def _rms_scale(x):
    return lax.rsqrt(jnp.mean(x * x, axis=-1, keepdims=True) + NORM_EPS)


def _sigmoid(x):
    return 1.0 / (1.0 + jnp.exp(-x))


class _CastJobs:
    def __init__(self, arrays, n_steps, flat_index):
        self.flat_index = flat_index
        self.views, self.slabs = list(arrays), []
        for a in arrays:
            rows, cols = a.shape
            assert rows % BF16_ROWS == 0
            n = max(k for k in range(1, n_steps + 1) if (rows // BF16_ROWS) % k == 0)
            self.slabs.append((rows // n, cols, n))

    def _spec(self, slab):
        r, c, n = slab
        return pl.BlockSpec((r, c), lambda *g: (jnp.minimum(self.flat_index(*g), n - 1), 0))

    @property
    def in_specs(self):
        return [self._spec(s) for s in self.slabs]

    out_specs = in_specs

    @property
    def out_shapes(self):
        return [jax.ShapeDtypeStruct(v.shape, BF16) for v in self.views]

    def __len__(self):
        return len(self.views)


def _run_cast_jobs(in_refs, out_refs):
    for i_ref, o_ref in zip(in_refs, out_refs):
        o_ref[...] = i_ref[...].astype(BF16)


def _ffn_kernel(*refs, n_jobs):
    x_ref, pre_g_ref, wg_ref, wu_ref, wd_ref, post_g_ref = refs[:6]
    job_in = refs[6:6 + n_jobs]
    o_ref = refs[6 + n_jobs]
    job_out = refs[7 + n_jobs:7 + 2 * n_jobs]
    h_ref = refs[7 + 2 * n_jobs]
    j = pl.program_id(1)
    last = pl.num_programs(1) - 1

    @pl.when(j == 0)
    def _():
        x = x_ref[...]
        h_ref[...] = (x * _rms_scale(x) * pre_g_ref[...]).astype(BF16)
        o_ref[...] = jnp.zeros_like(o_ref)

    _run_cast_jobs(job_in, job_out)
    h = h_ref[...]
    gate = jnp.dot(h, wg_ref[...].astype(BF16), preferred_element_type=F32)
    up = jnp.dot(h, wu_ref[...].astype(BF16), preferred_element_type=F32)
    act = (gate * _sigmoid(gate) * up).astype(BF16)
    o_ref[...] += jnp.dot(act, wd_ref[...].astype(BF16), preferred_element_type=F32)

    @pl.when(j == last)
    def _():
        acc = o_ref[...]
        o_ref[...] = x_ref[...] + 0.5 * (acc * _rms_scale(acc) * post_g_ref[...])


def _ffn(x, pre_g, w_gate, w_up, w_down, post_g, *, cast=(), tm=512, tf=512,
         single_buffer_x=False):
    t, d = x.shape
    f = w_gate.shape[1]
    assert t % tm == 0 and f % tf == 0
    nj = f // tf
    jobs = _CastJobs(cast, (t // tm) * nj, lambda i, j: i * nj + j)
    x_mode = dict(pipeline_mode=pl.Buffered(1)) if single_buffer_x else {}
    outs = pl.pallas_call(
        functools.partial(_ffn_kernel, n_jobs=len(jobs)),
        grid=(t // tm, nj),
        in_specs=[
            pl.BlockSpec((tm, d), lambda i, j: (i, 0), **x_mode),
            pl.BlockSpec((1, d), lambda i, j: (0, 0)),
            pl.BlockSpec((d, tf), lambda i, j: (0, j)),
            pl.BlockSpec((d, tf), lambda i, j: (0, j)),
            pl.BlockSpec((tf, d), lambda i, j: (j, 0)),
            pl.BlockSpec((1, d), lambda i, j: (0, 0)),
        ] + jobs.in_specs,
        out_specs=[pl.BlockSpec((tm, d), lambda i, j: (i, 0))] + jobs.out_specs,
        out_shape=[jax.ShapeDtypeStruct((t, d), F32)] + jobs.out_shapes,
        scratch_shapes=[pltpu.VMEM((tm, d), BF16)],
        compiler_params=pltpu.CompilerParams(
            dimension_semantics=("arbitrary", "arbitrary"),
            vmem_limit_bytes=VMEM_LIMIT_BYTES),
        name="ffn",
    )(x, pre_g.reshape(1, d), w_gate, w_up, w_down, post_g.reshape(1, d), *jobs.views)
    return outs[0], list(outs[1:])


def _in_proj_kernel(x_ref, g_ref, w_ref, qkv_ref, u_ref, h_ref, *, n_qkv):
    j = pl.program_id(1)

    @pl.when(j == 0)
    def _():
        x = x_ref[...]
        h_ref[...] = (x * _rms_scale(x) * g_ref[...]).astype(BF16)

    def project():
        return jnp.dot(h_ref[...], w_ref[...].astype(BF16), preferred_element_type=F32)

    @pl.when(j < n_qkv)
    def _():
        qkv_ref[...] = project().astype(BF16)

    @pl.when(j == n_qkv)
    def _():
        u_ref[...] = project()


def _in_proj(x, g, w_in, *, ssm_width, tm=1024):
    t, d = x.shape
    n = w_in.shape[1]
    tn = ssm_width
    n_qkv = (n - ssm_width) // tn
    assert t % tm == 0 and n == (n_qkv + 1) * tn
    return pl.pallas_call(
        functools.partial(_in_proj_kernel, n_qkv=n_qkv),
        grid=(t // tm, n_qkv + 1),
        in_specs=[
            pl.BlockSpec((tm, d), lambda i, j: (i, 0)),
            pl.BlockSpec((1, d), lambda i, j: (0, 0)),
            pl.BlockSpec((d, tn), lambda i, j: (0, j)),
        ],
        out_specs=[
            pl.BlockSpec((tm, tn), lambda i, j: (i, jnp.minimum(j, n_qkv - 1))),
            pl.BlockSpec((tm, tn), lambda i, j: (i, 0)),
        ],
        out_shape=[
            jax.ShapeDtypeStruct((t, n - ssm_width), BF16),
            jax.ShapeDtypeStruct((t, ssm_width), F32),
        ],
        scratch_shapes=[pltpu.VMEM((tm, d), BF16)],
        compiler_params=pltpu.CompilerParams(
            dimension_semantics=("parallel", "arbitrary"),
            vmem_limit_bytes=VMEM_LIMIT_BYTES),
        name="in_proj",
    )(x, g.reshape(1, d), w_in)


def _attn_kernel(*refs, tq, nq, lam_init, n_jobs):
    slopes_ref, lq1_ref, lk1_ref, lq2_ref, lk2_ref, hg_ref, q_ref = refs[:7]
    k_refs = refs[7:7 + nq]
    v_refs = refs[7 + nq:7 + 2 * nq]
    n_in = 7 + 2 * nq
    job_in = refs[n_in:n_in + n_jobs]
    o_ref = refs[n_in + n_jobs]
    job_out = refs[n_in + n_jobs + 1:n_in + 2 * n_jobs + 1]
    qf_ref, kf_ref, bias_ref, s_ref, p_ref = refs[n_in + 2 * n_jobs + 1:]
    h = pl.program_id(0)
    qi = pl.program_id(1)
    b = pl.program_id(2)
    e = q_ref.shape[1]

    @pl.when(b == 0)
    def _():
        slope = slopes_ref[h]
        lane = lax.broadcasted_iota(jnp.int32, (tq, e), 1)
        row = lax.broadcasted_iota(jnp.int32, (tq, e), 0)
        row_hi = (slope * POS_SPLIT) * (row // POS_SPLIT).astype(F32)
        row_lo = slope * (row % POS_SPLIT).astype(F32)
        q_base = jnp.where(lane == 0, row_hi, jnp.where(lane == 1, row_lo,
                           jnp.where(lane < 4, 1.0, 0.0)))
        k_base = jnp.where(lane < 2, -1.0, jnp.where(lane == 2, row_hi,
                           jnp.where(lane == 3, row_lo, 0.0)))
        hi_q = (lane == 0).astype(F32)
        hi_k = (lane == 2).astype(F32)
        qf_ref[...] = (q_base + (slope * tq) * qi.astype(F32) * hi_q).astype(BF16)
        for slot in range(nq):
            chunk = lax.rem(qi + slot, nq)
            sign = jnp.where(chunk < qi, 1.0, jnp.where(chunk > qi, -1.0, 0.0))
            kf = sign * k_base + (sign * (slope * tq) * chunk.astype(F32)) * hi_k
            kf_ref[slot * tq:(slot + 1) * tq, :] = kf.astype(BF16)
        col = lax.broadcasted_iota(jnp.int32, (tq, tq), 1)
        bias_ref[...] = slope * jnp.abs(lax.broadcasted_iota(jnp.int32, (tq, tq), 0)
                                        - col).astype(F32)

    _run_cast_jobs(job_in, job_out)
    lam = (jnp.exp(jnp.sum(lq1_ref[...] * lk1_ref[...], axis=-1, keepdims=True))
           - jnp.exp(jnp.sum(lq2_ref[...] * lk2_ref[...], axis=-1, keepdims=True))
           + lam_init)

    q = q_ref[...] * jnp.asarray(ATTN_HEAD_DIM ** -0.5, BF16)
    k_aug = jnp.concatenate([jnp.concatenate([r[...] for r in k_refs], axis=0), kf_ref[...]],
                            axis=1)
    v = jnp.concatenate([r[...] for r in v_refs], axis=0)
    v_ext = jnp.concatenate([v, jnp.ones_like(v)], axis=1)
    lane = lax.broadcasted_iota(jnp.int32, q.shape, 1)
    nt = (((1,), (1,)), ((), ()))
    for c in range(2):
        in_map = (lane >= c * ATTN_HEAD_DIM) & (lane < (c + 1) * ATTN_HEAD_DIM)
        q_aug = jnp.concatenate([jnp.where(in_map, q, jnp.zeros_like(q)), qf_ref[...]], axis=1)
        s_ref[c, :, :tq] = lax.dot_general(q_aug, k_aug[:tq], nt,
                                           preferred_element_type=F32) - bias_ref[...]
        s_ref[c, :, tq:] = lax.dot_general(q_aug, k_aug[tq:], nt, preferred_element_type=F32)
        s = s_ref[c]
        p_ref[c] = jnp.exp(s - jnp.max(s, axis=-1, keepdims=True)).astype(BF16)
    outs = []
    for c in range(2):
        pv = jnp.dot(p_ref[c], v_ext, preferred_element_type=F32)
        outs.append(pv[:, :e] / pv[:, e:])
    o = outs[0] - lam * outs[1]
    o_ref[...] = (o * _rms_scale(o) * hg_ref[...] * (1.0 - lam_init)).astype(o_ref.dtype)


def _attention(qkv, slopes, lq1, lk1, lq2, lk2, head_g, *, batch, seq, lam_init, cast=(), tq=512):
    t = qkv.shape[0]
    e = ATTN_VALUE_DIM
    nq = seq // tq
    nh = ATTN_HEADS
    vec = lambda a: a.reshape(1, -1).astype(F32)
    small = lambda n: pl.BlockSpec((1, n), lambda h, qi, b: (0, 0))
    jobs = _CastJobs(cast, nh * nq * batch, lambda h, qi, b: (h * nq + qi) * batch + b)

    def chunk_spec(slot, col0):
        return pl.BlockSpec((tq, e), lambda h, qi, b: (b * nq + lax.rem(qi + slot, nq), col0 + h))

    outs = pl.pallas_call(
        functools.partial(_attn_kernel, tq=tq, nq=nq, lam_init=lam_init, n_jobs=len(jobs)),
        grid=(nh, nq, batch),
        in_specs=[
            pl.BlockSpec(memory_space=pltpu.SMEM),
            small(ATTN_HEAD_DIM), small(ATTN_HEAD_DIM), small(ATTN_HEAD_DIM), small(ATTN_HEAD_DIM),
            small(e),
            pl.BlockSpec((tq, e), lambda h, qi, b: (b * nq + qi, h)),
        ] + [chunk_spec(s, nh) for s in range(nq)] + [chunk_spec(s, 2 * nh) for s in range(nq)]
        + jobs.in_specs,
        out_specs=[pl.BlockSpec((tq, e), lambda h, qi, b: (b * nq + qi, h))] + jobs.out_specs,
        out_shape=[jax.ShapeDtypeStruct((t, nh * e), BF16)] + jobs.out_shapes,
        scratch_shapes=[pltpu.VMEM((tq, e), BF16), pltpu.VMEM((seq, e), BF16),
                        pltpu.VMEM((tq, tq), F32), pltpu.VMEM((2, tq, seq), F32),
                        pltpu.VMEM((2, tq, seq), BF16)],
        compiler_params=pltpu.CompilerParams(
            dimension_semantics=("arbitrary", "arbitrary", "arbitrary"),
            vmem_limit_bytes=VMEM_LIMIT_BYTES),
        name="diff_attention",
    )(slopes, vec(lq1), vec(lk1), vec(lq2), vec(lk2), vec(head_g), qkv,
      *([qkv] * (2 * nq)), *jobs.views)
    return outs[0], list(outs[1:])


def _discretize(lam_re, lam_im, log_dt):
    dt = jnp.exp(log_dt)
    mag = jnp.exp(lam_re * dt)
    a_re = mag * jnp.cos(lam_im * dt)
    a_im = mag * jnp.sin(lam_im * dt)
    den = lam_re * lam_re + lam_im * lam_im
    nr = a_re - 1.0
    f_re = (nr * lam_re + a_im * lam_im) / den
    f_im = (a_im * lam_re - nr * lam_im) / den
    return a_re, a_im, f_re, f_im


def _cmul(ar, ai, br, bi):
    return ar * br - ai * bi, ar * bi + ai * br


def _split_bf16(x, parts):
    out = []
    for k in range(parts):
        piece = x.astype(BF16)
        out.append(piece)
        if k + 1 < parts:
            x = x - piece.astype(F32)
    return out


def _copy_dot(x, onehot, dims):
    return sum(lax.dot_general(piece, onehot, dims, preferred_element_type=F32)
               for piece in _split_bf16(x, 3))


def _dot_3pass(a, b, dims):
    a_hi, a_lo = _split_bf16(a, 2)
    b_hi, b_lo = _split_bf16(b, 2)
    dot = lambda x, y: lax.dot_general(x, y, dims, preferred_element_type=F32)
    return dot(a_hi, b_hi) + (dot(a_hi, b_lo) + dot(a_lo, b_hi))


def _int_power(a_re, a_im, e, nbits):
    shape = jnp.broadcast_shapes(a_re.shape, e.shape)
    p_re, p_im = jnp.ones(shape, F32), jnp.zeros(shape, F32)
    s_re, s_im = a_re, a_im
    for k in range(nbits):
        bit = (e & (1 << k)) != 0
        m_re, m_im = _cmul(p_re, p_im, s_re, s_im)
        p_re, p_im = jnp.where(bit, m_re, p_re), jnp.where(bit, m_im, p_im)
        if k + 1 < nbits:
            s_re, s_im = _cmul(s_re, s_im, s_re, s_im)
    return p_re, p_im


def _shift_lanes(x, s, left):
    if s == 0:
        return x
    n = x.shape[-1]
    lane = lax.broadcasted_iota(jnp.int32, x.shape, 1)
    if left:
        return jnp.where(lane < n - s, pltpu.roll(x, n - s, axis=1), 0.0)
    return jnp.where(lane >= s, pltpu.roll(x, s, axis=1), 0.0)


def _s5_prep_kernel(lre_ref, lim_ref, ldt_ref, bre_c_ref, bim_c_ref, ctre_ref, ctim_ref,
                    btre_r_ref, btim_r_ref, w1_ref, cm_ref, al_ref):
    for i in range(w1_ref.shape[0]):
        _s5_prep_group(i, lre_ref, lim_ref, ldt_ref, bre_c_ref, bim_c_ref, ctre_ref, ctim_ref,
                       btre_r_ref, btim_r_ref, w1_ref, cm_ref, al_ref)


def _s5_prep_group(i, lre_ref, lim_ref, ldt_ref, bre_c_ref, bim_c_ref, ctre_ref, ctim_ref,
                   btre_r_ref, btim_r_ref, w1_ref, cm_ref, al_ref):
    n, p, l = SSM_STATE, SSM_GROUP, CHUNK
    nbits = l.bit_length() - 1
    contract0 = (((0,), (0,)), ((), ()))
    contract_mm = (((1,), (0,)), ((), ()))

    a_re, a_im, f_re, f_im = _discretize(lre_ref[i], lim_ref[i], ldt_ref[i])
    bt_re, bt_im = btre_r_ref[i], btim_r_ref[i]
    bb_re, bb_im = _cmul(f_re, f_im, bt_re, bt_im)
    tau = lax.broadcasted_iota(jnp.int32, (l, 2 * n), 0)
    lane = lax.broadcasted_iota(jnp.int32, (l, 2 * n), 1)
    pw_re, pw_im = _int_power(a_re, a_im, jnp.where(lane < n, l - 1 - tau, tau), nbits)
    bm_re = pw_re[:, None, :] * bb_re[None, :, :] - pw_im[:, None, :] * bb_im[None, :, :]
    bm_im = pw_re[:, None, :] * bb_im[None, :, :] + pw_im[:, None, :] * bb_re[None, :, :]
    w1_ref[i, :, GROUP_LANES:GROUP_LANES + 2 * n] = bm_re.reshape(l * p, 2 * n).astype(BF16)
    w1_ref[i, :, GROUP_LANES + 2 * n:] = bm_im.reshape(l * p, 2 * n).astype(BF16)
    al_re, al_im = a_re, a_im
    for _ in range(nbits):
        al_re, al_im = _cmul(al_re, al_im, al_re, al_im)
    al_ref[i] = jnp.concatenate([al_re, al_im], axis=0)

    quant = jnp.concatenate([a_re, a_im, f_re, f_im, jnp.zeros((SUBLANES - 4, 2 * n), F32)], axis=0)
    quant_t = quant.T

    lanes_p = lax.broadcasted_iota(jnp.int32, (p, GROUP_LANES), 1)
    rows_p = lax.broadcasted_iota(jnp.int32, (p, GROUP_LANES), 0)
    tile_p = (lanes_p % p == rows_p).astype(BF16)
    tile_lag = ((l - 1) - lanes_p // p == rows_p).astype(BF16)
    pwt_re = _copy_dot(pw_re, tile_lag, contract0)
    pwt_im = _copy_dot(pw_im, tile_lag, contract0)
    t_rows = None
    cm_re, cm_im = [], []
    for d in range(2):
        states = slice(d * n, (d + 1) * n)
        col = lambda k: quant_t[states, k:k + 1]
        a_re_c, a_im_c, f_re_c, f_im_c = col(0), col(1), col(2), col(3)
        bb_re, bb_im = _cmul(f_re_c, f_im_c, bre_c_ref[d, i], bim_c_ref[d, i])
        ct_re = _copy_dot(ctre_ref[d, i], tile_p, contract_mm)
        ct_im = _copy_dot(ctim_ref[d, i], tile_p, contract_mm)
        ac_re, ac_im = _cmul(pwt_re[states], pwt_im[states], ct_re, ct_im)
        taps = (_dot_3pass(bb_re, ac_re, contract0)
                - _dot_3pass(bb_im, ac_im, contract0))
        rows = [_shift_lanes(taps, p * (tp if d == 0 else l - 1 - tp), left=(d == 1))
                for tp in range(l)]
        rows = jnp.concatenate(rows, axis=0)
        t_rows = rows if t_rows is None else t_rows + rows
        r_re, r_im = _cmul(a_re_c, a_im_c, ac_re, ac_im)
        cm_re.append(r_re)
        cm_im.append(-r_im)
    w1_ref[i, :, 0:GROUP_LANES] = t_rows.astype(BF16)
    cm_ref[i] = jnp.concatenate(cm_re + cm_im, axis=0).astype(BF16)


def _s5_prep(lam_re, lam_im, log_dt, b_re, b_im, c_re, c_im):
    _, g, n = lam_re.shape
    p = b_re.shape[-1]
    row = lambda a: jnp.concatenate([a[0], a[1]], axis=-1)[:, None, :]
    ldt_r = row(jnp.broadcast_to(log_dt[:, :, None], (2, g, n)))
    bt = lambda b: jnp.concatenate([jnp.swapaxes(b[0], 1, 2), jnp.swapaxes(b[1], 1, 2)], axis=-1)
    ct = lambda c: jnp.swapaxes(c, 2, 3)

    gb = PREP_GROUPS_PER_STEP
    assert g % gb == 0
    spec_c = pl.BlockSpec((2, gb, n, p), lambda i: (0, i, 0, 0))
    spec_r = lambda rows: pl.BlockSpec((gb, rows, 2 * n), lambda i: (i, 0, 0))
    return pl.pallas_call(
        _s5_prep_kernel,
        grid=(g // gb,),
        in_specs=[spec_r(1), spec_r(1), spec_r(1), spec_c, spec_c, spec_c, spec_c,
                  spec_r(p), spec_r(p)],
        out_specs=[
            pl.BlockSpec((gb, GROUP_LANES, GROUP_LANES + 4 * n), lambda i: (i, 0, 0)),
            pl.BlockSpec((gb, 4 * n, GROUP_LANES), lambda i: (i, 0, 0)),
            pl.BlockSpec((gb, 2, 2 * n), lambda i: (i, 0, 0)),
        ],
        out_shape=[
            jax.ShapeDtypeStruct((g, GROUP_LANES, GROUP_LANES + 4 * n), BF16),
            jax.ShapeDtypeStruct((g, 4 * n, GROUP_LANES), BF16),
            jax.ShapeDtypeStruct((g, 2, 2 * n), F32),
        ],
        compiler_params=pltpu.CompilerParams(dimension_semantics=("parallel",)),
        name="s5_prep",
    )(row(lam_re), row(lam_im), ldt_r, b_re, b_im, ct(c_re), ct(c_im), bt(b_re), bt(b_im))


def _swap_pieces(v):
    piece = lax.broadcasted_iota(jnp.int32, v[0].shape, 1) // SSM_GROUP
    v = list(v)
    for d in (4, 2, 1):
        hi = (piece & d) != 0
        for a in range(8):
            if a & d:
                continue
            va, vb = v[a], v[a + d]
            v[a] = jnp.where(hi, pltpu.roll(vb, d * SSM_GROUP, axis=1), va)
            v[a + d] = jnp.where(hi, vb, pltpu.roll(va, LANES - d * SSM_GROUP, axis=1))
    return v


def _s5_chunk_kernel(u_ref, w1_ref, cm_ref, al_ref, d_ref, y_ref, x_ref, s_ref, hf_ref, h_ref,
                     *, nb, nc):
    gl = GROUP_LANES
    gb = GROUPS_PER_BLOCK
    n2 = 2 * SSM_STATE
    half = CHUNK // 2
    ctile = SUBLANES * CHUNK

    def token_rows(ct, h, j):
        return pl.ds(ct * ctile + h * half + j, SUBLANES, stride=CHUNK)

    def chunk_rows(ct, b):
        return pl.ds(ct * SUBLANES * nb + b, SUBLANES, stride=nb)

    def gather_body(ct, carry):
        for b in range(nb):
            for h in range(2):
                v = [u_ref[b, token_rows(ct, h, j), :] for j in range(half)]
                w = _swap_pieces(v)
                for g in range(gb):
                    x_ref[2 * g + h, chunk_rows(ct, b), :] = w[g]
        return carry

    lax.fori_loop(0, nc // SUBLANES, gather_body, 0)

    for g in range(gb):
        x = jnp.concatenate([x_ref[2 * g], x_ref[2 * g + 1]], axis=1)
        r = jnp.dot(x.astype(BF16), w1_ref[g], preferred_element_type=F32)
        x_ref[2 * g] = r[:, :LANES]
        x_ref[2 * g + 1] = r[:, LANES:gl]
        s_ref[:, g * gl:(g + 1) * gl] = r[:, gl:]

    a_re = [al_ref[g, 0:1, :] for g in range(gb)]
    a_im = [al_ref[g, 1:2, :] for g in range(gb)]
    is_fwd = lax.broadcasted_iota(jnp.int32, (SUBLANES, n2), 1) < SSM_STATE

    def step(h, s):
        out = []
        for g in range(gb):
            hr, hi = h[:, g * gl:g * gl + n2], h[:, g * gl + n2:(g + 1) * gl]
            sr, si = s[:, g * gl:g * gl + n2], s[:, g * gl + n2:(g + 1) * gl]
            out.append(a_re[g] * hr - a_im[g] * hi + sr)
            out.append(a_re[g] * hi + a_im[g] * hr + si)
        return jnp.concatenate(out, axis=1)

    cpt = SUBLANES // nb
    n_tiles = nc // cpt
    tile_rows = lambda t: pl.ds(pl.multiple_of(t * SUBLANES, SUBLANES), SUBLANES)

    def fwd_body(t, h):
        s = s_ref[tile_rows(t), :]
        before = []
        for k in range(cpt):
            before.append(h)
            h = step(h, s[k * nb:(k + 1) * nb])
        hf_ref[tile_rows(t), :] = jnp.concatenate(before, axis=0)
        return h

    lax.fori_loop(0, n_tiles, fwd_body, jnp.zeros((nb, gb * gl), F32))

    def bwd_body(i, h):
        t = n_tiles - 1 - i
        s = s_ref[tile_rows(t), :]
        before = [None] * cpt
        for k in reversed(range(cpt)):
            before[k] = h
            h = step(h, s[k * nb:(k + 1) * nb])
        hb = jnp.concatenate(before, axis=0)
        hf = hf_ref[tile_rows(t), :]
        merged = [jnp.where(is_fwd, hf[:, k * n2:(k + 1) * n2], hb[:, k * n2:(k + 1) * n2])
                  for k in range(2 * gb)]
        h_ref[tile_rows(t), :] = jnp.concatenate(merged, axis=1)
        return h

    lax.fori_loop(0, n_tiles, bwd_body, jnp.zeros((nb, gb * gl), F32))

    for g in range(gb):
        r = jnp.dot(h_ref[:, g * gl:(g + 1) * gl].astype(BF16), cm_ref[g],
                    preferred_element_type=F32)
        x_ref[2 * g] += r[:, :LANES]
        x_ref[2 * g + 1] += r[:, LANES:]

    def scatter_body(ct, carry):
        for b in range(nb):
            for h in range(2):
                w = [x_ref[2 * g + h, chunk_rows(ct, b), :] for g in range(gb)]
                v = _swap_pieces(w)
                for j in range(half):
                    rows = token_rows(ct, h, j)
                    y_ref[b, rows, :] = v[j] + d_ref[...] * u_ref[b, rows, :]
        return carry

    lax.fori_loop(0, nc // SUBLANES, scatter_body, 0)


def _s5_chunks(u, w1, cm, al, d):
    nb, seq, width = u.shape
    nc = seq // CHUNK
    rows = nb * nc
    gb = GROUPS_PER_BLOCK
    assert SUBLANES % nb == 0 and nc % SUBLANES == 0 and width % LANES == 0
    blk = pl.BlockSpec((nb, seq, LANES), lambda i: (0, 0, i))
    return pl.pallas_call(
        functools.partial(_s5_chunk_kernel, nb=nb, nc=nc),
        grid=(width // LANES,),
        in_specs=[
            blk,
            pl.BlockSpec((gb,) + w1.shape[1:], lambda i: (i, 0, 0)),
            pl.BlockSpec((gb,) + cm.shape[1:], lambda i: (i, 0, 0)),
            pl.BlockSpec((gb,) + al.shape[1:], lambda i: (i, 0, 0)),
            pl.BlockSpec((1, LANES), lambda i: (0, i)),
        ],
        out_specs=blk,
        out_shape=jax.ShapeDtypeStruct(u.shape, F32),
        scratch_shapes=[pltpu.VMEM((2 * gb, rows, LANES), F32)]
        + [pltpu.VMEM((rows, gb * GROUP_LANES), F32)] * 3,
        compiler_params=pltpu.CompilerParams(
            dimension_semantics=("parallel",), vmem_limit_bytes=VMEM_LIMIT_BYTES),
        name="s5_chunks",
    )(u, w1, cm, al, d.reshape(1, width))


def _mix_out_kernel(x_ref, a_ref, y_ref, wglu_ref, bglu_ref, og_ref, wo_a_ref, wo_s_ref, pg_ref,
                    o_ref):
    y = y_ref[...]
    g = y * (0.5 * (1.0 + jnp.tanh(math.sqrt(2.0 / math.pi) * (y + 0.044715 * (y * y * y)))))
    z = jnp.dot(g.astype(BF16), wglu_ref[...], preferred_element_type=F32) + bglu_ref[...]
    s = g * _sigmoid(z)
    s = s * _rms_scale(s) * og_ref[...]
    mixed = (jnp.dot(a_ref[...], wo_a_ref[...], preferred_element_type=F32)
             + jnp.dot(s.astype(BF16), wo_s_ref[...], preferred_element_type=F32))
    o_ref[...] = x_ref[...] + mixed * _rms_scale(mixed) * pg_ref[...]


def _mix_out(x, a, y, w_glu, b_glu, out_g, w_out, post_g, *, tm=512):
    t, d = x.shape
    wa = a.shape[1]
    ws = y.shape[1]
    const = lambda shape: pl.BlockSpec(shape, lambda i: (0, 0))
    return pl.pallas_call(
        _mix_out_kernel,
        grid=(t // tm,),
        in_specs=[
            pl.BlockSpec((tm, d), lambda i: (i, 0)),
            pl.BlockSpec((tm, wa), lambda i: (i, 0)),
            pl.BlockSpec((tm, ws), lambda i: (i, 0)),
            const((ws, ws)), const((1, ws)), const((1, ws)),
            pl.BlockSpec((wa, d), lambda i: (0, 0)),
            pl.BlockSpec((ws, d), lambda i: (wa // ws, 0)),
            const((1, d)),
        ],
        out_specs=pl.BlockSpec((tm, d), lambda i: (i, 0)),
        out_shape=jax.ShapeDtypeStruct((t, d), F32),
        compiler_params=pltpu.CompilerParams(
            dimension_semantics=("parallel",), vmem_limit_bytes=VMEM_LIMIT_BYTES),
        name="mix_out",
    )(x, a, y, w_glu, b_glu.reshape(1, ws), out_g.reshape(1, ws), w_out, w_out,
      post_g.reshape(1, d))


def kernel(x, ff1_pre_g, ff1_w_gate, ff1_w_up, ff1_w_down, ff1_post_g, mix_pre_g, w_in, lam_q1, lam_k1, lam_q2, lam_k2, attn_head_g, ssm_lam_re, ssm_lam_im, ssm_log_dt, ssm_b_re, ssm_b_im, ssm_c_re, ssm_c_im, ssm_d, ssm_w_glu, ssm_b_glu, ssm_out_g, w_out, mix_post_g, ff2_pre_g, ff2_w_gate, ff2_w_up, ff2_w_down, ff2_post_g):
    batch, seq, d_model = x.shape
    depth = w_in.shape[0]
    ssm_width = ssm_w_glu.shape[-1]
    slopes = jnp.asarray([2.0 ** (-8.0 * (i + 1) / ATTN_HEADS) for i in range(ATTN_HEADS)], F32)
    bf = lambda w: w.astype(BF16)

    xt = x.reshape(batch * seq, d_model)
    for l in range(depth):
        xt, (w_in_bf, w_glu_bf, w_out_bf) = _ffn(
            xt, ff1_pre_g[l], ff1_w_gate[l], ff1_w_up[l], ff1_w_down[l], ff1_post_g[l],
            cast=(w_in[l], ssm_w_glu[l], w_out[l]), tm=1024, tf=256, single_buffer_x=True)

        qkv, u = _in_proj(xt, mix_pre_g[l], w_in_bf, ssm_width=ssm_width)
        lam_init = 0.8 - 0.6 * math.exp(-0.3 * l)
        a, ff2_w = _attention(qkv, slopes, lam_q1[l], lam_k1[l], lam_q2[l], lam_k2[l],
                              attn_head_g[l], batch=batch, seq=seq, lam_init=lam_init,
                              cast=(ff2_w_gate[l], ff2_w_up[l], ff2_w_down[l]))

        w1, cm, al = _s5_prep(ssm_lam_re[l], ssm_lam_im[l], ssm_log_dt[l], ssm_b_re[l],
                              ssm_b_im[l], ssm_c_re[l], ssm_c_im[l])
        y = _s5_chunks(u.reshape(batch, seq, ssm_width), w1, cm, al, ssm_d[l])
        y = y.reshape(batch * seq, ssm_width)

        xt = _mix_out(xt, a, y, w_glu_bf, ssm_b_glu[l], ssm_out_g[l], w_out_bf, mix_post_g[l])

        xt, _ = _ffn(xt, ff2_pre_g[l], *ff2_w, ff2_post_g[l])
    return xt.reshape(batch, seq, d_model)
```

```python
import functools
import math

import jax
import jax.numpy as jnp
from jax import lax
from jax.experimental import pallas as pl
from jax.experimental.pallas import tpu as pltpu

F32 = jnp.float32
BF16 = jnp.bfloat16

NORM_EPS = 1e-6
ATTN_HEADS = 8
ATTN_HEAD_DIM = 64
ATTN_VALUE_DIM = 2 * ATTN_HEAD_DIM
POS_SPLIT = 16
SSM_GROUP = 16
SSM_STATE = 64
CHUNK = 16
GROUP_LANES = CHUNK * SSM_GROUP
SUBLANES = 8
LANES = 128
BF16_ROWS = 16
GROUPS_PER_BLOCK = LANES // SSM_GROUP
PREP_GROUPS_PER_STEP = 4

VMEM_LIMIT_BYTES = 56 * 1024 * 1024


def _rms_scale(x):
    return lax.rsqrt(jnp.mean(x * x, axis=-1, keepdims=True) + NORM_EPS)


def _sigmoid(x):
    return 1.0 / (1.0 + jnp.exp(-x))


class _CastJobs:
    def __init__(self, arrays, n_steps, flat_index):
        self.flat_index = flat_index
        self.views, self.slabs = list(arrays), []
        for a in arrays:
            rows, cols = a.shape
            assert rows % BF16_ROWS == 0
            n = max(k for k in range(1, n_steps + 1) if (rows // BF16_ROWS) % k == 0)
            self.slabs.append((rows // n, cols, n))

    def _spec(self, slab):
        r, c, n = slab
        return pl.BlockSpec((r, c), lambda *g: (jnp.minimum(self.flat_index(*g), n - 1), 0))

    @property
    def in_specs(self):
        return [self._spec(s) for s in self.slabs]

    out_specs = in_specs

    @property
    def out_shapes(self):
        return [jax.ShapeDtypeStruct(v.shape, BF16) for v in self.views]

    def __len__(self):
        return len(self.views)


def _run_cast_jobs(in_refs, out_refs):
    for i_ref, o_ref in zip(in_refs, out_refs):
        o_ref[...] = i_ref[...].astype(BF16)


def _ffn_kernel(*refs, n_jobs, n_streams):
    ns = n_streams
    x_ref, pre_g_ref = refs[:2]
    wg_refs, wu_refs, wd_refs = (refs[2 + k * ns:2 + (k + 1) * ns] for k in range(3))
    post_g_ref = refs[2 + 3 * ns]
    n_in = 3 + 3 * ns
    job_in = refs[n_in:n_in + n_jobs]
    o_ref = refs[n_in + n_jobs]
    job_out = refs[n_in + n_jobs + 1:n_in + 2 * n_jobs + 1]
    h_ref = refs[n_in + 2 * n_jobs + 1]
    j = pl.program_id(1)
    last = pl.num_programs(1) - 1

    @pl.when(j == 0)
    def _():
        x = x_ref[...]
        h_ref[...] = (x * _rms_scale(x) * pre_g_ref[...]).astype(BF16)
        o_ref[...] = jnp.zeros_like(o_ref)

    _run_cast_jobs(job_in, job_out)
    tile = lambda parts, axis: jnp.concatenate([r[...].astype(BF16) for r in parts], axis=axis)
    h = h_ref[...]
    gate = jnp.dot(h, tile(wg_refs, 1), preferred_element_type=F32)
    up = jnp.dot(h, tile(wu_refs, 1), preferred_element_type=F32)
    act = (gate * _sigmoid(gate) * up).astype(BF16)
    o_ref[...] += jnp.dot(act, tile(wd_refs, 0), preferred_element_type=F32)

    @pl.when(j == last)
    def _():
        acc = o_ref[...]
        o_ref[...] = x_ref[...] + 0.5 * (acc * _rms_scale(acc) * post_g_ref[...])


def _ffn(x, pre_g, w_gate, w_up, w_down, post_g, *, cast=(), tm=512, tf=512,
         single_buffer_x=False, w_streams=1):
    t, d = x.shape
    f = w_gate.shape[1]
    ns = w_streams
    assert t % tm == 0 and f % tf == 0 and tf % (ns * LANES) == 0
    nj = f // tf
    ts = tf // ns
    jobs = _CastJobs(cast, (t // tm) * nj, lambda i, j: i * nj + j)
    x_mode = dict(pipeline_mode=pl.Buffered(1)) if single_buffer_x else {}
    cols = [pl.BlockSpec((d, ts), lambda i, j, s=s: (0, j * ns + s)) for s in range(ns)]
    rows = [pl.BlockSpec((ts, d), lambda i, j, s=s: (j * ns + s, 0)) for s in range(ns)]
    outs = pl.pallas_call(
        functools.partial(_ffn_kernel, n_jobs=len(jobs), n_streams=ns),
        grid=(t // tm, nj),
        in_specs=[
            pl.BlockSpec((tm, d), lambda i, j: (i, 0), **x_mode),
            pl.BlockSpec((1, d), lambda i, j: (0, 0)),
        ] + cols + cols + rows + [pl.BlockSpec((1, d), lambda i, j: (0, 0))] + jobs.in_specs,
        out_specs=[pl.BlockSpec((tm, d), lambda i, j: (i, 0))] + jobs.out_specs,
        out_shape=[jax.ShapeDtypeStruct((t, d), F32)] + jobs.out_shapes,
        scratch_shapes=[pltpu.VMEM((tm, d), BF16)],
        compiler_params=pltpu.CompilerParams(
            dimension_semantics=("arbitrary", "arbitrary"),
            vmem_limit_bytes=VMEM_LIMIT_BYTES),
        name="ffn",
    )(x, pre_g.reshape(1, d), *([w_gate] * ns), *([w_up] * ns), *([w_down] * ns),
      post_g.reshape(1, d), *jobs.views)
    return outs[0], list(outs[1:])


def _in_proj_kernel(x_ref, g_ref, w_ref, qkv_ref, u_ref, h_ref, *, n_qkv):
    j = pl.program_id(1)

    @pl.when(j == 0)
    def _():
        x = x_ref[...]
        h_ref[...] = (x * _rms_scale(x) * g_ref[...]).astype(BF16)

    def project():
        return jnp.dot(h_ref[...], w_ref[...].astype(BF16), preferred_element_type=F32)

    @pl.when(j < n_qkv)
    def _():
        qkv_ref[...] = project().astype(BF16)

    @pl.when(j == n_qkv)
    def _():
        u_ref[...] = project()


def _in_proj(x, g, w_in, *, ssm_width, tm=1024):
    t, d = x.shape
    n = w_in.shape[1]
    tn = ssm_width
    n_qkv = (n - ssm_width) // tn
    assert t % tm == 0 and n == (n_qkv + 1) * tn
    return pl.pallas_call(
        functools.partial(_in_proj_kernel, n_qkv=n_qkv),
        grid=(t // tm, n_qkv + 1),
        in_specs=[
            pl.BlockSpec((tm, d), lambda i, j: (i, 0)),
            pl.BlockSpec((1, d), lambda i, j: (0, 0)),
            pl.BlockSpec((d, tn), lambda i, j: (0, j)),
        ],
        out_specs=[
            pl.BlockSpec((tm, tn), lambda i, j: (i, jnp.minimum(j, n_qkv - 1))),
            pl.BlockSpec((tm, tn), lambda i, j: (i, 0)),
        ],
        out_shape=[
            jax.ShapeDtypeStruct((t, n - ssm_width), BF16),
            jax.ShapeDtypeStruct((t, ssm_width), F32),
        ],
        scratch_shapes=[pltpu.VMEM((tm, d), BF16)],
        compiler_params=pltpu.CompilerParams(
            dimension_semantics=("parallel", "arbitrary"),
            vmem_limit_bytes=VMEM_LIMIT_BYTES),
        name="in_proj",
    )(x, g.reshape(1, d), w_in)


def _attn_kernel(*refs, tq, nq, lam_init, n_jobs):
    slopes_ref, lq1_ref, lk1_ref, lq2_ref, lk2_ref, hg_ref, q_ref = refs[:7]
    k_refs = refs[7:7 + nq]
    v_refs = refs[7 + nq:7 + 2 * nq]
    n_in = 7 + 2 * nq
    job_in = refs[n_in:n_in + n_jobs]
    o_ref = refs[n_in + n_jobs]
    job_out = refs[n_in + n_jobs + 1:n_in + 2 * n_jobs + 1]
    qf_ref, kf_ref, bias_ref, s_ref, p_ref = refs[n_in + 2 * n_jobs + 1:]
    h = pl.program_id(0)
    qi = pl.program_id(1)
    b = pl.program_id(2)
    e = q_ref.shape[1]

    @pl.when(b == 0)
    def _():
        slope = slopes_ref[h]
        lane = lax.broadcasted_iota(jnp.int32, (tq, e), 1)
        row = lax.broadcasted_iota(jnp.int32, (tq, e), 0)
        row_hi = (slope * POS_SPLIT) * (row // POS_SPLIT).astype(F32)
        row_lo = slope * (row % POS_SPLIT).astype(F32)
        q_base = jnp.where(lane == 0, row_hi, jnp.where(lane == 1, row_lo,
                           jnp.where(lane < 4, 1.0, 0.0)))
        k_base = jnp.where(lane < 2, -1.0, jnp.where(lane == 2, row_hi,
                           jnp.where(lane == 3, row_lo, 0.0)))
        hi_q = (lane == 0).astype(F32)
        hi_k = (lane == 2).astype(F32)
        qf_ref[...] = (q_base + (slope * tq) * qi.astype(F32) * hi_q).astype(BF16)
        for slot in range(nq):
            chunk = lax.rem(qi + slot, nq)
            sign = jnp.where(chunk < qi, 1.0, jnp.where(chunk > qi, -1.0, 0.0))
            kf = sign * k_base + (sign * (slope * tq) * chunk.astype(F32)) * hi_k
            kf_ref[slot * tq:(slot + 1) * tq, :] = kf.astype(BF16)
        col = lax.broadcasted_iota(jnp.int32, (tq, tq), 1)
        bias_ref[...] = slope * jnp.abs(lax.broadcasted_iota(jnp.int32, (tq, tq), 0)
                                        - col).astype(F32)

    _run_cast_jobs(job_in, job_out)
    lam = (jnp.exp(jnp.sum(lq1_ref[...] * lk1_ref[...], axis=-1, keepdims=True))
           - jnp.exp(jnp.sum(lq2_ref[...] * lk2_ref[...], axis=-1, keepdims=True))
           + lam_init)

    q = q_ref[...] * jnp.asarray(ATTN_HEAD_DIM ** -0.5, BF16)
    k_aug = jnp.concatenate([jnp.concatenate([r[...] for r in k_refs], axis=0), kf_ref[...]],
                            axis=1)
    v = jnp.concatenate([r[...] for r in v_refs], axis=0)
    v_ext = jnp.concatenate([v, jnp.ones_like(v)], axis=1)
    lane = lax.broadcasted_iota(jnp.int32, q.shape, 1)
    nt = (((1,), (1,)), ((), ()))
    for c in range(2):
        in_map = (lane >= c * ATTN_HEAD_DIM) & (lane < (c + 1) * ATTN_HEAD_DIM)
        q_aug = jnp.concatenate([jnp.where(in_map, q, jnp.zeros_like(q)), qf_ref[...]], axis=1)
        s_ref[c, :, :tq] = lax.dot_general(q_aug, k_aug[:tq], nt,
                                           preferred_element_type=F32) - bias_ref[...]
        s_ref[c, :, tq:] = lax.dot_general(q_aug, k_aug[tq:], nt, preferred_element_type=F32)
        s = s_ref[c]
        p_ref[c] = jnp.exp(s - jnp.max(s, axis=-1, keepdims=True)).astype(BF16)
    outs = []
    for c in range(2):
        pv = jnp.dot(p_ref[c], v_ext, preferred_element_type=F32)
        outs.append(pv[:, :e] / pv[:, e:])
    o = outs[0] - lam * outs[1]
    o_ref[...] = (o * _rms_scale(o) * hg_ref[...] * (1.0 - lam_init)).astype(o_ref.dtype)


def _attention(qkv, slopes, lq1, lk1, lq2, lk2, head_g, *, batch, seq, lam_init, cast=(), tq=512):
    t = qkv.shape[0]
    e = ATTN_VALUE_DIM
    nq = seq // tq
    nh = ATTN_HEADS
    vec = lambda a: a.reshape(1, -1).astype(F32)
    small = lambda n: pl.BlockSpec((1, n), lambda h, qi, b: (0, 0))
    jobs = _CastJobs(cast, nh * nq * batch, lambda h, qi, b: (h * nq + qi) * batch + b)

    def chunk_spec(slot, col0):
        return pl.BlockSpec((tq, e), lambda h, qi, b: (b * nq + lax.rem(qi + slot, nq), col0 + h))

    outs = pl.pallas_call(
        functools.partial(_attn_kernel, tq=tq, nq=nq, lam_init=lam_init, n_jobs=len(jobs)),
        grid=(nh, nq, batch),
        in_specs=[
            pl.BlockSpec(memory_space=pltpu.SMEM),
            small(ATTN_HEAD_DIM), small(ATTN_HEAD_DIM), small(ATTN_HEAD_DIM), small(ATTN_HEAD_DIM),
            small(e),
            pl.BlockSpec((tq, e), lambda h, qi, b: (b * nq + qi, h)),
        ] + [chunk_spec(s, nh) for s in range(nq)] + [chunk_spec(s, 2 * nh) for s in range(nq)]
        + jobs.in_specs,
        out_specs=[pl.BlockSpec((tq, e), lambda h, qi, b: (b * nq + qi, h))] + jobs.out_specs,
        out_shape=[jax.ShapeDtypeStruct((t, nh * e), BF16)] + jobs.out_shapes,
        scratch_shapes=[pltpu.VMEM((tq, e), BF16), pltpu.VMEM((seq, e), BF16),
                        pltpu.VMEM((tq, tq), F32), pltpu.VMEM((2, tq, seq), F32),
                        pltpu.VMEM((2, tq, seq), BF16)],
        compiler_params=pltpu.CompilerParams(
            dimension_semantics=("arbitrary", "arbitrary", "arbitrary"),
            vmem_limit_bytes=VMEM_LIMIT_BYTES),
        name="diff_attention",
    )(slopes, vec(lq1), vec(lk1), vec(lq2), vec(lk2), vec(head_g), qkv,
      *([qkv] * (2 * nq)), *jobs.views)
    return outs[0], list(outs[1:])


def _discretize(lam_re, lam_im, log_dt):
    dt = jnp.exp(log_dt)
    mag = jnp.exp(lam_re * dt)
    a_re = mag * jnp.cos(lam_im * dt)
    a_im = mag * jnp.sin(lam_im * dt)
    den = lam_re * lam_re + lam_im * lam_im
    nr = a_re - 1.0
    f_re = (nr * lam_re + a_im * lam_im) / den
    f_im = (a_im * lam_re - nr * lam_im) / den
    return a_re, a_im, f_re, f_im


def _cmul(ar, ai, br, bi):
    return ar * br - ai * bi, ar * bi + ai * br


def _split_bf16(x, parts):
    out = []
    for k in range(parts):
        piece = x.astype(BF16)
        out.append(piece)
        if k + 1 < parts:
            x = x - piece.astype(F32)
    return out


def _copy_dot(x, onehot, dims):
    return sum(lax.dot_general(piece, onehot, dims, preferred_element_type=F32)
               for piece in _split_bf16(x, 3))


def _dot_3pass(a, b, dims):
    a_hi, a_lo = _split_bf16(a, 2)
    b_hi, b_lo = _split_bf16(b, 2)
    dot = lambda x, y: lax.dot_general(x, y, dims, preferred_element_type=F32)
    return dot(a_hi, b_hi) + (dot(a_hi, b_lo) + dot(a_lo, b_hi))


def _int_power(a_re, a_im, e, nbits):
    shape = jnp.broadcast_shapes(a_re.shape, e.shape)
    p_re, p_im = jnp.ones(shape, F32), jnp.zeros(shape, F32)
    s_re, s_im = a_re, a_im
    for k in range(nbits):
        bit = (e & (1 << k)) != 0
        m_re, m_im = _cmul(p_re, p_im, s_re, s_im)
        p_re, p_im = jnp.where(bit, m_re, p_re), jnp.where(bit, m_im, p_im)
        if k + 1 < nbits:
            s_re, s_im = _cmul(s_re, s_im, s_re, s_im)
    return p_re, p_im


def _shift_lanes(x, s, left):
    if s == 0:
        return x
    n = x.shape[-1]
    lane = lax.broadcasted_iota(jnp.int32, x.shape, 1)
    if left:
        return jnp.where(lane < n - s, pltpu.roll(x, n - s, axis=1), 0.0)
    return jnp.where(lane >= s, pltpu.roll(x, s, axis=1), 0.0)


def _s5_prep_kernel(lre_ref, lim_ref, ldt_ref, bre_c_ref, bim_c_ref, ctre_ref, ctim_ref,
                    btre_r_ref, btim_r_ref, w1_ref, cm_ref, al_ref):
    for i in range(w1_ref.shape[0]):
        _s5_prep_group(i, lre_ref, lim_ref, ldt_ref, bre_c_ref, bim_c_ref, ctre_ref, ctim_ref,
                       btre_r_ref, btim_r_ref, w1_ref, cm_ref, al_ref)


def _s5_prep_group(i, lre_ref, lim_ref, ldt_ref, bre_c_ref, bim_c_ref, ctre_ref, ctim_ref,
                   btre_r_ref, btim_r_ref, w1_ref, cm_ref, al_ref):
    n, p, l = SSM_STATE, SSM_GROUP, CHUNK
    nbits = l.bit_length() - 1
    contract0 = (((0,), (0,)), ((), ()))
    contract_mm = (((1,), (0,)), ((), ()))

    a_re, a_im, f_re, f_im = _discretize(lre_ref[i], lim_ref[i], ldt_ref[i])
    bt_re, bt_im = btre_r_ref[i], btim_r_ref[i]
    bb_re, bb_im = _cmul(f_re, f_im, bt_re, bt_im)
    tau = lax.broadcasted_iota(jnp.int32, (l, 2 * n), 0)
    lane = lax.broadcasted_iota(jnp.int32, (l, 2 * n), 1)
    pw_re, pw_im = _int_power(a_re, a_im, jnp.where(lane < n, l - 1 - tau, tau), nbits)
    bm_re = pw_re[:, None, :] * bb_re[None, :, :] - pw_im[:, None, :] * bb_im[None, :, :]
    bm_im = pw_re[:, None, :] * bb_im[None, :, :] + pw_im[:, None, :] * bb_re[None, :, :]
    w1_ref[i, :, GROUP_LANES:GROUP_LANES + 2 * n] = bm_re.reshape(l * p, 2 * n).astype(BF16)
    w1_ref[i, :, GROUP_LANES + 2 * n:] = bm_im.reshape(l * p, 2 * n).astype(BF16)
    al_re, al_im = a_re, a_im
    for _ in range(nbits):
        al_re, al_im = _cmul(al_re, al_im, al_re, al_im)
    al_ref[i] = jnp.concatenate([al_re, al_im], axis=0)

    quant = jnp.concatenate([a_re, a_im, f_re, f_im, jnp.zeros((SUBLANES - 4, 2 * n), F32)], axis=0)
    quant_t = quant.T

    lanes_p = lax.broadcasted_iota(jnp.int32, (p, GROUP_LANES), 1)
    rows_p = lax.broadcasted_iota(jnp.int32, (p, GROUP_LANES), 0)
    tile_p = (lanes_p % p == rows_p).astype(BF16)
    tile_lag = ((l - 1) - lanes_p // p == rows_p).astype(BF16)
    pwt_re = _copy_dot(pw_re, tile_lag, contract0)
    pwt_im = _copy_dot(pw_im, tile_lag, contract0)
    t_rows = None
    cm_re, cm_im = [], []
    for d in range(2):
        states = slice(d * n, (d + 1) * n)
        col = lambda k: quant_t[states, k:k + 1]
        a_re_c, a_im_c, f_re_c, f_im_c = col(0), col(1), col(2), col(3)
        bb_re, bb_im = _cmul(f_re_c, f_im_c, bre_c_ref[d, i], bim_c_ref[d, i])
        ct_re = _copy_dot(ctre_ref[d, i], tile_p, contract_mm)
        ct_im = _copy_dot(ctim_ref[d, i], tile_p, contract_mm)
        ac_re, ac_im = _cmul(pwt_re[states], pwt_im[states], ct_re, ct_im)
        taps = (_dot_3pass(bb_re, ac_re, contract0)
                - _dot_3pass(bb_im, ac_im, contract0))
        rows = [_shift_lanes(taps, p * (tp if d == 0 else l - 1 - tp), left=(d == 1))
                for tp in range(l)]
        rows = jnp.concatenate(rows, axis=0)
        t_rows = rows if t_rows is None else t_rows + rows
        r_re, r_im = _cmul(a_re_c, a_im_c, ac_re, ac_im)
        cm_re.append(r_re)
        cm_im.append(-r_im)
    w1_ref[i, :, 0:GROUP_LANES] = t_rows.astype(BF16)
    cm_ref[i] = jnp.concatenate(cm_re + cm_im, axis=0).astype(BF16)


def _s5_prep(lam_re, lam_im, log_dt, b_re, b_im, c_re, c_im):
    _, g, n = lam_re.shape
    p = b_re.shape[-1]
    row = lambda a: jnp.concatenate([a[0], a[1]], axis=-1)[:, None, :]
    ldt_r = row(jnp.broadcast_to(log_dt[:, :, None], (2, g, n)))
    bt = lambda b: jnp.concatenate([jnp.swapaxes(b[0], 1, 2), jnp.swapaxes(b[1], 1, 2)], axis=-1)
    ct = lambda c: jnp.swapaxes(c, 2, 3)

    gb = PREP_GROUPS_PER_STEP
    assert g % gb == 0
    spec_c = pl.BlockSpec((2, gb, n, p), lambda i: (0, i, 0, 0))
    spec_r = lambda rows: pl.BlockSpec((gb, rows, 2 * n), lambda i: (i, 0, 0))
    return pl.pallas_call(
        _s5_prep_kernel,
        grid=(g // gb,),
        in_specs=[spec_r(1), spec_r(1), spec_r(1), spec_c, spec_c, spec_c, spec_c,
                  spec_r(p), spec_r(p)],
        out_specs=[
            pl.BlockSpec((gb, GROUP_LANES, GROUP_LANES + 4 * n), lambda i: (i, 0, 0)),
            pl.BlockSpec((gb, 4 * n, GROUP_LANES), lambda i: (i, 0, 0)),
            pl.BlockSpec((gb, 2, 2 * n), lambda i: (i, 0, 0)),
        ],
        out_shape=[
            jax.ShapeDtypeStruct((g, GROUP_LANES, GROUP_LANES + 4 * n), BF16),
            jax.ShapeDtypeStruct((g, 4 * n, GROUP_LANES), BF16),
            jax.ShapeDtypeStruct((g, 2, 2 * n), F32),
        ],
        compiler_params=pltpu.CompilerParams(dimension_semantics=("parallel",)),
        name="s5_prep",
    )(row(lam_re), row(lam_im), ldt_r, b_re, b_im, ct(c_re), ct(c_im), bt(b_re), bt(b_im))


def _swap_pieces(v):
    piece = lax.broadcasted_iota(jnp.int32, v[0].shape, 1) // SSM_GROUP
    v = list(v)
    for d in (4, 2, 1):
        hi = (piece & d) != 0
        for a in range(8):
            if a & d:
                continue
            va, vb = v[a], v[a + d]
            v[a] = jnp.where(hi, pltpu.roll(vb, d * SSM_GROUP, axis=1), va)
            v[a + d] = jnp.where(hi, vb, pltpu.roll(va, LANES - d * SSM_GROUP, axis=1))
    return v


def _s5_chunk_kernel(u_ref, w1_ref, cm_ref, al_ref, d_ref, y_ref, x_ref, s_ref, hf_ref, h_ref,
                     *, nb, nc):
    gl = GROUP_LANES
    gb = GROUPS_PER_BLOCK
    n2 = 2 * SSM_STATE
    half = CHUNK // 2
    ctile = SUBLANES * CHUNK

    def token_rows(ct, h, j):
        return pl.ds(ct * ctile + h * half + j, SUBLANES, stride=CHUNK)

    def chunk_rows(ct, b):
        return pl.ds(ct * SUBLANES * nb + b, SUBLANES, stride=nb)

    def gather_body(ct, carry):
        for b in range(nb):
            for h in range(2):
                v = [u_ref[b, token_rows(ct, h, j), :] for j in range(half)]
                w = _swap_pieces(v)
                for g in range(gb):
                    x_ref[2 * g + h, chunk_rows(ct, b), :] = w[g]
        return carry

    lax.fori_loop(0, nc // SUBLANES, gather_body, 0)

    for g in range(gb):
        x = jnp.concatenate([x_ref[2 * g], x_ref[2 * g + 1]], axis=1)
        r = jnp.dot(x.astype(BF16), w1_ref[g], preferred_element_type=F32)
        x_ref[2 * g] = r[:, :LANES]
        x_ref[2 * g + 1] = r[:, LANES:gl]
        s_ref[:, g * gl:(g + 1) * gl] = r[:, gl:]

    a_re = [al_ref[g, 0:1, :] for g in range(gb)]
    a_im = [al_ref[g, 1:2, :] for g in range(gb)]
    is_fwd = lax.broadcasted_iota(jnp.int32, (SUBLANES, n2), 1) < SSM_STATE

    def step(h, s):
        out = []
        for g in range(gb):
            hr, hi = h[:, g * gl:g * gl + n2], h[:, g * gl + n2:(g + 1) * gl]
            sr, si = s[:, g * gl:g * gl + n2], s[:, g * gl + n2:(g + 1) * gl]
            out.append(a_re[g] * hr - a_im[g] * hi + sr)
            out.append(a_re[g] * hi + a_im[g] * hr + si)
        return jnp.concatenate(out, axis=1)

    cpt = SUBLANES // nb
    n_tiles = nc // cpt
    tile_rows = lambda t: pl.ds(pl.multiple_of(t * SUBLANES, SUBLANES), SUBLANES)

    def fwd_body(t, h):
        s = s_ref[tile_rows(t), :]
        before = []
        for k in range(cpt):
            before.append(h)
            h = step(h, s[k * nb:(k + 1) * nb])
        hf_ref[tile_rows(t), :] = jnp.concatenate(before, axis=0)
        return h

    lax.fori_loop(0, n_tiles, fwd_body, jnp.zeros((nb, gb * gl), F32))

    def bwd_body(i, h):
        t = n_tiles - 1 - i
        s = s_ref[tile_rows(t), :]
        before = [None] * cpt
        for k in reversed(range(cpt)):
            before[k] = h
            h = step(h, s[k * nb:(k + 1) * nb])
        hb = jnp.concatenate(before, axis=0)
        hf = hf_ref[tile_rows(t), :]
        merged = [jnp.where(is_fwd, hf[:, k * n2:(k + 1) * n2], hb[:, k * n2:(k + 1) * n2])
                  for k in range(2 * gb)]
        h_ref[tile_rows(t), :] = jnp.concatenate(merged, axis=1)
        return h

    lax.fori_loop(0, n_tiles, bwd_body, jnp.zeros((nb, gb * gl), F32))

    for g in range(gb):
        r = jnp.dot(h_ref[:, g * gl:(g + 1) * gl].astype(BF16), cm_ref[g],
                    preferred_element_type=F32)
        x_ref[2 * g] += r[:, :LANES]
        x_ref[2 * g + 1] += r[:, LANES:]

    def scatter_body(ct, carry):
        for b in range(nb):
            for h in range(2):
                w = [x_ref[2 * g + h, chunk_rows(ct, b), :] for g in range(gb)]
                v = _swap_pieces(w)
                for j in range(half):
                    rows = token_rows(ct, h, j)
                    y_ref[b, rows, :] = v[j] + d_ref[...] * u_ref[b, rows, :]
        return carry

    lax.fori_loop(0, nc // SUBLANES, scatter_body, 0)


def _s5_chunks(u, w1, cm, al, d):
    nb, seq, width = u.shape
    nc = seq // CHUNK
    rows = nb * nc
    gb = GROUPS_PER_BLOCK
    assert SUBLANES % nb == 0 and nc % SUBLANES == 0 and width % LANES == 0
    blk = pl.BlockSpec((nb, seq, LANES), lambda i: (0, 0, i))
    return pl.pallas_call(
        functools.partial(_s5_chunk_kernel, nb=nb, nc=nc),
        grid=(width // LANES,),
        in_specs=[
            blk,
            pl.BlockSpec((gb,) + w1.shape[1:], lambda i: (i, 0, 0)),
            pl.BlockSpec((gb,) + cm.shape[1:], lambda i: (i, 0, 0)),
            pl.BlockSpec((gb,) + al.shape[1:], lambda i: (i, 0, 0)),
            pl.BlockSpec((1, LANES), lambda i: (0, i)),
        ],
        out_specs=blk,
        out_shape=jax.ShapeDtypeStruct(u.shape, F32),
        scratch_shapes=[pltpu.VMEM((2 * gb, rows, LANES), F32)]
        + [pltpu.VMEM((rows, gb * GROUP_LANES), F32)] * 3,
        compiler_params=pltpu.CompilerParams(
            dimension_semantics=("parallel",), vmem_limit_bytes=VMEM_LIMIT_BYTES),
        name="s5_chunks",
    )(u, w1, cm, al, d.reshape(1, width))


def _mix_out_kernel(x_ref, a_ref, y_ref, wglu_ref, bglu_ref, og_ref, wo_a_ref, wo_s_ref, pg_ref,
                    o_ref):
    y = y_ref[...]
    g = y * (0.5 * (1.0 + jnp.tanh(math.sqrt(2.0 / math.pi) * (y + 0.044715 * (y * y * y)))))
    z = jnp.dot(g.astype(BF16), wglu_ref[...], preferred_element_type=F32) + bglu_ref[...]
    s = g * _sigmoid(z)
    s = s * _rms_scale(s) * og_ref[...]
    mixed = (jnp.dot(a_ref[...], wo_a_ref[...], preferred_element_type=F32)
             + jnp.dot(s.astype(BF16), wo_s_ref[...], preferred_element_type=F32))
    o_ref[...] = x_ref[...] + mixed * _rms_scale(mixed) * pg_ref[...]


def _mix_out(x, a, y, w_glu, b_glu, out_g, w_out, post_g, *, tm=512):
    t, d = x.shape
    wa = a.shape[1]
    ws = y.shape[1]
    const = lambda shape: pl.BlockSpec(shape, lambda i: (0, 0))
    return pl.pallas_call(
        _mix_out_kernel,
        grid=(t // tm,),
        in_specs=[
            pl.BlockSpec((tm, d), lambda i: (i, 0)),
            pl.BlockSpec((tm, wa), lambda i: (i, 0)),
            pl.BlockSpec((tm, ws), lambda i: (i, 0)),
            const((ws, ws)), const((1, ws)), const((1, ws)),
            pl.BlockSpec((wa, d), lambda i: (0, 0)),
            pl.BlockSpec((ws, d), lambda i: (wa // ws, 0)),
            const((1, d)),
        ],
        out_specs=pl.BlockSpec((tm, d), lambda i: (i, 0)),
        out_shape=jax.ShapeDtypeStruct((t, d), F32),
        compiler_params=pltpu.CompilerParams(
            dimension_semantics=("parallel",), vmem_limit_bytes=VMEM_LIMIT_BYTES),
        name="mix_out",
    )(x, a, y, w_glu, b_glu.reshape(1, ws), out_g.reshape(1, ws), w_out, w_out,
      post_g.reshape(1, d))


def kernel(x, ff1_pre_g, ff1_w_gate, ff1_w_up, ff1_w_down, ff1_post_g, mix_pre_g, w_in, lam_q1, lam_k1, lam_q2, lam_k2, attn_head_g, ssm_lam_re, ssm_lam_im, ssm_log_dt, ssm_b_re, ssm_b_im, ssm_c_re, ssm_c_im, ssm_d, ssm_w_glu, ssm_b_glu, ssm_out_g, w_out, mix_post_g, ff2_pre_g, ff2_w_gate, ff2_w_up, ff2_w_down, ff2_post_g):
    batch, seq, d_model = x.shape
    depth = w_in.shape[0]
    ssm_width = ssm_w_glu.shape[-1]
    slopes = jnp.asarray([2.0 ** (-8.0 * (i + 1) / ATTN_HEADS) for i in range(ATTN_HEADS)], F32)
    bf = lambda w: w.astype(BF16)

    xt = x.reshape(batch * seq, d_model)
    for l in range(depth):
        xt, (w_in_bf, w_glu_bf, w_out_bf) = _ffn(
            xt, ff1_pre_g[l], ff1_w_gate[l], ff1_w_up[l], ff1_w_down[l], ff1_post_g[l],
            cast=(w_in[l], ssm_w_glu[l], w_out[l]), tm=1024, tf=256, single_buffer_x=True,
            w_streams=2)

        qkv, u = _in_proj(xt, mix_pre_g[l], w_in_bf, ssm_width=ssm_width)
        lam_init = 0.8 - 0.6 * math.exp(-0.3 * l)
        a, ff2_w = _attention(qkv, slopes, lam_q1[l], lam_k1[l], lam_q2[l], lam_k2[l],
                              attn_head_g[l], batch=batch, seq=seq, lam_init=lam_init,
                              cast=(ff2_w_gate[l], ff2_w_up[l], ff2_w_down[l]))

        w1, cm, al = _s5_prep(ssm_lam_re[l], ssm_lam_im[l], ssm_log_dt[l], ssm_b_re[l],
                              ssm_b_im[l], ssm_c_re[l], ssm_c_im[l])
        y = _s5_chunks(u.reshape(batch, seq, ssm_width), w1, cm, al, ssm_d[l])
        y = y.reshape(batch * seq, ssm_width)

        xt = _mix_out(xt, a, y, w_glu_bf, ssm_b_glu[l], ssm_out_g[l], w_out_bf, mix_post_g[l])

        xt, _ = _ffn(xt, ff2_pre_g[l], *ff2_w, ff2_post_g[l])
    return xt.reshape(batch, seq, d_model)
```

```python
import functools
import math

import jax
import jax.numpy as jnp
from jax import lax
from jax.experimental import pallas as pl
from jax.experimental.pallas import tpu as pltpu

F32 = jnp.float32
BF16 = jnp.bfloat16

NORM_EPS = 1e-6
ATTN_HEADS = 8
ATTN_HEAD_DIM = 64
ATTN_VALUE_DIM = 2 * ATTN_HEAD_DIM
POS_SPLIT = 16
SSM_GROUP = 16
SSM_STATE = 64
CHUNK = 16
GROUP_LANES = CHUNK * SSM_GROUP
SUBLANES = 8
LANES = 128
BF16_ROWS = 16
GROUPS_PER_BLOCK = LANES // SSM_GROUP
PREP_GROUPS_PER_STEP = 4

VMEM_LIMIT_BYTES = 56 * 1024 * 1024


def _rms_scale(x):
    return lax.rsqrt(jnp.mean(x * x, axis=-1, keepdims=True) + NORM_EPS)


def _sigmoid(x):
    return 1.0 / (1.0 + jnp.exp(-x))


class _CastJobs:
    def __init__(self, arrays, n_steps, flat_index):
        self.flat_index = flat_index
        self.views, self.slabs = list(arrays), []
        for a in arrays:
            rows, cols = a.shape
            assert rows % BF16_ROWS == 0
            n = max(k for k in range(1, n_steps + 1) if (rows // BF16_ROWS) % k == 0)
            self.slabs.append((rows // n, cols, n))

    def _spec(self, slab):
        r, c, n = slab
        return pl.BlockSpec((r, c), lambda *g: (jnp.minimum(self.flat_index(*g), n - 1), 0))

    @property
    def in_specs(self):
        return [self._spec(s) for s in self.slabs]

    out_specs = in_specs

    @property
    def out_shapes(self):
        return [jax.ShapeDtypeStruct(v.shape, BF16) for v in self.views]

    def __len__(self):
        return len(self.views)


def _run_cast_jobs(in_refs, out_refs):
    for i_ref, o_ref in zip(in_refs, out_refs):
        o_ref[...] = i_ref[...].astype(BF16)


def _ffn_kernel(*refs, n_jobs):
    x_ref, pre_g_ref, wg_ref, wu_ref, wd_ref, post_g_ref = refs[:6]
    job_in = refs[6:6 + n_jobs]
    o_ref = refs[6 + n_jobs]
    job_out = refs[7 + n_jobs:7 + 2 * n_jobs]
    h_ref = refs[7 + 2 * n_jobs]
    j = pl.program_id(1)
    last = pl.num_programs(1) - 1

    @pl.when(j == 0)
    def _():
        x = x_ref[...]
        h_ref[...] = (x * _rms_scale(x) * pre_g_ref[...]).astype(BF16)
        o_ref[...] = jnp.zeros_like(o_ref)

    _run_cast_jobs(job_in, job_out)
    h = h_ref[...]
    gate = jnp.dot(h, wg_ref[...].astype(BF16), preferred_element_type=F32)
    up = jnp.dot(h, wu_ref[...].astype(BF16), preferred_element_type=F32)
    act = (gate * _sigmoid(gate) * up).astype(BF16)
    o_ref[...] += jnp.dot(act, wd_ref[...].astype(BF16), preferred_element_type=F32)

    @pl.when(j == last)
    def _():
        acc = o_ref[...]
        o_ref[...] = x_ref[...] + 0.5 * (acc * _rms_scale(acc) * post_g_ref[...])


def _ffn(x, pre_g, w_gate, w_up, w_down, post_g, *, cast=(), tm=512, tf=512,
         single_buffer_x=False):
    t, d = x.shape
    f = w_gate.shape[1]
    assert t % tm == 0 and f % tf == 0
    nj = f // tf
    jobs = _CastJobs(cast, (t // tm) * nj, lambda i, j: i * nj + j)
    x_mode = dict(pipeline_mode=pl.Buffered(1)) if single_buffer_x else {}
    outs = pl.pallas_call(
        functools.partial(_ffn_kernel, n_jobs=len(jobs)),
        grid=(t // tm, nj),
        in_specs=[
            pl.BlockSpec((tm, d), lambda i, j: (i, 0), **x_mode),
            pl.BlockSpec((1, d), lambda i, j: (0, 0)),
            pl.BlockSpec((d, tf), lambda i, j: (0, j)),
            pl.BlockSpec((d, tf), lambda i, j: (0, j)),
            pl.BlockSpec((tf, d), lambda i, j: (j, 0)),
            pl.BlockSpec((1, d), lambda i, j: (0, 0)),
        ] + jobs.in_specs,
        out_specs=[pl.BlockSpec((tm, d), lambda i, j: (i, 0))] + jobs.out_specs,
        out_shape=[jax.ShapeDtypeStruct((t, d), F32)] + jobs.out_shapes,
        scratch_shapes=[pltpu.VMEM((tm, d), BF16)],
        compiler_params=pltpu.CompilerParams(
            dimension_semantics=("arbitrary", "arbitrary"),
            vmem_limit_bytes=VMEM_LIMIT_BYTES),
        name="ffn",
    )(x, pre_g.reshape(1, d), w_gate, w_up, w_down, post_g.reshape(1, d), *jobs.views)
    return outs[0], list(outs[1:])


def _ffn_spread_kernel(*refs, n_jobs, nt, cpt):
    x0_ref, xn_ref, xp_ref, pre_g_ref, wg_ref, wu_ref, wd_ref, post_g_ref = refs[:8]
    job_in = refs[8:8 + n_jobs]
    o_ref = refs[8 + n_jobs]
    job_out = refs[9 + n_jobs:9 + 2 * n_jobs]
    h_ref, acc_ref = refs[9 + 2 * n_jobs:]
    i = pl.program_id(0)
    j = pl.program_id(1)
    rc = xn_ref.shape[0]
    rows = pl.ds(pl.multiple_of(jnp.minimum(j, cpt - 1) * rc, rc), rc)
    cur = lax.rem(i, 2)
    other = 1 - cur

    @pl.when((i == 0) & (j == 0))
    def _():
        x = x0_ref[...]
        h_ref[0] = (x * _rms_scale(x) * pre_g_ref[...]).astype(BF16)
        acc_ref[1] = jnp.zeros(acc_ref.shape[1:], F32)

    def finish_chunk():
        acc = acc_ref[other, rows, :]
        o_ref[...] = xp_ref[...] + 0.5 * (acc * _rms_scale(acc) * post_g_ref[...])

    def step(first):
        _run_cast_jobs(job_in, job_out)
        xn = xn_ref[...]
        h_ref[other, rows, :] = (xn * _rms_scale(xn) * pre_g_ref[...]).astype(BF16)
        finish_chunk()
        h = h_ref[cur]
        gate = jnp.dot(h, wg_ref[...].astype(BF16), preferred_element_type=F32)
        up = jnp.dot(h, wu_ref[...].astype(BF16), preferred_element_type=F32)
        act = (gate * _sigmoid(gate) * up).astype(BF16)
        part = jnp.dot(act, wd_ref[...].astype(BF16), preferred_element_type=F32)
        if first:
            acc_ref[cur] = part
        else:
            acc_ref[cur] += part

    pl.when((i < nt) & (j == 0))(lambda: step(True))
    pl.when((i < nt) & (j > 0))(lambda: step(False))
    pl.when(i == nt)(finish_chunk)


def _ffn_spread(x, pre_g, w_gate, w_up, w_down, post_g, *, cast=(), tm=512, tf=512, rc=64):
    t, d = x.shape
    f = w_gate.shape[1]
    assert t % tm == 0 and f % tf == 0 and tm % rc == 0
    nt, nj, cpt = t // tm, f // tf, tm // rc
    assert cpt <= nj
    jobs = _CastJobs(cast, nt * nj, lambda i, j: i * nj + j)
    chunk = lambda j: jnp.minimum(j, cpt - 1)
    nxt = lambda i, j: (jnp.minimum(i + 1, nt - 1) * cpt + chunk(j), 0)
    prv = lambda i, j: (jnp.where(i == 0, 0, (i - 1) * cpt + chunk(j)), 0)
    wj = lambda i, j: jnp.where(i < nt, j, nj - 1)
    outs = pl.pallas_call(
        functools.partial(_ffn_spread_kernel, n_jobs=len(jobs), nt=nt, cpt=cpt),
        grid=(nt + 1, nj),
        in_specs=[
            pl.BlockSpec((tm, d), lambda i, j: (0, 0), pipeline_mode=pl.Buffered(1)),
            pl.BlockSpec((rc, d), nxt),
            pl.BlockSpec((rc, d), prv),
            pl.BlockSpec((1, d), lambda i, j: (0, 0)),
            pl.BlockSpec((d, tf), lambda i, j: (0, wj(i, j))),
            pl.BlockSpec((d, tf), lambda i, j: (0, wj(i, j))),
            pl.BlockSpec((tf, d), lambda i, j: (wj(i, j), 0)),
            pl.BlockSpec((1, d), lambda i, j: (0, 0)),
        ] + jobs.in_specs,
        out_specs=[pl.BlockSpec((rc, d), prv)] + jobs.out_specs,
        out_shape=[jax.ShapeDtypeStruct((t, d), F32)] + jobs.out_shapes,
        scratch_shapes=[pltpu.VMEM((2, tm, d), BF16), pltpu.VMEM((2, tm, d), F32)],
        compiler_params=pltpu.CompilerParams(
            dimension_semantics=("arbitrary", "arbitrary"),
            vmem_limit_bytes=VMEM_LIMIT_BYTES),
        name="ffn",
    )(x, x, x, pre_g.reshape(1, d), w_gate, w_up, w_down, post_g.reshape(1, d), *jobs.views)
    return outs[0], list(outs[1:])


def _in_proj_kernel(x_ref, g_ref, w_ref, qkv_ref, u_ref, h_ref, *, n_qkv):
    j = pl.program_id(1)

    @pl.when(j == 0)
    def _():
        x = x_ref[...]
        h_ref[...] = (x * _rms_scale(x) * g_ref[...]).astype(BF16)

    def project():
        return jnp.dot(h_ref[...], w_ref[...].astype(BF16), preferred_element_type=F32)

    @pl.when(j < n_qkv)
    def _():
        qkv_ref[...] = project().astype(BF16)

    @pl.when(j == n_qkv)
    def _():
        u_ref[...] = project()


def _in_proj(x, g, w_in, *, ssm_width, tm=1024):
    t, d = x.shape
    n = w_in.shape[1]
    tn = ssm_width
    n_qkv = (n - ssm_width) // tn
    assert t % tm == 0 and n == (n_qkv + 1) * tn
    return pl.pallas_call(
        functools.partial(_in_proj_kernel, n_qkv=n_qkv),
        grid=(t // tm, n_qkv + 1),
        in_specs=[
            pl.BlockSpec((tm, d), lambda i, j: (i, 0)),
            pl.BlockSpec((1, d), lambda i, j: (0, 0)),
            pl.BlockSpec((d, tn), lambda i, j: (0, j)),
        ],
        out_specs=[
            pl.BlockSpec((tm, tn), lambda i, j: (i, jnp.minimum(j, n_qkv - 1))),
            pl.BlockSpec((tm, tn), lambda i, j: (i, 0)),
        ],
        out_shape=[
            jax.ShapeDtypeStruct((t, n - ssm_width), BF16),
            jax.ShapeDtypeStruct((t, ssm_width), F32),
        ],
        scratch_shapes=[pltpu.VMEM((tm, d), BF16)],
        compiler_params=pltpu.CompilerParams(
            dimension_semantics=("parallel", "arbitrary"),
            vmem_limit_bytes=VMEM_LIMIT_BYTES),
        name="in_proj",
    )(x, g.reshape(1, d), w_in)


def _attn_kernel(*refs, tq, nq, lam_init, n_jobs):
    slopes_ref, lq1_ref, lk1_ref, lq2_ref, lk2_ref, hg_ref, q_ref = refs[:7]
    k_refs = refs[7:7 + nq]
    v_refs = refs[7 + nq:7 + 2 * nq]
    n_in = 7 + 2 * nq
    job_in = refs[n_in:n_in + n_jobs]
    o_ref = refs[n_in + n_jobs]
    job_out = refs[n_in + n_jobs + 1:n_in + 2 * n_jobs + 1]
    qf_ref, kf_ref, bias_ref, s_ref, p_ref = refs[n_in + 2 * n_jobs + 1:]
    h = pl.program_id(0)
    qi = pl.program_id(1)
    b = pl.program_id(2)
    e = q_ref.shape[1]

    @pl.when(b == 0)
    def _():
        slope = slopes_ref[h]
        lane = lax.broadcasted_iota(jnp.int32, (tq, e), 1)
        row = lax.broadcasted_iota(jnp.int32, (tq, e), 0)
        row_hi = (slope * POS_SPLIT) * (row // POS_SPLIT).astype(F32)
        row_lo = slope * (row % POS_SPLIT).astype(F32)
        q_base = jnp.where(lane == 0, row_hi, jnp.where(lane == 1, row_lo,
                           jnp.where(lane < 4, 1.0, 0.0)))
        k_base = jnp.where(lane < 2, -1.0, jnp.where(lane == 2, row_hi,
                           jnp.where(lane == 3, row_lo, 0.0)))
        hi_q = (lane == 0).astype(F32)
        hi_k = (lane == 2).astype(F32)
        qf_ref[...] = (q_base + (slope * tq) * qi.astype(F32) * hi_q).astype(BF16)
        for slot in range(nq):
            chunk = lax.rem(qi + slot, nq)
            sign = jnp.where(chunk < qi, 1.0, jnp.where(chunk > qi, -1.0, 0.0))
            kf = sign * k_base + (sign * (slope * tq) * chunk.astype(F32)) * hi_k
            kf_ref[slot * tq:(slot + 1) * tq, :] = kf.astype(BF16)
        col = lax.broadcasted_iota(jnp.int32, (tq, tq), 1)
        bias_ref[...] = slope * jnp.abs(lax.broadcasted_iota(jnp.int32, (tq, tq), 0)
                                        - col).astype(F32)

    _run_cast_jobs(job_in, job_out)
    lam = (jnp.exp(jnp.sum(lq1_ref[...] * lk1_ref[...], axis=-1, keepdims=True))
           - jnp.exp(jnp.sum(lq2_ref[...] * lk2_ref[...], axis=-1, keepdims=True))
           + lam_init)

    q = q_ref[...] * jnp.asarray(ATTN_HEAD_DIM ** -0.5, BF16)
    k_aug = jnp.concatenate([jnp.concatenate([r[...] for r in k_refs], axis=0), kf_ref[...]],
                            axis=1)
    v = jnp.concatenate([r[...] for r in v_refs], axis=0)
    v_ext = jnp.concatenate([v, jnp.ones_like(v)], axis=1)
    lane = lax.broadcasted_iota(jnp.int32, q.shape, 1)
    nt = (((1,), (1,)), ((), ()))
    for c in range(2):
        in_map = (lane >= c * ATTN_HEAD_DIM) & (lane < (c + 1) * ATTN_HEAD_DIM)
        q_aug = jnp.concatenate([jnp.where(in_map, q, jnp.zeros_like(q)), qf_ref[...]], axis=1)
        s_ref[c, :, :tq] = lax.dot_general(q_aug, k_aug[:tq], nt,
                                           preferred_element_type=F32) - bias_ref[...]
        s_ref[c, :, tq:] = lax.dot_general(q_aug, k_aug[tq:], nt, preferred_element_type=F32)
        s = s_ref[c]
        p_ref[c] = jnp.exp(s - jnp.max(s, axis=-1, keepdims=True)).astype(BF16)
    outs = []
    for c in range(2):
        pv = jnp.dot(p_ref[c], v_ext, preferred_element_type=F32)
        outs.append(pv[:, :e] / pv[:, e:])
    o = outs[0] - lam * outs[1]
    o_ref[...] = (o * _rms_scale(o) * hg_ref[...] * (1.0 - lam_init)).astype(o_ref.dtype)


def _attention(qkv, slopes, lq1, lk1, lq2, lk2, head_g, *, batch, seq, lam_init, cast=(), tq=512):
    t = qkv.shape[0]
    e = ATTN_VALUE_DIM
    nq = seq // tq
    nh = ATTN_HEADS
    vec = lambda a: a.reshape(1, -1).astype(F32)
    small = lambda n: pl.BlockSpec((1, n), lambda h, qi, b: (0, 0))
    jobs = _CastJobs(cast, nh * nq * batch, lambda h, qi, b: (h * nq + qi) * batch + b)

    def chunk_spec(slot, col0):
        return pl.BlockSpec((tq, e), lambda h, qi, b: (b * nq + lax.rem(qi + slot, nq), col0 + h))

    outs = pl.pallas_call(
        functools.partial(_attn_kernel, tq=tq, nq=nq, lam_init=lam_init, n_jobs=len(jobs)),
        grid=(nh, nq, batch),
        in_specs=[
            pl.BlockSpec(memory_space=pltpu.SMEM),
            small(ATTN_HEAD_DIM), small(ATTN_HEAD_DIM), small(ATTN_HEAD_DIM), small(ATTN_HEAD_DIM),
            small(e),
            pl.BlockSpec((tq, e), lambda h, qi, b: (b * nq + qi, h)),
        ] + [chunk_spec(s, nh) for s in range(nq)] + [chunk_spec(s, 2 * nh) for s in range(nq)]
        + jobs.in_specs,
        out_specs=[pl.BlockSpec((tq, e), lambda h, qi, b: (b * nq + qi, h))] + jobs.out_specs,
        out_shape=[jax.ShapeDtypeStruct((t, nh * e), BF16)] + jobs.out_shapes,
        scratch_shapes=[pltpu.VMEM((tq, e), BF16), pltpu.VMEM((seq, e), BF16),
                        pltpu.VMEM((tq, tq), F32), pltpu.VMEM((2, tq, seq), F32),
                        pltpu.VMEM((2, tq, seq), BF16)],
        compiler_params=pltpu.CompilerParams(
            dimension_semantics=("arbitrary", "arbitrary", "arbitrary"),
            vmem_limit_bytes=VMEM_LIMIT_BYTES),
        name="diff_attention",
    )(slopes, vec(lq1), vec(lk1), vec(lq2), vec(lk2), vec(head_g), qkv,
      *([qkv] * (2 * nq)), *jobs.views)
    return outs[0], list(outs[1:])


def _discretize(lam_re, lam_im, log_dt):
    dt = jnp.exp(log_dt)
    mag = jnp.exp(lam_re * dt)
    a_re = mag * jnp.cos(lam_im * dt)
    a_im = mag * jnp.sin(lam_im * dt)
    den = lam_re * lam_re + lam_im * lam_im
    nr = a_re - 1.0
    f_re = (nr * lam_re + a_im * lam_im) / den
    f_im = (a_im * lam_re - nr * lam_im) / den
    return a_re, a_im, f_re, f_im


def _cmul(ar, ai, br, bi):
    return ar * br - ai * bi, ar * bi + ai * br


def _split_bf16(x, parts):
    out = []
    for k in range(parts):
        piece = x.astype(BF16)
        out.append(piece)
        if k + 1 < parts:
            x = x - piece.astype(F32)
    return out


def _copy_dot(x, onehot, dims):
    return sum(lax.dot_general(piece, onehot, dims, preferred_element_type=F32)
               for piece in _split_bf16(x, 3))


def _dot_3pass(a, b, dims):
    a_hi, a_lo = _split_bf16(a, 2)
    b_hi, b_lo = _split_bf16(b, 2)
    dot = lambda x, y: lax.dot_general(x, y, dims, preferred_element_type=F32)
    return dot(a_hi, b_hi) + (dot(a_hi, b_lo) + dot(a_lo, b_hi))


def _int_power(a_re, a_im, e, nbits):
    shape = jnp.broadcast_shapes(a_re.shape, e.shape)
    p_re, p_im = jnp.ones(shape, F32), jnp.zeros(shape, F32)
    s_re, s_im = a_re, a_im
    for k in range(nbits):
        bit = (e & (1 << k)) != 0
        m_re, m_im = _cmul(p_re, p_im, s_re, s_im)
        p_re, p_im = jnp.where(bit, m_re, p_re), jnp.where(bit, m_im, p_im)
        if k + 1 < nbits:
            s_re, s_im = _cmul(s_re, s_im, s_re, s_im)
    return p_re, p_im


def _shift_lanes(x, s, left):
    if s == 0:
        return x
    n = x.shape[-1]
    lane = lax.broadcasted_iota(jnp.int32, x.shape, 1)
    if left:
        return jnp.where(lane < n - s, pltpu.roll(x, n - s, axis=1), 0.0)
    return jnp.where(lane >= s, pltpu.roll(x, s, axis=1), 0.0)


def _s5_prep_kernel(lre_ref, lim_ref, ldt_ref, bre_c_ref, bim_c_ref, ctre_ref, ctim_ref,
                    btre_r_ref, btim_r_ref, w1_ref, cm_ref, al_ref):
    for i in range(w1_ref.shape[0]):
        _s5_prep_group(i, lre_ref, lim_ref, ldt_ref, bre_c_ref, bim_c_ref, ctre_ref, ctim_ref,
                       btre_r_ref, btim_r_ref, w1_ref, cm_ref, al_ref)


def _s5_prep_group(i, lre_ref, lim_ref, ldt_ref, bre_c_ref, bim_c_ref, ctre_ref, ctim_ref,
                   btre_r_ref, btim_r_ref, w1_ref, cm_ref, al_ref):
    n, p, l = SSM_STATE, SSM_GROUP, CHUNK
    nbits = l.bit_length() - 1
    contract0 = (((0,), (0,)), ((), ()))
    contract_mm = (((1,), (0,)), ((), ()))

    a_re, a_im, f_re, f_im = _discretize(lre_ref[i], lim_ref[i], ldt_ref[i])
    bt_re, bt_im = btre_r_ref[i], btim_r_ref[i]
    bb_re, bb_im = _cmul(f_re, f_im, bt_re, bt_im)
    tau = lax.broadcasted_iota(jnp.int32, (l, 2 * n), 0)
    lane = lax.broadcasted_iota(jnp.int32, (l, 2 * n), 1)
    pw_re, pw_im = _int_power(a_re, a_im, jnp.where(lane < n, l - 1 - tau, tau), nbits)
    bm_re = pw_re[:, None, :] * bb_re[None, :, :] - pw_im[:, None, :] * bb_im[None, :, :]
    bm_im = pw_re[:, None, :] * bb_im[None, :, :] + pw_im[:, None, :] * bb_re[None, :, :]
    w1_ref[i, :, GROUP_LANES:GROUP_LANES + 2 * n] = bm_re.reshape(l * p, 2 * n).astype(BF16)
    w1_ref[i, :, GROUP_LANES + 2 * n:] = bm_im.reshape(l * p, 2 * n).astype(BF16)
    al_re, al_im = a_re, a_im
    for _ in range(nbits):
        al_re, al_im = _cmul(al_re, al_im, al_re, al_im)
    al_ref[i] = jnp.concatenate([al_re, al_im], axis=0)

    quant = jnp.concatenate([a_re, a_im, f_re, f_im, jnp.zeros((SUBLANES - 4, 2 * n), F32)], axis=0)
    quant_t = quant.T

    lanes_p = lax.broadcasted_iota(jnp.int32, (p, GROUP_LANES), 1)
    rows_p = lax.broadcasted_iota(jnp.int32, (p, GROUP_LANES), 0)
    tile_p = (lanes_p % p == rows_p).astype(BF16)
    tile_lag = ((l - 1) - lanes_p // p == rows_p).astype(BF16)
    pwt_re = _copy_dot(pw_re, tile_lag, contract0)
    pwt_im = _copy_dot(pw_im, tile_lag, contract0)
    t_rows = None
    cm_re, cm_im = [], []
    for d in range(2):
        states = slice(d * n, (d + 1) * n)
        col = lambda k: quant_t[states, k:k + 1]
        a_re_c, a_im_c, f_re_c, f_im_c = col(0), col(1), col(2), col(3)
        bb_re, bb_im = _cmul(f_re_c, f_im_c, bre_c_ref[d, i], bim_c_ref[d, i])
        ct_re = _copy_dot(ctre_ref[d, i], tile_p, contract_mm)
        ct_im = _copy_dot(ctim_ref[d, i], tile_p, contract_mm)
        ac_re, ac_im = _cmul(pwt_re[states], pwt_im[states], ct_re, ct_im)
        taps = (_dot_3pass(bb_re, ac_re, contract0)
                - _dot_3pass(bb_im, ac_im, contract0))
        rows = [_shift_lanes(taps, p * (tp if d == 0 else l - 1 - tp), left=(d == 1))
                for tp in range(l)]
        rows = jnp.concatenate(rows, axis=0)
        t_rows = rows if t_rows is None else t_rows + rows
        r_re, r_im = _cmul(a_re_c, a_im_c, ac_re, ac_im)
        cm_re.append(r_re)
        cm_im.append(-r_im)
    w1_ref[i, :, 0:GROUP_LANES] = t_rows.astype(BF16)
    cm_ref[i] = jnp.concatenate(cm_re + cm_im, axis=0).astype(BF16)


def _s5_prep(lam_re, lam_im, log_dt, b_re, b_im, c_re, c_im):
    _, g, n = lam_re.shape
    p = b_re.shape[-1]
    row = lambda a: jnp.concatenate([a[0], a[1]], axis=-1)[:, None, :]
    ldt_r = row(jnp.broadcast_to(log_dt[:, :, None], (2, g, n)))
    bt = lambda b: jnp.concatenate([jnp.swapaxes(b[0], 1, 2), jnp.swapaxes(b[1], 1, 2)], axis=-1)
    ct = lambda c: jnp.swapaxes(c, 2, 3)

    gb = PREP_GROUPS_PER_STEP
    assert g % gb == 0
    spec_c = pl.BlockSpec((2, gb, n, p), lambda i: (0, i, 0, 0))
    spec_r = lambda rows: pl.BlockSpec((gb, rows, 2 * n), lambda i: (i, 0, 0))
    return pl.pallas_call(
        _s5_prep_kernel,
        grid=(g // gb,),
        in_specs=[spec_r(1), spec_r(1), spec_r(1), spec_c, spec_c, spec_c, spec_c,
                  spec_r(p), spec_r(p)],
        out_specs=[
            pl.BlockSpec((gb, GROUP_LANES, GROUP_LANES + 4 * n), lambda i: (i, 0, 0)),
            pl.BlockSpec((gb, 4 * n, GROUP_LANES), lambda i: (i, 0, 0)),
            pl.BlockSpec((gb, 2, 2 * n), lambda i: (i, 0, 0)),
        ],
        out_shape=[
            jax.ShapeDtypeStruct((g, GROUP_LANES, GROUP_LANES + 4 * n), BF16),
            jax.ShapeDtypeStruct((g, 4 * n, GROUP_LANES), BF16),
            jax.ShapeDtypeStruct((g, 2, 2 * n), F32),
        ],
        compiler_params=pltpu.CompilerParams(dimension_semantics=("parallel",)),
        name="s5_prep",
    )(row(lam_re), row(lam_im), ldt_r, b_re, b_im, ct(c_re), ct(c_im), bt(b_re), bt(b_im))


def _swap_pieces(v):
    piece = lax.broadcasted_iota(jnp.int32, v[0].shape, 1) // SSM_GROUP
    v = list(v)
    for d in (4, 2, 1):
        hi = (piece & d) != 0
        for a in range(8):
            if a & d:
                continue
            va, vb = v[a], v[a + d]
            v[a] = jnp.where(hi, pltpu.roll(vb, d * SSM_GROUP, axis=1), va)
            v[a + d] = jnp.where(hi, vb, pltpu.roll(va, LANES - d * SSM_GROUP, axis=1))
    return v


def _s5_chunk_kernel(u_ref, w1_ref, cm_ref, al_ref, d_ref, y_ref, x_ref, s_ref, hf_ref, h_ref,
                     *, nb, nc):
    gl = GROUP_LANES
    gb = GROUPS_PER_BLOCK
    n2 = 2 * SSM_STATE
    half = CHUNK // 2
    ctile = SUBLANES * CHUNK

    def token_rows(ct, h, j):
        return pl.ds(ct * ctile + h * half + j, SUBLANES, stride=CHUNK)

    def chunk_rows(ct, b):
        return pl.ds(ct * SUBLANES * nb + b, SUBLANES, stride=nb)

    def gather_body(ct, carry):
        for b in range(nb):
            for h in range(2):
                v = [u_ref[b, token_rows(ct, h, j), :] for j in range(half)]
                w = _swap_pieces(v)
                for g in range(gb):
                    x_ref[2 * g + h, chunk_rows(ct, b), :] = w[g]
        return carry

    lax.fori_loop(0, nc // SUBLANES, gather_body, 0)

    for g in range(gb):
        x = jnp.concatenate([x_ref[2 * g], x_ref[2 * g + 1]], axis=1)
        r = jnp.dot(x.astype(BF16), w1_ref[g], preferred_element_type=F32)
        x_ref[2 * g] = r[:, :LANES]
        x_ref[2 * g + 1] = r[:, LANES:gl]
        s_ref[:, g * gl:(g + 1) * gl] = r[:, gl:]

    a_re = [al_ref[g, 0:1, :] for g in range(gb)]
    a_im = [al_ref[g, 1:2, :] for g in range(gb)]
    is_fwd = lax.broadcasted_iota(jnp.int32, (SUBLANES, n2), 1) < SSM_STATE

    def step(h, s):
        out = []
        for g in range(gb):
            hr, hi = h[:, g * gl:g * gl + n2], h[:, g * gl + n2:(g + 1) * gl]
            sr, si = s[:, g * gl:g * gl + n2], s[:, g * gl + n2:(g + 1) * gl]
            out.append(a_re[g] * hr - a_im[g] * hi + sr)
            out.append(a_re[g] * hi + a_im[g] * hr + si)
        return jnp.concatenate(out, axis=1)

    cpt = SUBLANES // nb
    n_tiles = nc // cpt
    tile_rows = lambda t: pl.ds(pl.multiple_of(t * SUBLANES, SUBLANES), SUBLANES)

    def fwd_body(t, h):
        s = s_ref[tile_rows(t), :]
        before = []
        for k in range(cpt):
            before.append(h)
            h = step(h, s[k * nb:(k + 1) * nb])
        hf_ref[tile_rows(t), :] = jnp.concatenate(before, axis=0)
        return h

    lax.fori_loop(0, n_tiles, fwd_body, jnp.zeros((nb, gb * gl), F32))

    def bwd_body(i, h):
        t = n_tiles - 1 - i
        s = s_ref[tile_rows(t), :]
        before = [None] * cpt
        for k in reversed(range(cpt)):
            before[k] = h
            h = step(h, s[k * nb:(k + 1) * nb])
        hb = jnp.concatenate(before, axis=0)
        hf = hf_ref[tile_rows(t), :]
        merged = [jnp.where(is_fwd, hf[:, k * n2:(k + 1) * n2], hb[:, k * n2:(k + 1) * n2])
                  for k in range(2 * gb)]
        h_ref[tile_rows(t), :] = jnp.concatenate(merged, axis=1)
        return h

    lax.fori_loop(0, n_tiles, bwd_body, jnp.zeros((nb, gb * gl), F32))

    for g in range(gb):
        r = jnp.dot(h_ref[:, g * gl:(g + 1) * gl].astype(BF16), cm_ref[g],
                    preferred_element_type=F32)
        x_ref[2 * g] += r[:, :LANES]
        x_ref[2 * g + 1] += r[:, LANES:]

    def scatter_body(ct, carry):
        for b in range(nb):
            for h in range(2):
                w = [x_ref[2 * g + h, chunk_rows(ct, b), :] for g in range(gb)]
                v = _swap_pieces(w)
                for j in range(half):
                    rows = token_rows(ct, h, j)
                    y_ref[b, rows, :] = v[j] + d_ref[...] * u_ref[b, rows, :]
        return carry

    lax.fori_loop(0, nc // SUBLANES, scatter_body, 0)


def _s5_chunks(u, w1, cm, al, d):
    nb, seq, width = u.shape
    nc = seq // CHUNK
    rows = nb * nc
    gb = GROUPS_PER_BLOCK
    assert SUBLANES % nb == 0 and nc % SUBLANES == 0 and width % LANES == 0
    blk = pl.BlockSpec((nb, seq, LANES), lambda i: (0, 0, i))
    return pl.pallas_call(
        functools.partial(_s5_chunk_kernel, nb=nb, nc=nc),
        grid=(width // LANES,),
        in_specs=[
            blk,
            pl.BlockSpec((gb,) + w1.shape[1:], lambda i: (i, 0, 0)),
            pl.BlockSpec((gb,) + cm.shape[1:], lambda i: (i, 0, 0)),
            pl.BlockSpec((gb,) + al.shape[1:], lambda i: (i, 0, 0)),
            pl.BlockSpec((1, LANES), lambda i: (0, i)),
        ],
        out_specs=blk,
        out_shape=jax.ShapeDtypeStruct(u.shape, F32),
        scratch_shapes=[pltpu.VMEM((2 * gb, rows, LANES), F32)]
        + [pltpu.VMEM((rows, gb * GROUP_LANES), F32)] * 3,
        compiler_params=pltpu.CompilerParams(
            dimension_semantics=("parallel",), vmem_limit_bytes=VMEM_LIMIT_BYTES),
        name="s5_chunks",
    )(u, w1, cm, al, d.reshape(1, width))


def _mix_out_kernel(x_ref, a_ref, y_ref, wglu_ref, bglu_ref, og_ref, wo_a_ref, wo_s_ref, pg_ref,
                    o_ref):
    y = y_ref[...]
    g = y * (0.5 * (1.0 + jnp.tanh(math.sqrt(2.0 / math.pi) * (y + 0.044715 * (y * y * y)))))
    z = jnp.dot(g.astype(BF16), wglu_ref[...], preferred_element_type=F32) + bglu_ref[...]
    s = g * _sigmoid(z)
    s = s * _rms_scale(s) * og_ref[...]
    mixed = (jnp.dot(a_ref[...], wo_a_ref[...], preferred_element_type=F32)
             + jnp.dot(s.astype(BF16), wo_s_ref[...], preferred_element_type=F32))
    o_ref[...] = x_ref[...] + mixed * _rms_scale(mixed) * pg_ref[...]


def _mix_out(x, a, y, w_glu, b_glu, out_g, w_out, post_g, *, tm=512):
    t, d = x.shape
    wa = a.shape[1]
    ws = y.shape[1]
    const = lambda shape: pl.BlockSpec(shape, lambda i: (0, 0))
    return pl.pallas_call(
        _mix_out_kernel,
        grid=(t // tm,),
        in_specs=[
            pl.BlockSpec((tm, d), lambda i: (i, 0)),
            pl.BlockSpec((tm, wa), lambda i: (i, 0)),
            pl.BlockSpec((tm, ws), lambda i: (i, 0)),
            const((ws, ws)), const((1, ws)), const((1, ws)),
            pl.BlockSpec((wa, d), lambda i: (0, 0)),
            pl.BlockSpec((ws, d), lambda i: (wa // ws, 0)),
            const((1, d)),
        ],
        out_specs=pl.BlockSpec((tm, d), lambda i: (i, 0)),
        out_shape=jax.ShapeDtypeStruct((t, d), F32),
        compiler_params=pltpu.CompilerParams(
            dimension_semantics=("parallel",), vmem_limit_bytes=VMEM_LIMIT_BYTES),
        name="mix_out",
    )(x, a, y, w_glu, b_glu.reshape(1, ws), out_g.reshape(1, ws), w_out, w_out,
      post_g.reshape(1, d))


def kernel(x, ff1_pre_g, ff1_w_gate, ff1_w_up, ff1_w_down, ff1_post_g, mix_pre_g, w_in, lam_q1, lam_k1, lam_q2, lam_k2, attn_head_g, ssm_lam_re, ssm_lam_im, ssm_log_dt, ssm_b_re, ssm_b_im, ssm_c_re, ssm_c_im, ssm_d, ssm_w_glu, ssm_b_glu, ssm_out_g, w_out, mix_post_g, ff2_pre_g, ff2_w_gate, ff2_w_up, ff2_w_down, ff2_post_g):
    batch, seq, d_model = x.shape
    depth = w_in.shape[0]
    ssm_width = ssm_w_glu.shape[-1]
    slopes = jnp.asarray([2.0 ** (-8.0 * (i + 1) / ATTN_HEADS) for i in range(ATTN_HEADS)], F32)
    bf = lambda w: w.astype(BF16)

    xt = x.reshape(batch * seq, d_model)
    for l in range(depth):
        xt, (w_in_bf, w_glu_bf, w_out_bf) = _ffn_spread(
            xt, ff1_pre_g[l], bf(ff1_w_gate[l]), bf(ff1_w_up[l]), bf(ff1_w_down[l]),
            ff1_post_g[l], cast=(w_in[l], ssm_w_glu[l], w_out[l]))

        qkv, u = _in_proj(xt, mix_pre_g[l], w_in_bf, ssm_width=ssm_width)
        lam_init = 0.8 - 0.6 * math.exp(-0.3 * l)
        a, ff2_w = _attention(qkv, slopes, lam_q1[l], lam_k1[l], lam_q2[l], lam_k2[l],
                              attn_head_g[l], batch=batch, seq=seq, lam_init=lam_init,
                              cast=(ff2_w_gate[l], ff2_w_up[l], ff2_w_down[l]))

        w1, cm, al = _s5_prep(ssm_lam_re[l], ssm_lam_im[l], ssm_log_dt[l], ssm_b_re[l],
                              ssm_b_im[l], ssm_c_re[l], ssm_c_im[l])
        y = _s5_chunks(u.reshape(batch, seq, ssm_width), w1, cm, al, ssm_d[l])
        y = y.reshape(batch * seq, ssm_width)

        xt = _mix_out(xt, a, y, w_glu_bf, ssm_b_glu[l], ssm_out_g[l], w_out_bf, mix_post_g[l])

        xt, _ = _ffn_spread(xt, ff2_pre_g[l], *ff2_w, ff2_post_g[l])
    return xt.reshape(batch, seq, d_model)
```

```python
import functools
import math

import jax
import jax.numpy as jnp
from jax import lax
from jax.experimental import pallas as pl
from jax.experimental.pallas import tpu as pltpu

F32 = jnp.float32
BF16 = jnp.bfloat16

NORM_EPS = 1e-6
ATTN_HEADS = 8
ATTN_HEAD_DIM = 64
ATTN_VALUE_DIM = 2 * ATTN_HEAD_DIM
POS_SPLIT = 16
SSM_GROUP = 16
SSM_STATE = 64
CHUNK = 16
GROUP_LANES = CHUNK * SSM_GROUP
SUBLANES = 8
LANES = 128
BF16_ROWS = 16
GROUPS_PER_BLOCK = LANES // SSM_GROUP
PREP_GROUPS_PER_STEP = 8
FFN_HEAD_ROWS = 1024

VMEM_LIMIT_BYTES = 56 * 1024 * 1024


def _rms_scale(x):
    return lax.rsqrt(jnp.mean(x * x, axis=-1, keepdims=True) + NORM_EPS)


def _sigmoid(x):
    return 1.0 / (1.0 + jnp.exp(-x))


class _CastJobs:
    def __init__(self, arrays, n_steps, flat_index):
        self.flat_index = flat_index
        self.views, self.slabs = list(arrays), []
        for a in arrays:
            rows, cols = a.shape
            assert rows % BF16_ROWS == 0
            n = max(k for k in range(1, n_steps + 1) if (rows // BF16_ROWS) % k == 0)
            self.slabs.append((rows // n, cols, n))

    def _spec(self, slab):
        r, c, n = slab
        return pl.BlockSpec((r, c), lambda *g: (jnp.minimum(self.flat_index(*g), n - 1), 0))

    @property
    def in_specs(self):
        return [self._spec(s) for s in self.slabs]

    out_specs = in_specs

    @property
    def out_shapes(self):
        return [jax.ShapeDtypeStruct(v.shape, BF16) for v in self.views]

    def __len__(self):
        return len(self.views)


def _run_cast_jobs(in_refs, out_refs):
    for i_ref, o_ref in zip(in_refs, out_refs):
        o_ref[...] = i_ref[...].astype(BF16)


def _ffn_kernel(*refs, n_jobs, has_into, emit_weights):
    x_ref, pre_g_ref, wg_ref, wu_ref, wd_ref, post_g_ref = refs[:6]
    n_in = 6 + int(has_into)
    job_in = refs[n_in:n_in + n_jobs]
    o_ref = refs[n_in + n_jobs]
    n_out = n_in + n_jobs + 1
    w_out = refs[n_out:n_out + 3] if emit_weights else ()
    n_out += len(w_out)
    job_out = refs[n_out:n_out + n_jobs]
    h_ref = refs[n_out + n_jobs]
    j = pl.program_id(1)
    last = pl.num_programs(1) - 1

    @pl.when(j == 0)
    def _():
        x = x_ref[...]
        h_ref[...] = (x * _rms_scale(x) * pre_g_ref[...]).astype(BF16)
        o_ref[...] = jnp.zeros_like(o_ref)

    _run_cast_jobs(job_in, job_out)
    wg, wu, wd = (r[...].astype(BF16) for r in (wg_ref, wu_ref, wd_ref))
    for w_ref, w in zip(w_out, (wg, wu, wd)):
        w_ref[...] = w
    h = h_ref[...]
    gate = jnp.dot(h, wg, preferred_element_type=F32)
    up = jnp.dot(h, wu, preferred_element_type=F32)
    act = (gate * _sigmoid(gate) * up).astype(BF16)
    o_ref[...] += jnp.dot(act, wd, preferred_element_type=F32)

    @pl.when(j == last)
    def _():
        acc = o_ref[...]
        o_ref[...] = x_ref[...] + 0.5 * (acc * _rms_scale(acc) * post_g_ref[...])


def _ffn(x, pre_g, w_gate, w_up, w_down, post_g, *, cast=(), tm=512, tf=512,
         single_buffer_x=False, tiles=None, into=None, emit_weights=False):
    t, d = x.shape
    f = w_gate.shape[1]
    assert t % tm == 0 and f % tf == 0
    first, nt = tiles if tiles is not None else (0, t // tm)
    assert not emit_weights or nt == 1
    nj = f // tf
    jobs = _CastJobs(cast, nt * nj, lambda i, j: i * nj + j)
    x_mode = dict(pipeline_mode=pl.Buffered(1)) if single_buffer_x else {}
    w_cols = pl.BlockSpec((d, tf), lambda i, j: (0, j))
    w_rows = pl.BlockSpec((tf, d), lambda i, j: (j, 0))
    into_spec = [pl.BlockSpec(memory_space=pl.ANY)] if into is not None else []
    w_specs = [w_cols, w_cols, w_rows] if emit_weights else []
    w_shapes = [jax.ShapeDtypeStruct(w.shape, BF16) for w in (w_gate, w_up, w_down)]
    outs = pl.pallas_call(
        functools.partial(_ffn_kernel, n_jobs=len(jobs), has_into=into is not None,
                          emit_weights=emit_weights),
        grid=(nt, nj),
        in_specs=[
            pl.BlockSpec((tm, d), lambda i, j: (first + i, 0), **x_mode),
            pl.BlockSpec((1, d), lambda i, j: (0, 0)),
            w_cols, w_cols, w_rows,
            pl.BlockSpec((1, d), lambda i, j: (0, 0)),
        ] + into_spec + jobs.in_specs,
        out_specs=[pl.BlockSpec((tm, d), lambda i, j: (first + i, 0))] + w_specs + jobs.out_specs,
        out_shape=[jax.ShapeDtypeStruct((t, d), F32)] + w_shapes[:len(w_specs)] + jobs.out_shapes,
        input_output_aliases={6: 0} if into is not None else {},
        scratch_shapes=[pltpu.VMEM((tm, d), BF16)],
        compiler_params=pltpu.CompilerParams(
            dimension_semantics=("arbitrary", "arbitrary"),
            vmem_limit_bytes=VMEM_LIMIT_BYTES),
        name="ffn",
    )(x, pre_g.reshape(1, d), w_gate, w_up, w_down, post_g.reshape(1, d),
      *([into] if into is not None else []), *jobs.views)
    n_w = len(w_specs)
    return outs[0], list(outs[1:1 + n_w]), list(outs[1 + n_w:])


def _in_proj_kernel(x_ref, g_ref, w_ref, qkv_ref, u_ref, h_ref, *, n_qkv):
    j = pl.program_id(1)

    @pl.when(j == 0)
    def _():
        x = x_ref[...]
        h_ref[...] = (x * _rms_scale(x) * g_ref[...]).astype(BF16)

    def project():
        return jnp.dot(h_ref[...], w_ref[...].astype(BF16), preferred_element_type=F32)

    @pl.when(j < n_qkv)
    def _():
        qkv_ref[...] = project().astype(BF16)

    @pl.when(j == n_qkv)
    def _():
        u_ref[...] = project()


def _in_proj(x, g, w_in, *, ssm_width, tm=1024):
    t, d = x.shape
    n = w_in.shape[1]
    tn = ssm_width
    n_qkv = (n - ssm_width) // tn
    assert t % tm == 0 and n == (n_qkv + 1) * tn
    return pl.pallas_call(
        functools.partial(_in_proj_kernel, n_qkv=n_qkv),
        grid=(t // tm, n_qkv + 1),
        in_specs=[
            pl.BlockSpec((tm, d), lambda i, j: (i, 0)),
            pl.BlockSpec((1, d), lambda i, j: (0, 0)),
            pl.BlockSpec((d, tn), lambda i, j: (0, j)),
        ],
        out_specs=[
            pl.BlockSpec((tm, tn), lambda i, j: (i, jnp.minimum(j, n_qkv - 1))),
            pl.BlockSpec((tm, tn), lambda i, j: (i, 0)),
        ],
        out_shape=[
            jax.ShapeDtypeStruct((t, n - ssm_width), BF16),
            jax.ShapeDtypeStruct((t, ssm_width), F32),
        ],
        scratch_shapes=[pltpu.VMEM((tm, d), BF16)],
        compiler_params=pltpu.CompilerParams(
            dimension_semantics=("parallel", "arbitrary"),
            vmem_limit_bytes=VMEM_LIMIT_BYTES),
        name="in_proj",
    )(x, g.reshape(1, d), w_in)


def _attn_kernel(*refs, tq, nq, lam_init, n_jobs):
    slopes_ref, lq1_ref, lk1_ref, lq2_ref, lk2_ref, hg_ref, q_ref = refs[:7]
    k_refs = refs[7:7 + nq]
    v_refs = refs[7 + nq:7 + 2 * nq]
    n_in = 7 + 2 * nq
    job_in = refs[n_in:n_in + n_jobs]
    o_ref = refs[n_in + n_jobs]
    job_out = refs[n_in + n_jobs + 1:n_in + 2 * n_jobs + 1]
    qf_ref, kf_ref, bias_ref, s_ref, p_ref = refs[n_in + 2 * n_jobs + 1:]
    h = pl.program_id(0)
    qi = pl.program_id(1)
    b = pl.program_id(2)
    e = q_ref.shape[1]

    @pl.when(b == 0)
    def _():
        slope = slopes_ref[h]
        lane = lax.broadcasted_iota(jnp.int32, (tq, e), 1)
        row = lax.broadcasted_iota(jnp.int32, (tq, e), 0)
        row_hi = (slope * POS_SPLIT) * (row // POS_SPLIT).astype(F32)
        row_lo = slope * (row % POS_SPLIT).astype(F32)
        q_base = jnp.where(lane == 0, row_hi, jnp.where(lane == 1, row_lo,
                           jnp.where(lane < 4, 1.0, 0.0)))
        k_base = jnp.where(lane < 2, -1.0, jnp.where(lane == 2, row_hi,
                           jnp.where(lane == 3, row_lo, 0.0)))
        hi_q = (lane == 0).astype(F32)
        hi_k = (lane == 2).astype(F32)
        qf_ref[...] = (q_base + (slope * tq) * qi.astype(F32) * hi_q).astype(BF16)
        for slot in range(nq):
            chunk = lax.rem(qi + slot, nq)
            sign = jnp.where(chunk < qi, 1.0, jnp.where(chunk > qi, -1.0, 0.0))
            kf = sign * k_base + (sign * (slope * tq) * chunk.astype(F32)) * hi_k
            kf_ref[slot * tq:(slot + 1) * tq, :] = kf.astype(BF16)
        col = lax.broadcasted_iota(jnp.int32, (tq, tq), 1)
        bias_ref[...] = slope * jnp.abs(lax.broadcasted_iota(jnp.int32, (tq, tq), 0)
                                        - col).astype(F32)

    _run_cast_jobs(job_in, job_out)
    lam = (jnp.exp(jnp.sum(lq1_ref[...] * lk1_ref[...], axis=-1, keepdims=True))
           - jnp.exp(jnp.sum(lq2_ref[...] * lk2_ref[...], axis=-1, keepdims=True))
           + lam_init)

    q = q_ref[...] * jnp.asarray(ATTN_HEAD_DIM ** -0.5, BF16)
    k_aug = jnp.concatenate([jnp.concatenate([r[...] for r in k_refs], axis=0), kf_ref[...]],
                            axis=1)
    v = jnp.concatenate([r[...] for r in v_refs], axis=0)
    v_ext = jnp.concatenate([v, jnp.ones_like(v)], axis=1)
    lane = lax.broadcasted_iota(jnp.int32, q.shape, 1)
    nt = (((1,), (1,)), ((), ()))
    for c in range(2):
        in_map = (lane >= c * ATTN_HEAD_DIM) & (lane < (c + 1) * ATTN_HEAD_DIM)
        q_aug = jnp.concatenate([jnp.where(in_map, q, jnp.zeros_like(q)), qf_ref[...]], axis=1)
        s_ref[c, :, :tq] = lax.dot_general(q_aug, k_aug[:tq], nt,
                                           preferred_element_type=F32) - bias_ref[...]
        s_ref[c, :, tq:] = lax.dot_general(q_aug, k_aug[tq:], nt, preferred_element_type=F32)
        s = s_ref[c]
        p_ref[c] = jnp.exp(s - jnp.max(s, axis=-1, keepdims=True)).astype(BF16)
    outs = []
    for c in range(2):
        pv = jnp.dot(p_ref[c], v_ext, preferred_element_type=F32)
        outs.append(pv[:, :e] / pv[:, e:])
    o = outs[0] - lam * outs[1]
    o_ref[...] = (o * _rms_scale(o) * hg_ref[...] * (1.0 - lam_init)).astype(o_ref.dtype)


def _attention(qkv, slopes, lq1, lk1, lq2, lk2, head_g, *, batch, seq, lam_init, cast=(), tq=512):
    t = qkv.shape[0]
    e = ATTN_VALUE_DIM
    nq = seq // tq
    nh = ATTN_HEADS
    vec = lambda a: a.reshape(1, -1).astype(F32)
    small = lambda n: pl.BlockSpec((1, n), lambda h, qi, b: (0, 0))
    jobs = _CastJobs(cast, nh * nq * batch, lambda h, qi, b: (h * nq + qi) * batch + b)

    def chunk_spec(slot, col0):
        return pl.BlockSpec((tq, e), lambda h, qi, b: (b * nq + lax.rem(qi + slot, nq), col0 + h))

    outs = pl.pallas_call(
        functools.partial(_attn_kernel, tq=tq, nq=nq, lam_init=lam_init, n_jobs=len(jobs)),
        grid=(nh, nq, batch),
        in_specs=[
            pl.BlockSpec(memory_space=pltpu.SMEM),
            small(ATTN_HEAD_DIM), small(ATTN_HEAD_DIM), small(ATTN_HEAD_DIM), small(ATTN_HEAD_DIM),
            small(e),
            pl.BlockSpec((tq, e), lambda h, qi, b: (b * nq + qi, h)),
        ] + [chunk_spec(s, nh) for s in range(nq)] + [chunk_spec(s, 2 * nh) for s in range(nq)]
        + jobs.in_specs,
        out_specs=[pl.BlockSpec((tq, e), lambda h, qi, b: (b * nq + qi, h))] + jobs.out_specs,
        out_shape=[jax.ShapeDtypeStruct((t, nh * e), BF16)] + jobs.out_shapes,
        scratch_shapes=[pltpu.VMEM((tq, e), BF16), pltpu.VMEM((seq, e), BF16),
                        pltpu.VMEM((tq, tq), F32), pltpu.VMEM((2, tq, seq), F32),
                        pltpu.VMEM((2, tq, seq), BF16)],
        compiler_params=pltpu.CompilerParams(
            dimension_semantics=("arbitrary", "arbitrary", "arbitrary"),
            vmem_limit_bytes=VMEM_LIMIT_BYTES),
        name="diff_attention",
    )(slopes, vec(lq1), vec(lk1), vec(lq2), vec(lk2), vec(head_g), qkv,
      *([qkv] * (2 * nq)), *jobs.views)
    return outs[0], list(outs[1:])


def _discretize(lam_re, lam_im, log_dt):
    dt = jnp.exp(log_dt)
    mag = jnp.exp(lam_re * dt)
    a_re = mag * jnp.cos(lam_im * dt)
    a_im = mag * jnp.sin(lam_im * dt)
    den = lam_re * lam_re + lam_im * lam_im
    nr = a_re - 1.0
    f_re = (nr * lam_re + a_im * lam_im) / den
    f_im = (a_im * lam_re - nr * lam_im) / den
    return a_re, a_im, f_re, f_im


def _cmul(ar, ai, br, bi):
    return ar * br - ai * bi, ar * bi + ai * br


def _split_bf16(x, parts):
    out = []
    for k in range(parts):
        piece = x.astype(BF16)
        out.append(piece)
        if k + 1 < parts:
            x = x - piece.astype(F32)
    return out


def _copy_dot(x, onehot, dims):
    return sum(lax.dot_general(piece, onehot, dims, preferred_element_type=F32)
               for piece in _split_bf16(x, 3))


def _dot_3pass(a_parts, b_parts, dims):
    a_hi, a_lo = a_parts
    b_hi, b_lo = b_parts
    dot = lambda x, y: lax.dot_general(x, y, dims, preferred_element_type=F32)
    return dot(a_hi, b_hi) + (dot(a_hi, b_lo) + dot(a_lo, b_hi))


def _int_power(a_re, a_im, e, nbits):
    shape = jnp.broadcast_shapes(a_re.shape, e.shape)
    p_re, p_im = jnp.ones(shape, F32), jnp.zeros(shape, F32)
    s_re, s_im = a_re, a_im
    for k in range(nbits):
        bit = (e & (1 << k)) != 0
        m_re, m_im = _cmul(p_re, p_im, s_re, s_im)
        p_re, p_im = jnp.where(bit, m_re, p_re), jnp.where(bit, m_im, p_im)
        if k + 1 < nbits:
            s_re, s_im = _cmul(s_re, s_im, s_re, s_im)
    return p_re, p_im


def _shift_lanes(x, s, left):
    if s == 0:
        return x
    n = x.shape[-1]
    lane = lax.broadcasted_iota(jnp.int32, x.shape, 1)
    if left:
        return jnp.where(lane < n - s, pltpu.roll(x, n - s, axis=1), 0.0)
    return jnp.where(lane >= s, pltpu.roll(x, s, axis=1), 0.0)


def _s5_prep_kernel(lre_ref, lim_ref, ldt_ref, btre_ref, btim_ref, cre_ref, cim_ref,
                    w1_ref, cm_ref, al_ref):
    for i in range(w1_ref.shape[0]):
        _s5_prep_group(i, lre_ref, lim_ref, ldt_ref, btre_ref, btim_ref, cre_ref, cim_ref,
                       w1_ref, cm_ref, al_ref)


def _s5_prep_group(i, lre_ref, lim_ref, ldt_ref, btre_r_ref, btim_r_ref, cre_ref, cim_ref,
                   w1_ref, cm_ref, al_ref):
    n, p, l = SSM_STATE, SSM_GROUP, CHUNK
    nbits = l.bit_length() - 1
    contract0 = (((0,), (0,)), ((), ()))
    contract_mm = (((1,), (0,)), ((), ()))

    a_re, a_im, f_re, f_im = _discretize(lre_ref[i], lim_ref[i], ldt_ref[i])
    bt_re, bt_im = btre_r_ref[i], btim_r_ref[i]
    bb_re, bb_im = _cmul(f_re, f_im, bt_re, bt_im)
    tau = lax.broadcasted_iota(jnp.int32, (l, 2 * n), 0)
    lane = lax.broadcasted_iota(jnp.int32, (l, 2 * n), 1)
    pw_re, pw_im = _int_power(a_re, a_im, jnp.where(lane < n, l - 1 - tau, tau), nbits)
    bm_re = pw_re[:, None, :] * bb_re[None, :, :] - pw_im[:, None, :] * bb_im[None, :, :]
    bm_im = pw_re[:, None, :] * bb_im[None, :, :] + pw_im[:, None, :] * bb_re[None, :, :]
    w1_ref[i, :, GROUP_LANES:GROUP_LANES + 2 * n] = bm_re.reshape(l * p, 2 * n).astype(BF16)
    w1_ref[i, :, GROUP_LANES + 2 * n:] = bm_im.reshape(l * p, 2 * n).astype(BF16)
    al_re, al_im = a_re, a_im
    for _ in range(nbits):
        al_re, al_im = _cmul(al_re, al_im, al_re, al_im)
    al_ref[i] = jnp.concatenate([al_re, al_im], axis=0)

    quant = jnp.concatenate([a_re, a_im, jnp.zeros((SUBLANES - 2, 2 * n), F32)], axis=0)
    quant_t = quant.T
    a_re_c, a_im_c = quant_t[:, 0:1], quant_t[:, 1:2]

    lanes_p = lax.broadcasted_iota(jnp.int32, (p, GROUP_LANES), 1)
    rows_p = lax.broadcasted_iota(jnp.int32, (p, GROUP_LANES), 0)
    tile_p = (lanes_p % p == rows_p).astype(BF16)
    tile_lag = ((l - 1) - lanes_p // p == rows_p).astype(BF16)
    ac_re, ac_im = _cmul(_copy_dot(pw_re, tile_lag, contract0), _copy_dot(pw_im, tile_lag, contract0),
                         _copy_dot(cre_ref[i], tile_p, contract0),
                         _copy_dot(cim_ref[i], tile_p, contract0))
    ac_re_parts, ac_im_parts = _split_bf16(ac_re, 2), _split_bf16(ac_im, 2)
    is_fwd = lax.broadcasted_iota(jnp.int32, bb_re.shape, 1) < n
    t_rows = None
    for d in range(2):
        in_dir = is_fwd if d == 0 else ~is_fwd
        taps = (_dot_3pass(_split_bf16(jnp.where(in_dir, bb_re, 0.0), 2), ac_re_parts, contract_mm)
                - _dot_3pass(_split_bf16(jnp.where(in_dir, bb_im, 0.0), 2), ac_im_parts,
                             contract_mm))
        rows = [_shift_lanes(taps, p * (tp if d == 0 else l - 1 - tp), left=(d == 1))
                for tp in range(l)]
        rows = jnp.concatenate(rows, axis=0)
        t_rows = rows if t_rows is None else t_rows + rows
    w1_ref[i, :, 0:GROUP_LANES] = t_rows.astype(BF16)
    r_re, r_im = _cmul(a_re_c, a_im_c, ac_re, ac_im)
    cm_ref[i] = jnp.concatenate([r_re, -r_im], axis=0).astype(BF16)


def _s5_prep(lam_re, lam_im, log_dt, b_re, b_im, c_re, c_im):
    _, g, n = lam_re.shape
    p = b_re.shape[-1]
    row = lambda a: jnp.concatenate([a[0], a[1]], axis=-1)[:, None, :]
    ldt_r = row(jnp.broadcast_to(log_dt[:, :, None], (2, g, n)))
    bt = lambda b: jnp.concatenate([jnp.swapaxes(b[0], 1, 2), jnp.swapaxes(b[1], 1, 2)], axis=-1)
    cr = lambda c: jnp.concatenate([c[0], c[1]], axis=-1)

    gb = PREP_GROUPS_PER_STEP
    assert g % gb == 0
    spec_r = lambda rows: pl.BlockSpec((gb, rows, 2 * n), lambda i: (i, 0, 0))
    return pl.pallas_call(
        _s5_prep_kernel,
        grid=(g // gb,),
        in_specs=[spec_r(1), spec_r(1), spec_r(1), spec_r(p), spec_r(p), spec_r(p), spec_r(p)],
        out_specs=[
            pl.BlockSpec((gb, GROUP_LANES, GROUP_LANES + 4 * n), lambda i: (i, 0, 0)),
            pl.BlockSpec((gb, 4 * n, GROUP_LANES), lambda i: (i, 0, 0)),
            pl.BlockSpec((gb, 2, 2 * n), lambda i: (i, 0, 0)),
        ],
        out_shape=[
            jax.ShapeDtypeStruct((g, GROUP_LANES, GROUP_LANES + 4 * n), BF16),
            jax.ShapeDtypeStruct((g, 4 * n, GROUP_LANES), BF16),
            jax.ShapeDtypeStruct((g, 2, 2 * n), F32),
        ],
        compiler_params=pltpu.CompilerParams(dimension_semantics=("parallel",)),
        name="s5_prep",
    )(row(lam_re), row(lam_im), ldt_r, bt(b_re), bt(b_im), cr(c_re), cr(c_im))


def _swap_pieces(v):
    piece = lax.broadcasted_iota(jnp.int32, v[0].shape, 1) // SSM_GROUP
    v = list(v)
    for d in (4, 2, 1):
        hi = (piece & d) != 0
        for a in range(8):
            if a & d:
                continue
            va, vb = v[a], v[a + d]
            v[a] = jnp.where(hi, pltpu.roll(vb, d * SSM_GROUP, axis=1), va)
            v[a + d] = jnp.where(hi, vb, pltpu.roll(va, LANES - d * SSM_GROUP, axis=1))
    return v


def _s5_chunk_kernel(u_ref, w1_ref, cm_ref, al_ref, d_ref, y_ref, x_ref, s_ref, hf_ref, h_ref,
                     *, nb, nc):
    gl = GROUP_LANES
    gb = GROUPS_PER_BLOCK
    n2 = 2 * SSM_STATE
    half = CHUNK // 2
    ctile = SUBLANES * CHUNK

    def token_rows(ct, h, j):
        return pl.ds(ct * ctile + h * half + j, SUBLANES, stride=CHUNK)

    def chunk_rows(ct, b):
        return pl.ds(ct * SUBLANES * nb + b, SUBLANES, stride=nb)

    def gather_body(ct, carry):
        for b in range(nb):
            for h in range(2):
                v = [u_ref[b, token_rows(ct, h, j), :] for j in range(half)]
                w = _swap_pieces(v)
                for g in range(gb):
                    x_ref[2 * g + h, chunk_rows(ct, b), :] = w[g]
        return carry

    lax.fori_loop(0, nc // SUBLANES, gather_body, 0)

    for g in range(gb):
        x = jnp.concatenate([x_ref[2 * g], x_ref[2 * g + 1]], axis=1)
        r = jnp.dot(x.astype(BF16), w1_ref[g], preferred_element_type=F32)
        x_ref[2 * g] = r[:, :LANES]
        x_ref[2 * g + 1] = r[:, LANES:gl]
        s_ref[:, g * gl:(g + 1) * gl] = r[:, gl:]

    a_re = [al_ref[g, 0:1, :] for g in range(gb)]
    a_im = [al_ref[g, 1:2, :] for g in range(gb)]
    is_fwd = lax.broadcasted_iota(jnp.int32, (SUBLANES, n2), 1) < SSM_STATE

    def step(h, s):
        out = []
        for g in range(gb):
            hr, hi = h[:, g * gl:g * gl + n2], h[:, g * gl + n2:(g + 1) * gl]
            sr, si = s[:, g * gl:g * gl + n2], s[:, g * gl + n2:(g + 1) * gl]
            out.append(a_re[g] * hr - a_im[g] * hi + sr)
            out.append(a_re[g] * hi + a_im[g] * hr + si)
        return jnp.concatenate(out, axis=1)

    cpt = SUBLANES // nb
    n_tiles = nc // cpt
    tile_rows = lambda t: pl.ds(pl.multiple_of(t * SUBLANES, SUBLANES), SUBLANES)

    def fwd_body(t, h):
        s = s_ref[tile_rows(t), :]
        before = []
        for k in range(cpt):
            before.append(h)
            h = step(h, s[k * nb:(k + 1) * nb])
        hf_ref[tile_rows(t), :] = jnp.concatenate(before, axis=0)
        return h

    lax.fori_loop(0, n_tiles, fwd_body, jnp.zeros((nb, gb * gl), F32))

    def bwd_body(i, h):
        t = n_tiles - 1 - i
        s = s_ref[tile_rows(t), :]
        before = [None] * cpt
        for k in reversed(range(cpt)):
            before[k] = h
            h = step(h, s[k * nb:(k + 1) * nb])
        hb = jnp.concatenate(before, axis=0)
        hf = hf_ref[tile_rows(t), :]
        merged = [jnp.where(is_fwd, hf[:, k * n2:(k + 1) * n2], hb[:, k * n2:(k + 1) * n2])
                  for k in range(2 * gb)]
        h_ref[tile_rows(t), :] = jnp.concatenate(merged, axis=1)
        return h

    lax.fori_loop(0, n_tiles, bwd_body, jnp.zeros((nb, gb * gl), F32))

    for g in range(gb):
        r = jnp.dot(h_ref[:, g * gl:(g + 1) * gl].astype(BF16), cm_ref[g],
                    preferred_element_type=F32)
        x_ref[2 * g] += r[:, :LANES]
        x_ref[2 * g + 1] += r[:, LANES:]

    def scatter_body(ct, carry):
        for b in range(nb):
            for h in range(2):
                w = [x_ref[2 * g + h, chunk_rows(ct, b), :] for g in range(gb)]
                v = _swap_pieces(w)
                for j in range(half):
                    rows = token_rows(ct, h, j)
                    y_ref[b, rows, :] = v[j] + d_ref[...] * u_ref[b, rows, :]
        return carry

    lax.fori_loop(0, nc // SUBLANES, scatter_body, 0)


def _s5_chunks(u, w1, cm, al, d):
    nb, seq, width = u.shape
    nc = seq // CHUNK
    rows = nb * nc
    gb = GROUPS_PER_BLOCK
    assert SUBLANES % nb == 0 and nc % SUBLANES == 0 and width % LANES == 0
    blk = pl.BlockSpec((nb, seq, LANES), lambda i: (0, 0, i))
    return pl.pallas_call(
        functools.partial(_s5_chunk_kernel, nb=nb, nc=nc),
        grid=(width // LANES,),
        in_specs=[
            blk,
            pl.BlockSpec((gb,) + w1.shape[1:], lambda i: (i, 0, 0)),
            pl.BlockSpec((gb,) + cm.shape[1:], lambda i: (i, 0, 0)),
            pl.BlockSpec((gb,) + al.shape[1:], lambda i: (i, 0, 0)),
            pl.BlockSpec((1, LANES), lambda i: (0, i)),
        ],
        out_specs=blk,
        out_shape=jax.ShapeDtypeStruct(u.shape, F32),
        scratch_shapes=[pltpu.VMEM((2 * gb, rows, LANES), F32)]
        + [pltpu.VMEM((rows, gb * GROUP_LANES), F32)] * 3,
        compiler_params=pltpu.CompilerParams(
            dimension_semantics=("parallel",), vmem_limit_bytes=VMEM_LIMIT_BYTES),
        name="s5_chunks",
    )(u, w1, cm, al, d.reshape(1, width))


def _mix_out_kernel(x_ref, a_ref, y_ref, wglu_ref, bglu_ref, og_ref, wo_a_ref, wo_s_ref, pg_ref,
                    o_ref):
    y = y_ref[...]
    g = y * (0.5 * (1.0 + jnp.tanh(math.sqrt(2.0 / math.pi) * (y + 0.044715 * (y * y * y)))))
    z = jnp.dot(g.astype(BF16), wglu_ref[...], preferred_element_type=F32) + bglu_ref[...]
    s = g * _sigmoid(z)
    s = s * _rms_scale(s) * og_ref[...]
    mixed = (jnp.dot(a_ref[...], wo_a_ref[...], preferred_element_type=F32)
             + jnp.dot(s.astype(BF16), wo_s_ref[...], preferred_element_type=F32))
    o_ref[...] = x_ref[...] + mixed * _rms_scale(mixed) * pg_ref[...]


def _mix_out(x, a, y, w_glu, b_glu, out_g, w_out, post_g, *, tm=512):
    t, d = x.shape
    wa = a.shape[1]
    ws = y.shape[1]
    const = lambda shape: pl.BlockSpec(shape, lambda i: (0, 0))
    return pl.pallas_call(
        _mix_out_kernel,
        grid=(t // tm,),
        in_specs=[
            pl.BlockSpec((tm, d), lambda i: (i, 0)),
            pl.BlockSpec((tm, wa), lambda i: (i, 0)),
            pl.BlockSpec((tm, ws), lambda i: (i, 0)),
            const((ws, ws)), const((1, ws)), const((1, ws)),
            pl.BlockSpec((wa, d), lambda i: (0, 0)),
            pl.BlockSpec((ws, d), lambda i: (wa // ws, 0)),
            const((1, d)),
        ],
        out_specs=pl.BlockSpec((tm, d), lambda i: (i, 0)),
        out_shape=jax.ShapeDtypeStruct((t, d), F32),
        compiler_params=pltpu.CompilerParams(
            dimension_semantics=("parallel",), vmem_limit_bytes=VMEM_LIMIT_BYTES),
        name="mix_out",
    )(x, a, y, w_glu, b_glu.reshape(1, ws), out_g.reshape(1, ws), w_out, w_out,
      post_g.reshape(1, d))


def kernel(x, ff1_pre_g, ff1_w_gate, ff1_w_up, ff1_w_down, ff1_post_g, mix_pre_g, w_in, lam_q1, lam_k1, lam_q2, lam_k2, attn_head_g, ssm_lam_re, ssm_lam_im, ssm_log_dt, ssm_b_re, ssm_b_im, ssm_c_re, ssm_c_im, ssm_d, ssm_w_glu, ssm_b_glu, ssm_out_g, w_out, mix_post_g, ff2_pre_g, ff2_w_gate, ff2_w_up, ff2_w_down, ff2_post_g):
    batch, seq, d_model = x.shape
    depth = w_in.shape[0]
    ssm_width = ssm_w_glu.shape[-1]
    slopes = jnp.asarray([2.0 ** (-8.0 * (i + 1) / ATTN_HEADS) for i in range(ATTN_HEADS)], F32)
    bf = lambda w: w.astype(BF16)

    xt = x.reshape(batch * seq, d_model)
    for l in range(depth):
        ff1 = (ff1_pre_g[l], ff1_w_gate[l], ff1_w_up[l], ff1_w_down[l], ff1_post_g[l])
        head, ff1_w, _ = _ffn(xt, *ff1, tm=FFN_HEAD_ROWS, tf=256, single_buffer_x=True,
                              tiles=(0, 1), emit_weights=True)
        rest = (FFN_HEAD_ROWS // 512, (batch * seq - FFN_HEAD_ROWS) // 512)
        xt, _, (w_in_bf, w_glu_bf, w_out_bf) = _ffn(
            xt, ff1[0], *ff1_w, ff1[4], tiles=rest, into=head,
            cast=(w_in[l], ssm_w_glu[l], w_out[l]))

        qkv, u = _in_proj(xt, mix_pre_g[l], w_in_bf, ssm_width=ssm_width)
        lam_init = 0.8 - 0.6 * math.exp(-0.3 * l)
        a, ff2_w = _attention(qkv, slopes, lam_q1[l], lam_k1[l], lam_q2[l], lam_k2[l],
                              attn_head_g[l], batch=batch, seq=seq, lam_init=lam_init,
                              cast=(ff2_w_gate[l], ff2_w_up[l], ff2_w_down[l]))

        w1, cm, al = _s5_prep(ssm_lam_re[l], ssm_lam_im[l], ssm_log_dt[l], ssm_b_re[l],
                              ssm_b_im[l], ssm_c_re[l], ssm_c_im[l])
        y = _s5_chunks(u.reshape(batch, seq, ssm_width), w1, cm, al, ssm_d[l])
        y = y.reshape(batch * seq, ssm_width)

        xt = _mix_out(xt, a, y, w_glu_bf, ssm_b_glu[l], ssm_out_g[l], w_out_bf, mix_post_g[l])

        xt, _, _ = _ffn(xt, ff2_pre_g[l], *ff2_w, ff2_post_g[l])
    return xt.reshape(batch, seq, d_model)
```

```python
import functools
import math

import jax
import jax.numpy as jnp
import numpy as np
from jax import lax
from jax.experimental import pallas as pl
from jax.experimental.pallas import tpu as pltpu

F32 = jnp.float32
BF16 = jnp.bfloat16

NORM_EPS = 1e-6
ATTN_HEADS = 8
ATTN_HEAD_DIM = 64
ATTN_VALUE_DIM = 2 * ATTN_HEAD_DIM
POS_SPLIT = 16
SSM_GROUP = 16
SSM_STATE = 64
CHUNK = 16
GROUP_LANES = CHUNK * SSM_GROUP
SUBLANES = 8
LANES = 128
BF16_ROWS = 16
GROUPS_PER_BLOCK = LANES // SSM_GROUP
PREP_GROUPS_PER_STEP = 8
FFN_HEAD_ROWS = 1024

VMEM_LIMIT_BYTES = 56 * 1024 * 1024


def _rms_scale(x):
    return lax.rsqrt(jnp.mean(x * x, axis=-1, keepdims=True) + NORM_EPS)


def _sigmoid(x):
    return 1.0 / (1.0 + jnp.exp(-x))


class _CastJobs:
    def __init__(self, arrays, n_steps, flat_index):
        self.flat_index = flat_index
        self.views, self.slabs = list(arrays), []
        for a in arrays:
            rows, cols = a.shape
            assert rows % BF16_ROWS == 0
            n = max(k for k in range(1, n_steps + 1) if (rows // BF16_ROWS) % k == 0)
            self.slabs.append((rows // n, cols, n))

    def _spec(self, slab):
        r, c, n = slab
        return pl.BlockSpec((r, c), lambda *g: (jnp.minimum(self.flat_index(*g), n - 1), 0))

    @property
    def in_specs(self):
        return [self._spec(s) for s in self.slabs]

    out_specs = in_specs

    @property
    def out_shapes(self):
        return [jax.ShapeDtypeStruct(v.shape, BF16) for v in self.views]

    def __len__(self):
        return len(self.views)


def _run_cast_jobs(in_refs, out_refs):
    for i_ref, o_ref in zip(in_refs, out_refs):
        o_ref[...] = i_ref[...].astype(BF16)


def _ffn_kernel(*refs, n_jobs, has_into, emit_weights):
    x_ref, pre_g_ref, wg_ref, wu_ref, wd_ref, post_g_ref = refs[:6]
    n_in = 6 + int(has_into)
    job_in = refs[n_in:n_in + n_jobs]
    o_ref = refs[n_in + n_jobs]
    n_out = n_in + n_jobs + 1
    w_out = refs[n_out:n_out + 3] if emit_weights else ()
    n_out += len(w_out)
    job_out = refs[n_out:n_out + n_jobs]
    h_ref = refs[n_out + n_jobs]
    j = pl.program_id(1)
    last = pl.num_programs(1) - 1

    @pl.when(j == 0)
    def _():
        x = x_ref[...]
        h_ref[...] = (x * _rms_scale(x) * pre_g_ref[...]).astype(BF16)
        o_ref[...] = jnp.zeros_like(o_ref)

    _run_cast_jobs(job_in, job_out)
    wg, wu, wd = (r[...].astype(BF16) for r in (wg_ref, wu_ref, wd_ref))
    for w_ref, w in zip(w_out, (wg, wu, wd)):
        w_ref[...] = w
    h = h_ref[...]
    gate = jnp.dot(h, wg, preferred_element_type=F32)
    up = jnp.dot(h, wu, preferred_element_type=F32)
    act = (gate * _sigmoid(gate) * up).astype(BF16)
    o_ref[...] += jnp.dot(act, wd, preferred_element_type=F32)

    @pl.when(j == last)
    def _():
        acc = o_ref[...]
        o_ref[...] = x_ref[...] + 0.5 * (acc * _rms_scale(acc) * post_g_ref[...])


def _ffn(x, pre_g, w_gate, w_up, w_down, post_g, *, cast=(), tm=512, tf=512,
         single_buffer_x=False, tiles=None, into=None, emit_weights=False):
    t, d = x.shape
    f = w_gate.shape[1]
    assert t % tm == 0 and f % tf == 0
    first, nt = tiles if tiles is not None else (0, t // tm)
    assert not emit_weights or nt == 1
    nj = f // tf
    jobs = _CastJobs(cast, nt * nj, lambda i, j: i * nj + j)
    x_mode = dict(pipeline_mode=pl.Buffered(1)) if single_buffer_x else {}
    w_cols = pl.BlockSpec((d, tf), lambda i, j: (0, j))
    w_rows = pl.BlockSpec((tf, d), lambda i, j: (j, 0))
    into_spec = [pl.BlockSpec(memory_space=pl.ANY)] if into is not None else []
    w_specs = [w_cols, w_cols, w_rows] if emit_weights else []
    w_shapes = [jax.ShapeDtypeStruct(w.shape, BF16) for w in (w_gate, w_up, w_down)]
    outs = pl.pallas_call(
        functools.partial(_ffn_kernel, n_jobs=len(jobs), has_into=into is not None,
                          emit_weights=emit_weights),
        grid=(nt, nj),
        in_specs=[
            pl.BlockSpec((tm, d), lambda i, j: (first + i, 0), **x_mode),
            pl.BlockSpec((1, d), lambda i, j: (0, 0)),
            w_cols, w_cols, w_rows,
            pl.BlockSpec((1, d), lambda i, j: (0, 0)),
        ] + into_spec + jobs.in_specs,
        out_specs=[pl.BlockSpec((tm, d), lambda i, j: (first + i, 0))] + w_specs + jobs.out_specs,
        out_shape=[jax.ShapeDtypeStruct((t, d), F32)] + w_shapes[:len(w_specs)] + jobs.out_shapes,
        input_output_aliases={6: 0} if into is not None else {},
        scratch_shapes=[pltpu.VMEM((tm, d), BF16)],
        compiler_params=pltpu.CompilerParams(
            dimension_semantics=("arbitrary", "arbitrary"),
            vmem_limit_bytes=VMEM_LIMIT_BYTES),
        name="ffn",
    )(x, pre_g.reshape(1, d), w_gate, w_up, w_down, post_g.reshape(1, d),
      *([into] if into is not None else []), *jobs.views)
    n_w = len(w_specs)
    return outs[0], list(outs[1:1 + n_w]), list(outs[1 + n_w:])


def _in_proj_kernel(x_ref, g_ref, w_ref, qkv_ref, u_ref, h_ref, *, n_qkv):
    j = pl.program_id(1)

    @pl.when(j == 0)
    def _():
        x = x_ref[...]
        h_ref[...] = (x * _rms_scale(x) * g_ref[...]).astype(BF16)

    def project():
        return jnp.dot(h_ref[...], w_ref[...].astype(BF16), preferred_element_type=F32)

    @pl.when(j < n_qkv)
    def _():
        qkv_ref[...] = project().astype(BF16)

    @pl.when(j == n_qkv)
    def _():
        u_ref[...] = project()


def _in_proj(x, g, w_in, *, ssm_width, tm=1024):
    t, d = x.shape
    n = w_in.shape[1]
    tn = ssm_width
    n_qkv = (n - ssm_width) // tn
    assert t % tm == 0 and n == (n_qkv + 1) * tn
    return pl.pallas_call(
        functools.partial(_in_proj_kernel, n_qkv=n_qkv),
        grid=(t // tm, n_qkv + 1),
        in_specs=[
            pl.BlockSpec((tm, d), lambda i, j: (i, 0)),
            pl.BlockSpec((1, d), lambda i, j: (0, 0)),
            pl.BlockSpec((d, tn), lambda i, j: (0, j)),
        ],
        out_specs=[
            pl.BlockSpec((tm, tn), lambda i, j: (i, jnp.minimum(j, n_qkv - 1))),
            pl.BlockSpec((tm, tn), lambda i, j: (i, 0)),
        ],
        out_shape=[
            jax.ShapeDtypeStruct((t, n - ssm_width), BF16),
            jax.ShapeDtypeStruct((t, ssm_width), F32),
        ],
        scratch_shapes=[pltpu.VMEM((tm, d), BF16)],
        compiler_params=pltpu.CompilerParams(
            dimension_semantics=("parallel", "arbitrary"),
            vmem_limit_bytes=VMEM_LIMIT_BYTES),
        name="in_proj",
    )(x, g.reshape(1, d), w_in)


def _attn_kernel(*refs, tq, nq, lam_init, n_jobs):
    slopes_ref, lq1_ref, lk1_ref, lq2_ref, lk2_ref, hg_ref, q_ref = refs[:7]
    k_refs = refs[7:7 + nq]
    v_refs = refs[7 + nq:7 + 2 * nq]
    n_in = 7 + 2 * nq
    job_in = refs[n_in:n_in + n_jobs]
    o_ref = refs[n_in + n_jobs]
    job_out = refs[n_in + n_jobs + 1:n_in + 2 * n_jobs + 1]
    qf_ref, kf_ref, bias_ref, s_ref, p_ref = refs[n_in + 2 * n_jobs + 1:]
    h = pl.program_id(0)
    qi = pl.program_id(1)
    b = pl.program_id(2)
    e = q_ref.shape[1]

    @pl.when(b == 0)
    def _():
        slope = slopes_ref[h]
        lane = lax.broadcasted_iota(jnp.int32, (tq, e), 1)
        row = lax.broadcasted_iota(jnp.int32, (tq, e), 0)
        row_hi = (slope * POS_SPLIT) * (row // POS_SPLIT).astype(F32)
        row_lo = slope * (row % POS_SPLIT).astype(F32)
        q_base = jnp.where(lane == 0, row_hi, jnp.where(lane == 1, row_lo,
                           jnp.where(lane < 4, 1.0, 0.0)))
        k_base = jnp.where(lane < 2, -1.0, jnp.where(lane == 2, row_hi,
                           jnp.where(lane == 3, row_lo, 0.0)))
        hi_q = (lane == 0).astype(F32)
        hi_k = (lane == 2).astype(F32)
        qf_ref[...] = (q_base + (slope * tq) * qi.astype(F32) * hi_q).astype(BF16)
        for slot in range(nq):
            chunk = lax.rem(qi + slot, nq)
            sign = jnp.where(chunk < qi, 1.0, jnp.where(chunk > qi, -1.0, 0.0))
            kf = sign * k_base + (sign * (slope * tq) * chunk.astype(F32)) * hi_k
            kf_ref[slot * tq:(slot + 1) * tq, :] = kf.astype(BF16)
        col = lax.broadcasted_iota(jnp.int32, (tq, tq), 1)
        bias_ref[...] = slope * jnp.abs(lax.broadcasted_iota(jnp.int32, (tq, tq), 0)
                                        - col).astype(F32)

    _run_cast_jobs(job_in, job_out)
    lam = (jnp.exp(jnp.sum(lq1_ref[...] * lk1_ref[...], axis=-1, keepdims=True))
           - jnp.exp(jnp.sum(lq2_ref[...] * lk2_ref[...], axis=-1, keepdims=True))
           + lam_init)

    q = q_ref[...] * jnp.asarray(ATTN_HEAD_DIM ** -0.5, BF16)
    k_aug = jnp.concatenate([jnp.concatenate([r[...] for r in k_refs], axis=0), kf_ref[...]],
                            axis=1)
    v = jnp.concatenate([r[...] for r in v_refs], axis=0)
    v_ext = jnp.concatenate([v, jnp.ones_like(v)], axis=1)
    lane = lax.broadcasted_iota(jnp.int32, q.shape, 1)
    nt = (((1,), (1,)), ((), ()))
    for c in range(2):
        in_map = (lane >= c * ATTN_HEAD_DIM) & (lane < (c + 1) * ATTN_HEAD_DIM)
        q_aug = jnp.concatenate([jnp.where(in_map, q, jnp.zeros_like(q)), qf_ref[...]], axis=1)
        s_ref[c, :, :tq] = lax.dot_general(q_aug, k_aug[:tq], nt,
                                           preferred_element_type=F32) - bias_ref[...]
        s_ref[c, :, tq:] = lax.dot_general(q_aug, k_aug[tq:], nt, preferred_element_type=F32)
        s = s_ref[c]
        p_ref[c] = jnp.exp(s - jnp.max(s, axis=-1, keepdims=True)).astype(BF16)
    outs = []
    for c in range(2):
        pv = jnp.dot(p_ref[c], v_ext, preferred_element_type=F32)
        outs.append(pv[:, :e] / pv[:, e:])
    o = outs[0] - lam * outs[1]
    o_ref[...] = (o * _rms_scale(o) * hg_ref[...] * (1.0 - lam_init)).astype(o_ref.dtype)


def _attention(qkv, slopes, lq1, lk1, lq2, lk2, head_g, *, batch, seq, lam_init, cast=(), tq=512):
    t = qkv.shape[0]
    e = ATTN_VALUE_DIM
    nq = seq // tq
    nh = ATTN_HEADS
    vec = lambda a: a.reshape(1, -1).astype(F32)
    small = lambda n: pl.BlockSpec((1, n), lambda h, qi, b: (0, 0))
    jobs = _CastJobs(cast, nh * nq * batch, lambda h, qi, b: (h * nq + qi) * batch + b)

    def chunk_spec(slot, col0):
        return pl.BlockSpec((tq, e), lambda h, qi, b: (b * nq + lax.rem(qi + slot, nq), col0 + h))

    outs = pl.pallas_call(
        functools.partial(_attn_kernel, tq=tq, nq=nq, lam_init=lam_init, n_jobs=len(jobs)),
        grid=(nh, nq, batch),
        in_specs=[
            pl.BlockSpec(memory_space=pltpu.SMEM),
            small(ATTN_HEAD_DIM), small(ATTN_HEAD_DIM), small(ATTN_HEAD_DIM), small(ATTN_HEAD_DIM),
            small(e),
            pl.BlockSpec((tq, e), lambda h, qi, b: (b * nq + qi, h)),
        ] + [chunk_spec(s, nh) for s in range(nq)] + [chunk_spec(s, 2 * nh) for s in range(nq)]
        + jobs.in_specs,
        out_specs=[pl.BlockSpec((tq, e), lambda h, qi, b: (b * nq + qi, h))] + jobs.out_specs,
        out_shape=[jax.ShapeDtypeStruct((t, nh * e), BF16)] + jobs.out_shapes,
        scratch_shapes=[pltpu.VMEM((tq, e), BF16), pltpu.VMEM((seq, e), BF16),
                        pltpu.VMEM((tq, tq), F32), pltpu.VMEM((2, tq, seq), F32),
                        pltpu.VMEM((2, tq, seq), BF16)],
        compiler_params=pltpu.CompilerParams(
            dimension_semantics=("arbitrary", "arbitrary", "arbitrary"),
            vmem_limit_bytes=VMEM_LIMIT_BYTES),
        name="diff_attention",
    )(slopes, vec(lq1), vec(lk1), vec(lq2), vec(lk2), vec(head_g), qkv,
      *([qkv] * (2 * nq)), *jobs.views)
    return outs[0], list(outs[1:])


def _discretize(lam_re, lam_im, log_dt):
    dt = jnp.exp(log_dt)
    mag = jnp.exp(lam_re * dt)
    a_re = mag * jnp.cos(lam_im * dt)
    a_im = mag * jnp.sin(lam_im * dt)
    den = lam_re * lam_re + lam_im * lam_im
    nr = a_re - 1.0
    f_re = (nr * lam_re + a_im * lam_im) / den
    f_im = (a_im * lam_re - nr * lam_im) / den
    return a_re, a_im, f_re, f_im


def _cmul(ar, ai, br, bi):
    return ar * br - ai * bi, ar * bi + ai * br


def _split_bf16(x, parts):
    out = []
    for k in range(parts):
        piece = x.astype(BF16)
        out.append(piece)
        if k + 1 < parts:
            x = x - piece.astype(F32)
    return out


def _copy_dot(x, onehot, dims):
    return sum(lax.dot_general(piece, onehot, dims, preferred_element_type=F32)
               for piece in _split_bf16(x, 3))


def _dot_3pass(a_parts, b_parts, dims):
    a_hi, a_lo = a_parts
    b_hi, b_lo = b_parts
    dot = lambda x, y: lax.dot_general(x, y, dims, preferred_element_type=F32)
    return dot(a_hi, b_hi) + (dot(a_hi, b_lo) + dot(a_lo, b_hi))


def _int_power(a_re, a_im, e, nbits):
    shape = jnp.broadcast_shapes(a_re.shape, e.shape)
    p_re, p_im = jnp.ones(shape, F32), jnp.zeros(shape, F32)
    s_re, s_im = a_re, a_im
    for k in range(nbits):
        bit = (e & (1 << k)) != 0
        m_re, m_im = _cmul(p_re, p_im, s_re, s_im)
        p_re, p_im = jnp.where(bit, m_re, p_re), jnp.where(bit, m_im, p_im)
        if k + 1 < nbits:
            s_re, s_im = _cmul(s_re, s_im, s_re, s_im)
    return p_re, p_im


def _shift_lanes(x, s, left):
    if s == 0:
        return x
    n = x.shape[-1]
    lane = lax.broadcasted_iota(jnp.int32, x.shape, 1)
    if left:
        return jnp.where(lane < n - s, pltpu.roll(x, n - s, axis=1), 0.0)
    return jnp.where(lane >= s, pltpu.roll(x, s, axis=1), 0.0)


def _s5_prep_kernel(lre_ref, lim_ref, ldt_ref, btre_ref, btim_ref, cre_ref, cim_ref,
                    w1_ref, cm_ref, al_ref):
    for i in range(w1_ref.shape[0]):
        _s5_prep_group(i, lre_ref, lim_ref, ldt_ref, btre_ref, btim_ref, cre_ref, cim_ref,
                       w1_ref, cm_ref, al_ref)


def _s5_prep_group(i, lre_ref, lim_ref, ldt_ref, btre_r_ref, btim_r_ref, cre_ref, cim_ref,
                   w1_ref, cm_ref, al_ref):
    n, p, l = SSM_STATE, SSM_GROUP, CHUNK
    nbits = l.bit_length() - 1
    contract0 = (((0,), (0,)), ((), ()))
    contract_mm = (((1,), (0,)), ((), ()))

    a_re, a_im, f_re, f_im = _discretize(lre_ref[i], lim_ref[i], ldt_ref[i])
    bt_re, bt_im = btre_r_ref[i], btim_r_ref[i]
    bb_re, bb_im = _cmul(f_re, f_im, bt_re, bt_im)
    tau = lax.broadcasted_iota(jnp.int32, (l, 2 * n), 0)
    lane = lax.broadcasted_iota(jnp.int32, (l, 2 * n), 1)
    pw_re, pw_im = _int_power(a_re, a_im, jnp.where(lane < n, l - 1 - tau, tau), nbits)
    bm_re = pw_re[:, None, :] * bb_re[None, :, :] - pw_im[:, None, :] * bb_im[None, :, :]
    bm_im = pw_re[:, None, :] * bb_im[None, :, :] + pw_im[:, None, :] * bb_re[None, :, :]
    w1_ref[i, :, GROUP_LANES:GROUP_LANES + 2 * n] = bm_re.reshape(l * p, 2 * n).astype(BF16)
    w1_ref[i, :, GROUP_LANES + 2 * n:] = bm_im.reshape(l * p, 2 * n).astype(BF16)
    al_re, al_im = a_re, a_im
    for _ in range(nbits):
        al_re, al_im = _cmul(al_re, al_im, al_re, al_im)
    al_ref[i] = jnp.concatenate([al_re, al_im], axis=0)

    quant = jnp.concatenate([a_re, a_im, jnp.zeros((SUBLANES - 2, 2 * n), F32)], axis=0)
    quant_t = quant.T
    a_re_c, a_im_c = quant_t[:, 0:1], quant_t[:, 1:2]

    lanes_p = lax.broadcasted_iota(jnp.int32, (p, GROUP_LANES), 1)
    rows_p = lax.broadcasted_iota(jnp.int32, (p, GROUP_LANES), 0)
    tile_p = (lanes_p % p == rows_p).astype(BF16)
    tile_lag = ((l - 1) - lanes_p // p == rows_p).astype(BF16)
    ac_re, ac_im = _cmul(_copy_dot(pw_re, tile_lag, contract0), _copy_dot(pw_im, tile_lag, contract0),
                         _copy_dot(cre_ref[i], tile_p, contract0),
                         _copy_dot(cim_ref[i], tile_p, contract0))
    ac_re_parts, ac_im_parts = _split_bf16(ac_re, 2), _split_bf16(ac_im, 2)
    is_fwd = lax.broadcasted_iota(jnp.int32, bb_re.shape, 1) < n
    t_rows = None
    for d in range(2):
        in_dir = is_fwd if d == 0 else ~is_fwd
        taps = (_dot_3pass(_split_bf16(jnp.where(in_dir, bb_re, 0.0), 2), ac_re_parts, contract_mm)
                - _dot_3pass(_split_bf16(jnp.where(in_dir, bb_im, 0.0), 2), ac_im_parts,
                             contract_mm))
        rows = [_shift_lanes(taps, p * (tp if d == 0 else l - 1 - tp), left=(d == 1))
                for tp in range(l)]
        rows = jnp.concatenate(rows, axis=0)
        t_rows = rows if t_rows is None else t_rows + rows
    w1_ref[i, :, 0:GROUP_LANES] = t_rows.astype(BF16)
    r_re, r_im = _cmul(a_re_c, a_im_c, ac_re, ac_im)
    cm_ref[i] = jnp.concatenate([r_re, -r_im], axis=0).astype(BF16)


def _s5_prep(lam_re, lam_im, log_dt, b_re, b_im, c_re, c_im):
    _, g, n = lam_re.shape
    p = b_re.shape[-1]
    row = lambda a: jnp.concatenate([a[0], a[1]], axis=-1)[:, None, :]
    ldt_r = row(jnp.broadcast_to(log_dt[:, :, None], (2, g, n)))
    bt = lambda b: jnp.concatenate([jnp.swapaxes(b[0], 1, 2), jnp.swapaxes(b[1], 1, 2)], axis=-1)
    cr = lambda c: jnp.concatenate([c[0], c[1]], axis=-1)

    gb = PREP_GROUPS_PER_STEP
    assert g % gb == 0
    spec_r = lambda rows: pl.BlockSpec((gb, rows, 2 * n), lambda i: (i, 0, 0))
    return pl.pallas_call(
        _s5_prep_kernel,
        grid=(g // gb,),
        in_specs=[spec_r(1), spec_r(1), spec_r(1), spec_r(p), spec_r(p), spec_r(p), spec_r(p)],
        out_specs=[
            pl.BlockSpec((gb, GROUP_LANES, GROUP_LANES + 4 * n), lambda i: (i, 0, 0)),
            pl.BlockSpec((gb, 4 * n, GROUP_LANES), lambda i: (i, 0, 0)),
            pl.BlockSpec((gb, 2, 2 * n), lambda i: (i, 0, 0)),
        ],
        out_shape=[
            jax.ShapeDtypeStruct((g, GROUP_LANES, GROUP_LANES + 4 * n), BF16),
            jax.ShapeDtypeStruct((g, 4 * n, GROUP_LANES), BF16),
            jax.ShapeDtypeStruct((g, 2, 2 * n), F32),
        ],
        compiler_params=pltpu.CompilerParams(dimension_semantics=("parallel",)),
        name="s5_prep",
    )(row(lam_re), row(lam_im), ldt_r, bt(b_re), bt(b_im), cr(c_re), cr(c_im))


def _swap_pieces(v):
    piece = lax.broadcasted_iota(jnp.int32, v[0].shape, 1) // SSM_GROUP
    v = list(v)
    for d in (4, 2, 1):
        hi = (piece & d) != 0
        for a in range(8):
            if a & d:
                continue
            va, vb = v[a], v[a + d]
            v[a] = jnp.where(hi, pltpu.roll(vb, d * SSM_GROUP, axis=1), va)
            v[a + d] = jnp.where(hi, vb, pltpu.roll(va, LANES - d * SSM_GROUP, axis=1))
    return v


def _s5_chunk_kernel(u_ref, perm_ref, w1_ref, cm_ref, al_ref, d_ref, y_ref,
                     a_ref, x_ref, s_ref, hf_ref, h_ref, *, nb, nc):
    gl = GROUP_LANES
    gb = GROUPS_PER_BLOCK
    n2 = 2 * SSM_STATE
    half = CHUNK // 2
    ctile = SUBLANES * CHUNK

    def token_rows(ct, h, j):
        return pl.ds(ct * ctile + h * half + j, SUBLANES, stride=CHUNK)

    def chunk_rows(ct, b):
        return pl.ds(ct * SUBLANES * nb + b, SUBLANES, stride=nb)

    rows_all = nb * nc

    def gather_body(ct, carry):
        for b in range(nb):
            for h in range(2):
                for j in range(half):
                    a_ref[j, pl.ds(h * rows_all + ct * SUBLANES * nb + b, SUBLANES, stride=nb), :] = (
                        u_ref[b, token_rows(ct, h, j), :])
        return carry

    lax.fori_loop(0, nc // SUBLANES, gather_body, 0)
    lhs = jnp.concatenate([a_ref[j] for j in range(half)], axis=1).astype(BF16)
    xp = jnp.dot(lhs, perm_ref[...], preferred_element_type=F32).astype(BF16)

    for g in range(gb):
        x = jnp.concatenate([xp[:rows_all, g * LANES:(g + 1) * LANES],
                             xp[rows_all:, g * LANES:(g + 1) * LANES]], axis=1)
        r = jnp.dot(x, w1_ref[g], preferred_element_type=F32)
        x_ref[2 * g] = r[:, :LANES]
        x_ref[2 * g + 1] = r[:, LANES:gl]
        s_ref[:, g * gl:(g + 1) * gl] = r[:, gl:]

    a_re = [al_ref[g, 0:1, :] for g in range(gb)]
    a_im = [al_ref[g, 1:2, :] for g in range(gb)]
    is_fwd = lax.broadcasted_iota(jnp.int32, (SUBLANES, n2), 1) < SSM_STATE

    def step(h, s):
        out = []
        for g in range(gb):
            hr, hi = h[:, g * gl:g * gl + n2], h[:, g * gl + n2:(g + 1) * gl]
            sr, si = s[:, g * gl:g * gl + n2], s[:, g * gl + n2:(g + 1) * gl]
            out.append(a_re[g] * hr - a_im[g] * hi + sr)
            out.append(a_re[g] * hi + a_im[g] * hr + si)
        return jnp.concatenate(out, axis=1)

    cpt = SUBLANES // nb
    n_tiles = nc // cpt
    tile_rows = lambda t: pl.ds(pl.multiple_of(t * SUBLANES, SUBLANES), SUBLANES)

    def fwd_body(t, h):
        s = s_ref[tile_rows(t), :]
        before = []
        for k in range(cpt):
            before.append(h)
            h = step(h, s[k * nb:(k + 1) * nb])
        hf_ref[tile_rows(t), :] = jnp.concatenate(before, axis=0)
        return h

    lax.fori_loop(0, n_tiles, fwd_body, jnp.zeros((nb, gb * gl), F32))

    def bwd_body(i, h):
        t = n_tiles - 1 - i
        s = s_ref[tile_rows(t), :]
        before = [None] * cpt
        for k in reversed(range(cpt)):
            before[k] = h
            h = step(h, s[k * nb:(k + 1) * nb])
        hb = jnp.concatenate(before, axis=0)
        hf = hf_ref[tile_rows(t), :]
        merged = [jnp.where(is_fwd, hf[:, k * n2:(k + 1) * n2], hb[:, k * n2:(k + 1) * n2])
                  for k in range(2 * gb)]
        h_ref[tile_rows(t), :] = jnp.concatenate(merged, axis=1)
        return h

    lax.fori_loop(0, n_tiles, bwd_body, jnp.zeros((nb, gb * gl), F32))

    for g in range(gb):
        r = jnp.dot(h_ref[:, g * gl:(g + 1) * gl].astype(BF16), cm_ref[g],
                    preferred_element_type=F32)
        x_ref[2 * g] += r[:, :LANES]
        x_ref[2 * g + 1] += r[:, LANES:]

    def scatter_body(ct, carry):
        for b in range(nb):
            for h in range(2):
                w = [x_ref[2 * g + h, chunk_rows(ct, b), :] for g in range(gb)]
                v = _swap_pieces(w)
                for j in range(half):
                    rows = token_rows(ct, h, j)
                    y_ref[b, rows, :] = v[j] + d_ref[...] * u_ref[b, rows, :]
        return carry

    lax.fori_loop(0, nc // SUBLANES, scatter_body, 0)


def _s5_chunks(u, w1, cm, al, d):
    nb, seq, width = u.shape
    nc = seq // CHUNK
    rows = nb * nc
    gb = GROUPS_PER_BLOCK
    assert SUBLANES % nb == 0 and nc % SUBLANES == 0 and width % LANES == 0
    blk = pl.BlockSpec((nb, seq, LANES), lambda i: (0, 0, i))
    half = CHUNK // 2
    src = np.arange(half * LANES)
    j, g, p = src // LANES, (src % LANES) // SSM_GROUP, src % SSM_GROUP
    perm = np.zeros((half * LANES, half * LANES), np.float32)
    perm[src, g * LANES + j * SSM_GROUP + p] = 1.0
    return pl.pallas_call(
        functools.partial(_s5_chunk_kernel, nb=nb, nc=nc),
        grid=(width // LANES,),
        in_specs=[
            blk,
            pl.BlockSpec(perm.shape, lambda i: (0, 0), pipeline_mode=pl.Buffered(1)),
            pl.BlockSpec((gb,) + w1.shape[1:], lambda i: (i, 0, 0)),
            pl.BlockSpec((gb,) + cm.shape[1:], lambda i: (i, 0, 0)),
            pl.BlockSpec((gb,) + al.shape[1:], lambda i: (i, 0, 0)),
            pl.BlockSpec((1, LANES), lambda i: (0, i)),
        ],
        out_specs=blk,
        out_shape=jax.ShapeDtypeStruct(u.shape, F32),
        scratch_shapes=[pltpu.VMEM((half, 2 * rows, LANES), F32),
                        pltpu.VMEM((2 * gb, rows, LANES), F32)]
        + [pltpu.VMEM((rows, gb * GROUP_LANES), F32)] * 3,
        compiler_params=pltpu.CompilerParams(
            dimension_semantics=("parallel",), vmem_limit_bytes=VMEM_LIMIT_BYTES),
        name="s5_chunks",
    )(u, jnp.asarray(perm, BF16), w1, cm, al, d.reshape(1, width))


def _mix_out_kernel(x_ref, a_ref, y_ref, wglu_ref, bglu_ref, og_ref, wo_a_ref, wo_s_ref, pg_ref,
                    o_ref):
    y = y_ref[...]
    g = y * (0.5 * (1.0 + jnp.tanh(math.sqrt(2.0 / math.pi) * (y + 0.044715 * (y * y * y)))))
    z = jnp.dot(g.astype(BF16), wglu_ref[...], preferred_element_type=F32) + bglu_ref[...]
    s = g * _sigmoid(z)
    s = s * _rms_scale(s) * og_ref[...]
    mixed = (jnp.dot(a_ref[...], wo_a_ref[...], preferred_element_type=F32)
             + jnp.dot(s.astype(BF16), wo_s_ref[...], preferred_element_type=F32))
    o_ref[...] = x_ref[...] + mixed * _rms_scale(mixed) * pg_ref[...]


def _mix_out(x, a, y, w_glu, b_glu, out_g, w_out, post_g, *, tm=512):
    t, d = x.shape
    wa = a.shape[1]
    ws = y.shape[1]
    const = lambda shape: pl.BlockSpec(shape, lambda i: (0, 0))
    return pl.pallas_call(
        _mix_out_kernel,
        grid=(t // tm,),
        in_specs=[
            pl.BlockSpec((tm, d), lambda i: (i, 0)),
            pl.BlockSpec((tm, wa), lambda i: (i, 0)),
            pl.BlockSpec((tm, ws), lambda i: (i, 0)),
            const((ws, ws)), const((1, ws)), const((1, ws)),
            pl.BlockSpec((wa, d), lambda i: (0, 0)),
            pl.BlockSpec((ws, d), lambda i: (wa // ws, 0)),
            const((1, d)),
        ],
        out_specs=pl.BlockSpec((tm, d), lambda i: (i, 0)),
        out_shape=jax.ShapeDtypeStruct((t, d), F32),
        compiler_params=pltpu.CompilerParams(
            dimension_semantics=("parallel",), vmem_limit_bytes=VMEM_LIMIT_BYTES),
        name="mix_out",
    )(x, a, y, w_glu, b_glu.reshape(1, ws), out_g.reshape(1, ws), w_out, w_out,
      post_g.reshape(1, d))


def kernel(x, ff1_pre_g, ff1_w_gate, ff1_w_up, ff1_w_down, ff1_post_g, mix_pre_g, w_in, lam_q1, lam_k1, lam_q2, lam_k2, attn_head_g, ssm_lam_re, ssm_lam_im, ssm_log_dt, ssm_b_re, ssm_b_im, ssm_c_re, ssm_c_im, ssm_d, ssm_w_glu, ssm_b_glu, ssm_out_g, w_out, mix_post_g, ff2_pre_g, ff2_w_gate, ff2_w_up, ff2_w_down, ff2_post_g):
    batch, seq, d_model = x.shape
    depth = w_in.shape[0]
    ssm_width = ssm_w_glu.shape[-1]
    slopes = jnp.asarray([2.0 ** (-8.0 * (i + 1) / ATTN_HEADS) for i in range(ATTN_HEADS)], F32)
    bf = lambda w: w.astype(BF16)

    xt = x.reshape(batch * seq, d_model)
    for l in range(depth):
        ff1 = (ff1_pre_g[l], ff1_w_gate[l], ff1_w_up[l], ff1_w_down[l], ff1_post_g[l])
        head, ff1_w, _ = _ffn(xt, *ff1, tm=FFN_HEAD_ROWS, tf=256, single_buffer_x=True,
                              tiles=(0, 1), emit_weights=True)
        rest = (FFN_HEAD_ROWS // 512, (batch * seq - FFN_HEAD_ROWS) // 512)
        xt, _, (w_in_bf, w_glu_bf, w_out_bf) = _ffn(
            xt, ff1[0], *ff1_w, ff1[4], tiles=rest, into=head,
            cast=(w_in[l], ssm_w_glu[l], w_out[l]))

        qkv, u = _in_proj(xt, mix_pre_g[l], w_in_bf, ssm_width=ssm_width)
        lam_init = 0.8 - 0.6 * math.exp(-0.3 * l)
        a, ff2_w = _attention(qkv, slopes, lam_q1[l], lam_k1[l], lam_q2[l], lam_k2[l],
                              attn_head_g[l], batch=batch, seq=seq, lam_init=lam_init,
                              cast=(ff2_w_gate[l], ff2_w_up[l], ff2_w_down[l]))

        w1, cm, al = _s5_prep(ssm_lam_re[l], ssm_lam_im[l], ssm_log_dt[l], ssm_b_re[l],
                              ssm_b_im[l], ssm_c_re[l], ssm_c_im[l])
        y = _s5_chunks(u.reshape(batch, seq, ssm_width), w1, cm, al, ssm_d[l])
        y = y.reshape(batch * seq, ssm_width)

        xt = _mix_out(xt, a, y, w_glu_bf, ssm_b_glu[l], ssm_out_g[l], w_out_bf, mix_post_g[l])

        xt, _, _ = _ffn(xt, ff2_pre_g[l], *ff2_w, ff2_post_g[l])
    return xt.reshape(batch, seq, d_model)
```

```python
import functools
import math

import jax
import jax.numpy as jnp
import numpy as np
from jax import lax
from jax.experimental import pallas as pl
from jax.experimental.pallas import tpu as pltpu

F32 = jnp.float32
BF16 = jnp.bfloat16

NORM_EPS = 1e-6
ATTN_HEADS = 8
ATTN_HEAD_DIM = 64
ATTN_VALUE_DIM = 2 * ATTN_HEAD_DIM
POS_SPLIT = 16
SSM_GROUP = 16
SSM_STATE = 64
CHUNK = 16
GROUP_LANES = CHUNK * SSM_GROUP
SUBLANES = 8
LANES = 128
BF16_ROWS = 16
GROUPS_PER_BLOCK = LANES // SSM_GROUP
PREP_GROUPS_PER_STEP = 8
FFN_HEAD_ROWS = 1024

VMEM_LIMIT_BYTES = 56 * 1024 * 1024


def _rms_scale(x):
    return lax.rsqrt(jnp.mean(x * x, axis=-1, keepdims=True) + NORM_EPS)


def _sigmoid(x):
    return 1.0 / (1.0 + jnp.exp(-x))


class _CastJobs:
    def __init__(self, arrays, n_steps, flat_index):
        self.flat_index = flat_index
        self.views, self.slabs = list(arrays), []
        for a in arrays:
            rows, cols = a.shape
            assert rows % BF16_ROWS == 0
            n = max(k for k in range(1, n_steps + 1) if (rows // BF16_ROWS) % k == 0)
            self.slabs.append((rows // n, cols, n))

    def _spec(self, slab):
        r, c, n = slab
        return pl.BlockSpec((r, c), lambda *g: (jnp.minimum(self.flat_index(*g), n - 1), 0))

    @property
    def in_specs(self):
        return [self._spec(s) for s in self.slabs]

    out_specs = in_specs

    @property
    def out_shapes(self):
        return [jax.ShapeDtypeStruct(v.shape, BF16) for v in self.views]

    def __len__(self):
        return len(self.views)


def _run_cast_jobs(in_refs, out_refs):
    for i_ref, o_ref in zip(in_refs, out_refs):
        o_ref[...] = i_ref[...].astype(BF16)


def _ffn_kernel(*refs, n_jobs, has_into, emit_weights):
    x_ref, pre_g_ref, wg_ref, wu_ref, wd_ref, post_g_ref = refs[:6]
    n_in = 6 + int(has_into)
    job_in = refs[n_in:n_in + n_jobs]
    o_ref = refs[n_in + n_jobs]
    n_out = n_in + n_jobs + 1
    w_out = refs[n_out:n_out + 3] if emit_weights else ()
    n_out += len(w_out)
    job_out = refs[n_out:n_out + n_jobs]
    h_ref = refs[n_out + n_jobs]
    j = pl.program_id(1)
    last = pl.num_programs(1) - 1

    def step(first, final):
        if first:
            x = x_ref[...]
            h_ref[...] = (x * _rms_scale(x) * pre_g_ref[...]).astype(BF16)
        _run_cast_jobs(job_in, job_out)
        wg, wu, wd = (r[...].astype(BF16) for r in (wg_ref, wu_ref, wd_ref))
        for w_ref, w in zip(w_out, (wg, wu, wd)):
            w_ref[...] = w
        h = h_ref[...]
        gate = jnp.dot(h, wg, preferred_element_type=F32)
        up = jnp.dot(h, wu, preferred_element_type=F32)
        act = (gate * _sigmoid(gate) * up).astype(BF16)
        part = jnp.dot(act, wd, preferred_element_type=F32)
        if first:
            o_ref[...] = part
        elif final:
            acc = o_ref[...] + part
            o_ref[...] = x_ref[...] + 0.5 * (acc * _rms_scale(acc) * post_g_ref[...])
        else:
            o_ref[...] += part

    pl.when(j == 0)(lambda: step(True, False))
    pl.when((j > 0) & (j < last))(lambda: step(False, False))
    pl.when(j == last)(lambda: step(False, True))


def _ffn(x, pre_g, w_gate, w_up, w_down, post_g, *, cast=(), tm=512, tf=512,
         single_buffer_x=False, tiles=None, into=None, emit_weights=False):
    t, d = x.shape
    f = w_gate.shape[1]
    assert t % tm == 0 and f % tf == 0
    first, nt = tiles if tiles is not None else (0, t // tm)
    assert not emit_weights or nt == 1
    nj = f // tf
    assert nj >= 2
    jobs = _CastJobs(cast, nt * nj, lambda i, j: i * nj + j)
    x_mode = dict(pipeline_mode=pl.Buffered(1)) if single_buffer_x else {}
    w_cols = pl.BlockSpec((d, tf), lambda i, j: (0, j))
    w_rows = pl.BlockSpec((tf, d), lambda i, j: (j, 0))
    into_spec = [pl.BlockSpec(memory_space=pl.ANY)] if into is not None else []
    w_specs = [w_cols, w_cols, w_rows] if emit_weights else []
    w_shapes = [jax.ShapeDtypeStruct(w.shape, BF16) for w in (w_gate, w_up, w_down)]
    outs = pl.pallas_call(
        functools.partial(_ffn_kernel, n_jobs=len(jobs), has_into=into is not None,
                          emit_weights=emit_weights),
        grid=(nt, nj),
        in_specs=[
            pl.BlockSpec((tm, d), lambda i, j: (first + i, 0), **x_mode),
            pl.BlockSpec((1, d), lambda i, j: (0, 0)),
            w_cols, w_cols, w_rows,
            pl.BlockSpec((1, d), lambda i, j: (0, 0)),
        ] + into_spec + jobs.in_specs,
        out_specs=[pl.BlockSpec((tm, d), lambda i, j: (first + i, 0))] + w_specs + jobs.out_specs,
        out_shape=[jax.ShapeDtypeStruct((t, d), F32)] + w_shapes[:len(w_specs)] + jobs.out_shapes,
        input_output_aliases={6: 0} if into is not None else {},
        scratch_shapes=[pltpu.VMEM((tm, d), BF16)],
        compiler_params=pltpu.CompilerParams(
            dimension_semantics=("arbitrary", "arbitrary"),
            vmem_limit_bytes=VMEM_LIMIT_BYTES),
        name="ffn",
    )(x, pre_g.reshape(1, d), w_gate, w_up, w_down, post_g.reshape(1, d),
      *([into] if into is not None else []), *jobs.views)
    n_w = len(w_specs)
    return outs[0], list(outs[1:1 + n_w]), list(outs[1 + n_w:])


def _in_proj_kernel(x_ref, g_ref, w_ref, qkv_ref, u_ref, h_ref, *, n_qkv):
    j = pl.program_id(1)

    def project(first):
        if first:
            x = x_ref[...]
            h_ref[...] = (x * _rms_scale(x) * g_ref[...]).astype(BF16)
        return jnp.dot(h_ref[...], w_ref[...].astype(BF16), preferred_element_type=F32)

    @pl.when(j == 0)
    def _():
        qkv_ref[...] = project(True).astype(BF16)

    @pl.when((j > 0) & (j < n_qkv))
    def _():
        qkv_ref[...] = project(False).astype(BF16)

    @pl.when(j == n_qkv)
    def _():
        u_ref[...] = project(False)


def _in_proj(x, g, w_in, *, ssm_width, tm=1024):
    t, d = x.shape
    n = w_in.shape[1]
    tn = ssm_width
    n_qkv = (n - ssm_width) // tn
    assert t % tm == 0 and n == (n_qkv + 1) * tn
    return pl.pallas_call(
        functools.partial(_in_proj_kernel, n_qkv=n_qkv),
        grid=(t // tm, n_qkv + 1),
        in_specs=[
            pl.BlockSpec((tm, d), lambda i, j: (i, 0)),
            pl.BlockSpec((1, d), lambda i, j: (0, 0)),
            pl.BlockSpec((d, tn), lambda i, j: (0, j)),
        ],
        out_specs=[
            pl.BlockSpec((tm, tn), lambda i, j: (i, jnp.minimum(j, n_qkv - 1))),
            pl.BlockSpec((tm, tn), lambda i, j: (i, 0)),
        ],
        out_shape=[
            jax.ShapeDtypeStruct((t, n - ssm_width), BF16),
            jax.ShapeDtypeStruct((t, ssm_width), F32),
        ],
        scratch_shapes=[pltpu.VMEM((tm, d), BF16)],
        compiler_params=pltpu.CompilerParams(
            dimension_semantics=("parallel", "arbitrary"),
            vmem_limit_bytes=VMEM_LIMIT_BYTES),
        name="in_proj",
    )(x, g.reshape(1, d), w_in)


def _attn_kernel(*refs, tq, nq, lam_init, n_jobs):
    slopes_ref, lq1_ref, lk1_ref, lq2_ref, lk2_ref, hg_ref, q_ref = refs[:7]
    k_refs = refs[7:7 + nq]
    v_refs = refs[7 + nq:7 + 2 * nq]
    n_in = 7 + 2 * nq
    job_in = refs[n_in:n_in + n_jobs]
    o_ref = refs[n_in + n_jobs]
    job_out = refs[n_in + n_jobs + 1:n_in + 2 * n_jobs + 1]
    qf_ref, kf_ref, bias_ref, s_ref, p_ref = refs[n_in + 2 * n_jobs + 1:]
    h = pl.program_id(0)
    qi = pl.program_id(1)
    b = pl.program_id(2)
    e = q_ref.shape[1]

    @pl.when(b == 0)
    def _():
        slope = slopes_ref[h]
        lane = lax.broadcasted_iota(jnp.int32, (tq, e), 1)
        row = lax.broadcasted_iota(jnp.int32, (tq, e), 0)
        row_hi = (slope * POS_SPLIT) * (row // POS_SPLIT).astype(F32)
        row_lo = slope * (row % POS_SPLIT).astype(F32)
        q_base = jnp.where(lane == 0, row_hi, jnp.where(lane == 1, row_lo,
                           jnp.where(lane < 4, 1.0, 0.0)))
        k_base = jnp.where(lane < 2, -1.0, jnp.where(lane == 2, row_hi,
                           jnp.where(lane == 3, row_lo, 0.0)))
        hi_q = (lane == 0).astype(F32)
        hi_k = (lane == 2).astype(F32)
        qf_ref[...] = (q_base + (slope * tq) * qi.astype(F32) * hi_q).astype(BF16)
        for slot in range(nq):
            chunk = lax.rem(qi + slot, nq)
            sign = jnp.where(chunk < qi, 1.0, jnp.where(chunk > qi, -1.0, 0.0))
            kf = sign * k_base + (sign * (slope * tq) * chunk.astype(F32)) * hi_k
            kf_ref[slot * tq:(slot + 1) * tq, :] = kf.astype(BF16)
        col = lax.broadcasted_iota(jnp.int32, (tq, tq), 1)
        bias_ref[...] = slope * jnp.abs(lax.broadcasted_iota(jnp.int32, (tq, tq), 0)
                                        - col).astype(F32)

    _run_cast_jobs(job_in, job_out)
    lam = (jnp.exp(jnp.sum(lq1_ref[...] * lk1_ref[...], axis=-1, keepdims=True))
           - jnp.exp(jnp.sum(lq2_ref[...] * lk2_ref[...], axis=-1, keepdims=True))
           + lam_init)

    q = q_ref[...] * jnp.asarray(ATTN_HEAD_DIM ** -0.5, BF16)
    k_aug = jnp.concatenate([jnp.concatenate([r[...] for r in k_refs], axis=0), kf_ref[...]],
                            axis=1)
    v = jnp.concatenate([r[...] for r in v_refs], axis=0)
    v_ext = jnp.concatenate([v, jnp.ones_like(v)], axis=1)
    lane = lax.broadcasted_iota(jnp.int32, q.shape, 1)
    nt = (((1,), (1,)), ((), ()))
    for c in range(2):
        in_map = (lane >= c * ATTN_HEAD_DIM) & (lane < (c + 1) * ATTN_HEAD_DIM)
        q_aug = jnp.concatenate([jnp.where(in_map, q, jnp.zeros_like(q)), qf_ref[...]], axis=1)
        s_ref[c, :, :tq] = lax.dot_general(q_aug, k_aug[:tq], nt,
                                           preferred_element_type=F32) - bias_ref[...]
        s_ref[c, :, tq:] = lax.dot_general(q_aug, k_aug[tq:], nt, preferred_element_type=F32)
        s = s_ref[c]
        p_ref[c] = jnp.exp(s - jnp.max(s, axis=-1, keepdims=True)).astype(BF16)
    outs = []
    for c in range(2):
        pv = jnp.dot(p_ref[c], v_ext, preferred_element_type=F32)
        outs.append(pv[:, :e] / pv[:, e:])
    o = outs[0] - lam * outs[1]
    o_ref[...] = (o * _rms_scale(o) * hg_ref[...] * (1.0 - lam_init)).astype(o_ref.dtype)


def _attention(qkv, slopes, lq1, lk1, lq2, lk2, head_g, *, batch, seq, lam_init, cast=(), tq=512):
    t = qkv.shape[0]
    e = ATTN_VALUE_DIM
    nq = seq // tq
    nh = ATTN_HEADS
    vec = lambda a: a.reshape(1, -1).astype(F32)
    small = lambda n: pl.BlockSpec((1, n), lambda h, qi, b: (0, 0))
    jobs = _CastJobs(cast, nh * nq * batch, lambda h, qi, b: (h * nq + qi) * batch + b)

    def chunk_spec(slot, col0):
        return pl.BlockSpec((tq, e), lambda h, qi, b: (b * nq + lax.rem(qi + slot, nq), col0 + h))

    outs = pl.pallas_call(
        functools.partial(_attn_kernel, tq=tq, nq=nq, lam_init=lam_init, n_jobs=len(jobs)),
        grid=(nh, nq, batch),
        in_specs=[
            pl.BlockSpec(memory_space=pltpu.SMEM),
            small(ATTN_HEAD_DIM), small(ATTN_HEAD_DIM), small(ATTN_HEAD_DIM), small(ATTN_HEAD_DIM),
            small(e),
            pl.BlockSpec((tq, e), lambda h, qi, b: (b * nq + qi, h)),
        ] + [chunk_spec(s, nh) for s in range(nq)] + [chunk_spec(s, 2 * nh) for s in range(nq)]
        + jobs.in_specs,
        out_specs=[pl.BlockSpec((tq, e), lambda h, qi, b: (b * nq + qi, h))] + jobs.out_specs,
        out_shape=[jax.ShapeDtypeStruct((t, nh * e), BF16)] + jobs.out_shapes,
        scratch_shapes=[pltpu.VMEM((tq, e), BF16), pltpu.VMEM((seq, e), BF16),
                        pltpu.VMEM((tq, tq), F32), pltpu.VMEM((2, tq, seq), F32),
                        pltpu.VMEM((2, tq, seq), BF16)],
        compiler_params=pltpu.CompilerParams(
            dimension_semantics=("arbitrary", "arbitrary", "arbitrary"),
            vmem_limit_bytes=VMEM_LIMIT_BYTES),
        name="diff_attention",
    )(slopes, vec(lq1), vec(lk1), vec(lq2), vec(lk2), vec(head_g), qkv,
      *([qkv] * (2 * nq)), *jobs.views)
    return outs[0], list(outs[1:])


def _discretize(lam_re, lam_im, log_dt):
    dt = jnp.exp(log_dt)
    mag = jnp.exp(lam_re * dt)
    a_re = mag * jnp.cos(lam_im * dt)
    a_im = mag * jnp.sin(lam_im * dt)
    den = lam_re * lam_re + lam_im * lam_im
    nr = a_re - 1.0
    f_re = (nr * lam_re + a_im * lam_im) / den
    f_im = (a_im * lam_re - nr * lam_im) / den
    return a_re, a_im, f_re, f_im


def _cmul(ar, ai, br, bi):
    return ar * br - ai * bi, ar * bi + ai * br


def _split_bf16(x, parts):
    out = []
    for k in range(parts):
        piece = x.astype(BF16)
        out.append(piece)
        if k + 1 < parts:
            x = x - piece.astype(F32)
    return out


def _copy_dot(x, onehot, dims):
    return sum(lax.dot_general(piece, onehot, dims, preferred_element_type=F32)
               for piece in _split_bf16(x, 3))


def _dot_3pass(a_parts, b_parts, dims):
    a_hi, a_lo = a_parts
    b_hi, b_lo = b_parts
    dot = lambda x, y: lax.dot_general(x, y, dims, preferred_element_type=F32)
    return dot(a_hi, b_hi) + (dot(a_hi, b_lo) + dot(a_lo, b_hi))


def _int_power(a_re, a_im, e, nbits):
    shape = jnp.broadcast_shapes(a_re.shape, e.shape)
    p_re, p_im = jnp.ones(shape, F32), jnp.zeros(shape, F32)
    s_re, s_im = a_re, a_im
    for k in range(nbits):
        bit = (e & (1 << k)) != 0
        m_re, m_im = _cmul(p_re, p_im, s_re, s_im)
        p_re, p_im = jnp.where(bit, m_re, p_re), jnp.where(bit, m_im, p_im)
        if k + 1 < nbits:
            s_re, s_im = _cmul(s_re, s_im, s_re, s_im)
    return p_re, p_im


def _shift_lanes(x, s, left):
    if s == 0:
        return x
    n = x.shape[-1]
    lane = lax.broadcasted_iota(jnp.int32, x.shape, 1)
    if left:
        return jnp.where(lane < n - s, pltpu.roll(x, n - s, axis=1), 0.0)
    return jnp.where(lane >= s, pltpu.roll(x, s, axis=1), 0.0)


def _s5_prep_kernel(lre_ref, lim_ref, ldt_ref, btre_ref, btim_ref, cre_ref, cim_ref,
                    w1_ref, cm_ref, al_ref):
    for i in range(w1_ref.shape[0]):
        _s5_prep_group(i, lre_ref, lim_ref, ldt_ref, btre_ref, btim_ref, cre_ref, cim_ref,
                       w1_ref, cm_ref, al_ref)


def _s5_prep_group(i, lre_ref, lim_ref, ldt_ref, btre_r_ref, btim_r_ref, cre_ref, cim_ref,
                   w1_ref, cm_ref, al_ref):
    n, p, l = SSM_STATE, SSM_GROUP, CHUNK
    nbits = l.bit_length() - 1
    contract0 = (((0,), (0,)), ((), ()))
    contract_mm = (((1,), (0,)), ((), ()))

    a_re, a_im, f_re, f_im = _discretize(lre_ref[i], lim_ref[i], ldt_ref[i])
    bt_re, bt_im = btre_r_ref[i], btim_r_ref[i]
    bb_re, bb_im = _cmul(f_re, f_im, bt_re, bt_im)
    tau = lax.broadcasted_iota(jnp.int32, (l, 2 * n), 0)
    lane = lax.broadcasted_iota(jnp.int32, (l, 2 * n), 1)
    pw_re, pw_im = _int_power(a_re, a_im, jnp.where(lane < n, l - 1 - tau, tau), nbits)
    bm_re = pw_re[:, None, :] * bb_re[None, :, :] - pw_im[:, None, :] * bb_im[None, :, :]
    bm_im = pw_re[:, None, :] * bb_im[None, :, :] + pw_im[:, None, :] * bb_re[None, :, :]
    w1_ref[i, :, GROUP_LANES:GROUP_LANES + 2 * n] = bm_re.reshape(l * p, 2 * n).astype(BF16)
    w1_ref[i, :, GROUP_LANES + 2 * n:] = bm_im.reshape(l * p, 2 * n).astype(BF16)
    al_re, al_im = a_re, a_im
    for _ in range(nbits):
        al_re, al_im = _cmul(al_re, al_im, al_re, al_im)
    al_ref[i] = jnp.concatenate([al_re, al_im], axis=0)

    quant = jnp.concatenate([a_re, a_im, jnp.zeros((SUBLANES - 2, 2 * n), F32)], axis=0)
    quant_t = quant.T
    a_re_c, a_im_c = quant_t[:, 0:1], quant_t[:, 1:2]

    lanes_p = lax.broadcasted_iota(jnp.int32, (p, GROUP_LANES), 1)
    rows_p = lax.broadcasted_iota(jnp.int32, (p, GROUP_LANES), 0)
    tile_p = (lanes_p % p == rows_p).astype(BF16)
    tile_lag = ((l - 1) - lanes_p // p == rows_p).astype(BF16)
    ac_re, ac_im = _cmul(_copy_dot(pw_re, tile_lag, contract0), _copy_dot(pw_im, tile_lag, contract0),
                         _copy_dot(cre_ref[i], tile_p, contract0),
                         _copy_dot(cim_ref[i], tile_p, contract0))
    ac_re_parts, ac_im_parts = _split_bf16(ac_re, 2), _split_bf16(ac_im, 2)
    is_fwd = lax.broadcasted_iota(jnp.int32, bb_re.shape, 1) < n
    t_rows = None
    for d in range(2):
        in_dir = is_fwd if d == 0 else ~is_fwd
        taps = (_dot_3pass(_split_bf16(jnp.where(in_dir, bb_re, 0.0), 2), ac_re_parts, contract_mm)
                - _dot_3pass(_split_bf16(jnp.where(in_dir, bb_im, 0.0), 2), ac_im_parts,
                             contract_mm))
        rows = [_shift_lanes(taps, p * (tp if d == 0 else l - 1 - tp), left=(d == 1))
                for tp in range(l)]
        rows = jnp.concatenate(rows, axis=0)
        t_rows = rows if t_rows is None else t_rows + rows
    w1_ref[i, :, 0:GROUP_LANES] = t_rows.astype(BF16)
    r_re, r_im = _cmul(a_re_c, a_im_c, ac_re, ac_im)
    cm_ref[i] = jnp.concatenate([r_re, -r_im], axis=0).astype(BF16)


def _s5_prep(lam_re, lam_im, log_dt, b_re, b_im, c_re, c_im):
    _, g, n = lam_re.shape
    p = b_re.shape[-1]
    row = lambda a: jnp.concatenate([a[0], a[1]], axis=-1)[:, None, :]
    ldt_r = row(jnp.broadcast_to(log_dt[:, :, None], (2, g, n)))
    bt = lambda b: jnp.concatenate([jnp.swapaxes(b[0], 1, 2), jnp.swapaxes(b[1], 1, 2)], axis=-1)
    cr = lambda c: jnp.concatenate([c[0], c[1]], axis=-1)

    gb = PREP_GROUPS_PER_STEP
    assert g % gb == 0
    spec_r = lambda rows: pl.BlockSpec((gb, rows, 2 * n), lambda i: (i, 0, 0))
    return pl.pallas_call(
        _s5_prep_kernel,
        grid=(g // gb,),
        in_specs=[spec_r(1), spec_r(1), spec_r(1), spec_r(p), spec_r(p), spec_r(p), spec_r(p)],
        out_specs=[
            pl.BlockSpec((gb, GROUP_LANES, GROUP_LANES + 4 * n), lambda i: (i, 0, 0)),
            pl.BlockSpec((gb, 4 * n, GROUP_LANES), lambda i: (i, 0, 0)),
            pl.BlockSpec((gb, 2, 2 * n), lambda i: (i, 0, 0)),
        ],
        out_shape=[
            jax.ShapeDtypeStruct((g, GROUP_LANES, GROUP_LANES + 4 * n), BF16),
            jax.ShapeDtypeStruct((g, 4 * n, GROUP_LANES), BF16),
            jax.ShapeDtypeStruct((g, 2, 2 * n), F32),
        ],
        compiler_params=pltpu.CompilerParams(dimension_semantics=("parallel",)),
        name="s5_prep",
    )(row(lam_re), row(lam_im), ldt_r, bt(b_re), bt(b_im), cr(c_re), cr(c_im))


def _swap_pieces(v):
    piece = lax.broadcasted_iota(jnp.int32, v[0].shape, 1) // SSM_GROUP
    v = list(v)
    for d in (4, 2, 1):
        hi = (piece & d) != 0
        for a in range(8):
            if a & d:
                continue
            va, vb = v[a], v[a + d]
            v[a] = jnp.where(hi, pltpu.roll(vb, d * SSM_GROUP, axis=1), va)
            v[a + d] = jnp.where(hi, vb, pltpu.roll(va, LANES - d * SSM_GROUP, axis=1))
    return v


def _s5_chunk_kernel(u_ref, perm_ref, w1_ref, cm_ref, al_ref, d_ref, y_ref,
                     a_ref, x_ref, s_ref, hf_ref, h_ref, *, nb, nc):
    gl = GROUP_LANES
    gb = GROUPS_PER_BLOCK
    n2 = 2 * SSM_STATE
    half = CHUNK // 2
    ctile = SUBLANES * CHUNK

    def token_rows(ct, h, j):
        return pl.ds(ct * ctile + h * half + j, SUBLANES, stride=CHUNK)

    def chunk_rows(ct, b):
        return pl.ds(ct * SUBLANES * nb + b, SUBLANES, stride=nb)

    rows_all = nb * nc

    def gather_body(ct, carry):
        for b in range(nb):
            for h in range(2):
                for j in range(half):
                    a_ref[j, pl.ds(h * rows_all + ct * SUBLANES * nb + b, SUBLANES, stride=nb), :] = (
                        u_ref[b, token_rows(ct, h, j), :])
        return carry

    lax.fori_loop(0, nc // SUBLANES, gather_body, 0)
    lhs = jnp.concatenate([a_ref[j] for j in range(half)], axis=1).astype(BF16)
    xp = jnp.dot(lhs, perm_ref[...], preferred_element_type=F32).astype(BF16)

    for g in range(gb):
        x = jnp.concatenate([xp[:rows_all, g * LANES:(g + 1) * LANES],
                             xp[rows_all:, g * LANES:(g + 1) * LANES]], axis=1)
        r = jnp.dot(x, w1_ref[g], preferred_element_type=F32)
        x_ref[2 * g] = r[:, :LANES]
        x_ref[2 * g + 1] = r[:, LANES:gl]
        s_ref[:, g * gl:(g + 1) * gl] = r[:, gl:]

    a_re = [al_ref[g, 0:1, :] for g in range(gb)]
    a_im = [al_ref[g, 1:2, :] for g in range(gb)]
    is_fwd = lax.broadcasted_iota(jnp.int32, (SUBLANES, n2), 1) < SSM_STATE

    def step(h, s):
        out = []
        for g in range(gb):
            hr, hi = h[:, g * gl:g * gl + n2], h[:, g * gl + n2:(g + 1) * gl]
            sr, si = s[:, g * gl:g * gl + n2], s[:, g * gl + n2:(g + 1) * gl]
            out.append(a_re[g] * hr - a_im[g] * hi + sr)
            out.append(a_re[g] * hi + a_im[g] * hr + si)
        return jnp.concatenate(out, axis=1)

    cpt = SUBLANES // nb
    n_tiles = nc // cpt
    tile_rows = lambda t: pl.ds(pl.multiple_of(t * SUBLANES, SUBLANES), SUBLANES)

    def fwd_body(t, h):
        s = s_ref[tile_rows(t), :]
        before = []
        for k in range(cpt):
            before.append(h)
            h = step(h, s[k * nb:(k + 1) * nb])
        hf_ref[tile_rows(t), :] = jnp.concatenate(before, axis=0)
        return h

    lax.fori_loop(0, n_tiles, fwd_body, jnp.zeros((nb, gb * gl), F32))

    def bwd_body(i, h):
        t = n_tiles - 1 - i
        s = s_ref[tile_rows(t), :]
        before = [None] * cpt
        for k in reversed(range(cpt)):
            before[k] = h
            h = step(h, s[k * nb:(k + 1) * nb])
        hb = jnp.concatenate(before, axis=0)
        hf = hf_ref[tile_rows(t), :]
        merged = [jnp.where(is_fwd, hf[:, k * n2:(k + 1) * n2], hb[:, k * n2:(k + 1) * n2])
                  for k in range(2 * gb)]
        h_ref[tile_rows(t), :] = jnp.concatenate(merged, axis=1)
        return h

    lax.fori_loop(0, n_tiles, bwd_body, jnp.zeros((nb, gb * gl), F32))

    for g in range(gb):
        r = jnp.dot(h_ref[:, g * gl:(g + 1) * gl].astype(BF16), cm_ref[g],
                    preferred_element_type=F32)
        x_ref[2 * g] += r[:, :LANES]
        x_ref[2 * g + 1] += r[:, LANES:]

    def scatter_body(ct, carry):
        for b in range(nb):
            for h in range(2):
                w = [x_ref[2 * g + h, chunk_rows(ct, b), :] for g in range(gb)]
                v = _swap_pieces(w)
                for j in range(half):
                    rows = token_rows(ct, h, j)
                    y_ref[b, rows, :] = v[j] + d_ref[...] * u_ref[b, rows, :]
        return carry

    lax.fori_loop(0, nc // SUBLANES, scatter_body, 0)


def _s5_chunks(u, w1, cm, al, d):
    nb, seq, width = u.shape
    nc = seq // CHUNK
    rows = nb * nc
    gb = GROUPS_PER_BLOCK
    assert SUBLANES % nb == 0 and nc % SUBLANES == 0 and width % LANES == 0
    blk = pl.BlockSpec((nb, seq, LANES), lambda i: (0, 0, i))
    half = CHUNK // 2
    src = np.arange(half * LANES)
    j, g, p = src // LANES, (src % LANES) // SSM_GROUP, src % SSM_GROUP
    perm = np.zeros((half * LANES, half * LANES), np.float32)
    perm[src, g * LANES + j * SSM_GROUP + p] = 1.0
    return pl.pallas_call(
        functools.partial(_s5_chunk_kernel, nb=nb, nc=nc),
        grid=(width // LANES,),
        in_specs=[
            blk,
            pl.BlockSpec(perm.shape, lambda i: (0, 0), pipeline_mode=pl.Buffered(1)),
            pl.BlockSpec((gb,) + w1.shape[1:], lambda i: (i, 0, 0)),
            pl.BlockSpec((gb,) + cm.shape[1:], lambda i: (i, 0, 0)),
            pl.BlockSpec((gb,) + al.shape[1:], lambda i: (i, 0, 0)),
            pl.BlockSpec((1, LANES), lambda i: (0, i)),
        ],
        out_specs=blk,
        out_shape=jax.ShapeDtypeStruct(u.shape, F32),
        scratch_shapes=[pltpu.VMEM((half, 2 * rows, LANES), F32),
                        pltpu.VMEM((2 * gb, rows, LANES), F32)]
        + [pltpu.VMEM((rows, gb * GROUP_LANES), F32)] * 3,
        compiler_params=pltpu.CompilerParams(
            dimension_semantics=("parallel",), vmem_limit_bytes=VMEM_LIMIT_BYTES),
        name="s5_chunks",
    )(u, jnp.asarray(perm, BF16), w1, cm, al, d.reshape(1, width))


def _mix_out_kernel(x_ref, a_ref, y_ref, wglu_ref, bglu_ref, og_ref, wo_a_ref, wo_s_ref, pg_ref,
                    o_ref):
    y = y_ref[...]
    g = y * (0.5 * (1.0 + jnp.tanh(math.sqrt(2.0 / math.pi) * (y + 0.044715 * (y * y * y)))))
    z = jnp.dot(g.astype(BF16), wglu_ref[...], preferred_element_type=F32) + bglu_ref[...]
    s = g * _sigmoid(z)
    s = s * _rms_scale(s) * og_ref[...]
    mixed = (jnp.dot(a_ref[...], wo_a_ref[...], preferred_element_type=F32)
             + jnp.dot(s.astype(BF16), wo_s_ref[...], preferred_element_type=F32))
    o_ref[...] = x_ref[...] + mixed * _rms_scale(mixed) * pg_ref[...]


def _mix_out(x, a, y, w_glu, b_glu, out_g, w_out, post_g, *, tm=512):
    t, d = x.shape
    wa = a.shape[1]
    ws = y.shape[1]
    const = lambda shape: pl.BlockSpec(shape, lambda i: (0, 0))
    return pl.pallas_call(
        _mix_out_kernel,
        grid=(t // tm,),
        in_specs=[
            pl.BlockSpec((tm, d), lambda i: (i, 0)),
            pl.BlockSpec((tm, wa), lambda i: (i, 0)),
            pl.BlockSpec((tm, ws), lambda i: (i, 0)),
            const((ws, ws)), const((1, ws)), const((1, ws)),
            pl.BlockSpec((wa, d), lambda i: (0, 0)),
            pl.BlockSpec((ws, d), lambda i: (wa // ws, 0)),
            const((1, d)),
        ],
        out_specs=pl.BlockSpec((tm, d), lambda i: (i, 0)),
        out_shape=jax.ShapeDtypeStruct((t, d), F32),
        compiler_params=pltpu.CompilerParams(
            dimension_semantics=("parallel",), vmem_limit_bytes=VMEM_LIMIT_BYTES),
        name="mix_out",
    )(x, a, y, w_glu, b_glu.reshape(1, ws), out_g.reshape(1, ws), w_out, w_out,
      post_g.reshape(1, d))


def kernel(x, ff1_pre_g, ff1_w_gate, ff1_w_up, ff1_w_down, ff1_post_g, mix_pre_g, w_in, lam_q1, lam_k1, lam_q2, lam_k2, attn_head_g, ssm_lam_re, ssm_lam_im, ssm_log_dt, ssm_b_re, ssm_b_im, ssm_c_re, ssm_c_im, ssm_d, ssm_w_glu, ssm_b_glu, ssm_out_g, w_out, mix_post_g, ff2_pre_g, ff2_w_gate, ff2_w_up, ff2_w_down, ff2_post_g):
    batch, seq, d_model = x.shape
    depth = w_in.shape[0]
    ssm_width = ssm_w_glu.shape[-1]
    slopes = jnp.asarray([2.0 ** (-8.0 * (i + 1) / ATTN_HEADS) for i in range(ATTN_HEADS)], F32)
    bf = lambda w: w.astype(BF16)

    xt = x.reshape(batch * seq, d_model)
    for l in range(depth):
        ff1 = (ff1_pre_g[l], ff1_w_gate[l], ff1_w_up[l], ff1_w_down[l], ff1_post_g[l])
        head, ff1_w, _ = _ffn(xt, *ff1, tm=FFN_HEAD_ROWS, tf=256, single_buffer_x=True,
                              tiles=(0, 1), emit_weights=True)
        rest = (FFN_HEAD_ROWS // 512, (batch * seq - FFN_HEAD_ROWS) // 512)
        xt, _, (w_in_bf, w_glu_bf, w_out_bf) = _ffn(
            xt, ff1[0], *ff1_w, ff1[4], tiles=rest, into=head,
            cast=(w_in[l], ssm_w_glu[l], w_out[l]))

        qkv, u = _in_proj(xt, mix_pre_g[l], w_in_bf, ssm_width=ssm_width)
        lam_init = 0.8 - 0.6 * math.exp(-0.3 * l)
        a, ff2_w = _attention(qkv, slopes, lam_q1[l], lam_k1[l], lam_q2[l], lam_k2[l],
                              attn_head_g[l], batch=batch, seq=seq, lam_init=lam_init,
                              cast=(ff2_w_gate[l], ff2_w_up[l], ff2_w_down[l]))

        w1, cm, al = _s5_prep(ssm_lam_re[l], ssm_lam_im[l], ssm_log_dt[l], ssm_b_re[l],
                              ssm_b_im[l], ssm_c_re[l], ssm_c_im[l])
        y = _s5_chunks(u.reshape(batch, seq, ssm_width), w1, cm, al, ssm_d[l])
        y = y.reshape(batch * seq, ssm_width)

        xt = _mix_out(xt, a, y, w_glu_bf, ssm_b_glu[l], ssm_out_g[l], w_out_bf, mix_post_g[l])

        xt, _, _ = _ffn(xt, ff2_pre_g[l], *ff2_w, ff2_post_g[l])
    return xt.reshape(batch, seq, d_model)
```

```python
import functools
import math

import jax
import jax.numpy as jnp
import numpy as np
from jax import lax
from jax.experimental import pallas as pl
from jax.experimental.pallas import tpu as pltpu

F32 = jnp.float32
BF16 = jnp.bfloat16

NORM_EPS = 1e-6
ATTN_HEADS = 8
ATTN_HEAD_DIM = 64
ATTN_VALUE_DIM = 2 * ATTN_HEAD_DIM
POS_SPLIT = 16
SSM_GROUP = 16
SSM_STATE = 64
CHUNK = 16
GROUP_LANES = CHUNK * SSM_GROUP
SUBLANES = 8
LANES = 128
BF16_ROWS = 16
GROUPS_PER_BLOCK = LANES // SSM_GROUP
PREP_GROUPS_PER_STEP = 8
FFN_HEAD_ROWS = 1024

VMEM_LIMIT_BYTES = 56 * 1024 * 1024


def _rms_scale(x):
    return lax.rsqrt(jnp.mean(x * x, axis=-1, keepdims=True) + NORM_EPS)


def _sigmoid(x):
    return 1.0 / (1.0 + jnp.exp(-x))


class _CastJobs:
    def __init__(self, arrays, n_steps, flat_index):
        self.flat_index = flat_index
        self.views, self.slabs = list(arrays), []
        for a in arrays:
            rows, cols = a.shape
            assert rows % BF16_ROWS == 0
            n = max(k for k in range(1, n_steps + 1) if (rows // BF16_ROWS) % k == 0)
            self.slabs.append((rows // n, cols, n))

    def _spec(self, slab):
        r, c, n = slab
        return pl.BlockSpec((r, c), lambda *g: (jnp.minimum(self.flat_index(*g), n - 1), 0))

    @property
    def in_specs(self):
        return [self._spec(s) for s in self.slabs]

    out_specs = in_specs

    @property
    def out_shapes(self):
        return [jax.ShapeDtypeStruct(v.shape, BF16) for v in self.views]

    def __len__(self):
        return len(self.views)


def _run_cast_jobs(in_refs, out_refs):
    for i_ref, o_ref in zip(in_refs, out_refs):
        o_ref[...] = i_ref[...].astype(BF16)


def _ffn_kernel(*refs, n_jobs, has_into, emit_weights):
    x_ref, pre_g_ref, wg_ref, wu_ref, wd_ref, post_g_ref = refs[:6]
    n_in = 6 + int(has_into)
    job_in = refs[n_in:n_in + n_jobs]
    o_ref = refs[n_in + n_jobs]
    n_out = n_in + n_jobs + 1
    w_out = refs[n_out:n_out + 3] if emit_weights else ()
    n_out += len(w_out)
    job_out = refs[n_out:n_out + n_jobs]
    h_ref = refs[n_out + n_jobs]
    j = pl.program_id(1)
    last = pl.num_programs(1) - 1

    def step(first, final):
        if first:
            x = x_ref[...]
            h_ref[...] = (x * _rms_scale(x) * pre_g_ref[...]).astype(BF16)
        _run_cast_jobs(job_in, job_out)
        wg, wu, wd = (r[...].astype(BF16) for r in (wg_ref, wu_ref, wd_ref))
        for w_ref, w in zip(w_out, (wg, wu, wd)):
            w_ref[...] = w
        h = h_ref[...]
        gate = jnp.dot(h, wg, preferred_element_type=F32)
        up = jnp.dot(h, wu, preferred_element_type=F32)
        act = (gate * _sigmoid(gate) * up).astype(BF16)
        part = jnp.dot(act, wd, preferred_element_type=F32)
        if first:
            o_ref[...] = part
        elif final:
            acc = o_ref[...] + part
            o_ref[...] = x_ref[...] + 0.5 * (acc * _rms_scale(acc) * post_g_ref[...])
        else:
            o_ref[...] += part

    pl.when(j == 0)(lambda: step(True, False))
    pl.when((j > 0) & (j < last))(lambda: step(False, False))
    pl.when(j == last)(lambda: step(False, True))


def _ffn(x, pre_g, w_gate, w_up, w_down, post_g, *, cast=(), tm=512, tf=512,
         single_buffer_x=False, tiles=None, into=None, emit_weights=False):
    t, d = x.shape
    f = w_gate.shape[1]
    assert t % tm == 0 and f % tf == 0
    first, nt = tiles if tiles is not None else (0, t // tm)
    assert not emit_weights or nt == 1
    nj = f // tf
    assert nj >= 2
    jobs = _CastJobs(cast, nt * nj, lambda i, j: i * nj + j)
    x_mode = dict(pipeline_mode=pl.Buffered(1)) if single_buffer_x else {}
    w_cols = pl.BlockSpec((d, tf), lambda i, j: (0, j))
    w_rows = pl.BlockSpec((tf, d), lambda i, j: (j, 0))
    into_spec = [pl.BlockSpec(memory_space=pl.ANY)] if into is not None else []
    w_specs = [w_cols, w_cols, w_rows] if emit_weights else []
    w_shapes = [jax.ShapeDtypeStruct(w.shape, BF16) for w in (w_gate, w_up, w_down)]
    outs = pl.pallas_call(
        functools.partial(_ffn_kernel, n_jobs=len(jobs), has_into=into is not None,
                          emit_weights=emit_weights),
        grid=(nt, nj),
        in_specs=[
            pl.BlockSpec((tm, d), lambda i, j: (first + i, 0), **x_mode),
            pl.BlockSpec((1, d), lambda i, j: (0, 0)),
            w_cols, w_cols, w_rows,
            pl.BlockSpec((1, d), lambda i, j: (0, 0)),
        ] + into_spec + jobs.in_specs,
        out_specs=[pl.BlockSpec((tm, d), lambda i, j: (first + i, 0))] + w_specs + jobs.out_specs,
        out_shape=[jax.ShapeDtypeStruct((t, d), F32)] + w_shapes[:len(w_specs)] + jobs.out_shapes,
        input_output_aliases={6: 0} if into is not None else {},
        scratch_shapes=[pltpu.VMEM((tm, d), BF16)],
        compiler_params=pltpu.CompilerParams(
            dimension_semantics=("arbitrary", "arbitrary"),
            vmem_limit_bytes=VMEM_LIMIT_BYTES),
        name="ffn",
    )(x, pre_g.reshape(1, d), w_gate, w_up, w_down, post_g.reshape(1, d),
      *([into] if into is not None else []), *jobs.views)
    n_w = len(w_specs)
    return outs[0], list(outs[1:1 + n_w]), list(outs[1 + n_w:])


def _in_proj_kernel(x_ref, g_ref, w_ref, qkv_ref, u_ref, h_ref, *, n_qkv):
    j = pl.program_id(1)

    def project(first):
        if first:
            x = x_ref[...]
            h_ref[...] = (x * _rms_scale(x) * g_ref[...]).astype(BF16)
        return jnp.dot(h_ref[...], w_ref[...].astype(BF16), preferred_element_type=F32)

    @pl.when(j == 0)
    def _():
        qkv_ref[...] = project(True).astype(BF16)

    @pl.when((j > 0) & (j < n_qkv))
    def _():
        qkv_ref[...] = project(False).astype(BF16)

    @pl.when(j == n_qkv)
    def _():
        u_ref[...] = project(False)


def _in_proj(x, g, w_in, *, ssm_width, tm=1024):
    t, d = x.shape
    n = w_in.shape[1]
    tn = ssm_width
    n_qkv = (n - ssm_width) // tn
    assert t % tm == 0 and n == (n_qkv + 1) * tn
    return pl.pallas_call(
        functools.partial(_in_proj_kernel, n_qkv=n_qkv),
        grid=(t // tm, n_qkv + 1),
        in_specs=[
            pl.BlockSpec((tm, d), lambda i, j: (i, 0)),
            pl.BlockSpec((1, d), lambda i, j: (0, 0)),
            pl.BlockSpec((d, tn), lambda i, j: (0, j)),
        ],
        out_specs=[
            pl.BlockSpec((tm, tn), lambda i, j: (i, jnp.minimum(j, n_qkv - 1))),
            pl.BlockSpec((tm, tn), lambda i, j: (i, 0)),
        ],
        out_shape=[
            jax.ShapeDtypeStruct((t, n - ssm_width), BF16),
            jax.ShapeDtypeStruct((t, ssm_width), F32),
        ],
        scratch_shapes=[pltpu.VMEM((tm, d), BF16)],
        compiler_params=pltpu.CompilerParams(
            dimension_semantics=("parallel", "arbitrary"),
            vmem_limit_bytes=VMEM_LIMIT_BYTES),
        name="in_proj",
    )(x, g.reshape(1, d), w_in)


def _attn_kernel(*refs, tq, nq, lam_init, n_jobs):
    slopes_ref, lq1_ref, lk1_ref, lq2_ref, lk2_ref, hg_ref, q_ref = refs[:7]
    k_refs = refs[7:7 + nq]
    v_refs = refs[7 + nq:7 + 2 * nq]
    n_in = 7 + 2 * nq
    job_in = refs[n_in:n_in + n_jobs]
    o_ref = refs[n_in + n_jobs]
    job_out = refs[n_in + n_jobs + 1:n_in + 2 * n_jobs + 1]
    qf_ref, kf_ref, bias_ref, s_ref, p_ref = refs[n_in + 2 * n_jobs + 1:]
    h = pl.program_id(0)
    qi = pl.program_id(1)
    b = pl.program_id(2)
    e = q_ref.shape[1]

    @pl.when(b == 0)
    def _():
        slope = slopes_ref[h]
        lane = lax.broadcasted_iota(jnp.int32, (tq, e), 1)
        row = lax.broadcasted_iota(jnp.int32, (tq, e), 0)
        row_hi = (slope * POS_SPLIT) * (row // POS_SPLIT).astype(F32)
        row_lo = slope * (row % POS_SPLIT).astype(F32)
        q_base = jnp.where(lane == 0, row_hi, jnp.where(lane == 1, row_lo,
                           jnp.where(lane < 4, 1.0, 0.0)))
        k_base = jnp.where(lane < 2, -1.0, jnp.where(lane == 2, row_hi,
                           jnp.where(lane == 3, row_lo, 0.0)))
        hi_q = (lane == 0).astype(F32)
        hi_k = (lane == 2).astype(F32)
        qf_ref[...] = (q_base + (slope * tq) * qi.astype(F32) * hi_q).astype(BF16)
        for slot in range(nq):
            chunk = lax.rem(qi + slot, nq)
            sign = jnp.where(chunk < qi, 1.0, jnp.where(chunk > qi, -1.0, 0.0))
            kf = sign * k_base + (sign * (slope * tq) * chunk.astype(F32)) * hi_k
            kf_ref[slot * tq:(slot + 1) * tq, :] = kf.astype(BF16)
        col = lax.broadcasted_iota(jnp.int32, (tq, tq), 1)
        bias_ref[...] = slope * jnp.abs(lax.broadcasted_iota(jnp.int32, (tq, tq), 0)
                                        - col).astype(F32)

    _run_cast_jobs(job_in, job_out)
    lam = (jnp.exp(jnp.sum(lq1_ref[...] * lk1_ref[...], axis=-1, keepdims=True))
           - jnp.exp(jnp.sum(lq2_ref[...] * lk2_ref[...], axis=-1, keepdims=True))
           + lam_init)

    q = q_ref[...] * jnp.asarray(ATTN_HEAD_DIM ** -0.5, BF16)
    k_aug = jnp.concatenate([jnp.concatenate([r[...] for r in k_refs], axis=0), kf_ref[...]],
                            axis=1)
    v = jnp.concatenate([r[...] for r in v_refs], axis=0)
    v_ext = jnp.concatenate([v, jnp.ones_like(v)], axis=1)
    lane = lax.broadcasted_iota(jnp.int32, q.shape, 1)
    nt = (((1,), (1,)), ((), ()))
    for c in range(2):
        in_map = (lane >= c * ATTN_HEAD_DIM) & (lane < (c + 1) * ATTN_HEAD_DIM)
        q_aug = jnp.concatenate([jnp.where(in_map, q, jnp.zeros_like(q)), qf_ref[...]], axis=1)
        s_ref[c, :, :tq] = lax.dot_general(q_aug, k_aug[:tq], nt,
                                           preferred_element_type=F32) - bias_ref[...]
        s_ref[c, :, tq:] = lax.dot_general(q_aug, k_aug[tq:], nt, preferred_element_type=F32)
        s = s_ref[c]
        p_ref[c] = jnp.exp(s - jnp.max(s, axis=-1, keepdims=True)).astype(BF16)
    outs = []
    for c in range(2):
        pv = jnp.dot(p_ref[c], v_ext, preferred_element_type=F32)
        outs.append(pv[:, :e] / pv[:, e:])
    o = outs[0] - lam * outs[1]
    o_ref[...] = (o * _rms_scale(o) * hg_ref[...] * (1.0 - lam_init)).astype(o_ref.dtype)


def _attention(qkv, slopes, lq1, lk1, lq2, lk2, head_g, *, batch, seq, lam_init, cast=(), tq=512):
    t = qkv.shape[0]
    e = ATTN_VALUE_DIM
    nq = seq // tq
    nh = ATTN_HEADS
    vec = lambda a: a.reshape(1, -1).astype(F32)
    small = lambda n: pl.BlockSpec((1, n), lambda h, qi, b: (0, 0))
    jobs = _CastJobs(cast, nh * nq * batch, lambda h, qi, b: (h * nq + qi) * batch + b)

    def chunk_spec(slot, col0):
        return pl.BlockSpec((tq, e), lambda h, qi, b: (b * nq + lax.rem(qi + slot, nq), col0 + h))

    outs = pl.pallas_call(
        functools.partial(_attn_kernel, tq=tq, nq=nq, lam_init=lam_init, n_jobs=len(jobs)),
        grid=(nh, nq, batch),
        in_specs=[
            pl.BlockSpec(memory_space=pltpu.SMEM),
            small(ATTN_HEAD_DIM), small(ATTN_HEAD_DIM), small(ATTN_HEAD_DIM), small(ATTN_HEAD_DIM),
            small(e),
            pl.BlockSpec((tq, e), lambda h, qi, b: (b * nq + qi, h)),
        ] + [chunk_spec(s, nh) for s in range(nq)] + [chunk_spec(s, 2 * nh) for s in range(nq)]
        + jobs.in_specs,
        out_specs=[pl.BlockSpec((tq, e), lambda h, qi, b: (b * nq + qi, h))] + jobs.out_specs,
        out_shape=[jax.ShapeDtypeStruct((t, nh * e), BF16)] + jobs.out_shapes,
        scratch_shapes=[pltpu.VMEM((tq, e), BF16), pltpu.VMEM((seq, e), BF16),
                        pltpu.VMEM((tq, tq), F32), pltpu.VMEM((2, tq, seq), F32),
                        pltpu.VMEM((2, tq, seq), BF16)],
        compiler_params=pltpu.CompilerParams(
            dimension_semantics=("arbitrary", "arbitrary", "arbitrary"),
            vmem_limit_bytes=VMEM_LIMIT_BYTES),
        name="diff_attention",
    )(slopes, vec(lq1), vec(lk1), vec(lq2), vec(lk2), vec(head_g), qkv,
      *([qkv] * (2 * nq)), *jobs.views)
    return outs[0], list(outs[1:])


def _discretize(lam_re, lam_im, log_dt):
    dt = jnp.exp(log_dt)
    mag = jnp.exp(lam_re * dt)
    a_re = mag * jnp.cos(lam_im * dt)
    a_im = mag * jnp.sin(lam_im * dt)
    den = lam_re * lam_re + lam_im * lam_im
    nr = a_re - 1.0
    f_re = (nr * lam_re + a_im * lam_im) / den
    f_im = (a_im * lam_re - nr * lam_im) / den
    return a_re, a_im, f_re, f_im


def _cmul(ar, ai, br, bi):
    return ar * br - ai * bi, ar * bi + ai * br


def _split_bf16(x, parts):
    out = []
    for k in range(parts):
        piece = x.astype(BF16)
        out.append(piece)
        if k + 1 < parts:
            x = x - piece.astype(F32)
    return out


def _copy_dot(x, onehot, dims):
    return sum(lax.dot_general(piece, onehot, dims, preferred_element_type=F32)
               for piece in _split_bf16(x, 3))


def _dot_3pass(a_parts, b_parts, dims):
    a_hi, a_lo = a_parts
    b_hi, b_lo = b_parts
    dot = lambda x, y: lax.dot_general(x, y, dims, preferred_element_type=F32)
    return dot(a_hi, b_hi) + (dot(a_hi, b_lo) + dot(a_lo, b_hi))


def _int_power(a_re, a_im, e, nbits):
    shape = jnp.broadcast_shapes(a_re.shape, e.shape)
    p_re, p_im = jnp.ones(shape, F32), jnp.zeros(shape, F32)
    s_re, s_im = a_re, a_im
    for k in range(nbits):
        bit = (e & (1 << k)) != 0
        m_re, m_im = _cmul(p_re, p_im, s_re, s_im)
        p_re, p_im = jnp.where(bit, m_re, p_re), jnp.where(bit, m_im, p_im)
        if k + 1 < nbits:
            s_re, s_im = _cmul(s_re, s_im, s_re, s_im)
    return p_re, p_im


def _shift_lanes(x, s, left):
    if s == 0:
        return x
    n = x.shape[-1]
    lane = lax.broadcasted_iota(jnp.int32, x.shape, 1)
    if left:
        return jnp.where(lane < n - s, pltpu.roll(x, n - s, axis=1), 0.0)
    return jnp.where(lane >= s, pltpu.roll(x, s, axis=1), 0.0)


def _s5_prep_kernel(lre_ref, lim_ref, ldt_ref, btre_ref, btim_ref, cre_ref, cim_ref,
                    w1_ref, cm_ref, al_ref):
    for i in range(w1_ref.shape[0]):
        _s5_prep_group(i, lre_ref, lim_ref, ldt_ref, btre_ref, btim_ref, cre_ref, cim_ref,
                       w1_ref, cm_ref, al_ref)


def _s5_prep_group(i, lre_ref, lim_ref, ldt_ref, btre_r_ref, btim_r_ref, cre_ref, cim_ref,
                   w1_ref, cm_ref, al_ref):
    n, p, l = SSM_STATE, SSM_GROUP, CHUNK
    nbits = l.bit_length() - 1
    contract0 = (((0,), (0,)), ((), ()))
    contract_mm = (((1,), (0,)), ((), ()))

    a_re, a_im, f_re, f_im = _discretize(lre_ref[i], lim_ref[i], ldt_ref[i])
    bt_re, bt_im = btre_r_ref[i], btim_r_ref[i]
    bb_re, bb_im = _cmul(f_re, f_im, bt_re, bt_im)
    tau = lax.broadcasted_iota(jnp.int32, (l, 2 * n), 0)
    lane = lax.broadcasted_iota(jnp.int32, (l, 2 * n), 1)
    pw_re, pw_im = _int_power(a_re, a_im, jnp.where(lane < n, l - 1 - tau, tau), nbits)
    bm_re = pw_re[:, None, :] * bb_re[None, :, :] - pw_im[:, None, :] * bb_im[None, :, :]
    bm_im = pw_re[:, None, :] * bb_im[None, :, :] + pw_im[:, None, :] * bb_re[None, :, :]
    w1_ref[i, :, GROUP_LANES:GROUP_LANES + 2 * n] = bm_re.reshape(l * p, 2 * n).astype(BF16)
    w1_ref[i, :, GROUP_LANES + 2 * n:] = bm_im.reshape(l * p, 2 * n).astype(BF16)
    al_re, al_im = a_re, a_im
    for _ in range(nbits):
        al_re, al_im = _cmul(al_re, al_im, al_re, al_im)
    al_ref[i] = jnp.concatenate([al_re, al_im], axis=0)

    quant = jnp.concatenate([a_re, a_im, jnp.zeros((SUBLANES - 2, 2 * n), F32)], axis=0)
    quant_t = quant.T
    a_re_c, a_im_c = quant_t[:, 0:1], quant_t[:, 1:2]

    lanes_p = lax.broadcasted_iota(jnp.int32, (p, GROUP_LANES), 1)
    rows_p = lax.broadcasted_iota(jnp.int32, (p, GROUP_LANES), 0)
    tile_p = (lanes_p % p == rows_p).astype(BF16)
    tile_lag = ((l - 1) - lanes_p // p == rows_p).astype(BF16)
    ac_re, ac_im = _cmul(_copy_dot(pw_re, tile_lag, contract0), _copy_dot(pw_im, tile_lag, contract0),
                         _copy_dot(cre_ref[i], tile_p, contract0),
                         _copy_dot(cim_ref[i], tile_p, contract0))
    ac_re_parts, ac_im_parts = _split_bf16(ac_re, 2), _split_bf16(ac_im, 2)
    is_fwd = lax.broadcasted_iota(jnp.int32, bb_re.shape, 1) < n
    t_rows = None
    for d in range(2):
        in_dir = is_fwd if d == 0 else ~is_fwd
        taps = (_dot_3pass(_split_bf16(jnp.where(in_dir, bb_re, 0.0), 2), ac_re_parts, contract_mm)
                - _dot_3pass(_split_bf16(jnp.where(in_dir, bb_im, 0.0), 2), ac_im_parts,
                             contract_mm))
        rows = [_shift_lanes(taps, p * (tp if d == 0 else l - 1 - tp), left=(d == 1))
                for tp in range(l)]
        rows = jnp.concatenate(rows, axis=0)
        t_rows = rows if t_rows is None else t_rows + rows
    w1_ref[i, :, 0:GROUP_LANES] = t_rows.astype(BF16)
    r_re, r_im = _cmul(a_re_c, a_im_c, ac_re, ac_im)
    cm_ref[i] = jnp.concatenate([r_re, -r_im], axis=0).astype(BF16)


def _s5_prep(lam_re, lam_im, log_dt, b_re, b_im, c_re, c_im):
    _, g, n = lam_re.shape
    p = b_re.shape[-1]
    row = lambda a: jnp.concatenate([a[0], a[1]], axis=-1)[:, None, :]
    ldt_r = row(jnp.broadcast_to(log_dt[:, :, None], (2, g, n)))
    bt = lambda b: jnp.concatenate([jnp.swapaxes(b[0], 1, 2), jnp.swapaxes(b[1], 1, 2)], axis=-1)
    cr = lambda c: jnp.concatenate([c[0], c[1]], axis=-1)

    gb = PREP_GROUPS_PER_STEP
    assert g % gb == 0
    spec_r = lambda rows: pl.BlockSpec((gb, rows, 2 * n), lambda i: (i, 0, 0))
    return pl.pallas_call(
        _s5_prep_kernel,
        grid=(g // gb,),
        in_specs=[spec_r(1), spec_r(1), spec_r(1), spec_r(p), spec_r(p), spec_r(p), spec_r(p)],
        out_specs=[
            pl.BlockSpec((gb, GROUP_LANES, GROUP_LANES + 4 * n), lambda i: (i, 0, 0)),
            pl.BlockSpec((gb, 4 * n, GROUP_LANES), lambda i: (i, 0, 0)),
            pl.BlockSpec((gb, 2, 2 * n), lambda i: (i, 0, 0)),
        ],
        out_shape=[
            jax.ShapeDtypeStruct((g, GROUP_LANES, GROUP_LANES + 4 * n), BF16),
            jax.ShapeDtypeStruct((g, 4 * n, GROUP_LANES), BF16),
            jax.ShapeDtypeStruct((g, 2, 2 * n), F32),
        ],
        compiler_params=pltpu.CompilerParams(dimension_semantics=("parallel",)),
        name="s5_prep",
    )(row(lam_re), row(lam_im), ldt_r, bt(b_re), bt(b_im), cr(c_re), cr(c_im))


def _swap_pieces(v):
    piece = lax.broadcasted_iota(jnp.int32, v[0].shape, 1) // SSM_GROUP
    v = list(v)
    for d in (4, 2, 1):
        hi = (piece & d) != 0
        for a in range(8):
            if a & d:
                continue
            va, vb = v[a], v[a + d]
            v[a] = jnp.where(hi, pltpu.roll(vb, d * SSM_GROUP, axis=1), va)
            v[a + d] = jnp.where(hi, vb, pltpu.roll(va, LANES - d * SSM_GROUP, axis=1))
    return v


def _s5_chunk_kernel(u_ref, perm_ref, w1_ref, cm_ref, al_ref, d_ref, y_ref,
                     a_ref, x_ref, s_ref, hf_ref, h_ref, *, nb, nc):
    gl = GROUP_LANES
    gb = GROUPS_PER_BLOCK
    n2 = 2 * SSM_STATE
    half = CHUNK // 2
    ctile = SUBLANES * CHUNK

    def token_rows(ct, h, j):
        return pl.ds(ct * ctile + h * half + j, SUBLANES, stride=CHUNK)

    def chunk_rows(ct, b):
        return pl.ds(ct * SUBLANES * nb + b, SUBLANES, stride=nb)

    rows_all = nb * nc

    def gather_body(ct, carry):
        for b in range(nb):
            for h in range(2):
                for j in range(half):
                    a_ref[j, pl.ds(h * rows_all + ct * SUBLANES * nb + b, SUBLANES, stride=nb), :] = (
                        u_ref[b, token_rows(ct, h, j), :])
        return carry

    lax.fori_loop(0, nc // SUBLANES, gather_body, 0)
    lhs = jnp.concatenate([a_ref[j] for j in range(half)], axis=1).astype(BF16)
    xp = jnp.dot(lhs, perm_ref[...], preferred_element_type=F32).astype(BF16)

    for g in range(gb):
        x = jnp.concatenate([xp[:rows_all, g * LANES:(g + 1) * LANES],
                             xp[rows_all:, g * LANES:(g + 1) * LANES]], axis=1)
        r = jnp.dot(x, w1_ref[g], preferred_element_type=F32)
        x_ref[2 * g] = r[:, :LANES]
        x_ref[2 * g + 1] = r[:, LANES:gl]
        s_ref[:, g * gl:(g + 1) * gl] = r[:, gl:]

    a_re = [al_ref[g, 0:1, :] for g in range(gb)]
    a_im = [al_ref[g, 1:2, :] for g in range(gb)]
    is_fwd = lax.broadcasted_iota(jnp.int32, (SUBLANES, n2), 1) < SSM_STATE

    def step(h, s):
        out = []
        for g in range(gb):
            hr, hi = h[:, g * gl:g * gl + n2], h[:, g * gl + n2:(g + 1) * gl]
            sr, si = s[:, g * gl:g * gl + n2], s[:, g * gl + n2:(g + 1) * gl]
            out.append(a_re[g] * hr - a_im[g] * hi + sr)
            out.append(a_re[g] * hi + a_im[g] * hr + si)
        return jnp.concatenate(out, axis=1)

    cpt = SUBLANES // nb
    n_tiles = nc // cpt
    tile_rows = lambda t: pl.ds(pl.multiple_of(t * SUBLANES, SUBLANES), SUBLANES)

    def fwd_body(t, h):
        s = s_ref[tile_rows(t), :]
        before = []
        for k in range(cpt):
            before.append(h)
            h = step(h, s[k * nb:(k + 1) * nb])
        hf_ref[tile_rows(t), :] = jnp.concatenate(before, axis=0)
        return h

    lax.fori_loop(0, n_tiles, fwd_body, jnp.zeros((nb, gb * gl), F32))

    def bwd_body(i, h):
        t = n_tiles - 1 - i
        s = s_ref[tile_rows(t), :]
        before = [None] * cpt
        for k in reversed(range(cpt)):
            before[k] = h
            h = step(h, s[k * nb:(k + 1) * nb])
        hb = jnp.concatenate(before, axis=0)
        hf = hf_ref[tile_rows(t), :]
        merged = [jnp.where(is_fwd, hf[:, k * n2:(k + 1) * n2], hb[:, k * n2:(k + 1) * n2])
                  for k in range(2 * gb)]
        h_ref[tile_rows(t), :] = jnp.concatenate(merged, axis=1)
        return h

    lax.fori_loop(0, n_tiles, bwd_body, jnp.zeros((nb, gb * gl), F32))

    for g in range(gb):
        r = jnp.dot(h_ref[:, g * gl:(g + 1) * gl].astype(BF16), cm_ref[g],
                    preferred_element_type=F32)
        x_ref[2 * g] += r[:, :LANES]
        x_ref[2 * g + 1] += r[:, LANES:]

    def scatter_body(ct, carry):
        for b in range(nb):
            for h in range(2):
                w = [x_ref[2 * g + h, chunk_rows(ct, b), :] for g in range(gb)]
                v = _swap_pieces(w)
                for j in range(half):
                    rows = token_rows(ct, h, j)
                    y_ref[b, rows, :] = v[j] + d_ref[...] * u_ref[b, rows, :]
        return carry

    lax.fori_loop(0, nc // SUBLANES, scatter_body, 0)


def _s5_chunks(u, w1, cm, al, d):
    nb, seq, width = u.shape
    nc = seq // CHUNK
    rows = nb * nc
    gb = GROUPS_PER_BLOCK
    assert SUBLANES % nb == 0 and nc % SUBLANES == 0 and width % LANES == 0
    blk = pl.BlockSpec((nb, seq, LANES), lambda i: (0, 0, i))
    half = CHUNK // 2
    src = np.arange(half * LANES)
    j, g, p = src // LANES, (src % LANES) // SSM_GROUP, src % SSM_GROUP
    perm = np.zeros((half * LANES, half * LANES), np.float32)
    perm[src, g * LANES + j * SSM_GROUP + p] = 1.0
    return pl.pallas_call(
        functools.partial(_s5_chunk_kernel, nb=nb, nc=nc),
        grid=(width // LANES,),
        in_specs=[
            blk,
            pl.BlockSpec(perm.shape, lambda i: (0, 0), pipeline_mode=pl.Buffered(1)),
            pl.BlockSpec((gb,) + w1.shape[1:], lambda i: (i, 0, 0)),
            pl.BlockSpec((gb,) + cm.shape[1:], lambda i: (i, 0, 0)),
            pl.BlockSpec((gb,) + al.shape[1:], lambda i: (i, 0, 0)),
            pl.BlockSpec((1, LANES), lambda i: (0, i)),
        ],
        out_specs=blk,
        out_shape=jax.ShapeDtypeStruct(u.shape, F32),
        scratch_shapes=[pltpu.VMEM((half, 2 * rows, LANES), F32),
                        pltpu.VMEM((2 * gb, rows, LANES), F32)]
        + [pltpu.VMEM((rows, gb * GROUP_LANES), F32)] * 3,
        compiler_params=pltpu.CompilerParams(
            dimension_semantics=("parallel",), vmem_limit_bytes=VMEM_LIMIT_BYTES),
        name="s5_chunks",
    )(u, jnp.asarray(perm, BF16), w1, cm, al, d.reshape(1, width))


def _mix_out_kernel(x_ref, a_ref, y_ref, wglu_ref, bglu_ref, og_ref, wo_a_ref, wo_s_ref, pg_ref,
                    o_ref):
    y = y_ref[...]
    g = y * (0.5 * (1.0 + jnp.tanh(math.sqrt(2.0 / math.pi) * (y + 0.044715 * (y * y * y)))))
    z = jnp.dot(g.astype(BF16), wglu_ref[...], preferred_element_type=F32) + bglu_ref[...]
    s = g * _sigmoid(z)
    s = s * _rms_scale(s) * og_ref[...]
    mixed = (jnp.dot(a_ref[...], wo_a_ref[...], preferred_element_type=F32)
             + jnp.dot(s.astype(BF16), wo_s_ref[...], preferred_element_type=F32))
    o_ref[...] = x_ref[...] + mixed * _rms_scale(mixed) * pg_ref[...]


def _mix_out(x, a, y, w_glu, b_glu, out_g, w_out, post_g, *, tm=512):
    t, d = x.shape
    wa = a.shape[1]
    ws = y.shape[1]
    const = lambda shape: pl.BlockSpec(shape, lambda i: (0, 0))
    return pl.pallas_call(
        _mix_out_kernel,
        grid=(t // tm,),
        in_specs=[
            pl.BlockSpec((tm, d), lambda i: (i, 0)),
            pl.BlockSpec((tm, wa), lambda i: (i, 0)),
            pl.BlockSpec((tm, ws), lambda i: (i, 0)),
            const((ws, ws)), const((1, ws)), const((1, ws)),
            pl.BlockSpec((wa, d), lambda i: (0, 0)),
            pl.BlockSpec((ws, d), lambda i: (wa // ws, 0)),
            const((1, d)),
        ],
        out_specs=pl.BlockSpec((tm, d), lambda i: (i, 0)),
        out_shape=jax.ShapeDtypeStruct((t, d), F32),
        compiler_params=pltpu.CompilerParams(
            dimension_semantics=("parallel",), vmem_limit_bytes=VMEM_LIMIT_BYTES),
        name="mix_out",
    )(x, a, y, w_glu, b_glu.reshape(1, ws), out_g.reshape(1, ws), w_out, w_out,
      post_g.reshape(1, d))


def kernel(x, ff1_pre_g, ff1_w_gate, ff1_w_up, ff1_w_down, ff1_post_g, mix_pre_g, w_in, lam_q1, lam_k1, lam_q2, lam_k2, attn_head_g, ssm_lam_re, ssm_lam_im, ssm_log_dt, ssm_b_re, ssm_b_im, ssm_c_re, ssm_c_im, ssm_d, ssm_w_glu, ssm_b_glu, ssm_out_g, w_out, mix_post_g, ff2_pre_g, ff2_w_gate, ff2_w_up, ff2_w_down, ff2_post_g):
    batch, seq, d_model = x.shape
    depth = w_in.shape[0]
    ssm_width = ssm_w_glu.shape[-1]
    slopes = jnp.asarray([2.0 ** (-8.0 * (i + 1) / ATTN_HEADS) for i in range(ATTN_HEADS)], F32)
    bf = lambda w: w.astype(BF16)

    xt = x.reshape(batch * seq, d_model)
    for l in range(depth):
        row = lambda p: p[l:l + 1]
        ff1 = (row(ff1_pre_g), ff1_w_gate[l], ff1_w_up[l], ff1_w_down[l], row(ff1_post_g))
        head, ff1_w, _ = _ffn(xt, *ff1, tm=FFN_HEAD_ROWS, tf=256, single_buffer_x=True,
                              tiles=(0, 1), emit_weights=True)
        rest = (FFN_HEAD_ROWS // 512, (batch * seq - FFN_HEAD_ROWS) // 512)
        xt, _, (w_in_bf, w_glu_bf, w_out_bf) = _ffn(
            xt, ff1[0], *ff1_w, ff1[4], tiles=rest, into=head,
            cast=(w_in[l], ssm_w_glu[l], w_out[l]))

        qkv, u = _in_proj(xt, row(mix_pre_g), w_in_bf, ssm_width=ssm_width)
        lam_init = 0.8 - 0.6 * math.exp(-0.3 * l)
        a, ff2_w = _attention(qkv, slopes, row(lam_q1), row(lam_k1), row(lam_q2), row(lam_k2),
                              row(attn_head_g), batch=batch, seq=seq, lam_init=lam_init,
                              cast=(ff2_w_gate[l], ff2_w_up[l], ff2_w_down[l]))

        w1, cm, al = _s5_prep(ssm_lam_re[l], ssm_lam_im[l], ssm_log_dt[l], ssm_b_re[l],
                              ssm_b_im[l], ssm_c_re[l], ssm_c_im[l])
        y = _s5_chunks(u.reshape(batch, seq, ssm_width), w1, cm, al, ssm_d[l])
        y = y.reshape(batch * seq, ssm_width)

        xt = _mix_out(xt, a, y, w_glu_bf, row(ssm_b_glu), row(ssm_out_g), w_out_bf,
                      row(mix_post_g))

        xt, _, _ = _ffn(xt, row(ff2_pre_g), *ff2_w, row(ff2_post_g))
    return xt.reshape(batch, seq, d_model)
```

```python
import functools
import math

import jax
import jax.numpy as jnp
import numpy as np
from jax import lax
from jax.experimental import pallas as pl
from jax.experimental.pallas import tpu as pltpu

F32 = jnp.float32
BF16 = jnp.bfloat16

NORM_EPS = 1e-6
ATTN_HEADS = 8
ATTN_HEAD_DIM = 64
ATTN_VALUE_DIM = 2 * ATTN_HEAD_DIM
POS_SPLIT = 16
SSM_GROUP = 16
SSM_STATE = 64
CHUNK = 16
GROUP_LANES = CHUNK * SSM_GROUP
SUBLANES = 8
LANES = 128
BF16_ROWS = 16
GROUPS_PER_BLOCK = LANES // SSM_GROUP
PREP_GROUPS_PER_STEP = 8
FFN_HEAD_ROWS = 1024
FFN_TILE_ROWS = 512
FFN_TILE_COLS = 512
FFN_HEAD_COLS = 256
BF16_EXACT_INTS = 256

V7X_VMEM_BYTES = 64 * 1024 * 1024
VMEM_LIMIT_BYTES = V7X_VMEM_BYTES - 8 * 1024 * 1024


def _rms_scale(x):
    return lax.rsqrt(jnp.mean(x * x, axis=-1, keepdims=True) + NORM_EPS)


def _sigmoid(x):
    return 1.0 / (1.0 + jnp.exp(-x))


class _CastJobs:
    def __init__(self, arrays, n_steps, flat_index):
        self.flat_index = flat_index
        self.views, self.slabs = list(arrays), []
        for a in arrays:
            rows, cols = a.shape
            assert rows % BF16_ROWS == 0
            n = max(k for k in range(1, n_steps + 1) if (rows // BF16_ROWS) % k == 0)
            self.slabs.append((rows // n, cols, n))

    def _spec(self, slab):
        r, c, n = slab
        return pl.BlockSpec((r, c), lambda *g: (jnp.minimum(self.flat_index(*g), n - 1), 0))

    @property
    def in_specs(self):
        return [self._spec(s) for s in self.slabs]

    out_specs = in_specs

    @property
    def out_shapes(self):
        return [jax.ShapeDtypeStruct(v.shape, BF16) for v in self.views]

    def __len__(self):
        return len(self.views)


def _run_cast_jobs(in_refs, out_refs):
    for i_ref, o_ref in zip(in_refs, out_refs):
        o_ref[...] = i_ref[...].astype(BF16)


def _ffn_kernel(*refs, n_jobs, has_into, emit_weights):
    x_ref, pre_g_ref, wg_ref, wu_ref, wd_ref, post_g_ref = refs[:6]
    n_in = 6 + int(has_into)
    job_in = refs[n_in:n_in + n_jobs]
    o_ref = refs[n_in + n_jobs]
    n_out = n_in + n_jobs + 1
    w_out = refs[n_out:n_out + 3] if emit_weights else ()
    n_out += len(w_out)
    job_out = refs[n_out:n_out + n_jobs]
    h_ref = refs[n_out + n_jobs]
    j = pl.program_id(1)
    last = pl.num_programs(1) - 1

    def step(first, final):
        if first:
            x = x_ref[...]
            h_ref[...] = (x * _rms_scale(x) * pre_g_ref[...]).astype(BF16)
        _run_cast_jobs(job_in, job_out)
        wg, wu, wd = (r[...].astype(BF16) for r in (wg_ref, wu_ref, wd_ref))
        for w_ref, w in zip(w_out, (wg, wu, wd)):
            w_ref[...] = w
        h = h_ref[...]
        gate = jnp.dot(h, wg, preferred_element_type=F32)
        up = jnp.dot(h, wu, preferred_element_type=F32)
        act = (gate * _sigmoid(gate) * up).astype(BF16)
        part = jnp.dot(act, wd, preferred_element_type=F32)
        if first:
            o_ref[...] = part
        elif final:
            acc = o_ref[...] + part
            o_ref[...] = x_ref[...] + 0.5 * (acc * _rms_scale(acc) * post_g_ref[...])
        else:
            o_ref[...] += part

    pl.when(j == 0)(lambda: step(True, False))
    pl.when((j > 0) & (j < last))(lambda: step(False, False))
    pl.when(j == last)(lambda: step(False, True))


def _ffn(x, pre_g, w_gate, w_up, w_down, post_g, *, cast=(), tm=FFN_TILE_ROWS, tf=FFN_TILE_COLS,
         single_buffer_x=False, tiles=None, into=None, emit_weights=False):
    t, d = x.shape
    f = w_gate.shape[1]
    assert t % tm == 0 and f % tf == 0
    first, nt = tiles if tiles is not None else (0, t // tm)
    assert not emit_weights or nt == 1
    nj = f // tf
    assert nj >= 2
    jobs = _CastJobs(cast, nt * nj, lambda i, j: i * nj + j)
    x_mode = dict(pipeline_mode=pl.Buffered(1)) if single_buffer_x else {}
    w_cols = pl.BlockSpec((d, tf), lambda i, j: (0, j))
    w_rows = pl.BlockSpec((tf, d), lambda i, j: (j, 0))
    into_spec = [pl.BlockSpec(memory_space=pl.ANY)] if into is not None else []
    w_specs = [w_cols, w_cols, w_rows] if emit_weights else []
    w_shapes = [jax.ShapeDtypeStruct(w.shape, BF16) for w in (w_gate, w_up, w_down)]
    outs = pl.pallas_call(
        functools.partial(_ffn_kernel, n_jobs=len(jobs), has_into=into is not None,
                          emit_weights=emit_weights),
        grid=(nt, nj),
        in_specs=[
            pl.BlockSpec((tm, d), lambda i, j: (first + i, 0), **x_mode),
            pl.BlockSpec((1, d), lambda i, j: (0, 0)),
            w_cols, w_cols, w_rows,
            pl.BlockSpec((1, d), lambda i, j: (0, 0)),
        ] + into_spec + jobs.in_specs,
        out_specs=[pl.BlockSpec((tm, d), lambda i, j: (first + i, 0))] + w_specs + jobs.out_specs,
        out_shape=[jax.ShapeDtypeStruct((t, d), F32)] + w_shapes[:len(w_specs)] + jobs.out_shapes,
        input_output_aliases={6: 0} if into is not None else {},
        scratch_shapes=[pltpu.VMEM((tm, d), BF16)],
        compiler_params=pltpu.CompilerParams(
            dimension_semantics=("arbitrary", "arbitrary"),
            vmem_limit_bytes=VMEM_LIMIT_BYTES),
        name="ffn",
    )(x, pre_g.reshape(1, d), w_gate, w_up, w_down, post_g.reshape(1, d),
      *([into] if into is not None else []), *jobs.views)
    n_w = len(w_specs)
    return outs[0], list(outs[1:1 + n_w]), list(outs[1 + n_w:])


def _in_proj_kernel(x_ref, g_ref, w_ref, qkv_ref, u_ref, h_ref, *, n_qkv):
    j = pl.program_id(1)

    def project(first):
        if first:
            x = x_ref[...]
            h_ref[...] = (x * _rms_scale(x) * g_ref[...]).astype(BF16)
        return jnp.dot(h_ref[...], w_ref[...].astype(BF16), preferred_element_type=F32)

    @pl.when(j == 0)
    def _():
        qkv_ref[...] = project(True).astype(BF16)

    @pl.when((j > 0) & (j < n_qkv))
    def _():
        qkv_ref[...] = project(False).astype(BF16)

    @pl.when(j == n_qkv)
    def _():
        u_ref[...] = project(False)


def _in_proj(x, g, w_in, *, ssm_width, tm=1024):
    t, d = x.shape
    n = w_in.shape[1]
    tn = ssm_width
    n_qkv = (n - ssm_width) // tn
    assert t % tm == 0 and n == (n_qkv + 1) * tn
    return pl.pallas_call(
        functools.partial(_in_proj_kernel, n_qkv=n_qkv),
        grid=(t // tm, n_qkv + 1),
        in_specs=[
            pl.BlockSpec((tm, d), lambda i, j: (i, 0)),
            pl.BlockSpec((1, d), lambda i, j: (0, 0)),
            pl.BlockSpec((d, tn), lambda i, j: (0, j)),
        ],
        out_specs=[
            pl.BlockSpec((tm, tn), lambda i, j: (i, jnp.minimum(j, n_qkv - 1))),
            pl.BlockSpec((tm, tn), lambda i, j: (i, 0)),
        ],
        out_shape=[
            jax.ShapeDtypeStruct((t, n - ssm_width), BF16),
            jax.ShapeDtypeStruct((t, ssm_width), F32),
        ],
        scratch_shapes=[pltpu.VMEM((tm, d), BF16)],
        compiler_params=pltpu.CompilerParams(
            dimension_semantics=("parallel", "arbitrary"),
            vmem_limit_bytes=VMEM_LIMIT_BYTES),
        name="in_proj",
    )(x, g.reshape(1, d), w_in)


def _attn_kernel(*refs, tq, nq, lam_init, n_jobs):
    slopes_ref, lq1_ref, lk1_ref, lq2_ref, lk2_ref, hg_ref, q_ref = refs[:7]
    k_refs = refs[7:7 + nq]
    v_refs = refs[7 + nq:7 + 2 * nq]
    n_in = 7 + 2 * nq
    job_in = refs[n_in:n_in + n_jobs]
    o_ref = refs[n_in + n_jobs]
    job_out = refs[n_in + n_jobs + 1:n_in + 2 * n_jobs + 1]
    qf_ref, kf_ref, bias_ref, s_ref, p_ref = refs[n_in + 2 * n_jobs + 1:]
    h = pl.program_id(0)
    qi = pl.program_id(1)
    b = pl.program_id(2)
    e = q_ref.shape[1]

    @pl.when(b == 0)
    def _():
        slope = slopes_ref[h]
        lane = lax.broadcasted_iota(jnp.int32, (tq, e), 1)
        row = lax.broadcasted_iota(jnp.int32, (tq, e), 0)
        row_hi = (slope * POS_SPLIT) * (row // POS_SPLIT).astype(F32)
        row_lo = slope * (row % POS_SPLIT).astype(F32)
        q_base = jnp.where(lane == 0, row_hi, jnp.where(lane == 1, row_lo,
                           jnp.where(lane < 4, 1.0, 0.0)))
        k_base = jnp.where(lane < 2, -1.0, jnp.where(lane == 2, row_hi,
                           jnp.where(lane == 3, row_lo, 0.0)))
        hi_q = (lane == 0).astype(F32)
        hi_k = (lane == 2).astype(F32)
        qf_ref[...] = (q_base + (slope * tq) * qi.astype(F32) * hi_q).astype(BF16)
        for slot in range(nq):
            chunk = lax.rem(qi + slot, nq)
            sign = jnp.where(chunk < qi, 1.0, jnp.where(chunk > qi, -1.0, 0.0))
            kf = sign * k_base + (sign * (slope * tq) * chunk.astype(F32)) * hi_k
            kf_ref[slot * tq:(slot + 1) * tq, :] = kf.astype(BF16)
        col = lax.broadcasted_iota(jnp.int32, (tq, tq), 1)
        bias_ref[...] = slope * jnp.abs(lax.broadcasted_iota(jnp.int32, (tq, tq), 0)
                                        - col).astype(F32)

    _run_cast_jobs(job_in, job_out)
    lam = (jnp.exp(jnp.sum(lq1_ref[...] * lk1_ref[...], axis=-1, keepdims=True))
           - jnp.exp(jnp.sum(lq2_ref[...] * lk2_ref[...], axis=-1, keepdims=True))
           + lam_init)

    q = q_ref[...] * jnp.asarray(ATTN_HEAD_DIM ** -0.5, BF16)
    k_aug = jnp.concatenate([jnp.concatenate([r[...] for r in k_refs], axis=0), kf_ref[...]],
                            axis=1)
    v = jnp.concatenate([r[...] for r in v_refs], axis=0)
    v_ext = jnp.concatenate([v, jnp.ones_like(v)], axis=1)
    lane = lax.broadcasted_iota(jnp.int32, q.shape, 1)
    nt = (((1,), (1,)), ((), ()))
    for c in range(2):
        in_map = (lane >= c * ATTN_HEAD_DIM) & (lane < (c + 1) * ATTN_HEAD_DIM)
        q_aug = jnp.concatenate([jnp.where(in_map, q, jnp.zeros_like(q)), qf_ref[...]], axis=1)
        s_ref[c, :, :tq] = lax.dot_general(q_aug, k_aug[:tq], nt,
                                           preferred_element_type=F32) - bias_ref[...]
        s_ref[c, :, tq:] = lax.dot_general(q_aug, k_aug[tq:], nt, preferred_element_type=F32)
        s = s_ref[c]
        p_ref[c] = jnp.exp(s - jnp.max(s, axis=-1, keepdims=True)).astype(BF16)
    outs = []
    for c in range(2):
        pv = jnp.dot(p_ref[c], v_ext, preferred_element_type=F32)
        outs.append(pv[:, :e] / pv[:, e:])
    o = outs[0] - lam * outs[1]
    o_ref[...] = (o * _rms_scale(o) * hg_ref[...] * (1.0 - lam_init)).astype(o_ref.dtype)


def _attention(qkv, slopes, lq1, lk1, lq2, lk2, head_g, *, batch, seq, lam_init, cast=(), tq=512):
    t = qkv.shape[0]
    e = ATTN_VALUE_DIM
    nq = seq // tq
    nh = ATTN_HEADS
    assert 8 % nh == 0 and seq // POS_SPLIT <= BF16_EXACT_INTS
    assert tq % POS_SPLIT == 0 and seq % tq == 0
    vec = lambda a: a.reshape(1, -1).astype(F32)
    small = lambda n: pl.BlockSpec((1, n), lambda h, qi, b: (0, 0))
    jobs = _CastJobs(cast, nh * nq * batch, lambda h, qi, b: (h * nq + qi) * batch + b)

    def chunk_spec(slot, col0):
        return pl.BlockSpec((tq, e), lambda h, qi, b: (b * nq + lax.rem(qi + slot, nq), col0 + h))

    outs = pl.pallas_call(
        functools.partial(_attn_kernel, tq=tq, nq=nq, lam_init=lam_init, n_jobs=len(jobs)),
        grid=(nh, nq, batch),
        in_specs=[
            pl.BlockSpec(memory_space=pltpu.SMEM),
            small(ATTN_HEAD_DIM), small(ATTN_HEAD_DIM), small(ATTN_HEAD_DIM), small(ATTN_HEAD_DIM),
            small(e),
            pl.BlockSpec((tq, e), lambda h, qi, b: (b * nq + qi, h)),
        ] + [chunk_spec(s, nh) for s in range(nq)] + [chunk_spec(s, 2 * nh) for s in range(nq)]
        + jobs.in_specs,
        out_specs=[pl.BlockSpec((tq, e), lambda h, qi, b: (b * nq + qi, h))] + jobs.out_specs,
        out_shape=[jax.ShapeDtypeStruct((t, nh * e), BF16)] + jobs.out_shapes,
        scratch_shapes=[pltpu.VMEM((tq, e), BF16), pltpu.VMEM((seq, e), BF16),
                        pltpu.VMEM((tq, tq), F32), pltpu.VMEM((2, tq, seq), F32),
                        pltpu.VMEM((2, tq, seq), BF16)],
        compiler_params=pltpu.CompilerParams(
            dimension_semantics=("arbitrary", "arbitrary", "arbitrary"),
            vmem_limit_bytes=VMEM_LIMIT_BYTES),
        name="diff_attention",
    )(slopes, vec(lq1), vec(lk1), vec(lq2), vec(lk2), vec(head_g), qkv,
      *([qkv] * (2 * nq)), *jobs.views)
    return outs[0], list(outs[1:])


def _discretize(lam_re, lam_im, log_dt):
    dt = jnp.exp(log_dt)
    mag = jnp.exp(lam_re * dt)
    a_re = mag * jnp.cos(lam_im * dt)
    a_im = mag * jnp.sin(lam_im * dt)
    den = lam_re * lam_re + lam_im * lam_im
    nr = a_re - 1.0
    f_re = (nr * lam_re + a_im * lam_im) / den
    f_im = (a_im * lam_re - nr * lam_im) / den
    return a_re, a_im, f_re, f_im


def _cmul(ar, ai, br, bi):
    return ar * br - ai * bi, ar * bi + ai * br


def _split_bf16(x, parts):
    out = []
    for k in range(parts):
        piece = x.astype(BF16)
        out.append(piece)
        if k + 1 < parts:
            x = x - piece.astype(F32)
    return out


def _copy_dot(x, onehot, dims):
    return sum(lax.dot_general(piece, onehot, dims, preferred_element_type=F32)
               for piece in _split_bf16(x, 3))


def _dot_3pass(a_parts, b_parts, dims):
    a_hi, a_lo = a_parts
    b_hi, b_lo = b_parts
    dot = lambda x, y: lax.dot_general(x, y, dims, preferred_element_type=F32)
    return dot(a_hi, b_hi) + (dot(a_hi, b_lo) + dot(a_lo, b_hi))


def _int_power(a_re, a_im, e, nbits):
    shape = jnp.broadcast_shapes(a_re.shape, e.shape)
    p_re, p_im = jnp.ones(shape, F32), jnp.zeros(shape, F32)
    s_re, s_im = a_re, a_im
    for k in range(nbits):
        bit = (e & (1 << k)) != 0
        m_re, m_im = _cmul(p_re, p_im, s_re, s_im)
        p_re, p_im = jnp.where(bit, m_re, p_re), jnp.where(bit, m_im, p_im)
        if k + 1 < nbits:
            s_re, s_im = _cmul(s_re, s_im, s_re, s_im)
    return p_re, p_im


def _shift_lanes(x, s, left):
    if s == 0:
        return x
    n = x.shape[-1]
    lane = lax.broadcasted_iota(jnp.int32, x.shape, 1)
    if left:
        return jnp.where(lane < n - s, pltpu.roll(x, n - s, axis=1), 0.0)
    return jnp.where(lane >= s, pltpu.roll(x, s, axis=1), 0.0)


def _s5_prep_kernel(lre_ref, lim_ref, ldt_ref, btre_ref, btim_ref, cre_ref, cim_ref,
                    w1_ref, cm_ref, al_ref):
    for i in range(w1_ref.shape[0]):
        _s5_prep_group(i, lre_ref, lim_ref, ldt_ref, btre_ref, btim_ref, cre_ref, cim_ref,
                       w1_ref, cm_ref, al_ref)


def _s5_prep_group(i, lre_ref, lim_ref, ldt_ref, btre_r_ref, btim_r_ref, cre_ref, cim_ref,
                   w1_ref, cm_ref, al_ref):
    n, p, l = SSM_STATE, SSM_GROUP, CHUNK
    nbits = l.bit_length() - 1
    contract0 = (((0,), (0,)), ((), ()))
    contract_mm = (((1,), (0,)), ((), ()))

    a_re, a_im, f_re, f_im = _discretize(lre_ref[i], lim_ref[i], ldt_ref[i])
    bt_re, bt_im = btre_r_ref[i], btim_r_ref[i]
    bb_re, bb_im = _cmul(f_re, f_im, bt_re, bt_im)
    tau = lax.broadcasted_iota(jnp.int32, (l, 2 * n), 0)
    lane = lax.broadcasted_iota(jnp.int32, (l, 2 * n), 1)
    pw_re, pw_im = _int_power(a_re, a_im, jnp.where(lane < n, l - 1 - tau, tau), nbits)
    bm_re = pw_re[:, None, :] * bb_re[None, :, :] - pw_im[:, None, :] * bb_im[None, :, :]
    bm_im = pw_re[:, None, :] * bb_im[None, :, :] + pw_im[:, None, :] * bb_re[None, :, :]
    w1_ref[i, :, GROUP_LANES:GROUP_LANES + 2 * n] = bm_re.reshape(l * p, 2 * n).astype(BF16)
    w1_ref[i, :, GROUP_LANES + 2 * n:] = bm_im.reshape(l * p, 2 * n).astype(BF16)
    al_re, al_im = a_re, a_im
    for _ in range(nbits):
        al_re, al_im = _cmul(al_re, al_im, al_re, al_im)
    al_ref[i] = jnp.concatenate([al_re, al_im], axis=0)

    quant = jnp.concatenate([a_re, a_im, jnp.zeros((SUBLANES - 2, 2 * n), F32)], axis=0)
    quant_t = quant.T
    a_re_c, a_im_c = quant_t[:, 0:1], quant_t[:, 1:2]

    lanes_p = lax.broadcasted_iota(jnp.int32, (p, GROUP_LANES), 1)
    rows_p = lax.broadcasted_iota(jnp.int32, (p, GROUP_LANES), 0)
    tile_p = (lanes_p % p == rows_p).astype(BF16)
    tile_lag = ((l - 1) - lanes_p // p == rows_p).astype(BF16)
    ac_re, ac_im = _cmul(_copy_dot(pw_re, tile_lag, contract0), _copy_dot(pw_im, tile_lag, contract0),
                         _copy_dot(cre_ref[i], tile_p, contract0),
                         _copy_dot(cim_ref[i], tile_p, contract0))
    ac_re_parts, ac_im_parts = _split_bf16(ac_re, 2), _split_bf16(ac_im, 2)
    is_fwd = lax.broadcasted_iota(jnp.int32, bb_re.shape, 1) < n
    t_rows = None
    for d in range(2):
        in_dir = is_fwd if d == 0 else ~is_fwd
        taps = (_dot_3pass(_split_bf16(jnp.where(in_dir, bb_re, 0.0), 2), ac_re_parts, contract_mm)
                - _dot_3pass(_split_bf16(jnp.where(in_dir, bb_im, 0.0), 2), ac_im_parts,
                             contract_mm))
        rows = [_shift_lanes(taps, p * (tp if d == 0 else l - 1 - tp), left=(d == 1))
                for tp in range(l)]
        rows = jnp.concatenate(rows, axis=0)
        t_rows = rows if t_rows is None else t_rows + rows
    w1_ref[i, :, 0:GROUP_LANES] = t_rows.astype(BF16)
    r_re, r_im = _cmul(a_re_c, a_im_c, ac_re, ac_im)
    cm_ref[i] = jnp.concatenate([r_re, -r_im], axis=0).astype(BF16)


def _s5_prep(lam_re, lam_im, log_dt, b_re, b_im, c_re, c_im):
    _, g, n = lam_re.shape
    p = b_re.shape[-1]
    row = lambda a: jnp.concatenate([a[0], a[1]], axis=-1)[:, None, :]
    ldt_r = row(jnp.broadcast_to(log_dt[:, :, None], (2, g, n)))
    bt = lambda b: jnp.concatenate([jnp.swapaxes(b[0], 1, 2), jnp.swapaxes(b[1], 1, 2)], axis=-1)
    cr = lambda c: jnp.concatenate([c[0], c[1]], axis=-1)

    gb = PREP_GROUPS_PER_STEP
    assert g % gb == 0
    spec_r = lambda rows: pl.BlockSpec((gb, rows, 2 * n), lambda i: (i, 0, 0))
    return pl.pallas_call(
        _s5_prep_kernel,
        grid=(g // gb,),
        in_specs=[spec_r(1), spec_r(1), spec_r(1), spec_r(p), spec_r(p), spec_r(p), spec_r(p)],
        out_specs=[
            pl.BlockSpec((gb, GROUP_LANES, GROUP_LANES + 4 * n), lambda i: (i, 0, 0)),
            pl.BlockSpec((gb, 4 * n, GROUP_LANES), lambda i: (i, 0, 0)),
            pl.BlockSpec((gb, 2, 2 * n), lambda i: (i, 0, 0)),
        ],
        out_shape=[
            jax.ShapeDtypeStruct((g, GROUP_LANES, GROUP_LANES + 4 * n), BF16),
            jax.ShapeDtypeStruct((g, 4 * n, GROUP_LANES), BF16),
            jax.ShapeDtypeStruct((g, 2, 2 * n), F32),
        ],
        compiler_params=pltpu.CompilerParams(dimension_semantics=("parallel",)),
        name="s5_prep",
    )(row(lam_re), row(lam_im), ldt_r, bt(b_re), bt(b_im), cr(c_re), cr(c_im))


def _swap_pieces(v):
    piece = lax.broadcasted_iota(jnp.int32, v[0].shape, 1) // SSM_GROUP
    v = list(v)
    for d in (4, 2, 1):
        hi = (piece & d) != 0
        for a in range(8):
            if a & d:
                continue
            va, vb = v[a], v[a + d]
            v[a] = jnp.where(hi, pltpu.roll(vb, d * SSM_GROUP, axis=1), va)
            v[a + d] = jnp.where(hi, vb, pltpu.roll(va, LANES - d * SSM_GROUP, axis=1))
    return v


def _s5_chunk_kernel(u_ref, perm_ref, w1_ref, cm_ref, al_ref, d_ref, y_ref,
                     a_ref, x_ref, s_ref, hf_ref, h_ref, *, nb, nc):
    gl = GROUP_LANES
    gb = GROUPS_PER_BLOCK
    n2 = 2 * SSM_STATE
    half = CHUNK // 2
    ctile = SUBLANES * CHUNK

    def token_rows(ct, h, j):
        return pl.ds(ct * ctile + h * half + j, SUBLANES, stride=CHUNK)

    def chunk_rows(ct, b):
        return pl.ds(ct * SUBLANES * nb + b, SUBLANES, stride=nb)

    rows_all = nb * nc

    def gather_body(ct, carry):
        for b in range(nb):
            for h in range(2):
                for j in range(half):
                    a_ref[j, pl.ds(h * rows_all + ct * SUBLANES * nb + b, SUBLANES, stride=nb), :] = (
                        u_ref[b, token_rows(ct, h, j), :])
        return carry

    lax.fori_loop(0, nc // SUBLANES, gather_body, 0)
    lhs = jnp.concatenate([a_ref[j] for j in range(half)], axis=1).astype(BF16)
    xp = jnp.dot(lhs, perm_ref[...], preferred_element_type=F32).astype(BF16)

    for g in range(gb):
        x = jnp.concatenate([xp[:rows_all, g * LANES:(g + 1) * LANES],
                             xp[rows_all:, g * LANES:(g + 1) * LANES]], axis=1)
        r = jnp.dot(x, w1_ref[g], preferred_element_type=F32)
        x_ref[2 * g] = r[:, :LANES]
        x_ref[2 * g + 1] = r[:, LANES:gl]
        s_ref[:, g * gl:(g + 1) * gl] = r[:, gl:]

    a_re = [al_ref[g, 0:1, :] for g in range(gb)]
    a_im = [al_ref[g, 1:2, :] for g in range(gb)]
    is_fwd = lax.broadcasted_iota(jnp.int32, (SUBLANES, n2), 1) < SSM_STATE

    def step(h, s):
        out = []
        for g in range(gb):
            hr, hi = h[:, g * gl:g * gl + n2], h[:, g * gl + n2:(g + 1) * gl]
            sr, si = s[:, g * gl:g * gl + n2], s[:, g * gl + n2:(g + 1) * gl]
            out.append(a_re[g] * hr - a_im[g] * hi + sr)
            out.append(a_re[g] * hi + a_im[g] * hr + si)
        return jnp.concatenate(out, axis=1)

    cpt = SUBLANES // nb
    n_tiles = nc // cpt
    tile_rows = lambda t: pl.ds(pl.multiple_of(t * SUBLANES, SUBLANES), SUBLANES)

    def pick(fwd_rows, bwd_rows):
        return jnp.concatenate(
            [jnp.where(is_fwd[:fwd_rows.shape[0]], fwd_rows[:, k * n2:(k + 1) * n2],
                       bwd_rows[:, k * n2:(k + 1) * n2]) for k in range(2 * gb)], axis=1)

    def sweep_body(t, h, *, second_half):
        tb = n_tiles - 1 - t
        s_f = s_ref[tile_rows(t), :]
        s_b = s_ref[tile_rows(tb), :]
        before = []
        for k in range(cpt):
            kb = cpt - 1 - k
            before.append(h)
            h = step(h, pick(s_f[k * nb:(k + 1) * nb], s_b[kb * nb:(kb + 1) * nb]))
        fwd_valid = jnp.concatenate(before, axis=0)
        bwd_valid = jnp.concatenate(before[::-1], axis=0)
        if second_half:
            h_ref[tile_rows(t), :] = pick(fwd_valid, h_ref[tile_rows(t), :])
            h_ref[tile_rows(tb), :] = pick(hf_ref[tile_rows(tb), :], bwd_valid)
        else:
            hf_ref[tile_rows(t), :] = fwd_valid
            h_ref[tile_rows(tb), :] = bwd_valid
        return h

    assert n_tiles % 2 == 0
    h_mid = lax.fori_loop(0, n_tiles // 2, functools.partial(sweep_body, second_half=False),
                          jnp.zeros((nb, gb * gl), F32))
    lax.fori_loop(n_tiles // 2, n_tiles, functools.partial(sweep_body, second_half=True), h_mid)

    for g in range(gb):
        r = jnp.dot(h_ref[:, g * gl:(g + 1) * gl].astype(BF16), cm_ref[g],
                    preferred_element_type=F32)
        x_ref[2 * g] += r[:, :LANES]
        x_ref[2 * g + 1] += r[:, LANES:]

    def scatter_body(ct, carry):
        for b in range(nb):
            for h in range(2):
                w = [x_ref[2 * g + h, chunk_rows(ct, b), :] for g in range(gb)]
                v = _swap_pieces(w)
                for j in range(half):
                    rows = token_rows(ct, h, j)
                    y_ref[b, rows, :] = v[j] + d_ref[...] * u_ref[b, rows, :]
        return carry

    lax.fori_loop(0, nc // SUBLANES, scatter_body, 0)


def _s5_chunks(u, w1, cm, al, d):
    nb, seq, width = u.shape
    nc = seq // CHUNK
    rows = nb * nc
    gb = GROUPS_PER_BLOCK
    assert SUBLANES % nb == 0 and nc % SUBLANES == 0 and width % LANES == 0
    blk = pl.BlockSpec((nb, seq, LANES), lambda i: (0, 0, i))
    half = CHUNK // 2
    src = np.arange(half * LANES)
    j, g, p = src // LANES, (src % LANES) // SSM_GROUP, src % SSM_GROUP
    perm = np.zeros((half * LANES, half * LANES), np.float32)
    perm[src, g * LANES + j * SSM_GROUP + p] = 1.0
    return pl.pallas_call(
        functools.partial(_s5_chunk_kernel, nb=nb, nc=nc),
        grid=(width // LANES,),
        in_specs=[
            blk,
            pl.BlockSpec(perm.shape, lambda i: (0, 0), pipeline_mode=pl.Buffered(1)),
            pl.BlockSpec((gb,) + w1.shape[1:], lambda i: (i, 0, 0)),
            pl.BlockSpec((gb,) + cm.shape[1:], lambda i: (i, 0, 0)),
            pl.BlockSpec((gb,) + al.shape[1:], lambda i: (i, 0, 0)),
            pl.BlockSpec((1, LANES), lambda i: (0, i)),
        ],
        out_specs=blk,
        out_shape=jax.ShapeDtypeStruct(u.shape, F32),
        scratch_shapes=[pltpu.VMEM((half, 2 * rows, LANES), F32),
                        pltpu.VMEM((2 * gb, rows, LANES), F32)]
        + [pltpu.VMEM((rows, gb * GROUP_LANES), F32)] * 3,
        compiler_params=pltpu.CompilerParams(
            dimension_semantics=("parallel",), vmem_limit_bytes=VMEM_LIMIT_BYTES),
        name="s5_chunks",
    )(u, jnp.asarray(perm, BF16), w1, cm, al, d.reshape(1, width))


def _mix_out_kernel(x_ref, a_ref, y_ref, wglu_ref, bglu_ref, og_ref, wo_a_ref, wo_s_ref, pg_ref,
                    o_ref):
    y = y_ref[...]
    g = y * (0.5 * (1.0 + jnp.tanh(math.sqrt(2.0 / math.pi) * (y + 0.044715 * (y * y * y)))))
    z = jnp.dot(g.astype(BF16), wglu_ref[...], preferred_element_type=F32) + bglu_ref[...]
    s = g * _sigmoid(z)
    s = s * _rms_scale(s) * og_ref[...]
    mixed = (jnp.dot(a_ref[...], wo_a_ref[...], preferred_element_type=F32)
             + jnp.dot(s.astype(BF16), wo_s_ref[...], preferred_element_type=F32))
    o_ref[...] = x_ref[...] + mixed * _rms_scale(mixed) * pg_ref[...]


def _mix_out(x, a, y, w_glu, b_glu, out_g, w_out, post_g, *, tm=512):
    t, d = x.shape
    wa = a.shape[1]
    ws = y.shape[1]
    const = lambda shape: pl.BlockSpec(shape, lambda i: (0, 0))
    return pl.pallas_call(
        _mix_out_kernel,
        grid=(t // tm,),
        in_specs=[
            pl.BlockSpec((tm, d), lambda i: (i, 0)),
            pl.BlockSpec((tm, wa), lambda i: (i, 0)),
            pl.BlockSpec((tm, ws), lambda i: (i, 0)),
            const((ws, ws)), const((1, ws)), const((1, ws)),
            pl.BlockSpec((wa, d), lambda i: (0, 0)),
            pl.BlockSpec((ws, d), lambda i: (wa // ws, 0)),
            const((1, d)),
        ],
        out_specs=pl.BlockSpec((tm, d), lambda i: (i, 0)),
        out_shape=jax.ShapeDtypeStruct((t, d), F32),
        compiler_params=pltpu.CompilerParams(
            dimension_semantics=("parallel",), vmem_limit_bytes=VMEM_LIMIT_BYTES),
        name="mix_out",
    )(x, a, y, w_glu, b_glu.reshape(1, ws), out_g.reshape(1, ws), w_out, w_out,
      post_g.reshape(1, d))


def kernel(x, ff1_pre_g, ff1_w_gate, ff1_w_up, ff1_w_down, ff1_post_g, mix_pre_g, w_in, lam_q1, lam_k1, lam_q2, lam_k2, attn_head_g, ssm_lam_re, ssm_lam_im, ssm_log_dt, ssm_b_re, ssm_b_im, ssm_c_re, ssm_c_im, ssm_d, ssm_w_glu, ssm_b_glu, ssm_out_g, w_out, mix_post_g, ff2_pre_g, ff2_w_gate, ff2_w_up, ff2_w_down, ff2_post_g):
    batch, seq, d_model = x.shape
    depth = w_in.shape[0]
    ssm_width = ssm_w_glu.shape[-1]
    slopes = jnp.asarray([2.0 ** (-8.0 * (i + 1) / ATTN_HEADS) for i in range(ATTN_HEADS)], F32)
    bf = lambda w: w.astype(BF16)

    xt = x.reshape(batch * seq, d_model)
    for l in range(depth):
        row = lambda p: p[l:l + 1]
        ff1 = (row(ff1_pre_g), ff1_w_gate[l], ff1_w_up[l], ff1_w_down[l], row(ff1_post_g))
        head, ff1_w, _ = _ffn(xt, *ff1, tm=FFN_HEAD_ROWS, tf=FFN_HEAD_COLS, single_buffer_x=True,
                              tiles=(0, 1), emit_weights=True)
        rest = (FFN_HEAD_ROWS // FFN_TILE_ROWS, (batch * seq - FFN_HEAD_ROWS) // FFN_TILE_ROWS)
        xt, _, (w_in_bf, w_glu_bf, w_out_bf) = _ffn(
            xt, ff1[0], *ff1_w, ff1[4], tiles=rest, into=head,
            cast=(w_in[l], ssm_w_glu[l], w_out[l]))

        qkv, u = _in_proj(xt, row(mix_pre_g), w_in_bf, ssm_width=ssm_width)
        lam_init = 0.8 - 0.6 * math.exp(-0.3 * l)
        a, ff2_w = _attention(qkv, slopes, row(lam_q1), row(lam_k1), row(lam_q2), row(lam_k2),
                              row(attn_head_g), batch=batch, seq=seq, lam_init=lam_init,
                              cast=(ff2_w_gate[l], ff2_w_up[l], ff2_w_down[l]))

        w1, cm, al = _s5_prep(ssm_lam_re[l], ssm_lam_im[l], ssm_log_dt[l], ssm_b_re[l],
                              ssm_b_im[l], ssm_c_re[l], ssm_c_im[l])
        y = _s5_chunks(u.reshape(batch, seq, ssm_width), w1, cm, al, ssm_d[l])
        y = y.reshape(batch * seq, ssm_width)

        xt = _mix_out(xt, a, y, w_glu_bf, row(ssm_b_glu), row(ssm_out_g), w_out_bf,
                      row(mix_post_g))

        xt, _, _ = _ffn(xt, row(ff2_pre_g), *ff2_w, row(ff2_post_g))
    return xt.reshape(batch, seq, d_model)
```

```python
import functools
import math

import jax
import jax.numpy as jnp
import numpy as np
from jax import lax
from jax.experimental import pallas as pl
from jax.experimental.pallas import tpu as pltpu

F32 = jnp.float32
BF16 = jnp.bfloat16

NORM_EPS = 1e-6
ATTN_HEADS = 8
ATTN_HEAD_DIM = 64
ATTN_VALUE_DIM = 2 * ATTN_HEAD_DIM
POS_SPLIT = 16
SSM_GROUP = 16
SSM_STATE = 64
CHUNK = 16
GROUP_LANES = CHUNK * SSM_GROUP
SUBLANES = 8
LANES = 128
BF16_ROWS = 16
GROUPS_PER_BLOCK = LANES // SSM_GROUP
PREP_GROUPS_PER_STEP = 16
FFN_HEAD_ROWS = 1024
FFN_TILE_ROWS = 512
FFN_TILE_COLS = 512
FFN_HEAD_COLS = 256
BF16_EXACT_INTS = 256

V7X_VMEM_BYTES = 64 * 1024 * 1024
VMEM_LIMIT_BYTES = V7X_VMEM_BYTES - 8 * 1024 * 1024


def _rms_scale(x):
    return lax.rsqrt(jnp.mean(x * x, axis=-1, keepdims=True) + NORM_EPS)


def _sigmoid(x):
    return 1.0 / (1.0 + jnp.exp(-x))


class _CastJobs:
    def __init__(self, arrays, n_steps, flat_index):
        self.flat_index = flat_index
        self.views, self.slabs = list(arrays), []
        for a in arrays:
            rows, cols = a.shape
            assert rows % BF16_ROWS == 0
            n = max(k for k in range(1, n_steps + 1) if (rows // BF16_ROWS) % k == 0)
            self.slabs.append((rows // n, cols, n))

    def _spec(self, slab):
        r, c, n = slab
        return pl.BlockSpec((r, c), lambda *g: (jnp.minimum(self.flat_index(*g), n - 1), 0))

    @property
    def in_specs(self):
        return [self._spec(s) for s in self.slabs]

    out_specs = in_specs

    @property
    def out_shapes(self):
        return [jax.ShapeDtypeStruct(v.shape, BF16) for v in self.views]

    def __len__(self):
        return len(self.views)


def _run_cast_jobs(in_refs, out_refs):
    for i_ref, o_ref in zip(in_refs, out_refs):
        o_ref[...] = i_ref[...].astype(BF16)


def _ffn_kernel(*refs, n_jobs, has_into, emit_weights):
    x_ref, pre_g_ref, wg_ref, wu_ref, wd_ref, post_g_ref = refs[:6]
    n_in = 6 + int(has_into)
    job_in = refs[n_in:n_in + n_jobs]
    o_ref = refs[n_in + n_jobs]
    n_out = n_in + n_jobs + 1
    w_out = refs[n_out:n_out + 3] if emit_weights else ()
    n_out += len(w_out)
    job_out = refs[n_out:n_out + n_jobs]
    h_ref = refs[n_out + n_jobs]
    j = pl.program_id(1)
    last = pl.num_programs(1) - 1

    def step(first, final):
        if first:
            x = x_ref[...]
            h_ref[...] = (x * _rms_scale(x) * pre_g_ref[...]).astype(BF16)
        _run_cast_jobs(job_in, job_out)
        wg, wu, wd = (r[...].astype(BF16) for r in (wg_ref, wu_ref, wd_ref))
        for w_ref, w in zip(w_out, (wg, wu, wd)):
            w_ref[...] = w
        h = h_ref[...]
        gate = jnp.dot(h, wg, preferred_element_type=F32)
        up = jnp.dot(h, wu, preferred_element_type=F32)
        act = (gate * _sigmoid(gate) * up).astype(BF16)
        part = jnp.dot(act, wd, preferred_element_type=F32)
        if first:
            o_ref[...] = part
        elif final:
            acc = o_ref[...] + part
            o_ref[...] = x_ref[...] + 0.5 * (acc * _rms_scale(acc) * post_g_ref[...])
        else:
            o_ref[...] += part

    pl.when(j == 0)(lambda: step(True, False))
    pl.when((j > 0) & (j < last))(lambda: step(False, False))
    pl.when(j == last)(lambda: step(False, True))


def _ffn(x, pre_g, w_gate, w_up, w_down, post_g, *, cast=(), tm=FFN_TILE_ROWS, tf=FFN_TILE_COLS,
         single_buffer_x=False, tiles=None, into=None, emit_weights=False):
    t, d = x.shape
    f = w_gate.shape[1]
    assert t % tm == 0 and f % tf == 0
    first, nt = tiles if tiles is not None else (0, t // tm)
    assert not emit_weights or nt == 1
    nj = f // tf
    assert nj >= 2
    jobs = _CastJobs(cast, nt * nj, lambda i, j: i * nj + j)
    x_mode = dict(pipeline_mode=pl.Buffered(1)) if single_buffer_x else {}
    w_cols = pl.BlockSpec((d, tf), lambda i, j: (0, j))
    w_rows = pl.BlockSpec((tf, d), lambda i, j: (j, 0))
    into_spec = [pl.BlockSpec(memory_space=pl.ANY)] if into is not None else []
    w_specs = [w_cols, w_cols, w_rows] if emit_weights else []
    w_shapes = [jax.ShapeDtypeStruct(w.shape, BF16) for w in (w_gate, w_up, w_down)]
    outs = pl.pallas_call(
        functools.partial(_ffn_kernel, n_jobs=len(jobs), has_into=into is not None,
                          emit_weights=emit_weights),
        grid=(nt, nj),
        in_specs=[
            pl.BlockSpec((tm, d), lambda i, j: (first + i, 0), **x_mode),
            pl.BlockSpec((1, d), lambda i, j: (0, 0)),
            w_cols, w_cols, w_rows,
            pl.BlockSpec((1, d), lambda i, j: (0, 0)),
        ] + into_spec + jobs.in_specs,
        out_specs=[pl.BlockSpec((tm, d), lambda i, j: (first + i, 0))] + w_specs + jobs.out_specs,
        out_shape=[jax.ShapeDtypeStruct((t, d), F32)] + w_shapes[:len(w_specs)] + jobs.out_shapes,
        input_output_aliases={6: 0} if into is not None else {},
        scratch_shapes=[pltpu.VMEM((tm, d), BF16)],
        compiler_params=pltpu.CompilerParams(
            dimension_semantics=("arbitrary", "arbitrary"),
            vmem_limit_bytes=VMEM_LIMIT_BYTES),
        name="ffn",
    )(x, pre_g.reshape(1, d), w_gate, w_up, w_down, post_g.reshape(1, d),
      *([into] if into is not None else []), *jobs.views)
    n_w = len(w_specs)
    return outs[0], list(outs[1:1 + n_w]), list(outs[1 + n_w:])


def _in_proj_kernel(x_ref, g_ref, w_ref, qkv_ref, u_ref, h_ref, *, n_qkv):
    j = pl.program_id(1)

    def project(first):
        if first:
            x = x_ref[...]
            h_ref[...] = (x * _rms_scale(x) * g_ref[...]).astype(BF16)
        return jnp.dot(h_ref[...], w_ref[...].astype(BF16), preferred_element_type=F32)

    @pl.when(j == 0)
    def _():
        qkv_ref[...] = project(True).astype(BF16)

    @pl.when((j > 0) & (j < n_qkv))
    def _():
        qkv_ref[...] = project(False).astype(BF16)

    @pl.when(j == n_qkv)
    def _():
        u_ref[...] = project(False)


def _in_proj(x, g, w_in, *, ssm_width, tm=1024):
    t, d = x.shape
    n = w_in.shape[1]
    tn = ssm_width
    n_qkv = (n - ssm_width) // tn
    assert t % tm == 0 and n == (n_qkv + 1) * tn
    return pl.pallas_call(
        functools.partial(_in_proj_kernel, n_qkv=n_qkv),
        grid=(t // tm, n_qkv + 1),
        in_specs=[
            pl.BlockSpec((tm, d), lambda i, j: (i, 0)),
            pl.BlockSpec((1, d), lambda i, j: (0, 0)),
            pl.BlockSpec((d, tn), lambda i, j: (0, j)),
        ],
        out_specs=[
            pl.BlockSpec((tm, tn), lambda i, j: (i, jnp.minimum(j, n_qkv - 1))),
            pl.BlockSpec((tm, tn), lambda i, j: (i, 0)),
        ],
        out_shape=[
            jax.ShapeDtypeStruct((t, n - ssm_width), BF16),
            jax.ShapeDtypeStruct((t, ssm_width), F32),
        ],
        scratch_shapes=[pltpu.VMEM((tm, d), BF16)],
        compiler_params=pltpu.CompilerParams(
            dimension_semantics=("parallel", "arbitrary"),
            vmem_limit_bytes=VMEM_LIMIT_BYTES),
        name="in_proj",
    )(x, g.reshape(1, d), w_in)


def _attn_kernel(*refs, tq, nq, lam_init, n_jobs):
    slopes_ref, lq1_ref, lk1_ref, lq2_ref, lk2_ref, hg_ref, q_ref = refs[:7]
    k_refs = refs[7:7 + nq]
    v_refs = refs[7 + nq:7 + 2 * nq]
    n_in = 7 + 2 * nq
    job_in = refs[n_in:n_in + n_jobs]
    o_ref = refs[n_in + n_jobs]
    job_out = refs[n_in + n_jobs + 1:n_in + 2 * n_jobs + 1]
    qf_ref, kf_ref, bias_ref, s_ref, p_ref = refs[n_in + 2 * n_jobs + 1:]
    h = pl.program_id(0)
    qi = pl.program_id(1)
    b = pl.program_id(2)
    e = q_ref.shape[1]

    @pl.when(b == 0)
    def _():
        slope = slopes_ref[h]
        lane = lax.broadcasted_iota(jnp.int32, (tq, e), 1)
        row = lax.broadcasted_iota(jnp.int32, (tq, e), 0)
        row_hi = (slope * POS_SPLIT) * (row // POS_SPLIT).astype(F32)
        row_lo = slope * (row % POS_SPLIT).astype(F32)
        q_base = jnp.where(lane == 0, row_hi, jnp.where(lane == 1, row_lo,
                           jnp.where(lane < 4, 1.0, 0.0)))
        k_base = jnp.where(lane < 2, -1.0, jnp.where(lane == 2, row_hi,
                           jnp.where(lane == 3, row_lo, 0.0)))
        hi_q = (lane == 0).astype(F32)
        hi_k = (lane == 2).astype(F32)
        qf_ref[...] = (q_base + (slope * tq) * qi.astype(F32) * hi_q).astype(BF16)
        for slot in range(nq):
            chunk = lax.rem(qi + slot, nq)
            sign = jnp.where(chunk < qi, 1.0, jnp.where(chunk > qi, -1.0, 0.0))
            kf = sign * k_base + (sign * (slope * tq) * chunk.astype(F32)) * hi_k
            kf_ref[slot * tq:(slot + 1) * tq, :] = kf.astype(BF16)
        col = lax.broadcasted_iota(jnp.int32, (tq, tq), 1)
        bias_ref[...] = slope * jnp.abs(lax.broadcasted_iota(jnp.int32, (tq, tq), 0)
                                        - col).astype(F32)

    _run_cast_jobs(job_in, job_out)
    lam = (jnp.exp(jnp.sum(lq1_ref[...] * lk1_ref[...], axis=-1, keepdims=True))
           - jnp.exp(jnp.sum(lq2_ref[...] * lk2_ref[...], axis=-1, keepdims=True))
           + lam_init)

    q = q_ref[...] * jnp.asarray(ATTN_HEAD_DIM ** -0.5, BF16)
    k_aug = jnp.concatenate([jnp.concatenate([r[...] for r in k_refs], axis=0), kf_ref[...]],
                            axis=1)
    v = jnp.concatenate([r[...] for r in v_refs], axis=0)
    v_ext = jnp.concatenate([v, jnp.ones_like(v)], axis=1)
    lane = lax.broadcasted_iota(jnp.int32, q.shape, 1)
    nt = (((1,), (1,)), ((), ()))
    for c in range(2):
        in_map = (lane >= c * ATTN_HEAD_DIM) & (lane < (c + 1) * ATTN_HEAD_DIM)
        q_aug = jnp.concatenate([jnp.where(in_map, q, jnp.zeros_like(q)), qf_ref[...]], axis=1)
        s_ref[c, :, :tq] = lax.dot_general(q_aug, k_aug[:tq], nt,
                                           preferred_element_type=F32) - bias_ref[...]
        s_ref[c, :, tq:] = lax.dot_general(q_aug, k_aug[tq:], nt, preferred_element_type=F32)
        s = s_ref[c]
        p_ref[c] = jnp.exp(s - jnp.max(s, axis=-1, keepdims=True)).astype(BF16)
    outs = []
    for c in range(2):
        pv = jnp.dot(p_ref[c], v_ext, preferred_element_type=F32)
        outs.append(pv[:, :e] / pv[:, e:])
    o = outs[0] - lam * outs[1]
    o_ref[...] = (o * _rms_scale(o) * hg_ref[...] * (1.0 - lam_init)).astype(o_ref.dtype)


def _attention(qkv, slopes, lq1, lk1, lq2, lk2, head_g, *, batch, seq, lam_init, cast=(), tq=512):
    t = qkv.shape[0]
    e = ATTN_VALUE_DIM
    nq = seq // tq
    nh = ATTN_HEADS
    assert 8 % nh == 0 and seq // POS_SPLIT <= BF16_EXACT_INTS
    assert tq % POS_SPLIT == 0 and seq % tq == 0
    vec = lambda a: a.reshape(1, -1).astype(F32)
    small = lambda n: pl.BlockSpec((1, n), lambda h, qi, b: (0, 0))
    jobs = _CastJobs(cast, nh * nq * batch, lambda h, qi, b: (h * nq + qi) * batch + b)

    def chunk_spec(slot, col0):
        return pl.BlockSpec((tq, e), lambda h, qi, b: (b * nq + lax.rem(qi + slot, nq), col0 + h))

    outs = pl.pallas_call(
        functools.partial(_attn_kernel, tq=tq, nq=nq, lam_init=lam_init, n_jobs=len(jobs)),
        grid=(nh, nq, batch),
        in_specs=[
            pl.BlockSpec(memory_space=pltpu.SMEM),
            small(ATTN_HEAD_DIM), small(ATTN_HEAD_DIM), small(ATTN_HEAD_DIM), small(ATTN_HEAD_DIM),
            small(e),
            pl.BlockSpec((tq, e), lambda h, qi, b: (b * nq + qi, h)),
        ] + [chunk_spec(s, nh) for s in range(nq)] + [chunk_spec(s, 2 * nh) for s in range(nq)]
        + jobs.in_specs,
        out_specs=[pl.BlockSpec((tq, e), lambda h, qi, b: (b * nq + qi, h))] + jobs.out_specs,
        out_shape=[jax.ShapeDtypeStruct((t, nh * e), BF16)] + jobs.out_shapes,
        scratch_shapes=[pltpu.VMEM((tq, e), BF16), pltpu.VMEM((seq, e), BF16),
                        pltpu.VMEM((tq, tq), F32), pltpu.VMEM((2, tq, seq), F32),
                        pltpu.VMEM((2, tq, seq), BF16)],
        compiler_params=pltpu.CompilerParams(
            dimension_semantics=("arbitrary", "arbitrary", "arbitrary"),
            vmem_limit_bytes=VMEM_LIMIT_BYTES),
        name="diff_attention",
    )(slopes, vec(lq1), vec(lk1), vec(lq2), vec(lk2), vec(head_g), qkv,
      *([qkv] * (2 * nq)), *jobs.views)
    return outs[0], list(outs[1:])


def _discretize(lam_re, lam_im, log_dt):
    dt = jnp.exp(log_dt)
    mag = jnp.exp(lam_re * dt)
    a_re = mag * jnp.cos(lam_im * dt)
    a_im = mag * jnp.sin(lam_im * dt)
    den = lam_re * lam_re + lam_im * lam_im
    nr = a_re - 1.0
    f_re = (nr * lam_re + a_im * lam_im) / den
    f_im = (a_im * lam_re - nr * lam_im) / den
    return a_re, a_im, f_re, f_im


def _cmul(ar, ai, br, bi):
    return ar * br - ai * bi, ar * bi + ai * br


def _split_bf16(x, parts):
    out = []
    for k in range(parts):
        piece = x.astype(BF16)
        out.append(piece)
        if k + 1 < parts:
            x = x - piece.astype(F32)
    return out


def _copy_dot(x, onehot, dims):
    return sum(lax.dot_general(piece, onehot, dims, preferred_element_type=F32)
               for piece in _split_bf16(x, 3))


def _dot_3pass(a_parts, b_parts, dims):
    a_hi, a_lo = a_parts
    b_hi, b_lo = b_parts
    dot = lambda x, y: lax.dot_general(x, y, dims, preferred_element_type=F32)
    return dot(a_hi, b_hi) + (dot(a_hi, b_lo) + dot(a_lo, b_hi))


def _int_power(a_re, a_im, e, nbits):
    shape = jnp.broadcast_shapes(a_re.shape, e.shape)
    p_re, p_im = jnp.ones(shape, F32), jnp.zeros(shape, F32)
    s_re, s_im = a_re, a_im
    for k in range(nbits):
        bit = (e & (1 << k)) != 0
        m_re, m_im = _cmul(p_re, p_im, s_re, s_im)
        p_re, p_im = jnp.where(bit, m_re, p_re), jnp.where(bit, m_im, p_im)
        if k + 1 < nbits:
            s_re, s_im = _cmul(s_re, s_im, s_re, s_im)
    return p_re, p_im


def _shift_lanes(x, s, left):
    if s == 0:
        return x
    n = x.shape[-1]
    lane = lax.broadcasted_iota(jnp.int32, x.shape, 1)
    if left:
        return jnp.where(lane < n - s, pltpu.roll(x, n - s, axis=1), 0.0)
    return jnp.where(lane >= s, pltpu.roll(x, s, axis=1), 0.0)


def _s5_prep_kernel(lre_ref, lim_ref, ldt_ref, btre_ref, btim_ref, cre_ref, cim_ref,
                    w1_ref, cm_ref, al_ref):
    for i in range(w1_ref.shape[0]):
        _s5_prep_group(i, lre_ref, lim_ref, ldt_ref, btre_ref, btim_ref, cre_ref, cim_ref,
                       w1_ref, cm_ref, al_ref)


def _s5_prep_group(i, lre_ref, lim_ref, ldt_ref, btre_r_ref, btim_r_ref, cre_ref, cim_ref,
                   w1_ref, cm_ref, al_ref):
    n, p, l = SSM_STATE, SSM_GROUP, CHUNK
    nbits = l.bit_length() - 1
    contract0 = (((0,), (0,)), ((), ()))
    contract_mm = (((1,), (0,)), ((), ()))

    a_re, a_im, f_re, f_im = _discretize(lre_ref[i], lim_ref[i], ldt_ref[i])
    bt_re, bt_im = btre_r_ref[i], btim_r_ref[i]
    bb_re, bb_im = _cmul(f_re, f_im, bt_re, bt_im)
    tau = lax.broadcasted_iota(jnp.int32, (l, 2 * n), 0)
    lane = lax.broadcasted_iota(jnp.int32, (l, 2 * n), 1)
    pw_re, pw_im = _int_power(a_re, a_im, jnp.where(lane < n, l - 1 - tau, tau), nbits)
    bm_re = pw_re[:, None, :] * bb_re[None, :, :] - pw_im[:, None, :] * bb_im[None, :, :]
    bm_im = pw_re[:, None, :] * bb_im[None, :, :] + pw_im[:, None, :] * bb_re[None, :, :]
    w1_ref[i, :, GROUP_LANES:GROUP_LANES + 2 * n] = bm_re.reshape(l * p, 2 * n).astype(BF16)
    w1_ref[i, :, GROUP_LANES + 2 * n:] = bm_im.reshape(l * p, 2 * n).astype(BF16)
    al_re, al_im = a_re, a_im
    for _ in range(nbits):
        al_re, al_im = _cmul(al_re, al_im, al_re, al_im)
    al_ref[i] = jnp.concatenate([al_re, al_im], axis=0)

    quant = jnp.concatenate([a_re, a_im, jnp.zeros((SUBLANES - 2, 2 * n), F32)], axis=0)
    quant_t = quant.T
    a_re_c, a_im_c = quant_t[:, 0:1], quant_t[:, 1:2]

    lanes_p = lax.broadcasted_iota(jnp.int32, (p, GROUP_LANES), 1)
    rows_p = lax.broadcasted_iota(jnp.int32, (p, GROUP_LANES), 0)
    tile_p = (lanes_p % p == rows_p).astype(BF16)
    tile_lag = ((l - 1) - lanes_p // p == rows_p).astype(BF16)
    ac_re, ac_im = _cmul(_copy_dot(pw_re, tile_lag, contract0), _copy_dot(pw_im, tile_lag, contract0),
                         _copy_dot(cre_ref[i], tile_p, contract0),
                         _copy_dot(cim_ref[i], tile_p, contract0))
    ac_re_parts, ac_im_parts = _split_bf16(ac_re, 2), _split_bf16(ac_im, 2)
    is_fwd = lax.broadcasted_iota(jnp.int32, bb_re.shape, 1) < n
    t_rows = None
    for d in range(2):
        in_dir = is_fwd if d == 0 else ~is_fwd
        taps = (_dot_3pass(_split_bf16(jnp.where(in_dir, bb_re, 0.0), 2), ac_re_parts, contract_mm)
                - _dot_3pass(_split_bf16(jnp.where(in_dir, bb_im, 0.0), 2), ac_im_parts,
                             contract_mm))
        rows = [_shift_lanes(taps, p * (tp if d == 0 else l - 1 - tp), left=(d == 1))
                for tp in range(l)]
        rows = jnp.concatenate(rows, axis=0)
        t_rows = rows if t_rows is None else t_rows + rows
    w1_ref[i, :, 0:GROUP_LANES] = t_rows.astype(BF16)
    r_re, r_im = _cmul(a_re_c, a_im_c, ac_re, ac_im)
    cm_ref[i] = jnp.concatenate([r_re, -r_im], axis=0).astype(BF16)


def _s5_prep(lam_re, lam_im, log_dt, b_re, b_im, c_re, c_im):
    _, g, n = lam_re.shape
    p = b_re.shape[-1]
    row = lambda a: jnp.concatenate([a[0], a[1]], axis=-1)[:, None, :]
    ldt_r = row(jnp.broadcast_to(log_dt[:, :, None], (2, g, n)))
    bt = lambda b: jnp.concatenate([jnp.swapaxes(b[0], 1, 2), jnp.swapaxes(b[1], 1, 2)], axis=-1)
    cr = lambda c: jnp.concatenate([c[0], c[1]], axis=-1)

    gb = PREP_GROUPS_PER_STEP
    assert g % gb == 0
    spec_r = lambda rows: pl.BlockSpec((gb, rows, 2 * n), lambda i: (i, 0, 0))
    return pl.pallas_call(
        _s5_prep_kernel,
        grid=(g // gb,),
        in_specs=[spec_r(1), spec_r(1), spec_r(1), spec_r(p), spec_r(p), spec_r(p), spec_r(p)],
        out_specs=[
            pl.BlockSpec((gb, GROUP_LANES, GROUP_LANES + 4 * n), lambda i: (i, 0, 0)),
            pl.BlockSpec((gb, 4 * n, GROUP_LANES), lambda i: (i, 0, 0)),
            pl.BlockSpec((gb, 2, 2 * n), lambda i: (i, 0, 0)),
        ],
        out_shape=[
            jax.ShapeDtypeStruct((g, GROUP_LANES, GROUP_LANES + 4 * n), BF16),
            jax.ShapeDtypeStruct((g, 4 * n, GROUP_LANES), BF16),
            jax.ShapeDtypeStruct((g, 2, 2 * n), F32),
        ],
        compiler_params=pltpu.CompilerParams(dimension_semantics=("parallel",)),
        name="s5_prep",
    )(row(lam_re), row(lam_im), ldt_r, bt(b_re), bt(b_im), cr(c_re), cr(c_im))


def _swap_pieces(v):
    piece = lax.broadcasted_iota(jnp.int32, v[0].shape, 1) // SSM_GROUP
    v = list(v)
    for d in (4, 2, 1):
        hi = (piece & d) != 0
        for a in range(8):
            if a & d:
                continue
            va, vb = v[a], v[a + d]
            v[a] = jnp.where(hi, pltpu.roll(vb, d * SSM_GROUP, axis=1), va)
            v[a + d] = jnp.where(hi, vb, pltpu.roll(va, LANES - d * SSM_GROUP, axis=1))
    return v


def _s5_chunk_kernel(u_ref, perm_ref, w1_ref, cm_ref, al_ref, x_ref,
                     a_ref, s_ref, hf_ref, h_ref, *, nb, nc):
    gl = GROUP_LANES
    gb = GROUPS_PER_BLOCK
    n2 = 2 * SSM_STATE
    half = CHUNK // 2
    ctile = SUBLANES * CHUNK

    def token_rows(ct, h, j):
        return pl.ds(ct * ctile + h * half + j, SUBLANES, stride=CHUNK)

    rows_all = nb * nc

    def gather_body(ct, carry):
        for b in range(nb):
            for h in range(2):
                for j in range(half):
                    a_ref[j, pl.ds(h * rows_all + ct * SUBLANES * nb + b, SUBLANES, stride=nb), :] = (
                        u_ref[b, token_rows(ct, h, j), :])
        return carry

    lax.fori_loop(0, nc // SUBLANES, gather_body, 0)
    lhs = jnp.concatenate([a_ref[j] for j in range(half)], axis=1).astype(BF16)
    xp = jnp.dot(lhs, perm_ref[...], preferred_element_type=F32).astype(BF16)

    for g in range(gb):
        x = jnp.concatenate([xp[:rows_all, g * LANES:(g + 1) * LANES],
                             xp[rows_all:, g * LANES:(g + 1) * LANES]], axis=1)
        r = jnp.dot(x, w1_ref[g], preferred_element_type=F32)
        x_ref[2 * g] = r[:, :LANES]
        x_ref[2 * g + 1] = r[:, LANES:gl]
        s_ref[:, g * gl:(g + 1) * gl] = r[:, gl:]

    a_re = [al_ref[g, 0:1, :] for g in range(gb)]
    a_im = [al_ref[g, 1:2, :] for g in range(gb)]
    is_fwd = lax.broadcasted_iota(jnp.int32, (SUBLANES, n2), 1) < SSM_STATE

    def step(h, s):
        out = []
        for g in range(gb):
            hr, hi = h[:, g * gl:g * gl + n2], h[:, g * gl + n2:(g + 1) * gl]
            sr, si = s[:, g * gl:g * gl + n2], s[:, g * gl + n2:(g + 1) * gl]
            out.append(a_re[g] * hr - a_im[g] * hi + sr)
            out.append(a_re[g] * hi + a_im[g] * hr + si)
        return jnp.concatenate(out, axis=1)

    cpt = SUBLANES // nb
    n_tiles = nc // cpt
    tile_rows = lambda t: pl.ds(pl.multiple_of(t * SUBLANES, SUBLANES), SUBLANES)

    def pick(fwd_rows, bwd_rows):
        return jnp.concatenate(
            [jnp.where(is_fwd[:fwd_rows.shape[0]], fwd_rows[:, k * n2:(k + 1) * n2],
                       bwd_rows[:, k * n2:(k + 1) * n2]) for k in range(2 * gb)], axis=1)

    def sweep_body(t, h, *, second_half):
        tb = n_tiles - 1 - t
        s_f = s_ref[tile_rows(t), :]
        s_b = s_ref[tile_rows(tb), :]
        before = []
        for k in range(cpt):
            kb = cpt - 1 - k
            before.append(h)
            h = step(h, pick(s_f[k * nb:(k + 1) * nb], s_b[kb * nb:(kb + 1) * nb]))
        fwd_valid = jnp.concatenate(before, axis=0)
        bwd_valid = jnp.concatenate(before[::-1], axis=0)
        if second_half:
            h_ref[tile_rows(t), :] = pick(fwd_valid, h_ref[tile_rows(t), :])
            h_ref[tile_rows(tb), :] = pick(hf_ref[tile_rows(tb), :], bwd_valid)
        else:
            hf_ref[tile_rows(t), :] = fwd_valid
            h_ref[tile_rows(tb), :] = bwd_valid
        return h

    assert n_tiles % 2 == 0
    h_mid = lax.fori_loop(0, n_tiles // 2, functools.partial(sweep_body, second_half=False),
                          jnp.zeros((nb, gb * gl), F32))
    lax.fori_loop(n_tiles // 2, n_tiles, functools.partial(sweep_body, second_half=True), h_mid)

    for g in range(gb):
        r = jnp.dot(h_ref[:, g * gl:(g + 1) * gl].astype(BF16), cm_ref[g],
                    preferred_element_type=F32)
        x_ref[2 * g] += r[:, :LANES]
        x_ref[2 * g + 1] += r[:, LANES:]


def _s5_chunks(u, w1, cm, al):
    nb, seq, width = u.shape
    nc = seq // CHUNK
    rows = nb * nc
    gb = GROUPS_PER_BLOCK
    assert SUBLANES % nb == 0 and nc % SUBLANES == 0 and width % LANES == 0
    blk = pl.BlockSpec((nb, seq, LANES), lambda i: (0, 0, i))
    half = CHUNK // 2
    src = np.arange(half * LANES)
    j, g, p = src // LANES, (src % LANES) // SSM_GROUP, src % SSM_GROUP
    perm = np.zeros((half * LANES, half * LANES), np.float32)
    perm[src, g * LANES + j * SSM_GROUP + p] = 1.0
    return pl.pallas_call(
        functools.partial(_s5_chunk_kernel, nb=nb, nc=nc),
        grid=(width // LANES,),
        in_specs=[
            blk,
            pl.BlockSpec(perm.shape, lambda i: (0, 0), pipeline_mode=pl.Buffered(1)),
            pl.BlockSpec((gb,) + w1.shape[1:], lambda i: (i, 0, 0)),
            pl.BlockSpec((gb,) + cm.shape[1:], lambda i: (i, 0, 0)),
            pl.BlockSpec((gb,) + al.shape[1:], lambda i: (i, 0, 0)),
        ],
        out_specs=pl.BlockSpec((2 * gb, rows, LANES), lambda i: (i, 0, 0)),
        out_shape=jax.ShapeDtypeStruct((2 * width // SSM_GROUP, rows, LANES), F32),
        scratch_shapes=[pltpu.VMEM((half, 2 * rows, LANES), F32)]
        + [pltpu.VMEM((rows, gb * GROUP_LANES), F32)] * 3,
        compiler_params=pltpu.CompilerParams(
            dimension_semantics=("parallel",), vmem_limit_bytes=VMEM_LIMIT_BYTES),
        name="s5_chunks",
    )(u, jnp.asarray(perm, BF16), w1, cm, al)


def _mix_out_kernel(x_ref, a_ref, ys_ref, u_ref, d_ref, wglu_ref, bglu_ref, og_ref, wo_a_ref,
                    wo_s_ref, pg_ref, o_ref, yn_ref):
    nb, tt, d = x_ref.shape
    gb = GROUPS_PER_BLOCK
    half = CHUNK // 2
    rows = nb * tt
    for o in range(yn_ref.shape[0]):
        for b in range(nb):
            for h in range(2):
                w = [ys_ref[(o * gb + g) * 2 + h, pl.ds(b, SUBLANES, stride=nb), :]
                     for g in range(gb)]
                v = _swap_pieces(w)
                for j in range(half):
                    yn_ref[o, pl.ds(b * tt + h * half + j, SUBLANES, stride=CHUNK), :] = v[j]
    y = jnp.concatenate([yn_ref[o] for o in range(yn_ref.shape[0])], axis=1)
    y = y + d_ref[...] * u_ref[...].reshape(rows, -1)
    g = y * (0.5 * (1.0 + jnp.tanh(math.sqrt(2.0 / math.pi) * (y + 0.044715 * (y * y * y)))))
    z = jnp.dot(g.astype(BF16), wglu_ref[...], preferred_element_type=F32) + bglu_ref[...]
    s = g * _sigmoid(z)
    s = s * _rms_scale(s) * og_ref[...]
    mixed = (jnp.dot(a_ref[...].reshape(rows, -1), wo_a_ref[...], preferred_element_type=F32)
             + jnp.dot(s.astype(BF16), wo_s_ref[...], preferred_element_type=F32))
    out = x_ref[...].reshape(rows, d) + mixed * _rms_scale(mixed) * pg_ref[...]
    o_ref[...] = out.reshape(nb, tt, d)


def _mix_out(x, a, ys, u, skip_d, w_glu, b_glu, out_g, w_out, post_g):
    nb, seq, d = x.shape
    wa = a.shape[2]
    ws = u.shape[2]
    tt = SUBLANES * CHUNK
    assert seq % tt == 0 and ys.shape == (2 * ws // SSM_GROUP, nb * seq // CHUNK, LANES)
    const = lambda shape: pl.BlockSpec(shape, lambda i: (0, 0))
    tile = lambda width: pl.BlockSpec((nb, tt, width), lambda i: (0, i, 0))
    return pl.pallas_call(
        _mix_out_kernel,
        grid=(seq // tt,),
        in_specs=[
            tile(d), tile(wa),
            pl.BlockSpec((ys.shape[0], SUBLANES * nb, LANES), lambda i: (0, i, 0)),
            tile(ws), const((1, ws)),
            const((ws, ws)), const((1, ws)), const((1, ws)),
            pl.BlockSpec((wa, d), lambda i: (0, 0)),
            pl.BlockSpec((ws, d), lambda i: (wa // ws, 0)),
            const((1, d)),
        ],
        out_specs=tile(d),
        out_shape=jax.ShapeDtypeStruct(x.shape, F32),
        scratch_shapes=[pltpu.VMEM((ws // LANES, nb * tt, LANES), F32)],
        compiler_params=pltpu.CompilerParams(
            dimension_semantics=("parallel",), vmem_limit_bytes=VMEM_LIMIT_BYTES),
        name="mix_out",
    )(x, a, ys, u, skip_d.reshape(1, ws), w_glu, b_glu.reshape(1, ws), out_g.reshape(1, ws),
      w_out, w_out, post_g.reshape(1, d))


def kernel(x, ff1_pre_g, ff1_w_gate, ff1_w_up, ff1_w_down, ff1_post_g, mix_pre_g, w_in, lam_q1, lam_k1, lam_q2, lam_k2, attn_head_g, ssm_lam_re, ssm_lam_im, ssm_log_dt, ssm_b_re, ssm_b_im, ssm_c_re, ssm_c_im, ssm_d, ssm_w_glu, ssm_b_glu, ssm_out_g, w_out, mix_post_g, ff2_pre_g, ff2_w_gate, ff2_w_up, ff2_w_down, ff2_post_g):
    batch, seq, d_model = x.shape
    depth = w_in.shape[0]
    ssm_width = ssm_w_glu.shape[-1]
    slopes = jnp.asarray([2.0 ** (-8.0 * (i + 1) / ATTN_HEADS) for i in range(ATTN_HEADS)], F32)
    bf = lambda w: w.astype(BF16)

    xt = x.reshape(batch * seq, d_model)
    for l in range(depth):
        row = lambda p: p[l:l + 1]
        ff1 = (row(ff1_pre_g), ff1_w_gate[l], ff1_w_up[l], ff1_w_down[l], row(ff1_post_g))
        head, ff1_w, _ = _ffn(xt, *ff1, tm=FFN_HEAD_ROWS, tf=FFN_HEAD_COLS, single_buffer_x=True,
                              tiles=(0, 1), emit_weights=True)
        rest = (FFN_HEAD_ROWS // FFN_TILE_ROWS, (batch * seq - FFN_HEAD_ROWS) // FFN_TILE_ROWS)
        xt, _, (w_in_bf, w_glu_bf, w_out_bf) = _ffn(
            xt, ff1[0], *ff1_w, ff1[4], tiles=rest, into=head,
            cast=(w_in[l], ssm_w_glu[l], w_out[l]))

        qkv, u = _in_proj(xt, row(mix_pre_g), w_in_bf, ssm_width=ssm_width)
        lam_init = 0.8 - 0.6 * math.exp(-0.3 * l)
        a, ff2_w = _attention(qkv, slopes, row(lam_q1), row(lam_k1), row(lam_q2), row(lam_k2),
                              row(attn_head_g), batch=batch, seq=seq, lam_init=lam_init,
                              cast=(ff2_w_gate[l], ff2_w_up[l], ff2_w_down[l]))

        w1, cm, al = _s5_prep(ssm_lam_re[l], ssm_lam_im[l], ssm_log_dt[l], ssm_b_re[l],
                              ssm_b_im[l], ssm_c_re[l], ssm_c_im[l])
        u3 = u.reshape(batch, seq, ssm_width)
        ys = _s5_chunks(u3, w1, cm, al)

        xt = _mix_out(xt.reshape(batch, seq, d_model), a.reshape(batch, seq, -1), ys, u3, ssm_d[l],
                      w_glu_bf, row(ssm_b_glu), row(ssm_out_g), w_out_bf, row(mix_post_g))
        xt = xt.reshape(batch * seq, d_model)

        xt, _, _ = _ffn(xt, row(ff2_pre_g), *ff2_w, row(ff2_post_g))
    return xt.reshape(batch, seq, d_model)
```

```python
import functools
import math

import jax
import jax.numpy as jnp
import numpy as np
from jax import lax
from jax.experimental import pallas as pl
from jax.experimental.pallas import tpu as pltpu

F32 = jnp.float32
BF16 = jnp.bfloat16

NORM_EPS = 1e-6
ATTN_HEADS = 8
ATTN_HEAD_DIM = 64
ATTN_VALUE_DIM = 2 * ATTN_HEAD_DIM
POS_SPLIT = 16
SSM_GROUP = 16
SSM_STATE = 64
CHUNK = 16
GROUP_LANES = CHUNK * SSM_GROUP
SUBLANES = 8
LANES = 128
BF16_ROWS = 16
GROUPS_PER_BLOCK = LANES // SSM_GROUP
PREP_GROUPS_PER_STEP = 16
FFN_HEAD_ROWS = 1024
FFN_TILE_ROWS = 512
FFN_TILE_COLS = 512
FFN_HEAD_COLS = 256
BF16_EXACT_INTS = 256

V7X_VMEM_BYTES = 64 * 1024 * 1024
VMEM_LIMIT_BYTES = V7X_VMEM_BYTES - 8 * 1024 * 1024


def _rms_scale(x):
    return lax.rsqrt(jnp.mean(x * x, axis=-1, keepdims=True) + NORM_EPS)


def _sigmoid(x):
    return 1.0 / (1.0 + jnp.exp(-x))


class _CastJobs:
    def __init__(self, arrays, n_steps, flat_index):
        self.flat_index = flat_index
        self.views, self.slabs = list(arrays), []
        for a in arrays:
            rows, cols = a.shape
            assert rows % BF16_ROWS == 0
            n = max(k for k in range(1, n_steps + 1) if (rows // BF16_ROWS) % k == 0)
            self.slabs.append((rows // n, cols, n))

    def _spec(self, slab):
        r, c, n = slab
        return pl.BlockSpec((r, c), lambda *g: (jnp.minimum(self.flat_index(*g), n - 1), 0))

    @property
    def in_specs(self):
        return [self._spec(s) for s in self.slabs]

    out_specs = in_specs

    @property
    def out_shapes(self):
        return [jax.ShapeDtypeStruct(v.shape, BF16) for v in self.views]

    def __len__(self):
        return len(self.views)


def _run_cast_jobs(in_refs, out_refs):
    for i_ref, o_ref in zip(in_refs, out_refs):
        o_ref[...] = i_ref[...].astype(BF16)


def _ffn_kernel(*refs, n_jobs, has_into, emit_weights):
    x_ref, pre_g_ref, wg_ref, wu_ref, wd_ref, post_g_ref = refs[:6]
    n_in = 6 + int(has_into)
    job_in = refs[n_in:n_in + n_jobs]
    o_ref = refs[n_in + n_jobs]
    n_out = n_in + n_jobs + 1
    w_out = refs[n_out:n_out + 3] if emit_weights else ()
    n_out += len(w_out)
    job_out = refs[n_out:n_out + n_jobs]
    h_ref = refs[n_out + n_jobs]
    j = pl.program_id(1)
    last = pl.num_programs(1) - 1

    def step(first, final):
        if first:
            x = x_ref[...]
            h_ref[...] = (x * _rms_scale(x) * pre_g_ref[...]).astype(BF16)
        _run_cast_jobs(job_in, job_out)
        wg, wu, wd = (r[...].astype(BF16) for r in (wg_ref, wu_ref, wd_ref))
        for w_ref, w in zip(w_out, (wg, wu, wd)):
            w_ref[...] = w
        h = h_ref[...]
        gate = jnp.dot(h, wg, preferred_element_type=F32)
        up = jnp.dot(h, wu, preferred_element_type=F32)
        act = (gate * _sigmoid(gate) * up).astype(BF16)
        part = jnp.dot(act, wd, preferred_element_type=F32)
        if first:
            o_ref[...] = part
        elif final:
            acc = o_ref[...] + part
            o_ref[...] = x_ref[...] + 0.5 * (acc * _rms_scale(acc) * post_g_ref[...])
        else:
            o_ref[...] += part

    pl.when(j == 0)(lambda: step(True, False))
    pl.when((j > 0) & (j < last))(lambda: step(False, False))
    pl.when(j == last)(lambda: step(False, True))


def _ffn(x, pre_g, w_gate, w_up, w_down, post_g, *, cast=(), tm=FFN_TILE_ROWS, tf=FFN_TILE_COLS,
         single_buffer_x=False, tiles=None, into=None, emit_weights=False):
    t, d = x.shape
    f = w_gate.shape[1]
    assert t % tm == 0 and f % tf == 0
    first, nt = tiles if tiles is not None else (0, t // tm)
    assert not emit_weights or nt == 1
    nj = f // tf
    assert nj >= 2
    jobs = _CastJobs(cast, nt * nj, lambda i, j: i * nj + j)
    x_mode = dict(pipeline_mode=pl.Buffered(1)) if single_buffer_x else {}
    w_cols = pl.BlockSpec((d, tf), lambda i, j: (0, j))
    w_rows = pl.BlockSpec((tf, d), lambda i, j: (j, 0))
    into_spec = [pl.BlockSpec(memory_space=pl.ANY)] if into is not None else []
    w_specs = [w_cols, w_cols, w_rows] if emit_weights else []
    w_shapes = [jax.ShapeDtypeStruct(w.shape, BF16) for w in (w_gate, w_up, w_down)]
    outs = pl.pallas_call(
        functools.partial(_ffn_kernel, n_jobs=len(jobs), has_into=into is not None,
                          emit_weights=emit_weights),
        grid=(nt, nj),
        in_specs=[
            pl.BlockSpec((tm, d), lambda i, j: (first + i, 0), **x_mode),
            pl.BlockSpec((1, d), lambda i, j: (0, 0)),
            w_cols, w_cols, w_rows,
            pl.BlockSpec((1, d), lambda i, j: (0, 0)),
        ] + into_spec + jobs.in_specs,
        out_specs=[pl.BlockSpec((tm, d), lambda i, j: (first + i, 0))] + w_specs + jobs.out_specs,
        out_shape=[jax.ShapeDtypeStruct((t, d), F32)] + w_shapes[:len(w_specs)] + jobs.out_shapes,
        input_output_aliases={6: 0} if into is not None else {},
        scratch_shapes=[pltpu.VMEM((tm, d), BF16)],
        compiler_params=pltpu.CompilerParams(
            dimension_semantics=("arbitrary", "arbitrary"),
            vmem_limit_bytes=VMEM_LIMIT_BYTES),
        name="ffn",
    )(x, pre_g.reshape(1, d), w_gate, w_up, w_down, post_g.reshape(1, d),
      *([into] if into is not None else []), *jobs.views)
    n_w = len(w_specs)
    return outs[0], list(outs[1:1 + n_w]), list(outs[1 + n_w:])


def _in_proj_kernel(x_ref, g_ref, w_ref, qkv_ref, u_ref, h_ref, *, n_qkv):
    j = pl.program_id(1)

    def project(first):
        if first:
            x = x_ref[...]
            h_ref[...] = (x * _rms_scale(x) * g_ref[...]).astype(BF16)
        return jnp.dot(h_ref[...], w_ref[...].astype(BF16), preferred_element_type=F32)

    @pl.when(j == 0)
    def _():
        qkv_ref[...] = project(True).astype(BF16)

    @pl.when((j > 0) & (j < n_qkv))
    def _():
        qkv_ref[...] = project(False).astype(BF16)

    @pl.when(j == n_qkv)
    def _():
        u_ref[...] = project(False)


def _in_proj(x, g, w_in, *, ssm_width, tm=1024):
    t, d = x.shape
    n = w_in.shape[1]
    tn = ssm_width
    n_qkv = (n - ssm_width) // tn
    assert t % tm == 0 and n == (n_qkv + 1) * tn
    return pl.pallas_call(
        functools.partial(_in_proj_kernel, n_qkv=n_qkv),
        grid=(t // tm, n_qkv + 1),
        in_specs=[
            pl.BlockSpec((tm, d), lambda i, j: (i, 0)),
            pl.BlockSpec((1, d), lambda i, j: (0, 0)),
            pl.BlockSpec((d, tn), lambda i, j: (0, j)),
        ],
        out_specs=[
            pl.BlockSpec((tm, tn), lambda i, j: (i, jnp.minimum(j, n_qkv - 1))),
            pl.BlockSpec((tm, tn), lambda i, j: (i, 0)),
        ],
        out_shape=[
            jax.ShapeDtypeStruct((t, n - ssm_width), BF16),
            jax.ShapeDtypeStruct((t, ssm_width), F32),
        ],
        scratch_shapes=[pltpu.VMEM((tm, d), BF16)],
        compiler_params=pltpu.CompilerParams(
            dimension_semantics=("parallel", "arbitrary"),
            vmem_limit_bytes=VMEM_LIMIT_BYTES),
        name="in_proj",
    )(x, g.reshape(1, d), w_in)


def _attn_kernel(*refs, tq, nq, lam_init, n_jobs):
    slopes_ref, lq1_ref, lk1_ref, lq2_ref, lk2_ref, hg_ref, q_ref = refs[:7]
    k_refs = refs[7:7 + nq]
    v_refs = refs[7 + nq:7 + 2 * nq]
    n_in = 7 + 2 * nq
    job_in = refs[n_in:n_in + n_jobs]
    o_ref = refs[n_in + n_jobs]
    job_out = refs[n_in + n_jobs + 1:n_in + 2 * n_jobs + 1]
    qf_ref, kf_ref, bias_ref, s_ref, p_ref = refs[n_in + 2 * n_jobs + 1:]
    h = pl.program_id(0)
    qi = pl.program_id(1)
    b = pl.program_id(2)
    e = q_ref.shape[1]

    @pl.when(b == 0)
    def _():
        slope = slopes_ref[h]
        lane = lax.broadcasted_iota(jnp.int32, (tq, e), 1)
        row = lax.broadcasted_iota(jnp.int32, (tq, e), 0)
        row_hi = (slope * POS_SPLIT) * (row // POS_SPLIT).astype(F32)
        row_lo = slope * (row % POS_SPLIT).astype(F32)
        q_base = jnp.where(lane == 0, row_hi, jnp.where(lane == 1, row_lo,
                           jnp.where(lane < 4, 1.0, 0.0)))
        k_base = jnp.where(lane < 2, -1.0, jnp.where(lane == 2, row_hi,
                           jnp.where(lane == 3, row_lo, 0.0)))
        hi_q = (lane == 0).astype(F32)
        hi_k = (lane == 2).astype(F32)
        qf_ref[...] = (q_base + (slope * tq) * qi.astype(F32) * hi_q).astype(BF16)
        for slot in range(nq):
            chunk = lax.rem(qi + slot, nq)
            sign = jnp.where(chunk < qi, 1.0, jnp.where(chunk > qi, -1.0, 0.0))
            kf = sign * k_base + (sign * (slope * tq) * chunk.astype(F32)) * hi_k
            kf_ref[slot * tq:(slot + 1) * tq, :] = kf.astype(BF16)
        col = lax.broadcasted_iota(jnp.int32, (tq, tq), 1)
        bias_ref[...] = slope * jnp.abs(lax.broadcasted_iota(jnp.int32, (tq, tq), 0)
                                        - col).astype(F32)

    _run_cast_jobs(job_in, job_out)
    lam = (jnp.exp(jnp.sum(lq1_ref[...] * lk1_ref[...], axis=-1, keepdims=True))
           - jnp.exp(jnp.sum(lq2_ref[...] * lk2_ref[...], axis=-1, keepdims=True))
           + lam_init)

    q = q_ref[...] * jnp.asarray(ATTN_HEAD_DIM ** -0.5, BF16)
    k_aug = jnp.concatenate([jnp.concatenate([r[...] for r in k_refs], axis=0), kf_ref[...]],
                            axis=1)
    v = jnp.concatenate([r[...] for r in v_refs], axis=0)
    v_ext = jnp.concatenate([v, jnp.ones_like(v)], axis=1)
    lane = lax.broadcasted_iota(jnp.int32, q.shape, 1)
    nt = (((1,), (1,)), ((), ()))
    for c in range(2):
        in_map = (lane >= c * ATTN_HEAD_DIM) & (lane < (c + 1) * ATTN_HEAD_DIM)
        q_aug = jnp.concatenate([jnp.where(in_map, q, jnp.zeros_like(q)), qf_ref[...]], axis=1)
        s_ref[c, :, :tq] = lax.dot_general(q_aug, k_aug[:tq], nt,
                                           preferred_element_type=F32) - bias_ref[...]
        s_ref[c, :, tq:] = lax.dot_general(q_aug, k_aug[tq:], nt, preferred_element_type=F32)
        s = s_ref[c]
        p_ref[c] = jnp.exp(s - jnp.max(s, axis=-1, keepdims=True)).astype(BF16)
    outs = []
    for c in range(2):
        pv = jnp.dot(p_ref[c], v_ext, preferred_element_type=F32)
        outs.append(pv[:, :e] / pv[:, e:])
    o = outs[0] - lam * outs[1]
    o_ref[...] = (o * _rms_scale(o) * hg_ref[...] * (1.0 - lam_init)).astype(o_ref.dtype)


def _attention(qkv, slopes, lq1, lk1, lq2, lk2, head_g, *, batch, seq, lam_init, cast=(), tq=1024):
    t = qkv.shape[0]
    e = ATTN_VALUE_DIM
    nq = seq // tq
    nh = ATTN_HEADS
    assert 8 % nh == 0 and seq // POS_SPLIT <= BF16_EXACT_INTS
    assert tq % POS_SPLIT == 0 and seq % tq == 0
    vec = lambda a: a.reshape(1, -1).astype(F32)
    small = lambda n: pl.BlockSpec((1, n), lambda h, qi, b: (0, 0))
    jobs = _CastJobs(cast, nh * nq * batch, lambda h, qi, b: (h * nq + qi) * batch + b)

    def chunk_spec(slot, col0):
        return pl.BlockSpec((tq, e), lambda h, qi, b: (b * nq + lax.rem(qi + slot, nq), col0 + h))

    outs = pl.pallas_call(
        functools.partial(_attn_kernel, tq=tq, nq=nq, lam_init=lam_init, n_jobs=len(jobs)),
        grid=(nh, nq, batch),
        in_specs=[
            pl.BlockSpec(memory_space=pltpu.SMEM),
            small(ATTN_HEAD_DIM), small(ATTN_HEAD_DIM), small(ATTN_HEAD_DIM), small(ATTN_HEAD_DIM),
            small(e),
            pl.BlockSpec((tq, e), lambda h, qi, b: (b * nq + qi, h)),
        ] + [chunk_spec(s, nh) for s in range(nq)] + [chunk_spec(s, 2 * nh) for s in range(nq)]
        + jobs.in_specs,
        out_specs=[pl.BlockSpec((tq, e), lambda h, qi, b: (b * nq + qi, h))] + jobs.out_specs,
        out_shape=[jax.ShapeDtypeStruct((t, nh * e), BF16)] + jobs.out_shapes,
        scratch_shapes=[pltpu.VMEM((tq, e), BF16), pltpu.VMEM((seq, e), BF16),
                        pltpu.VMEM((tq, tq), F32), pltpu.VMEM((2, tq, seq), F32),
                        pltpu.VMEM((2, tq, seq), BF16)],
        compiler_params=pltpu.CompilerParams(
            dimension_semantics=("arbitrary", "arbitrary", "arbitrary"),
            vmem_limit_bytes=VMEM_LIMIT_BYTES),
        name="diff_attention",
    )(slopes, vec(lq1), vec(lk1), vec(lq2), vec(lk2), vec(head_g), qkv,
      *([qkv] * (2 * nq)), *jobs.views)
    return outs[0], list(outs[1:])


def _discretize(lam_re, lam_im, log_dt):
    dt = jnp.exp(log_dt)
    mag = jnp.exp(lam_re * dt)
    a_re = mag * jnp.cos(lam_im * dt)
    a_im = mag * jnp.sin(lam_im * dt)
    den = lam_re * lam_re + lam_im * lam_im
    nr = a_re - 1.0
    f_re = (nr * lam_re + a_im * lam_im) / den
    f_im = (a_im * lam_re - nr * lam_im) / den
    return a_re, a_im, f_re, f_im


def _cmul(ar, ai, br, bi):
    return ar * br - ai * bi, ar * bi + ai * br


def _split_bf16(x, parts):
    out = []
    for k in range(parts):
        piece = x.astype(BF16)
        out.append(piece)
        if k + 1 < parts:
            x = x - piece.astype(F32)
    return out


def _copy_dot(x, onehot, dims):
    return sum(lax.dot_general(piece, onehot, dims, preferred_element_type=F32)
               for piece in _split_bf16(x, 3))


def _dot_3pass(a_parts, b_parts, dims):
    a_hi, a_lo = a_parts
    b_hi, b_lo = b_parts
    dot = lambda x, y: lax.dot_general(x, y, dims, preferred_element_type=F32)
    return dot(a_hi, b_hi) + (dot(a_hi, b_lo) + dot(a_lo, b_hi))


def _int_power(a_re, a_im, e, nbits):
    shape = jnp.broadcast_shapes(a_re.shape, e.shape)
    p_re, p_im = jnp.ones(shape, F32), jnp.zeros(shape, F32)
    s_re, s_im = a_re, a_im
    for k in range(nbits):
        bit = (e & (1 << k)) != 0
        m_re, m_im = _cmul(p_re, p_im, s_re, s_im)
        p_re, p_im = jnp.where(bit, m_re, p_re), jnp.where(bit, m_im, p_im)
        if k + 1 < nbits:
            s_re, s_im = _cmul(s_re, s_im, s_re, s_im)
    return p_re, p_im


def _shift_lanes(x, s, left):
    if s == 0:
        return x
    n = x.shape[-1]
    lane = lax.broadcasted_iota(jnp.int32, x.shape, 1)
    if left:
        return jnp.where(lane < n - s, pltpu.roll(x, n - s, axis=1), 0.0)
    return jnp.where(lane >= s, pltpu.roll(x, s, axis=1), 0.0)


def _s5_prep_kernel(lre_ref, lim_ref, ldt_ref, btre_ref, btim_ref, cre_ref, cim_ref,
                    w1_ref, cm_ref, al_ref):
    for i in range(w1_ref.shape[0]):
        _s5_prep_group(i, lre_ref, lim_ref, ldt_ref, btre_ref, btim_ref, cre_ref, cim_ref,
                       w1_ref, cm_ref, al_ref)


def _s5_prep_group(i, lre_ref, lim_ref, ldt_ref, btre_r_ref, btim_r_ref, cre_ref, cim_ref,
                   w1_ref, cm_ref, al_ref):
    n, p, l = SSM_STATE, SSM_GROUP, CHUNK
    nbits = l.bit_length() - 1
    contract0 = (((0,), (0,)), ((), ()))
    contract_mm = (((1,), (0,)), ((), ()))

    a_re, a_im, f_re, f_im = _discretize(lre_ref[i], lim_ref[i], ldt_ref[i])
    bt_re, bt_im = btre_r_ref[i], btim_r_ref[i]
    bb_re, bb_im = _cmul(f_re, f_im, bt_re, bt_im)
    tau = lax.broadcasted_iota(jnp.int32, (l, 2 * n), 0)
    lane = lax.broadcasted_iota(jnp.int32, (l, 2 * n), 1)
    pw_re, pw_im = _int_power(a_re, a_im, jnp.where(lane < n, l - 1 - tau, tau), nbits)
    bm_re = pw_re[:, None, :] * bb_re[None, :, :] - pw_im[:, None, :] * bb_im[None, :, :]
    bm_im = pw_re[:, None, :] * bb_im[None, :, :] + pw_im[:, None, :] * bb_re[None, :, :]
    w1_ref[i, :, GROUP_LANES:GROUP_LANES + 2 * n] = bm_re.reshape(l * p, 2 * n).astype(BF16)
    w1_ref[i, :, GROUP_LANES + 2 * n:] = bm_im.reshape(l * p, 2 * n).astype(BF16)
    al_re, al_im = a_re, a_im
    for _ in range(nbits):
        al_re, al_im = _cmul(al_re, al_im, al_re, al_im)
    al_ref[i] = jnp.concatenate([al_re, al_im], axis=0)

    quant = jnp.concatenate([a_re, a_im, jnp.zeros((SUBLANES - 2, 2 * n), F32)], axis=0)
    quant_t = quant.T
    a_re_c, a_im_c = quant_t[:, 0:1], quant_t[:, 1:2]

    lanes_p = lax.broadcasted_iota(jnp.int32, (p, GROUP_LANES), 1)
    rows_p = lax.broadcasted_iota(jnp.int32, (p, GROUP_LANES), 0)
    tile_p = (lanes_p % p == rows_p).astype(BF16)
    tile_lag = ((l - 1) - lanes_p // p == rows_p).astype(BF16)
    ac_re, ac_im = _cmul(_copy_dot(pw_re, tile_lag, contract0), _copy_dot(pw_im, tile_lag, contract0),
                         _copy_dot(cre_ref[i], tile_p, contract0),
                         _copy_dot(cim_ref[i], tile_p, contract0))
    ac_re_parts, ac_im_parts = _split_bf16(ac_re, 2), _split_bf16(ac_im, 2)
    is_fwd = lax.broadcasted_iota(jnp.int32, bb_re.shape, 1) < n
    t_rows = None
    for d in range(2):
        in_dir = is_fwd if d == 0 else ~is_fwd
        taps = (_dot_3pass(_split_bf16(jnp.where(in_dir, bb_re, 0.0), 2), ac_re_parts, contract_mm)
                - _dot_3pass(_split_bf16(jnp.where(in_dir, bb_im, 0.0), 2), ac_im_parts,
                             contract_mm))
        rows = [_shift_lanes(taps, p * (tp if d == 0 else l - 1 - tp), left=(d == 1))
                for tp in range(l)]
        rows = jnp.concatenate(rows, axis=0)
        t_rows = rows if t_rows is None else t_rows + rows
    w1_ref[i, :, 0:GROUP_LANES] = t_rows.astype(BF16)
    r_re, r_im = _cmul(a_re_c, a_im_c, ac_re, ac_im)
    cm_ref[i] = jnp.concatenate([r_re, -r_im], axis=0).astype(BF16)


def _s5_prep(lam_re, lam_im, log_dt, b_re, b_im, c_re, c_im):
    _, g, n = lam_re.shape
    p = b_re.shape[-1]
    row = lambda a: jnp.concatenate([a[0], a[1]], axis=-1)[:, None, :]
    ldt_r = row(jnp.broadcast_to(log_dt[:, :, None], (2, g, n)))
    bt = lambda b: jnp.concatenate([jnp.swapaxes(b[0], 1, 2), jnp.swapaxes(b[1], 1, 2)], axis=-1)
    cr = lambda c: jnp.concatenate([c[0], c[1]], axis=-1)

    gb = PREP_GROUPS_PER_STEP
    assert g % gb == 0
    spec_r = lambda rows: pl.BlockSpec((gb, rows, 2 * n), lambda i: (i, 0, 0))
    return pl.pallas_call(
        _s5_prep_kernel,
        grid=(g // gb,),
        in_specs=[spec_r(1), spec_r(1), spec_r(1), spec_r(p), spec_r(p), spec_r(p), spec_r(p)],
        out_specs=[
            pl.BlockSpec((gb, GROUP_LANES, GROUP_LANES + 4 * n), lambda i: (i, 0, 0)),
            pl.BlockSpec((gb, 4 * n, GROUP_LANES), lambda i: (i, 0, 0)),
            pl.BlockSpec((gb, 2, 2 * n), lambda i: (i, 0, 0)),
        ],
        out_shape=[
            jax.ShapeDtypeStruct((g, GROUP_LANES, GROUP_LANES + 4 * n), BF16),
            jax.ShapeDtypeStruct((g, 4 * n, GROUP_LANES), BF16),
            jax.ShapeDtypeStruct((g, 2, 2 * n), F32),
        ],
        compiler_params=pltpu.CompilerParams(dimension_semantics=("parallel",)),
        name="s5_prep",
    )(row(lam_re), row(lam_im), ldt_r, bt(b_re), bt(b_im), cr(c_re), cr(c_im))


def _swap_pieces(v):
    piece = lax.broadcasted_iota(jnp.int32, v[0].shape, 1) // SSM_GROUP
    v = list(v)
    for d in (4, 2, 1):
        hi = (piece & d) != 0
        for a in range(8):
            if a & d:
                continue
            va, vb = v[a], v[a + d]
            v[a] = jnp.where(hi, pltpu.roll(vb, d * SSM_GROUP, axis=1), va)
            v[a + d] = jnp.where(hi, vb, pltpu.roll(va, LANES - d * SSM_GROUP, axis=1))
    return v


def _s5_chunk_kernel(u_ref, perm_ref, w1_ref, cm_ref, al_ref, x_ref,
                     a_ref, s_ref, hf_ref, h_ref, *, nb, nc):
    gl = GROUP_LANES
    gb = GROUPS_PER_BLOCK
    n2 = 2 * SSM_STATE
    half = CHUNK // 2
    ctile = SUBLANES * CHUNK

    def token_rows(ct, h, j):
        return pl.ds(ct * ctile + h * half + j, SUBLANES, stride=CHUNK)

    rows_all = nb * nc

    def gather_body(ct, carry):
        for b in range(nb):
            for h in range(2):
                for j in range(half):
                    a_ref[j, pl.ds(h * rows_all + ct * SUBLANES * nb + b, SUBLANES, stride=nb), :] = (
                        u_ref[b, token_rows(ct, h, j), :])
        return carry

    lax.fori_loop(0, nc // SUBLANES, gather_body, 0)
    lhs = jnp.concatenate([a_ref[j] for j in range(half)], axis=1).astype(BF16)
    xp = jnp.dot(lhs, perm_ref[...], preferred_element_type=F32).astype(BF16)

    for g in range(gb):
        x = jnp.concatenate([xp[:rows_all, g * LANES:(g + 1) * LANES],
                             xp[rows_all:, g * LANES:(g + 1) * LANES]], axis=1)
        r = jnp.dot(x, w1_ref[g], preferred_element_type=F32)
        x_ref[2 * g] = r[:, :LANES]
        x_ref[2 * g + 1] = r[:, LANES:gl]
        s_ref[:, g * gl:(g + 1) * gl] = r[:, gl:]

    a_re = [al_ref[g, 0:1, :] for g in range(gb)]
    a_im = [al_ref[g, 1:2, :] for g in range(gb)]
    is_fwd = lax.broadcasted_iota(jnp.int32, (SUBLANES, n2), 1) < SSM_STATE

    def step(h, s):
        out = []
        for g in range(gb):
            hr, hi = h[:, g * gl:g * gl + n2], h[:, g * gl + n2:(g + 1) * gl]
            sr, si = s[:, g * gl:g * gl + n2], s[:, g * gl + n2:(g + 1) * gl]
            out.append(a_re[g] * hr - a_im[g] * hi + sr)
            out.append(a_re[g] * hi + a_im[g] * hr + si)
        return jnp.concatenate(out, axis=1)

    cpt = SUBLANES // nb
    n_tiles = nc // cpt
    tile_rows = lambda t: pl.ds(pl.multiple_of(t * SUBLANES, SUBLANES), SUBLANES)

    def pick(fwd_rows, bwd_rows):
        return jnp.concatenate(
            [jnp.where(is_fwd[:fwd_rows.shape[0]], fwd_rows[:, k * n2:(k + 1) * n2],
                       bwd_rows[:, k * n2:(k + 1) * n2]) for k in range(2 * gb)], axis=1)

    def sweep_body(t, h, *, second_half):
        tb = n_tiles - 1 - t
        s_f = s_ref[tile_rows(t), :]
        s_b = s_ref[tile_rows(tb), :]
        before = []
        for k in range(cpt):
            kb = cpt - 1 - k
            before.append(h)
            h = step(h, pick(s_f[k * nb:(k + 1) * nb], s_b[kb * nb:(kb + 1) * nb]))
        fwd_valid = jnp.concatenate(before, axis=0)
        bwd_valid = jnp.concatenate(before[::-1], axis=0)
        if second_half:
            h_ref[tile_rows(t), :] = pick(fwd_valid, h_ref[tile_rows(t), :])
            h_ref[tile_rows(tb), :] = pick(hf_ref[tile_rows(tb), :], bwd_valid)
        else:
            hf_ref[tile_rows(t), :] = fwd_valid
            h_ref[tile_rows(tb), :] = bwd_valid
        return h

    assert n_tiles % 2 == 0
    h_mid = lax.fori_loop(0, n_tiles // 2, functools.partial(sweep_body, second_half=False),
                          jnp.zeros((nb, gb * gl), F32))
    lax.fori_loop(n_tiles // 2, n_tiles, functools.partial(sweep_body, second_half=True), h_mid)

    for g in range(gb):
        r = jnp.dot(h_ref[:, g * gl:(g + 1) * gl].astype(BF16), cm_ref[g],
                    preferred_element_type=F32)
        x_ref[2 * g] += r[:, :LANES]
        x_ref[2 * g + 1] += r[:, LANES:]


def _s5_chunks(u, w1, cm, al):
    nb, seq, width = u.shape
    nc = seq // CHUNK
    rows = nb * nc
    gb = GROUPS_PER_BLOCK
    assert SUBLANES % nb == 0 and nc % SUBLANES == 0 and width % LANES == 0
    blk = pl.BlockSpec((nb, seq, LANES), lambda i: (0, 0, i))
    half = CHUNK // 2
    src = np.arange(half * LANES)
    j, g, p = src // LANES, (src % LANES) // SSM_GROUP, src % SSM_GROUP
    perm = np.zeros((half * LANES, half * LANES), np.float32)
    perm[src, g * LANES + j * SSM_GROUP + p] = 1.0
    return pl.pallas_call(
        functools.partial(_s5_chunk_kernel, nb=nb, nc=nc),
        grid=(width // LANES,),
        in_specs=[
            blk,
            pl.BlockSpec(perm.shape, lambda i: (0, 0), pipeline_mode=pl.Buffered(1)),
            pl.BlockSpec((gb,) + w1.shape[1:], lambda i: (i, 0, 0)),
            pl.BlockSpec((gb,) + cm.shape[1:], lambda i: (i, 0, 0)),
            pl.BlockSpec((gb,) + al.shape[1:], lambda i: (i, 0, 0)),
        ],
        out_specs=pl.BlockSpec((2 * gb, rows, LANES), lambda i: (i, 0, 0)),
        out_shape=jax.ShapeDtypeStruct((2 * width // SSM_GROUP, rows, LANES), F32),
        scratch_shapes=[pltpu.VMEM((half, 2 * rows, LANES), F32)]
        + [pltpu.VMEM((rows, gb * GROUP_LANES), F32)] * 3,
        compiler_params=pltpu.CompilerParams(
            dimension_semantics=("parallel",), vmem_limit_bytes=VMEM_LIMIT_BYTES),
        name="s5_chunks",
    )(u, jnp.asarray(perm, BF16), w1, cm, al)


def _mix_out_kernel(x_ref, a_ref, ys_ref, u_ref, d_ref, wglu_ref, bglu_ref, og_ref, wo_a_ref,
                    wo_s_ref, pg_ref, o_ref, yn_ref):
    nb, tt, d = x_ref.shape
    gb = GROUPS_PER_BLOCK
    half = CHUNK // 2
    rows = nb * tt
    for o in range(yn_ref.shape[0]):
        for b in range(nb):
            for h in range(2):
                w = [ys_ref[(o * gb + g) * 2 + h, pl.ds(b, SUBLANES, stride=nb), :]
                     for g in range(gb)]
                v = _swap_pieces(w)
                for j in range(half):
                    yn_ref[o, pl.ds(b * tt + h * half + j, SUBLANES, stride=CHUNK), :] = v[j]
    y = jnp.concatenate([yn_ref[o] for o in range(yn_ref.shape[0])], axis=1)
    y = y + d_ref[...] * u_ref[...].reshape(rows, -1)
    g = y * (0.5 * (1.0 + jnp.tanh(math.sqrt(2.0 / math.pi) * (y + 0.044715 * (y * y * y)))))
    z = jnp.dot(g.astype(BF16), wglu_ref[...], preferred_element_type=F32) + bglu_ref[...]
    s = g * _sigmoid(z)
    s = s * _rms_scale(s) * og_ref[...]
    mixed = (jnp.dot(a_ref[...].reshape(rows, -1), wo_a_ref[...], preferred_element_type=F32)
             + jnp.dot(s.astype(BF16), wo_s_ref[...], preferred_element_type=F32))
    out = x_ref[...].reshape(rows, d) + mixed * _rms_scale(mixed) * pg_ref[...]
    o_ref[...] = out.reshape(nb, tt, d)


def _mix_out(x, a, ys, u, skip_d, w_glu, b_glu, out_g, w_out, post_g):
    nb, seq, d = x.shape
    wa = a.shape[2]
    ws = u.shape[2]
    tt = SUBLANES * CHUNK
    assert seq % tt == 0 and ys.shape == (2 * ws // SSM_GROUP, nb * seq // CHUNK, LANES)
    const = lambda shape: pl.BlockSpec(shape, lambda i: (0, 0))
    tile = lambda width: pl.BlockSpec((nb, tt, width), lambda i: (0, i, 0))
    return pl.pallas_call(
        _mix_out_kernel,
        grid=(seq // tt,),
        in_specs=[
            tile(d), tile(wa),
            pl.BlockSpec((ys.shape[0], SUBLANES * nb, LANES), lambda i: (0, i, 0)),
            tile(ws), const((1, ws)),
            const((ws, ws)), const((1, ws)), const((1, ws)),
            pl.BlockSpec((wa, d), lambda i: (0, 0)),
            pl.BlockSpec((ws, d), lambda i: (wa // ws, 0)),
            const((1, d)),
        ],
        out_specs=tile(d),
        out_shape=jax.ShapeDtypeStruct(x.shape, F32),
        scratch_shapes=[pltpu.VMEM((ws // LANES, nb * tt, LANES), F32)],
        compiler_params=pltpu.CompilerParams(
            dimension_semantics=("parallel",), vmem_limit_bytes=VMEM_LIMIT_BYTES),
        name="mix_out",
    )(x, a, ys, u, skip_d.reshape(1, ws), w_glu, b_glu.reshape(1, ws), out_g.reshape(1, ws),
      w_out, w_out, post_g.reshape(1, d))


def kernel(x, ff1_pre_g, ff1_w_gate, ff1_w_up, ff1_w_down, ff1_post_g, mix_pre_g, w_in, lam_q1, lam_k1, lam_q2, lam_k2, attn_head_g, ssm_lam_re, ssm_lam_im, ssm_log_dt, ssm_b_re, ssm_b_im, ssm_c_re, ssm_c_im, ssm_d, ssm_w_glu, ssm_b_glu, ssm_out_g, w_out, mix_post_g, ff2_pre_g, ff2_w_gate, ff2_w_up, ff2_w_down, ff2_post_g):
    batch, seq, d_model = x.shape
    depth = w_in.shape[0]
    ssm_width = ssm_w_glu.shape[-1]
    slopes = jnp.asarray([2.0 ** (-8.0 * (i + 1) / ATTN_HEADS) for i in range(ATTN_HEADS)], F32)
    bf = lambda w: w.astype(BF16)

    xt = x.reshape(batch * seq, d_model)
    for l in range(depth):
        row = lambda p: p[l:l + 1]
        ff1 = (row(ff1_pre_g), ff1_w_gate[l], ff1_w_up[l], ff1_w_down[l], row(ff1_post_g))
        head, ff1_w, _ = _ffn(xt, *ff1, tm=FFN_HEAD_ROWS, tf=FFN_HEAD_COLS, single_buffer_x=True,
                              tiles=(0, 1), emit_weights=True)
        rest = (FFN_HEAD_ROWS // FFN_TILE_ROWS, (batch * seq - FFN_HEAD_ROWS) // FFN_TILE_ROWS)
        xt, _, (w_in_bf, w_glu_bf, w_out_bf) = _ffn(
            xt, ff1[0], *ff1_w, ff1[4], tiles=rest, into=head,
            cast=(w_in[l], ssm_w_glu[l], w_out[l]))

        qkv, u = _in_proj(xt, row(mix_pre_g), w_in_bf, ssm_width=ssm_width)
        lam_init = 0.8 - 0.6 * math.exp(-0.3 * l)
        a, ff2_w = _attention(qkv, slopes, row(lam_q1), row(lam_k1), row(lam_q2), row(lam_k2),
                              row(attn_head_g), batch=batch, seq=seq, lam_init=lam_init,
                              cast=(ff2_w_gate[l], ff2_w_up[l], ff2_w_down[l]))

        w1, cm, al = _s5_prep(ssm_lam_re[l], ssm_lam_im[l], ssm_log_dt[l], ssm_b_re[l],
                              ssm_b_im[l], ssm_c_re[l], ssm_c_im[l])
        u3 = u.reshape(batch, seq, ssm_width)
        ys = _s5_chunks(u3, w1, cm, al)

        xt = _mix_out(xt.reshape(batch, seq, d_model), a.reshape(batch, seq, -1), ys, u3, ssm_d[l],
                      w_glu_bf, row(ssm_b_glu), row(ssm_out_g), w_out_bf, row(mix_post_g))
        xt = xt.reshape(batch * seq, d_model)

        xt, _, _ = _ffn(xt, row(ff2_pre_g), *ff2_w, row(ff2_post_g))
    return xt.reshape(batch, seq, d_model)
```

```python
import functools
import math

import jax
import jax.numpy as jnp
import numpy as np
from jax import lax
from jax.experimental import pallas as pl
from jax.experimental.pallas import tpu as pltpu

F32 = jnp.float32
BF16 = jnp.bfloat16

NORM_EPS = 1e-6
ATTN_HEADS = 8
ATTN_HEAD_DIM = 64
ATTN_VALUE_DIM = 2 * ATTN_HEAD_DIM
POS_SPLIT = 16
SSM_GROUP = 16
SSM_STATE = 64
CHUNK = 16
GROUP_LANES = CHUNK * SSM_GROUP
SUBLANES = 8
LANES = 128
BF16_ROWS = 16
GROUPS_PER_BLOCK = LANES // SSM_GROUP
PREP_GROUPS_PER_STEP = 16
FFN_HEAD_ROWS = 1024
FFN_TILE_ROWS = 512
FFN_TILE_COLS = 512
FFN_HEAD_COLS = 256
BF16_EXACT_INTS = 256

V7X_VMEM_BYTES = 64 * 1024 * 1024
VMEM_LIMIT_BYTES = V7X_VMEM_BYTES - 8 * 1024 * 1024


def _rms_scale(x):
    return lax.rsqrt(jnp.mean(x * x, axis=-1, keepdims=True) + NORM_EPS)


def _sigmoid(x):
    return 1.0 / (1.0 + jnp.exp(-x))


class _CastJobs:
    def __init__(self, arrays, n_steps, flat_index):
        self.flat_index = flat_index
        self.views, self.slabs = list(arrays), []
        for a in arrays:
            rows, cols = a.shape
            assert rows % BF16_ROWS == 0
            n = max(k for k in range(1, n_steps + 1) if (rows // BF16_ROWS) % k == 0)
            self.slabs.append((rows // n, cols, n))

    def _spec(self, slab):
        r, c, n = slab
        return pl.BlockSpec((r, c), lambda *g: (jnp.minimum(self.flat_index(*g), n - 1), 0))

    @property
    def in_specs(self):
        return [self._spec(s) for s in self.slabs]

    out_specs = in_specs

    @property
    def out_shapes(self):
        return [jax.ShapeDtypeStruct(v.shape, BF16) for v in self.views]

    def __len__(self):
        return len(self.views)


def _run_cast_jobs(in_refs, out_refs):
    for i_ref, o_ref in zip(in_refs, out_refs):
        o_ref[...] = i_ref[...].astype(BF16)


def _ffn_kernel(*refs, n_jobs, has_into, emit_weights):
    x_ref, pre_g_ref, wg_ref, wu_ref, wd_ref, post_g_ref = refs[:6]
    n_in = 6 + int(has_into)
    job_in = refs[n_in:n_in + n_jobs]
    o_ref = refs[n_in + n_jobs]
    n_out = n_in + n_jobs + 1
    w_out = refs[n_out:n_out + 3] if emit_weights else ()
    n_out += len(w_out)
    job_out = refs[n_out:n_out + n_jobs]
    h_ref = refs[n_out + n_jobs]
    j = pl.program_id(1)
    last = pl.num_programs(1) - 1

    def step(first, final):
        if first:
            x = x_ref[...]
            h_ref[...] = (x * _rms_scale(x) * pre_g_ref[...]).astype(BF16)
        _run_cast_jobs(job_in, job_out)
        wg, wu, wd = (r[...].astype(BF16) for r in (wg_ref, wu_ref, wd_ref))
        for w_ref, w in zip(w_out, (wg, wu, wd)):
            w_ref[...] = w
        h = h_ref[...]
        gate = jnp.dot(h, wg, preferred_element_type=F32)
        up = jnp.dot(h, wu, preferred_element_type=F32)
        act = (gate * _sigmoid(gate) * up).astype(BF16)
        part = jnp.dot(act, wd, preferred_element_type=F32)
        if first:
            o_ref[...] = part
        elif final:
            acc = o_ref[...] + part
            o_ref[...] = x_ref[...] + 0.5 * (acc * _rms_scale(acc) * post_g_ref[...])
        else:
            o_ref[...] += part

    pl.when(j == 0)(lambda: step(True, False))
    pl.when((j > 0) & (j < last))(lambda: step(False, False))
    pl.when(j == last)(lambda: step(False, True))


def _ffn(x, pre_g, w_gate, w_up, w_down, post_g, *, cast=(), tm=FFN_TILE_ROWS, tf=FFN_TILE_COLS,
         single_buffer_x=False, tiles=None, into=None, emit_weights=False):
    t, d = x.shape
    f = w_gate.shape[1]
    assert t % tm == 0 and f % tf == 0
    first, nt = tiles if tiles is not None else (0, t // tm)
    assert not emit_weights or nt == 1
    nj = f // tf
    assert nj >= 2
    jobs = _CastJobs(cast, nt * nj, lambda i, j: i * nj + j)
    x_mode = dict(pipeline_mode=pl.Buffered(1)) if single_buffer_x else {}
    w_cols = pl.BlockSpec((d, tf), lambda i, j: (0, j))
    w_rows = pl.BlockSpec((tf, d), lambda i, j: (j, 0))
    into_spec = [pl.BlockSpec(memory_space=pl.ANY)] if into is not None else []
    w_specs = [w_cols, w_cols, w_rows] if emit_weights else []
    w_shapes = [jax.ShapeDtypeStruct(w.shape, BF16) for w in (w_gate, w_up, w_down)]
    outs = pl.pallas_call(
        functools.partial(_ffn_kernel, n_jobs=len(jobs), has_into=into is not None,
                          emit_weights=emit_weights),
        grid=(nt, nj),
        in_specs=[
            pl.BlockSpec((tm, d), lambda i, j: (first + i, 0), **x_mode),
            pl.BlockSpec((1, d), lambda i, j: (0, 0)),
            w_cols, w_cols, w_rows,
            pl.BlockSpec((1, d), lambda i, j: (0, 0)),
        ] + into_spec + jobs.in_specs,
        out_specs=[pl.BlockSpec((tm, d), lambda i, j: (first + i, 0))] + w_specs + jobs.out_specs,
        out_shape=[jax.ShapeDtypeStruct((t, d), F32)] + w_shapes[:len(w_specs)] + jobs.out_shapes,
        input_output_aliases={6: 0} if into is not None else {},
        scratch_shapes=[pltpu.VMEM((tm, d), BF16)],
        compiler_params=pltpu.CompilerParams(
            dimension_semantics=("arbitrary", "arbitrary"),
            vmem_limit_bytes=VMEM_LIMIT_BYTES),
        name="ffn",
    )(x, pre_g.reshape(1, d), w_gate, w_up, w_down, post_g.reshape(1, d),
      *([into] if into is not None else []), *jobs.views)
    n_w = len(w_specs)
    return outs[0], list(outs[1:1 + n_w]), list(outs[1 + n_w:])


def _in_proj_kernel(x_ref, g_ref, w_ref, qkv_ref, u_ref, h_ref, *, n_qkv):
    j = pl.program_id(1)

    def project(first):
        if first:
            x = x_ref[...]
            h_ref[...] = (x * _rms_scale(x) * g_ref[...]).astype(BF16)
        return jnp.dot(h_ref[...], w_ref[...].astype(BF16), preferred_element_type=F32)

    @pl.when(j == 0)
    def _():
        qkv_ref[...] = project(True).astype(BF16)

    @pl.when((j > 0) & (j < n_qkv))
    def _():
        qkv_ref[...] = project(False).astype(BF16)

    @pl.when(j == n_qkv)
    def _():
        u_ref[...] = project(False)


def _in_proj(x, g, w_in, *, ssm_width, tm=1024):
    t, d = x.shape
    n = w_in.shape[1]
    tn = ssm_width
    n_qkv = (n - ssm_width) // tn
    assert t % tm == 0 and n == (n_qkv + 1) * tn
    return pl.pallas_call(
        functools.partial(_in_proj_kernel, n_qkv=n_qkv),
        grid=(t // tm, n_qkv + 1),
        in_specs=[
            pl.BlockSpec((tm, d), lambda i, j: (i, 0)),
            pl.BlockSpec((1, d), lambda i, j: (0, 0)),
            pl.BlockSpec((d, tn), lambda i, j: (0, j)),
        ],
        out_specs=[
            pl.BlockSpec((tm, tn), lambda i, j: (i, jnp.minimum(j, n_qkv - 1))),
            pl.BlockSpec((tm, tn), lambda i, j: (i, 0)),
        ],
        out_shape=[
            jax.ShapeDtypeStruct((t, n - ssm_width), BF16),
            jax.ShapeDtypeStruct((t, ssm_width), F32),
        ],
        scratch_shapes=[pltpu.VMEM((tm, d), BF16)],
        compiler_params=pltpu.CompilerParams(
            dimension_semantics=("parallel", "arbitrary"),
            vmem_limit_bytes=VMEM_LIMIT_BYTES),
        name="in_proj",
    )(x, g.reshape(1, d), w_in)


def _attn_kernel(*refs, tq, nq, lam_init, n_jobs):
    slopes_ref, lq1_ref, lk1_ref, lq2_ref, lk2_ref, hg_ref, q_ref = refs[:7]
    k_refs = refs[7:7 + nq]
    v_refs = refs[7 + nq:7 + 2 * nq]
    n_in = 7 + 2 * nq
    job_in = refs[n_in:n_in + n_jobs]
    o_ref = refs[n_in + n_jobs]
    job_out = refs[n_in + n_jobs + 1:n_in + 2 * n_jobs + 1]
    qf_ref, kf_ref, bias_ref, s_ref, p_ref = refs[n_in + 2 * n_jobs + 1:]
    h = pl.program_id(0)
    qi = pl.program_id(1)
    b = pl.program_id(2)
    e = q_ref.shape[1]

    @pl.when(b == 0)
    def _():
        slope = slopes_ref[h]
        lane = lax.broadcasted_iota(jnp.int32, (tq, e), 1)
        row = lax.broadcasted_iota(jnp.int32, (tq, e), 0)
        row_hi = (slope * POS_SPLIT) * (row // POS_SPLIT).astype(F32)
        row_lo = slope * (row % POS_SPLIT).astype(F32)
        q_base = jnp.where(lane == 0, row_hi, jnp.where(lane == 1, row_lo,
                           jnp.where(lane < 4, 1.0, 0.0)))
        k_base = jnp.where(lane < 2, -1.0, jnp.where(lane == 2, row_hi,
                           jnp.where(lane == 3, row_lo, 0.0)))
        hi_q = (lane == 0).astype(F32)
        hi_k = (lane == 2).astype(F32)
        qf_ref[...] = (q_base + (slope * tq) * qi.astype(F32) * hi_q).astype(BF16)
        for slot in range(nq):
            chunk = lax.rem(qi + slot, nq)
            sign = jnp.where(chunk < qi, 1.0, jnp.where(chunk > qi, -1.0, 0.0))
            kf = sign * k_base + (sign * (slope * tq) * chunk.astype(F32)) * hi_k
            kf_ref[slot * tq:(slot + 1) * tq, :] = kf.astype(BF16)
        col = lax.broadcasted_iota(jnp.int32, (tq, tq), 1)
        bias_ref[...] = slope * jnp.abs(lax.broadcasted_iota(jnp.int32, (tq, tq), 0)
                                        - col).astype(F32)

    _run_cast_jobs(job_in, job_out)
    lam = (jnp.exp(jnp.sum(lq1_ref[...] * lk1_ref[...], axis=-1, keepdims=True))
           - jnp.exp(jnp.sum(lq2_ref[...] * lk2_ref[...], axis=-1, keepdims=True))
           + lam_init)

    q = q_ref[...] * jnp.asarray(ATTN_HEAD_DIM ** -0.5, BF16)
    k_aug = jnp.concatenate([jnp.concatenate([r[...] for r in k_refs], axis=0), kf_ref[...]],
                            axis=1)
    v = jnp.concatenate([r[...] for r in v_refs], axis=0)
    v_ext = jnp.concatenate([v, jnp.ones_like(v)], axis=1)
    lane = lax.broadcasted_iota(jnp.int32, q.shape, 1)
    nt = (((1,), (1,)), ((), ()))
    for c in range(2):
        in_map = (lane >= c * ATTN_HEAD_DIM) & (lane < (c + 1) * ATTN_HEAD_DIM)
        q_aug = jnp.concatenate([jnp.where(in_map, q, jnp.zeros_like(q)), qf_ref[...]], axis=1)
        s_ref[c, :, :tq] = lax.dot_general(q_aug, k_aug[:tq], nt,
                                           preferred_element_type=F32) - bias_ref[...]
        s_ref[c, :, tq:] = lax.dot_general(q_aug, k_aug[tq:], nt, preferred_element_type=F32)
        s = s_ref[c]
        p_ref[c] = jnp.exp(s - jnp.max(s, axis=-1, keepdims=True)).astype(BF16)
    outs = []
    for c in range(2):
        pv = jnp.dot(p_ref[c], v_ext, preferred_element_type=F32)
        outs.append(pv[:, :e] / pv[:, e:])
    o = outs[0] - lam * outs[1]
    o_ref[...] = (o * _rms_scale(o) * hg_ref[...] * (1.0 - lam_init)).astype(o_ref.dtype)


def _attention(qkv, slopes, lq1, lk1, lq2, lk2, head_g, *, batch, seq, lam_init, cast=(), tq=1024):
    t = qkv.shape[0]
    e = ATTN_VALUE_DIM
    nq = seq // tq
    nh = ATTN_HEADS
    assert 8 % nh == 0 and seq // POS_SPLIT <= BF16_EXACT_INTS
    assert tq % POS_SPLIT == 0 and seq % tq == 0 and nq >= 2
    vec = lambda a: a.reshape(1, -1).astype(F32)
    small = lambda n: pl.BlockSpec((1, n), lambda h, qi, b: (0, 0))
    jobs = _CastJobs(cast, nh * nq * batch, lambda h, qi, b: (h * nq + qi) * batch + b)

    def chunk_spec(slot, col0):
        return pl.BlockSpec((tq, e), lambda h, qi, b: (b * nq + lax.rem(qi + slot, nq), col0 + h))

    outs = pl.pallas_call(
        functools.partial(_attn_kernel, tq=tq, nq=nq, lam_init=lam_init, n_jobs=len(jobs)),
        grid=(nh, nq, batch),
        in_specs=[
            pl.BlockSpec(memory_space=pltpu.SMEM),
            small(ATTN_HEAD_DIM), small(ATTN_HEAD_DIM), small(ATTN_HEAD_DIM), small(ATTN_HEAD_DIM),
            small(e),
            pl.BlockSpec((tq, e), lambda h, qi, b: (b * nq + qi, h)),
        ] + [chunk_spec(s, nh) for s in range(nq)] + [chunk_spec(s, 2 * nh) for s in range(nq)]
        + jobs.in_specs,
        out_specs=[pl.BlockSpec((tq, e), lambda h, qi, b: (b * nq + qi, h))] + jobs.out_specs,
        out_shape=[jax.ShapeDtypeStruct((t, nh * e), BF16)] + jobs.out_shapes,
        scratch_shapes=[pltpu.VMEM((tq, e), BF16), pltpu.VMEM((seq, e), BF16),
                        pltpu.VMEM((tq, tq), F32), pltpu.VMEM((2, tq, seq), F32),
                        pltpu.VMEM((2, tq, seq), BF16)],
        compiler_params=pltpu.CompilerParams(
            dimension_semantics=("arbitrary", "arbitrary", "arbitrary"),
            vmem_limit_bytes=VMEM_LIMIT_BYTES),
        name="diff_attention",
    )(slopes, vec(lq1), vec(lk1), vec(lq2), vec(lk2), vec(head_g), qkv,
      *([qkv] * (2 * nq)), *jobs.views)
    return outs[0], list(outs[1:])


def _discretize(lam_re, lam_im, log_dt):
    dt = jnp.exp(log_dt)
    mag = jnp.exp(lam_re * dt)
    a_re = mag * jnp.cos(lam_im * dt)
    a_im = mag * jnp.sin(lam_im * dt)
    den = lam_re * lam_re + lam_im * lam_im
    nr = a_re - 1.0
    f_re = (nr * lam_re + a_im * lam_im) / den
    f_im = (a_im * lam_re - nr * lam_im) / den
    return a_re, a_im, f_re, f_im


def _cmul(ar, ai, br, bi):
    return ar * br - ai * bi, ar * bi + ai * br


def _split_bf16(x, parts):
    out = []
    for k in range(parts):
        piece = x.astype(BF16)
        out.append(piece)
        if k + 1 < parts:
            x = x - piece.astype(F32)
    return out


def _copy_dot(x, onehot, dims):
    return sum(lax.dot_general(piece, onehot, dims, preferred_element_type=F32)
               for piece in _split_bf16(x, 3))


def _dot_3pass(a_parts, b_parts, dims):
    a_hi, a_lo = a_parts
    b_hi, b_lo = b_parts
    dot = lambda x, y: lax.dot_general(x, y, dims, preferred_element_type=F32)
    return dot(a_hi, b_hi) + (dot(a_hi, b_lo) + dot(a_lo, b_hi))


def _int_power(a_re, a_im, e, nbits):
    shape = jnp.broadcast_shapes(a_re.shape, e.shape)
    p_re, p_im = jnp.ones(shape, F32), jnp.zeros(shape, F32)
    s_re, s_im = a_re, a_im
    for k in range(nbits):
        bit = (e & (1 << k)) != 0
        m_re, m_im = _cmul(p_re, p_im, s_re, s_im)
        p_re, p_im = jnp.where(bit, m_re, p_re), jnp.where(bit, m_im, p_im)
        if k + 1 < nbits:
            s_re, s_im = _cmul(s_re, s_im, s_re, s_im)
    return p_re, p_im


def _shift_lanes(x, s, left):
    if s == 0:
        return x
    n = x.shape[-1]
    lane = lax.broadcasted_iota(jnp.int32, x.shape, 1)
    if left:
        return jnp.where(lane < n - s, pltpu.roll(x, n - s, axis=1), 0.0)
    return jnp.where(lane >= s, pltpu.roll(x, s, axis=1), 0.0)


def _s5_prep_kernel(lre_ref, lim_ref, ldt_ref, btre_ref, btim_ref, cre_ref, cim_ref,
                    w1_ref, cm_ref, al_ref):
    for i in range(w1_ref.shape[0]):
        _s5_prep_group(i, lre_ref, lim_ref, ldt_ref, btre_ref, btim_ref, cre_ref, cim_ref,
                       w1_ref, cm_ref, al_ref)


def _s5_prep_group(i, lre_ref, lim_ref, ldt_ref, btre_r_ref, btim_r_ref, cre_ref, cim_ref,
                   w1_ref, cm_ref, al_ref):
    n, p, l = SSM_STATE, SSM_GROUP, CHUNK
    nbits = l.bit_length() - 1
    contract0 = (((0,), (0,)), ((), ()))
    contract_mm = (((1,), (0,)), ((), ()))

    a_re, a_im, f_re, f_im = _discretize(lre_ref[i], lim_ref[i], ldt_ref[i])
    bt_re, bt_im = btre_r_ref[i], btim_r_ref[i]
    bb_re, bb_im = _cmul(f_re, f_im, bt_re, bt_im)
    tau = lax.broadcasted_iota(jnp.int32, (l, 2 * n), 0)
    lane = lax.broadcasted_iota(jnp.int32, (l, 2 * n), 1)
    pw_re, pw_im = _int_power(a_re, a_im, jnp.where(lane < n, l - 1 - tau, tau), nbits)
    bm_re = pw_re[:, None, :] * bb_re[None, :, :] - pw_im[:, None, :] * bb_im[None, :, :]
    bm_im = pw_re[:, None, :] * bb_im[None, :, :] + pw_im[:, None, :] * bb_re[None, :, :]
    w1_ref[i, :, GROUP_LANES:GROUP_LANES + 2 * n] = bm_re.reshape(l * p, 2 * n).astype(BF16)
    w1_ref[i, :, GROUP_LANES + 2 * n:] = bm_im.reshape(l * p, 2 * n).astype(BF16)
    al_re, al_im = a_re, a_im
    for _ in range(nbits):
        al_re, al_im = _cmul(al_re, al_im, al_re, al_im)
    al_ref[i] = jnp.concatenate([al_re, al_im], axis=0)

    quant = jnp.concatenate([a_re, a_im, jnp.zeros((SUBLANES - 2, 2 * n), F32)], axis=0)
    quant_t = quant.T
    a_re_c, a_im_c = quant_t[:, 0:1], quant_t[:, 1:2]

    lanes_p = lax.broadcasted_iota(jnp.int32, (p, GROUP_LANES), 1)
    rows_p = lax.broadcasted_iota(jnp.int32, (p, GROUP_LANES), 0)
    tile_p = (lanes_p % p == rows_p).astype(BF16)
    tile_lag = ((l - 1) - lanes_p // p == rows_p).astype(BF16)
    ac_re, ac_im = _cmul(_copy_dot(pw_re, tile_lag, contract0), _copy_dot(pw_im, tile_lag, contract0),
                         _copy_dot(cre_ref[i], tile_p, contract0),
                         _copy_dot(cim_ref[i], tile_p, contract0))
    ac_re_parts, ac_im_parts = _split_bf16(ac_re, 2), _split_bf16(ac_im, 2)
    is_fwd = lax.broadcasted_iota(jnp.int32, bb_re.shape, 1) < n
    t_rows = None
    for d in range(2):
        in_dir = is_fwd if d == 0 else ~is_fwd
        taps = (_dot_3pass(_split_bf16(jnp.where(in_dir, bb_re, 0.0), 2), ac_re_parts, contract_mm)
                - _dot_3pass(_split_bf16(jnp.where(in_dir, bb_im, 0.0), 2), ac_im_parts,
                             contract_mm))
        rows = [_shift_lanes(taps, p * (tp if d == 0 else l - 1 - tp), left=(d == 1))
                for tp in range(l)]
        rows = jnp.concatenate(rows, axis=0)
        t_rows = rows if t_rows is None else t_rows + rows
    w1_ref[i, :, 0:GROUP_LANES] = t_rows.astype(BF16)
    r_re, r_im = _cmul(a_re_c, a_im_c, ac_re, ac_im)
    cm_ref[i] = jnp.concatenate([r_re, -r_im], axis=0).astype(BF16)


def _s5_prep(lam_re, lam_im, log_dt, b_re, b_im, c_re, c_im):
    _, g, n = lam_re.shape
    p = b_re.shape[-1]
    row = lambda a: jnp.concatenate([a[0], a[1]], axis=-1)[:, None, :]
    ldt_r = row(jnp.broadcast_to(log_dt[:, :, None], (2, g, n)))
    bt = lambda b: jnp.concatenate([jnp.swapaxes(b[0], 1, 2), jnp.swapaxes(b[1], 1, 2)], axis=-1)
    cr = lambda c: jnp.concatenate([c[0], c[1]], axis=-1)

    gb = PREP_GROUPS_PER_STEP
    assert g % gb == 0
    spec_r = lambda rows: pl.BlockSpec((gb, rows, 2 * n), lambda i: (i, 0, 0))
    return pl.pallas_call(
        _s5_prep_kernel,
        grid=(g // gb,),
        in_specs=[spec_r(1), spec_r(1), spec_r(1), spec_r(p), spec_r(p), spec_r(p), spec_r(p)],
        out_specs=[
            pl.BlockSpec((gb, GROUP_LANES, GROUP_LANES + 4 * n), lambda i: (i, 0, 0)),
            pl.BlockSpec((gb, 4 * n, GROUP_LANES), lambda i: (i, 0, 0)),
            pl.BlockSpec((gb, 2, 2 * n), lambda i: (i, 0, 0)),
        ],
        out_shape=[
            jax.ShapeDtypeStruct((g, GROUP_LANES, GROUP_LANES + 4 * n), BF16),
            jax.ShapeDtypeStruct((g, 4 * n, GROUP_LANES), BF16),
            jax.ShapeDtypeStruct((g, 2, 2 * n), F32),
        ],
        compiler_params=pltpu.CompilerParams(dimension_semantics=("parallel",)),
        name="s5_prep",
    )(row(lam_re), row(lam_im), ldt_r, bt(b_re), bt(b_im), cr(c_re), cr(c_im))


def _swap_pieces(v):
    piece = lax.broadcasted_iota(jnp.int32, v[0].shape, 1) // SSM_GROUP
    v = list(v)
    for d in (4, 2, 1):
        hi = (piece & d) != 0
        for a in range(8):
            if a & d:
                continue
            va, vb = v[a], v[a + d]
            v[a] = jnp.where(hi, pltpu.roll(vb, d * SSM_GROUP, axis=1), va)
            v[a + d] = jnp.where(hi, vb, pltpu.roll(va, LANES - d * SSM_GROUP, axis=1))
    return v


def _s5_chunk_kernel(u_ref, perm_ref, w1_ref, cm_ref, al_ref, x_ref,
                     a_ref, s_ref, hf_ref, h_ref, *, nb, nc):
    gl = GROUP_LANES
    gb = GROUPS_PER_BLOCK
    n2 = 2 * SSM_STATE
    half = CHUNK // 2
    ctile = SUBLANES * CHUNK

    def token_rows(ct, h, j):
        return pl.ds(ct * ctile + h * half + j, SUBLANES, stride=CHUNK)

    rows_all = nb * nc

    def gather_body(ct, carry):
        for b in range(nb):
            for h in range(2):
                for j in range(half):
                    a_ref[j, pl.ds(h * rows_all + ct * SUBLANES * nb + b, SUBLANES, stride=nb), :] = (
                        u_ref[b, token_rows(ct, h, j), :])
        return carry

    lax.fori_loop(0, nc // SUBLANES, gather_body, 0)
    lhs = jnp.concatenate([a_ref[j] for j in range(half)], axis=1).astype(BF16)
    xp = jnp.dot(lhs, perm_ref[...], preferred_element_type=F32).astype(BF16)

    for g in range(gb):
        x = jnp.concatenate([xp[:rows_all, g * LANES:(g + 1) * LANES],
                             xp[rows_all:, g * LANES:(g + 1) * LANES]], axis=1)
        r = jnp.dot(x, w1_ref[g], preferred_element_type=F32)
        x_ref[2 * g] = r[:, :LANES]
        x_ref[2 * g + 1] = r[:, LANES:gl]
        s_ref[:, g * gl:(g + 1) * gl] = r[:, gl:]

    a_re = [al_ref[g, 0:1, :] for g in range(gb)]
    a_im = [al_ref[g, 1:2, :] for g in range(gb)]
    is_fwd = lax.broadcasted_iota(jnp.int32, (SUBLANES, n2), 1) < SSM_STATE

    def step(h, s):
        out = []
        for g in range(gb):
            hr, hi = h[:, g * gl:g * gl + n2], h[:, g * gl + n2:(g + 1) * gl]
            sr, si = s[:, g * gl:g * gl + n2], s[:, g * gl + n2:(g + 1) * gl]
            out.append(a_re[g] * hr - a_im[g] * hi + sr)
            out.append(a_re[g] * hi + a_im[g] * hr + si)
        return jnp.concatenate(out, axis=1)

    cpt = SUBLANES // nb
    n_tiles = nc // cpt
    tile_rows = lambda t: pl.ds(pl.multiple_of(t * SUBLANES, SUBLANES), SUBLANES)

    def pick(fwd_rows, bwd_rows):
        return jnp.concatenate(
            [jnp.where(is_fwd[:fwd_rows.shape[0]], fwd_rows[:, k * n2:(k + 1) * n2],
                       bwd_rows[:, k * n2:(k + 1) * n2]) for k in range(2 * gb)], axis=1)

    def sweep_body(t, h, *, second_half):
        tb = n_tiles - 1 - t
        s_f = s_ref[tile_rows(t), :]
        s_b = s_ref[tile_rows(tb), :]
        before = []
        for k in range(cpt):
            kb = cpt - 1 - k
            before.append(h)
            h = step(h, pick(s_f[k * nb:(k + 1) * nb], s_b[kb * nb:(kb + 1) * nb]))
        fwd_valid = jnp.concatenate(before, axis=0)
        bwd_valid = jnp.concatenate(before[::-1], axis=0)
        if second_half:
            h_ref[tile_rows(t), :] = pick(fwd_valid, h_ref[tile_rows(t), :])
            h_ref[tile_rows(tb), :] = pick(hf_ref[tile_rows(tb), :], bwd_valid)
        else:
            hf_ref[tile_rows(t), :] = fwd_valid
            h_ref[tile_rows(tb), :] = bwd_valid
        return h

    assert n_tiles % 2 == 0
    h_mid = lax.fori_loop(0, n_tiles // 2, functools.partial(sweep_body, second_half=False),
                          jnp.zeros((nb, gb * gl), F32))
    lax.fori_loop(n_tiles // 2, n_tiles, functools.partial(sweep_body, second_half=True), h_mid)

    for g in range(gb):
        r = jnp.dot(h_ref[:, g * gl:(g + 1) * gl].astype(BF16), cm_ref[g],
                    preferred_element_type=F32)
        x_ref[2 * g] += r[:, :LANES]
        x_ref[2 * g + 1] += r[:, LANES:]


def _s5_chunks(u, w1, cm, al):
    nb, seq, width = u.shape
    nc = seq // CHUNK
    rows = nb * nc
    gb = GROUPS_PER_BLOCK
    assert SUBLANES % nb == 0 and nc % SUBLANES == 0 and width % LANES == 0
    blk = pl.BlockSpec((nb, seq, LANES), lambda i: (0, 0, i))
    half = CHUNK // 2
    src = np.arange(half * LANES)
    j, g, p = src // LANES, (src % LANES) // SSM_GROUP, src % SSM_GROUP
    perm = np.zeros((half * LANES, half * LANES), np.float32)
    perm[src, g * LANES + j * SSM_GROUP + p] = 1.0
    return pl.pallas_call(
        functools.partial(_s5_chunk_kernel, nb=nb, nc=nc),
        grid=(width // LANES,),
        in_specs=[
            blk,
            pl.BlockSpec(perm.shape, lambda i: (0, 0), pipeline_mode=pl.Buffered(1)),
            pl.BlockSpec((gb,) + w1.shape[1:], lambda i: (i, 0, 0)),
            pl.BlockSpec((gb,) + cm.shape[1:], lambda i: (i, 0, 0)),
            pl.BlockSpec((gb,) + al.shape[1:], lambda i: (i, 0, 0)),
        ],
        out_specs=pl.BlockSpec((2 * gb, rows, LANES), lambda i: (i, 0, 0)),
        out_shape=jax.ShapeDtypeStruct((2 * width // SSM_GROUP, rows, LANES), F32),
        scratch_shapes=[pltpu.VMEM((half, 2 * rows, LANES), F32)]
        + [pltpu.VMEM((rows, gb * GROUP_LANES), F32)] * 3,
        compiler_params=pltpu.CompilerParams(
            dimension_semantics=("parallel",), vmem_limit_bytes=VMEM_LIMIT_BYTES),
        name="s5_chunks",
    )(u, jnp.asarray(perm, BF16), w1, cm, al)


def _mix_out_kernel(x_ref, a_ref, ys_ref, u_ref, d_ref, wglu_ref, bglu_ref, og_ref, wo_a_ref,
                    wo_s_ref, pg_ref, o_ref, yn_ref):
    nb, tt, d = x_ref.shape
    gb = GROUPS_PER_BLOCK
    half = CHUNK // 2
    rows = nb * tt
    for o in range(yn_ref.shape[0]):
        for b in range(nb):
            for h in range(2):
                w = [ys_ref[(o * gb + g) * 2 + h, pl.ds(b, SUBLANES, stride=nb), :]
                     for g in range(gb)]
                v = _swap_pieces(w)
                for j in range(half):
                    yn_ref[o, pl.ds(b * tt + h * half + j, SUBLANES, stride=CHUNK), :] = v[j]
    y = jnp.concatenate([yn_ref[o] for o in range(yn_ref.shape[0])], axis=1)
    y = y + d_ref[...] * u_ref[...].reshape(rows, -1)
    g = y * (0.5 * (1.0 + jnp.tanh(math.sqrt(2.0 / math.pi) * (y + 0.044715 * (y * y * y)))))
    z = jnp.dot(g.astype(BF16), wglu_ref[...], preferred_element_type=F32) + bglu_ref[...]
    s = g * _sigmoid(z)
    s = s * _rms_scale(s) * og_ref[...]
    mixed = (jnp.dot(a_ref[...].reshape(rows, -1), wo_a_ref[...], preferred_element_type=F32)
             + jnp.dot(s.astype(BF16), wo_s_ref[...], preferred_element_type=F32))
    out = x_ref[...].reshape(rows, d) + mixed * _rms_scale(mixed) * pg_ref[...]
    o_ref[...] = out.reshape(nb, tt, d)


def _mix_out(x, a, ys, u, skip_d, w_glu, b_glu, out_g, w_out, post_g):
    nb, seq, d = x.shape
    wa = a.shape[2]
    ws = u.shape[2]
    tt = SUBLANES * CHUNK
    assert seq % tt == 0 and ys.shape == (2 * ws // SSM_GROUP, nb * seq // CHUNK, LANES)
    const = lambda shape: pl.BlockSpec(shape, lambda i: (0, 0))
    tile = lambda width: pl.BlockSpec((nb, tt, width), lambda i: (0, i, 0))
    return pl.pallas_call(
        _mix_out_kernel,
        grid=(seq // tt,),
        in_specs=[
            tile(d), tile(wa),
            pl.BlockSpec((ys.shape[0], SUBLANES * nb, LANES), lambda i: (0, i, 0)),
            tile(ws), const((1, ws)),
            const((ws, ws)), const((1, ws)), const((1, ws)),
            pl.BlockSpec((wa, d), lambda i: (0, 0)),
            pl.BlockSpec((ws, d), lambda i: (wa // ws, 0)),
            const((1, d)),
        ],
        out_specs=tile(d),
        out_shape=jax.ShapeDtypeStruct(x.shape, F32),
        scratch_shapes=[pltpu.VMEM((ws // LANES, nb * tt, LANES), F32)],
        compiler_params=pltpu.CompilerParams(
            dimension_semantics=("parallel",), vmem_limit_bytes=VMEM_LIMIT_BYTES),
        name="mix_out",
    )(x, a, ys, u, skip_d.reshape(1, ws), w_glu, b_glu.reshape(1, ws), out_g.reshape(1, ws),
      w_out, w_out, post_g.reshape(1, d))


def kernel(x, ff1_pre_g, ff1_w_gate, ff1_w_up, ff1_w_down, ff1_post_g, mix_pre_g, w_in, lam_q1, lam_k1, lam_q2, lam_k2, attn_head_g, ssm_lam_re, ssm_lam_im, ssm_log_dt, ssm_b_re, ssm_b_im, ssm_c_re, ssm_c_im, ssm_d, ssm_w_glu, ssm_b_glu, ssm_out_g, w_out, mix_post_g, ff2_pre_g, ff2_w_gate, ff2_w_up, ff2_w_down, ff2_post_g):
    batch, seq, d_model = x.shape
    depth = w_in.shape[0]
    ssm_width = ssm_w_glu.shape[-1]
    slopes = jnp.asarray([2.0 ** (-8.0 * (i + 1) / ATTN_HEADS) for i in range(ATTN_HEADS)], F32)
    bf = lambda w: w.astype(BF16)

    xt = x.reshape(batch * seq, d_model)
    for l in range(depth):
        row = lambda p: p[l:l + 1]
        ff1 = (row(ff1_pre_g), ff1_w_gate[l], ff1_w_up[l], ff1_w_down[l], row(ff1_post_g))
        head, ff1_w, _ = _ffn(xt, *ff1, tm=FFN_HEAD_ROWS, tf=FFN_HEAD_COLS, single_buffer_x=True,
                              tiles=(0, 1), emit_weights=True)
        rest = (FFN_HEAD_ROWS // FFN_TILE_ROWS, (batch * seq - FFN_HEAD_ROWS) // FFN_TILE_ROWS)
        xt, _, (w_in_bf, w_glu_bf, w_out_bf) = _ffn(
            xt, ff1[0], *ff1_w, ff1[4], tiles=rest, into=head,
            cast=(w_in[l], ssm_w_glu[l], w_out[l]))

        qkv, u = _in_proj(xt, row(mix_pre_g), w_in_bf, ssm_width=ssm_width)
        lam_init = 0.8 - 0.6 * math.exp(-0.3 * l)
        a, ff2_w = _attention(qkv, slopes, row(lam_q1), row(lam_k1), row(lam_q2), row(lam_k2),
                              row(attn_head_g), batch=batch, seq=seq, lam_init=lam_init,
                              cast=(ff2_w_gate[l], ff2_w_up[l], ff2_w_down[l]))

        w1, cm, al = _s5_prep(ssm_lam_re[l], ssm_lam_im[l], ssm_log_dt[l], ssm_b_re[l],
                              ssm_b_im[l], ssm_c_re[l], ssm_c_im[l])
        u3 = u.reshape(batch, seq, ssm_width)
        ys = _s5_chunks(u3, w1, cm, al)

        xt = _mix_out(xt.reshape(batch, seq, d_model), a.reshape(batch, seq, -1), ys, u3, ssm_d[l],
                      w_glu_bf, row(ssm_b_glu), row(ssm_out_g), w_out_bf, row(mix_post_g))
        xt = xt.reshape(batch * seq, d_model)

        xt, _, _ = _ffn(xt, row(ff2_pre_g), *ff2_w, row(ff2_post_g))
    return xt.reshape(batch, seq, d_model)
```

```python
import functools
import math

import jax
import jax.numpy as jnp
import numpy as np
from jax import lax
from jax.experimental import pallas as pl
from jax.experimental.pallas import tpu as pltpu

F32 = jnp.float32
BF16 = jnp.bfloat16

NORM_EPS = 1e-6
ATTN_HEADS = 8
ATTN_HEAD_DIM = 64
ATTN_VALUE_DIM = 2 * ATTN_HEAD_DIM
POS_SPLIT = 16
SSM_GROUP = 16
SSM_STATE = 64
CHUNK = 16
GROUP_LANES = CHUNK * SSM_GROUP
SUBLANES = 8
LANES = 128
BF16_ROWS = 16
GROUPS_PER_BLOCK = LANES // SSM_GROUP
PREP_GROUPS_PER_STEP = 16
FFN_HEAD_ROWS = 1024
FFN_TILE_ROWS = 512
FFN_TILE_COLS = 512
FFN_HEAD_COLS = 256
BF16_EXACT_INTS = 256

V7X_VMEM_BYTES = 64 * 1024 * 1024
VMEM_LIMIT_BYTES = V7X_VMEM_BYTES - 8 * 1024 * 1024


def _rms_scale(x):
    return lax.rsqrt(jnp.mean(x * x, axis=-1, keepdims=True) + NORM_EPS)


def _sigmoid(x):
    return 1.0 / (1.0 + jnp.exp(-x))


class _CastJobs:
    def __init__(self, arrays, n_steps, flat_index):
        self.flat_index = flat_index
        self.views, self.slabs = list(arrays), []
        for a in arrays:
            rows, cols = a.shape
            assert rows % BF16_ROWS == 0
            n = max(k for k in range(1, n_steps + 1) if (rows // BF16_ROWS) % k == 0)
            self.slabs.append((rows // n, cols, n))

    def _spec(self, slab):
        r, c, n = slab
        return pl.BlockSpec((r, c), lambda *g: (jnp.minimum(self.flat_index(*g), n - 1), 0))

    @property
    def in_specs(self):
        return [self._spec(s) for s in self.slabs]

    out_specs = in_specs

    @property
    def out_shapes(self):
        return [jax.ShapeDtypeStruct(v.shape, BF16) for v in self.views]

    def __len__(self):
        return len(self.views)


def _run_cast_jobs(in_refs, out_refs):
    for i_ref, o_ref in zip(in_refs, out_refs):
        o_ref[...] = i_ref[...].astype(BF16)


def _ffn_kernel(*refs, n_jobs, has_into, emit_weights):
    x_ref, pre_g_ref, wg_ref, wu_ref, wd_ref, post_g_ref = refs[:6]
    n_in = 6 + int(has_into)
    job_in = refs[n_in:n_in + n_jobs]
    o_ref = refs[n_in + n_jobs]
    n_out = n_in + n_jobs + 1
    w_out = refs[n_out:n_out + 3] if emit_weights else ()
    n_out += len(w_out)
    job_out = refs[n_out:n_out + n_jobs]
    h_ref = refs[n_out + n_jobs]
    j = pl.program_id(1)
    last = pl.num_programs(1) - 1

    def step(first, final):
        if first:
            x = x_ref[...]
            h_ref[...] = (x * _rms_scale(x) * pre_g_ref[...]).astype(BF16)
        _run_cast_jobs(job_in, job_out)
        wg, wu, wd = (r[...].astype(BF16) for r in (wg_ref, wu_ref, wd_ref))
        for w_ref, w in zip(w_out, (wg, wu, wd)):
            w_ref[...] = w
        h = h_ref[...]
        gate = jnp.dot(h, wg, preferred_element_type=F32)
        up = jnp.dot(h, wu, preferred_element_type=F32)
        act = (gate * _sigmoid(gate) * up).astype(BF16)
        part = jnp.dot(act, wd, preferred_element_type=F32)
        if first:
            o_ref[...] = part
        elif final:
            acc = o_ref[...] + part
            o_ref[...] = x_ref[...] + 0.5 * (acc * _rms_scale(acc) * post_g_ref[...])
        else:
            o_ref[...] += part

    pl.when(j == 0)(lambda: step(True, False))
    pl.when((j > 0) & (j < last))(lambda: step(False, False))
    pl.when(j == last)(lambda: step(False, True))


def _ffn(x, pre_g, w_gate, w_up, w_down, post_g, *, cast=(), tm=FFN_TILE_ROWS, tf=FFN_TILE_COLS,
         single_buffer_x=False, tiles=None, into=None, emit_weights=False):
    t, d = x.shape
    f = w_gate.shape[1]
    assert t % tm == 0 and f % tf == 0
    first, nt = tiles if tiles is not None else (0, t // tm)
    assert not emit_weights or nt == 1
    nj = f // tf
    assert nj >= 2
    jobs = _CastJobs(cast, nt * nj, lambda i, j: i * nj + j)
    x_mode = dict(pipeline_mode=pl.Buffered(1)) if single_buffer_x else {}
    w_cols = pl.BlockSpec((d, tf), lambda i, j: (0, j))
    w_rows = pl.BlockSpec((tf, d), lambda i, j: (j, 0))
    into_spec = [pl.BlockSpec(memory_space=pl.ANY)] if into is not None else []
    w_specs = [w_cols, w_cols, w_rows] if emit_weights else []
    w_shapes = [jax.ShapeDtypeStruct(w.shape, BF16) for w in (w_gate, w_up, w_down)]
    outs = pl.pallas_call(
        functools.partial(_ffn_kernel, n_jobs=len(jobs), has_into=into is not None,
                          emit_weights=emit_weights),
        grid=(nt, nj),
        in_specs=[
            pl.BlockSpec((tm, d), lambda i, j: (first + i, 0), **x_mode),
            pl.BlockSpec((1, d), lambda i, j: (0, 0)),
            w_cols, w_cols, w_rows,
            pl.BlockSpec((1, d), lambda i, j: (0, 0)),
        ] + into_spec + jobs.in_specs,
        out_specs=[pl.BlockSpec((tm, d), lambda i, j: (first + i, 0))] + w_specs + jobs.out_specs,
        out_shape=[jax.ShapeDtypeStruct((t, d), F32)] + w_shapes[:len(w_specs)] + jobs.out_shapes,
        input_output_aliases={6: 0} if into is not None else {},
        scratch_shapes=[pltpu.VMEM((tm, d), BF16)],
        compiler_params=pltpu.CompilerParams(
            dimension_semantics=("arbitrary", "arbitrary"),
            vmem_limit_bytes=VMEM_LIMIT_BYTES),
        name="ffn",
    )(x, pre_g.reshape(1, d), w_gate, w_up, w_down, post_g.reshape(1, d),
      *([into] if into is not None else []), *jobs.views)
    n_w = len(w_specs)
    return outs[0], list(outs[1:1 + n_w]), list(outs[1 + n_w:])


def _in_proj_kernel(x_ref, g_ref, w_ref, qkv_ref, u_ref, h_ref, *, n_qkv):
    j = pl.program_id(1)

    def project(first):
        if first:
            x = x_ref[...]
            h_ref[...] = (x * _rms_scale(x) * g_ref[...]).astype(BF16)
        return jnp.dot(h_ref[...], w_ref[...].astype(BF16), preferred_element_type=F32)

    @pl.when(j == 0)
    def _():
        qkv_ref[...] = project(True).astype(BF16)

    @pl.when((j > 0) & (j < n_qkv))
    def _():
        qkv_ref[...] = project(False).astype(BF16)

    @pl.when(j == n_qkv)
    def _():
        u_ref[...] = project(False)


def _in_proj(x, g, w_in, *, ssm_width, tm=1024):
    t, d = x.shape
    n = w_in.shape[1]
    tn = ssm_width
    n_qkv = (n - ssm_width) // tn
    assert t % tm == 0 and n == (n_qkv + 1) * tn
    return pl.pallas_call(
        functools.partial(_in_proj_kernel, n_qkv=n_qkv),
        grid=(t // tm, n_qkv + 1),
        in_specs=[
            pl.BlockSpec((tm, d), lambda i, j: (i, 0)),
            pl.BlockSpec((1, d), lambda i, j: (0, 0)),
            pl.BlockSpec((d, tn), lambda i, j: (0, j)),
        ],
        out_specs=[
            pl.BlockSpec((tm, tn), lambda i, j: (i, jnp.minimum(j, n_qkv - 1))),
            pl.BlockSpec((tm, tn), lambda i, j: (i, 0)),
        ],
        out_shape=[
            jax.ShapeDtypeStruct((t, n - ssm_width), BF16),
            jax.ShapeDtypeStruct((t, ssm_width), F32),
        ],
        scratch_shapes=[pltpu.VMEM((tm, d), BF16)],
        compiler_params=pltpu.CompilerParams(
            dimension_semantics=("parallel", "arbitrary"),
            vmem_limit_bytes=VMEM_LIMIT_BYTES),
        name="in_proj",
    )(x, g.reshape(1, d), w_in)


def _attn_kernel(*refs, tq, nq, lam_init, n_jobs):
    slopes_ref, lq1_ref, lk1_ref, lq2_ref, lk2_ref, hg_ref, q_ref = refs[:7]
    k_refs = refs[7:7 + nq]
    v_refs = refs[7 + nq:7 + 2 * nq]
    n_in = 7 + 2 * nq
    job_in = refs[n_in:n_in + n_jobs]
    o_ref = refs[n_in + n_jobs]
    job_out = refs[n_in + n_jobs + 1:n_in + 2 * n_jobs + 1]
    qf_ref, kf_ref, bias_ref, s_ref, p_ref = refs[n_in + 2 * n_jobs + 1:]
    h = pl.program_id(0)
    qi = pl.program_id(1)
    b = pl.program_id(2)
    e = q_ref.shape[1]

    @pl.when(b == 0)
    def _():
        slope = slopes_ref[h]
        lane = lax.broadcasted_iota(jnp.int32, (tq, e), 1)
        row = lax.broadcasted_iota(jnp.int32, (tq, e), 0)
        row_hi = (slope * POS_SPLIT) * (row // POS_SPLIT).astype(F32)
        row_lo = slope * (row % POS_SPLIT).astype(F32)
        q_base = jnp.where(lane == 0, row_hi, jnp.where(lane == 1, row_lo,
                           jnp.where(lane < 4, 1.0, 0.0)))
        k_base = jnp.where(lane < 2, -1.0, jnp.where(lane == 2, row_hi,
                           jnp.where(lane == 3, row_lo, 0.0)))
        hi_q = (lane == 0).astype(F32)
        hi_k = (lane == 2).astype(F32)
        qf_ref[...] = (q_base + (slope * tq) * qi.astype(F32) * hi_q).astype(BF16)
        for slot in range(nq):
            chunk = lax.rem(qi + slot, nq)
            sign = jnp.where(chunk < qi, 1.0, jnp.where(chunk > qi, -1.0, 0.0))
            kf = sign * k_base + (sign * (slope * tq) * chunk.astype(F32)) * hi_k
            kf_ref[slot * tq:(slot + 1) * tq, :] = kf.astype(BF16)
        col = lax.broadcasted_iota(jnp.int32, (tq, tq), 1)
        bias_ref[...] = slope * jnp.abs(lax.broadcasted_iota(jnp.int32, (tq, tq), 0)
                                        - col).astype(F32)

    _run_cast_jobs(job_in, job_out)
    lam = (jnp.exp(jnp.sum(lq1_ref[...] * lk1_ref[...], axis=-1, keepdims=True))
           - jnp.exp(jnp.sum(lq2_ref[...] * lk2_ref[...], axis=-1, keepdims=True))
           + lam_init)

    q = q_ref[...] * jnp.asarray(ATTN_HEAD_DIM ** -0.5, BF16)
    k_aug = jnp.concatenate([jnp.concatenate([r[...] for r in k_refs], axis=0), kf_ref[...]],
                            axis=1)
    v = jnp.concatenate([r[...] for r in v_refs], axis=0)
    v_ext = jnp.concatenate([v, jnp.ones_like(v)], axis=1)
    lane = lax.broadcasted_iota(jnp.int32, q.shape, 1)
    nt = (((1,), (1,)), ((), ()))
    for c in range(2):
        in_map = (lane >= c * ATTN_HEAD_DIM) & (lane < (c + 1) * ATTN_HEAD_DIM)
        q_aug = jnp.concatenate([jnp.where(in_map, q, jnp.zeros_like(q)), qf_ref[...]], axis=1)
        s_ref[c, :, :tq] = lax.dot_general(q_aug, k_aug[:tq], nt,
                                           preferred_element_type=F32) - bias_ref[...]
        s_ref[c, :, tq:] = lax.dot_general(q_aug, k_aug[tq:], nt, preferred_element_type=F32)
        s = s_ref[c]
        p_ref[c] = jnp.exp(s - jnp.max(s, axis=-1, keepdims=True)).astype(BF16)
    outs = []
    for c in range(2):
        pv = jnp.dot(p_ref[c], v_ext, preferred_element_type=F32)
        outs.append(pv[:, :e] / pv[:, e:])
    o = outs[0] - lam * outs[1]
    o_ref[...] = (o * _rms_scale(o) * hg_ref[...] * (1.0 - lam_init)).astype(o_ref.dtype)


def _attention(qkv, slopes, lq1, lk1, lq2, lk2, head_g, *, batch, seq, lam_init, cast=(), tq=1024):
    t = qkv.shape[0]
    e = ATTN_VALUE_DIM
    nq = seq // tq
    nh = ATTN_HEADS
    assert 8 % nh == 0 and seq // POS_SPLIT <= BF16_EXACT_INTS
    assert tq % POS_SPLIT == 0 and seq % tq == 0 and nq >= 2
    vec = lambda a: a.reshape(1, -1).astype(F32)
    small = lambda n: pl.BlockSpec((1, n), lambda h, qi, b: (0, 0))
    jobs = _CastJobs(cast, nh * nq * batch, lambda h, qi, b: (h * nq + qi) * batch + b)

    def chunk_spec(slot, col0):
        return pl.BlockSpec((tq, e), lambda h, qi, b: (b * nq + lax.rem(qi + slot, nq), col0 + h))

    outs = pl.pallas_call(
        functools.partial(_attn_kernel, tq=tq, nq=nq, lam_init=lam_init, n_jobs=len(jobs)),
        grid=(nh, nq, batch),
        in_specs=[
            pl.BlockSpec(memory_space=pltpu.SMEM),
            small(ATTN_HEAD_DIM), small(ATTN_HEAD_DIM), small(ATTN_HEAD_DIM), small(ATTN_HEAD_DIM),
            small(e),
            pl.BlockSpec((tq, e), lambda h, qi, b: (b * nq + qi, h)),
        ] + [chunk_spec(s, nh) for s in range(nq)] + [chunk_spec(s, 2 * nh) for s in range(nq)]
        + jobs.in_specs,
        out_specs=[pl.BlockSpec((tq, e), lambda h, qi, b: (b * nq + qi, h))] + jobs.out_specs,
        out_shape=[jax.ShapeDtypeStruct((t, nh * e), BF16)] + jobs.out_shapes,
        scratch_shapes=[pltpu.VMEM((tq, e), BF16), pltpu.VMEM((seq, e), BF16),
                        pltpu.VMEM((tq, tq), F32), pltpu.VMEM((2, tq, seq), F32),
                        pltpu.VMEM((2, tq, seq), BF16)],
        compiler_params=pltpu.CompilerParams(
            dimension_semantics=("arbitrary", "arbitrary", "arbitrary"),
            vmem_limit_bytes=VMEM_LIMIT_BYTES),
        name="diff_attention",
    )(slopes, vec(lq1), vec(lk1), vec(lq2), vec(lk2), vec(head_g), qkv,
      *([qkv] * (2 * nq)), *jobs.views)
    return outs[0], list(outs[1:])


def _discretize(lam_re, lam_im, log_dt):
    dt = jnp.exp(log_dt)
    mag = jnp.exp(lam_re * dt)
    a_re = mag * jnp.cos(lam_im * dt)
    a_im = mag * jnp.sin(lam_im * dt)
    den = lam_re * lam_re + lam_im * lam_im
    nr = a_re - 1.0
    f_re = (nr * lam_re + a_im * lam_im) / den
    f_im = (a_im * lam_re - nr * lam_im) / den
    return a_re, a_im, f_re, f_im


def _cmul(ar, ai, br, bi):
    return ar * br - ai * bi, ar * bi + ai * br


def _split_bf16(x, parts):
    out = []
    for k in range(parts):
        piece = x.astype(BF16)
        out.append(piece)
        if k + 1 < parts:
            x = x - piece.astype(F32)
    return out


def _copy_dot(x, onehot, dims):
    return sum(lax.dot_general(piece, onehot, dims, preferred_element_type=F32)
               for piece in _split_bf16(x, 3))


def _dot_3pass(a_parts, b_parts, dims):
    a_hi, a_lo = a_parts
    b_hi, b_lo = b_parts
    dot = lambda x, y: lax.dot_general(x, y, dims, preferred_element_type=F32)
    return dot(a_hi, b_hi) + (dot(a_hi, b_lo) + dot(a_lo, b_hi))


def _int_power(a_re, a_im, e, nbits):
    shape = jnp.broadcast_shapes(a_re.shape, e.shape)
    p_re, p_im = jnp.ones(shape, F32), jnp.zeros(shape, F32)
    s_re, s_im = a_re, a_im
    for k in range(nbits):
        bit = (e & (1 << k)) != 0
        m_re, m_im = _cmul(p_re, p_im, s_re, s_im)
        p_re, p_im = jnp.where(bit, m_re, p_re), jnp.where(bit, m_im, p_im)
        if k + 1 < nbits:
            s_re, s_im = _cmul(s_re, s_im, s_re, s_im)
    return p_re, p_im


def _shift_lanes(x, s, left):
    if s == 0:
        return x
    n = x.shape[-1]
    lane = lax.broadcasted_iota(jnp.int32, x.shape, 1)
    if left:
        return jnp.where(lane < n - s, pltpu.roll(x, n - s, axis=1), 0.0)
    return jnp.where(lane >= s, pltpu.roll(x, s, axis=1), 0.0)


def _s5_prep_kernel(lre_ref, lim_ref, ldt_ref, btre_ref, btim_ref, cre_ref, cim_ref,
                    w1_ref, cm_ref, al_ref):
    for i in range(w1_ref.shape[0]):
        _s5_prep_group(i, lre_ref, lim_ref, ldt_ref, btre_ref, btim_ref, cre_ref, cim_ref,
                       w1_ref, cm_ref, al_ref)


def _s5_prep_group(i, lre_ref, lim_ref, ldt_ref, btre_r_ref, btim_r_ref, cre_ref, cim_ref,
                   w1_ref, cm_ref, al_ref):
    n, p, l = SSM_STATE, SSM_GROUP, CHUNK
    nbits = l.bit_length() - 1
    contract0 = (((0,), (0,)), ((), ()))
    contract_mm = (((1,), (0,)), ((), ()))

    a_re, a_im, f_re, f_im = _discretize(lre_ref[i], lim_ref[i], ldt_ref[i])
    bt_re, bt_im = btre_r_ref[i], btim_r_ref[i]
    bb_re, bb_im = _cmul(f_re, f_im, bt_re, bt_im)
    tau = lax.broadcasted_iota(jnp.int32, (l, 2 * n), 0)
    lane = lax.broadcasted_iota(jnp.int32, (l, 2 * n), 1)
    pw_re, pw_im = _int_power(a_re, a_im, jnp.where(lane < n, l - 1 - tau, tau), nbits)
    bm_re = pw_re[:, None, :] * bb_re[None, :, :] - pw_im[:, None, :] * bb_im[None, :, :]
    bm_im = pw_re[:, None, :] * bb_im[None, :, :] + pw_im[:, None, :] * bb_re[None, :, :]
    w1_ref[i, :, GROUP_LANES:GROUP_LANES + 2 * n] = bm_re.reshape(l * p, 2 * n).astype(BF16)
    w1_ref[i, :, GROUP_LANES + 2 * n:] = bm_im.reshape(l * p, 2 * n).astype(BF16)
    al_re, al_im = a_re, a_im
    for _ in range(nbits):
        al_re, al_im = _cmul(al_re, al_im, al_re, al_im)
    al_ref[i] = jnp.concatenate([al_re, al_im], axis=0)

    quant = jnp.concatenate([a_re, a_im, jnp.zeros((SUBLANES - 2, 2 * n), F32)], axis=0)
    quant_t = quant.T
    a_re_c, a_im_c = quant_t[:, 0:1], quant_t[:, 1:2]

    lanes_p = lax.broadcasted_iota(jnp.int32, (p, GROUP_LANES), 1)
    rows_p = lax.broadcasted_iota(jnp.int32, (p, GROUP_LANES), 0)
    tile_p = (lanes_p % p == rows_p).astype(BF16)
    tile_lag = ((l - 1) - lanes_p // p == rows_p).astype(BF16)
    ac_re, ac_im = _cmul(_copy_dot(pw_re, tile_lag, contract0), _copy_dot(pw_im, tile_lag, contract0),
                         _copy_dot(cre_ref[i], tile_p, contract0),
                         _copy_dot(cim_ref[i], tile_p, contract0))
    ac_re_parts, ac_im_parts = _split_bf16(ac_re, 2), _split_bf16(ac_im, 2)
    is_fwd = lax.broadcasted_iota(jnp.int32, bb_re.shape, 1) < n
    t_rows = None
    for d in range(2):
        in_dir = is_fwd if d == 0 else ~is_fwd
        taps = (_dot_3pass(_split_bf16(jnp.where(in_dir, bb_re, 0.0), 2), ac_re_parts, contract_mm)
                - _dot_3pass(_split_bf16(jnp.where(in_dir, bb_im, 0.0), 2), ac_im_parts,
                             contract_mm))
        rows = [_shift_lanes(taps, p * (tp if d == 0 else l - 1 - tp), left=(d == 1))
                for tp in range(l)]
        rows = jnp.concatenate(rows, axis=0)
        t_rows = rows if t_rows is None else t_rows + rows
    w1_ref[i, :, 0:GROUP_LANES] = t_rows.astype(BF16)
    r_re, r_im = _cmul(a_re_c, a_im_c, ac_re, ac_im)
    cm_ref[i] = jnp.concatenate([r_re, -r_im], axis=0).astype(BF16)


def _s5_prep(lam_re, lam_im, log_dt, b_re, b_im, c_re, c_im):
    _, g, n = lam_re.shape
    p = b_re.shape[-1]
    row = lambda a: jnp.transpose(a, (1, 0, 2)).reshape(g, 1, 2 * n)
    ldt_r = row(jnp.broadcast_to(log_dt[:, :, None], (2, g, n)))
    bt = lambda b: jnp.transpose(b, (1, 3, 0, 2)).reshape(g, p, 2 * n)
    cr = lambda c: jnp.transpose(c, (1, 2, 0, 3)).reshape(g, p, 2 * n)

    gb = PREP_GROUPS_PER_STEP
    assert g % gb == 0
    spec_r = lambda rows: pl.BlockSpec((gb, rows, 2 * n), lambda i: (i, 0, 0))
    return pl.pallas_call(
        _s5_prep_kernel,
        grid=(g // gb,),
        in_specs=[spec_r(1), spec_r(1), spec_r(1), spec_r(p), spec_r(p), spec_r(p), spec_r(p)],
        out_specs=[
            pl.BlockSpec((gb, GROUP_LANES, GROUP_LANES + 4 * n), lambda i: (i, 0, 0)),
            pl.BlockSpec((gb, 4 * n, GROUP_LANES), lambda i: (i, 0, 0)),
            pl.BlockSpec((gb, 2, 2 * n), lambda i: (i, 0, 0)),
        ],
        out_shape=[
            jax.ShapeDtypeStruct((g, GROUP_LANES, GROUP_LANES + 4 * n), BF16),
            jax.ShapeDtypeStruct((g, 4 * n, GROUP_LANES), BF16),
            jax.ShapeDtypeStruct((g, 2, 2 * n), F32),
        ],
        compiler_params=pltpu.CompilerParams(dimension_semantics=("parallel",)),
        name="s5_prep",
    )(row(lam_re), row(lam_im), ldt_r, bt(b_re), bt(b_im), cr(c_re), cr(c_im))


def _swap_pieces(v):
    piece = lax.broadcasted_iota(jnp.int32, v[0].shape, 1) // SSM_GROUP
    v = list(v)
    for d in (4, 2, 1):
        hi = (piece & d) != 0
        for a in range(8):
            if a & d:
                continue
            va, vb = v[a], v[a + d]
            v[a] = jnp.where(hi, pltpu.roll(vb, d * SSM_GROUP, axis=1), va)
            v[a + d] = jnp.where(hi, vb, pltpu.roll(va, LANES - d * SSM_GROUP, axis=1))
    return v


def _s5_chunk_kernel(u_ref, perm_ref, w1_ref, cm_ref, al_ref, x_ref,
                     a_ref, s_ref, hf_ref, h_ref, *, nb, nc):
    gl = GROUP_LANES
    gb = GROUPS_PER_BLOCK
    n2 = 2 * SSM_STATE
    half = CHUNK // 2
    ctile = SUBLANES * CHUNK

    def token_rows(ct, h, j):
        return pl.ds(ct * ctile + h * half + j, SUBLANES, stride=CHUNK)

    rows_all = nb * nc

    def gather_body(ct, carry):
        for b in range(nb):
            for h in range(2):
                for j in range(half):
                    a_ref[j, pl.ds(h * rows_all + ct * SUBLANES * nb + b, SUBLANES, stride=nb), :] = (
                        u_ref[b, token_rows(ct, h, j), :])
        return carry

    lax.fori_loop(0, nc // SUBLANES, gather_body, 0)
    lhs = jnp.concatenate([a_ref[j] for j in range(half)], axis=1).astype(BF16)
    xp = jnp.dot(lhs, perm_ref[...], preferred_element_type=F32).astype(BF16)

    for g in range(gb):
        x = jnp.concatenate([xp[:rows_all, g * LANES:(g + 1) * LANES],
                             xp[rows_all:, g * LANES:(g + 1) * LANES]], axis=1)
        r = jnp.dot(x, w1_ref[g], preferred_element_type=F32)
        x_ref[2 * g] = r[:, :LANES]
        x_ref[2 * g + 1] = r[:, LANES:gl]
        s_ref[:, g * gl:(g + 1) * gl] = r[:, gl:]

    a_re = [al_ref[g, 0:1, :] for g in range(gb)]
    a_im = [al_ref[g, 1:2, :] for g in range(gb)]
    is_fwd = lax.broadcasted_iota(jnp.int32, (SUBLANES, n2), 1) < SSM_STATE

    def step(h, s):
        out = []
        for g in range(gb):
            hr, hi = h[:, g * gl:g * gl + n2], h[:, g * gl + n2:(g + 1) * gl]
            sr, si = s[:, g * gl:g * gl + n2], s[:, g * gl + n2:(g + 1) * gl]
            out.append(a_re[g] * hr - a_im[g] * hi + sr)
            out.append(a_re[g] * hi + a_im[g] * hr + si)
        return jnp.concatenate(out, axis=1)

    cpt = SUBLANES // nb
    n_tiles = nc // cpt
    tile_rows = lambda t: pl.ds(pl.multiple_of(t * SUBLANES, SUBLANES), SUBLANES)

    def pick(fwd_rows, bwd_rows):
        return jnp.concatenate(
            [jnp.where(is_fwd[:fwd_rows.shape[0]], fwd_rows[:, k * n2:(k + 1) * n2],
                       bwd_rows[:, k * n2:(k + 1) * n2]) for k in range(2 * gb)], axis=1)

    def sweep_body(t, h, *, second_half):
        tb = n_tiles - 1 - t
        s_f = s_ref[tile_rows(t), :]
        s_b = s_ref[tile_rows(tb), :]
        before = []
        for k in range(cpt):
            kb = cpt - 1 - k
            before.append(h)
            h = step(h, pick(s_f[k * nb:(k + 1) * nb], s_b[kb * nb:(kb + 1) * nb]))
        fwd_valid = jnp.concatenate(before, axis=0)
        bwd_valid = jnp.concatenate(before[::-1], axis=0)
        if second_half:
            h_ref[tile_rows(t), :] = pick(fwd_valid, h_ref[tile_rows(t), :])
            h_ref[tile_rows(tb), :] = pick(hf_ref[tile_rows(tb), :], bwd_valid)
        else:
            hf_ref[tile_rows(t), :] = fwd_valid
            h_ref[tile_rows(tb), :] = bwd_valid
        return h

    assert n_tiles % 2 == 0
    h_mid = lax.fori_loop(0, n_tiles // 2, functools.partial(sweep_body, second_half=False),
                          jnp.zeros((nb, gb * gl), F32))
    lax.fori_loop(n_tiles // 2, n_tiles, functools.partial(sweep_body, second_half=True), h_mid)

    for g in range(gb):
        r = jnp.dot(h_ref[:, g * gl:(g + 1) * gl].astype(BF16), cm_ref[g],
                    preferred_element_type=F32)
        x_ref[2 * g] += r[:, :LANES]
        x_ref[2 * g + 1] += r[:, LANES:]


def _s5_chunks(u, w1, cm, al):
    nb, seq, width = u.shape
    nc = seq // CHUNK
    rows = nb * nc
    gb = GROUPS_PER_BLOCK
    assert SUBLANES % nb == 0 and nc % SUBLANES == 0 and width % LANES == 0
    blk = pl.BlockSpec((nb, seq, LANES), lambda i: (0, 0, i))
    half = CHUNK // 2
    src = np.arange(half * LANES)
    j, g, p = src // LANES, (src % LANES) // SSM_GROUP, src % SSM_GROUP
    perm = np.zeros((half * LANES, half * LANES), np.float32)
    perm[src, g * LANES + j * SSM_GROUP + p] = 1.0
    return pl.pallas_call(
        functools.partial(_s5_chunk_kernel, nb=nb, nc=nc),
        grid=(width // LANES,),
        in_specs=[
            blk,
            pl.BlockSpec(perm.shape, lambda i: (0, 0), pipeline_mode=pl.Buffered(1)),
            pl.BlockSpec((gb,) + w1.shape[1:], lambda i: (i, 0, 0)),
            pl.BlockSpec((gb,) + cm.shape[1:], lambda i: (i, 0, 0)),
            pl.BlockSpec((gb,) + al.shape[1:], lambda i: (i, 0, 0)),
        ],
        out_specs=pl.BlockSpec((2 * gb, rows, LANES), lambda i: (i, 0, 0)),
        out_shape=jax.ShapeDtypeStruct((2 * width // SSM_GROUP, rows, LANES), F32),
        scratch_shapes=[pltpu.VMEM((half, 2 * rows, LANES), F32)]
        + [pltpu.VMEM((rows, gb * GROUP_LANES), F32)] * 3,
        compiler_params=pltpu.CompilerParams(
            dimension_semantics=("parallel",), vmem_limit_bytes=VMEM_LIMIT_BYTES),
        name="s5_chunks",
    )(u, jnp.asarray(perm, BF16), w1, cm, al)


def _mix_out_kernel(x_ref, a_ref, ys_ref, u_ref, d_ref, wglu_ref, bglu_ref, og_ref, wo_a_ref,
                    wo_s_ref, pg_ref, o_ref, yn_ref):
    nb, tt, d = x_ref.shape
    gb = GROUPS_PER_BLOCK
    half = CHUNK // 2
    rows = nb * tt
    for o in range(yn_ref.shape[0]):
        for b in range(nb):
            for h in range(2):
                w = [ys_ref[(o * gb + g) * 2 + h, pl.ds(b, SUBLANES, stride=nb), :]
                     for g in range(gb)]
                v = _swap_pieces(w)
                for j in range(half):
                    yn_ref[o, pl.ds(b * tt + h * half + j, SUBLANES, stride=CHUNK), :] = v[j]
    y = jnp.concatenate([yn_ref[o] for o in range(yn_ref.shape[0])], axis=1)
    y = y + d_ref[...] * u_ref[...].reshape(rows, -1)
    g = y * (0.5 * (1.0 + jnp.tanh(math.sqrt(2.0 / math.pi) * (y + 0.044715 * (y * y * y)))))
    z = jnp.dot(g.astype(BF16), wglu_ref[...], preferred_element_type=F32) + bglu_ref[...]
    s = g * _sigmoid(z)
    s = s * _rms_scale(s) * og_ref[...]
    mixed = (jnp.dot(a_ref[...].reshape(rows, -1), wo_a_ref[...], preferred_element_type=F32)
             + jnp.dot(s.astype(BF16), wo_s_ref[...], preferred_element_type=F32))
    out = x_ref[...].reshape(rows, d) + mixed * _rms_scale(mixed) * pg_ref[...]
    o_ref[...] = out.reshape(nb, tt, d)


def _mix_out(x, a, ys, u, skip_d, w_glu, b_glu, out_g, w_out, post_g):
    nb, seq, d = x.shape
    wa = a.shape[2]
    ws = u.shape[2]
    tt = SUBLANES * CHUNK
    assert seq % tt == 0 and ys.shape == (2 * ws // SSM_GROUP, nb * seq // CHUNK, LANES)
    const = lambda shape: pl.BlockSpec(shape, lambda i: (0, 0))
    tile = lambda width: pl.BlockSpec((nb, tt, width), lambda i: (0, i, 0))
    return pl.pallas_call(
        _mix_out_kernel,
        grid=(seq // tt,),
        in_specs=[
            tile(d), tile(wa),
            pl.BlockSpec((ys.shape[0], SUBLANES * nb, LANES), lambda i: (0, i, 0)),
            tile(ws), const((1, ws)),
            const((ws, ws)), const((1, ws)), const((1, ws)),
            pl.BlockSpec((wa, d), lambda i: (0, 0)),
            pl.BlockSpec((ws, d), lambda i: (wa // ws, 0)),
            const((1, d)),
        ],
        out_specs=tile(d),
        out_shape=jax.ShapeDtypeStruct(x.shape, F32),
        scratch_shapes=[pltpu.VMEM((ws // LANES, nb * tt, LANES), F32)],
        compiler_params=pltpu.CompilerParams(
            dimension_semantics=("parallel",), vmem_limit_bytes=VMEM_LIMIT_BYTES),
        name="mix_out",
    )(x, a, ys, u, skip_d.reshape(1, ws), w_glu, b_glu.reshape(1, ws), out_g.reshape(1, ws),
      w_out, w_out, post_g.reshape(1, d))


def kernel(x, ff1_pre_g, ff1_w_gate, ff1_w_up, ff1_w_down, ff1_post_g, mix_pre_g, w_in, lam_q1, lam_k1, lam_q2, lam_k2, attn_head_g, ssm_lam_re, ssm_lam_im, ssm_log_dt, ssm_b_re, ssm_b_im, ssm_c_re, ssm_c_im, ssm_d, ssm_w_glu, ssm_b_glu, ssm_out_g, w_out, mix_post_g, ff2_pre_g, ff2_w_gate, ff2_w_up, ff2_w_down, ff2_post_g):
    batch, seq, d_model = x.shape
    depth = w_in.shape[0]
    ssm_width = ssm_w_glu.shape[-1]
    slopes = jnp.asarray([2.0 ** (-8.0 * (i + 1) / ATTN_HEADS) for i in range(ATTN_HEADS)], F32)
    bf = lambda w: w.astype(BF16)

    xt = x.reshape(batch * seq, d_model)
    for l in range(depth):
        row = lambda p: p[l:l + 1]
        ff1 = (row(ff1_pre_g), ff1_w_gate[l], ff1_w_up[l], ff1_w_down[l], row(ff1_post_g))
        head, ff1_w, _ = _ffn(xt, *ff1, tm=FFN_HEAD_ROWS, tf=FFN_HEAD_COLS, single_buffer_x=True,
                              tiles=(0, 1), emit_weights=True)
        rest = (FFN_HEAD_ROWS // FFN_TILE_ROWS, (batch * seq - FFN_HEAD_ROWS) // FFN_TILE_ROWS)
        xt, _, (w_in_bf, w_glu_bf, w_out_bf) = _ffn(
            xt, ff1[0], *ff1_w, ff1[4], tiles=rest, into=head,
            cast=(w_in[l], ssm_w_glu[l], w_out[l]))

        qkv, u = _in_proj(xt, row(mix_pre_g), w_in_bf, ssm_width=ssm_width)
        lam_init = 0.8 - 0.6 * math.exp(-0.3 * l)
        a, ff2_w = _attention(qkv, slopes, row(lam_q1), row(lam_k1), row(lam_q2), row(lam_k2),
                              row(attn_head_g), batch=batch, seq=seq, lam_init=lam_init,
                              cast=(ff2_w_gate[l], ff2_w_up[l], ff2_w_down[l]))

        w1, cm, al = _s5_prep(ssm_lam_re[l], ssm_lam_im[l], ssm_log_dt[l], ssm_b_re[l],
                              ssm_b_im[l], ssm_c_re[l], ssm_c_im[l])
        u3 = u.reshape(batch, seq, ssm_width)
        ys = _s5_chunks(u3, w1, cm, al)

        xt = _mix_out(xt.reshape(batch, seq, d_model), a.reshape(batch, seq, -1), ys, u3, ssm_d[l],
                      w_glu_bf, row(ssm_b_glu), row(ssm_out_g), w_out_bf, row(mix_post_g))
        xt = xt.reshape(batch * seq, d_model)

        xt, _, _ = _ffn(xt, row(ff2_pre_g), *ff2_w, row(ff2_post_g))
    return xt.reshape(batch, seq, d_model)
```

```python
import functools
import math

import jax
import jax.numpy as jnp
import numpy as np
from jax import lax
from jax.experimental import pallas as pl
from jax.experimental.pallas import tpu as pltpu

F32 = jnp.float32
BF16 = jnp.bfloat16

NORM_EPS = 1e-6
ATTN_HEADS = 8
ATTN_HEAD_DIM = 64
ATTN_VALUE_DIM = 2 * ATTN_HEAD_DIM
POS_SPLIT = 16
SSM_GROUP = 16
SSM_STATE = 64
CHUNK = 16
GROUP_LANES = CHUNK * SSM_GROUP
SUBLANES = 8
LANES = 128
BF16_ROWS = 16
GROUPS_PER_BLOCK = LANES // SSM_GROUP
FFN_HEAD_ROWS = 1024
FFN_TILE_ROWS = 512
FFN_TILE_COLS = 512
FFN_HEAD_COLS = 256
BF16_EXACT_INTS = 256

V7X_VMEM_BYTES = 64 * 1024 * 1024
VMEM_LIMIT_BYTES = V7X_VMEM_BYTES - 8 * 1024 * 1024


def _rms_scale(x):
    return lax.rsqrt(jnp.mean(x * x, axis=-1, keepdims=True) + NORM_EPS)


def _sigmoid(x):
    return 1.0 / (1.0 + jnp.exp(-x))


class _CastJobs:
    def __init__(self, arrays, n_steps, flat_index):
        self.flat_index = flat_index
        self.views, self.slabs = list(arrays), []
        for a in arrays:
            rows, cols = a.shape
            assert rows % BF16_ROWS == 0
            n = max(k for k in range(1, n_steps + 1) if (rows // BF16_ROWS) % k == 0)
            self.slabs.append((rows // n, cols, n))

    def _spec(self, slab):
        r, c, n = slab
        return pl.BlockSpec((r, c), lambda *g: (jnp.minimum(self.flat_index(*g), n - 1), 0))

    @property
    def in_specs(self):
        return [self._spec(s) for s in self.slabs]

    out_specs = in_specs

    @property
    def out_shapes(self):
        return [jax.ShapeDtypeStruct(v.shape, BF16) for v in self.views]

    def __len__(self):
        return len(self.views)


def _run_cast_jobs(in_refs, out_refs):
    for i_ref, o_ref in zip(in_refs, out_refs):
        o_ref[...] = i_ref[...].astype(BF16)


def _ffn_kernel(*refs, n_jobs, has_into, emit_weights):
    x_ref, pre_g_ref, wg_ref, wu_ref, wd_ref, post_g_ref = refs[:6]
    n_in = 6 + int(has_into)
    job_in = refs[n_in:n_in + n_jobs]
    o_ref = refs[n_in + n_jobs]
    n_out = n_in + n_jobs + 1
    w_out = refs[n_out:n_out + 3] if emit_weights else ()
    n_out += len(w_out)
    job_out = refs[n_out:n_out + n_jobs]
    h_ref = refs[n_out + n_jobs]
    j = pl.program_id(1)
    last = pl.num_programs(1) - 1

    def step(first, final):
        if first:
            x = x_ref[...]
            h_ref[...] = (x * _rms_scale(x) * pre_g_ref[...]).astype(BF16)
        _run_cast_jobs(job_in, job_out)
        wg, wu, wd = (r[...].astype(BF16) for r in (wg_ref, wu_ref, wd_ref))
        for w_ref, w in zip(w_out, (wg, wu, wd)):
            w_ref[...] = w
        h = h_ref[...]
        gate = jnp.dot(h, wg, preferred_element_type=F32)
        up = jnp.dot(h, wu, preferred_element_type=F32)
        act = (gate * _sigmoid(gate) * up).astype(BF16)
        part = jnp.dot(act, wd, preferred_element_type=F32)
        if first:
            o_ref[...] = part
        elif final:
            acc = o_ref[...] + part
            o_ref[...] = x_ref[...] + 0.5 * (acc * _rms_scale(acc) * post_g_ref[...])
        else:
            o_ref[...] += part

    pl.when(j == 0)(lambda: step(True, False))
    pl.when((j > 0) & (j < last))(lambda: step(False, False))
    pl.when(j == last)(lambda: step(False, True))


def _ffn(x, pre_g, w_gate, w_up, w_down, post_g, *, cast=(), tm=FFN_TILE_ROWS, tf=FFN_TILE_COLS,
         single_buffer_x=False, tiles=None, into=None, emit_weights=False):
    t, d = x.shape
    f = w_gate.shape[1]
    assert t % tm == 0 and f % tf == 0
    first, nt = tiles if tiles is not None else (0, t // tm)
    assert not emit_weights or nt == 1
    nj = f // tf
    assert nj >= 2
    jobs = _CastJobs(cast, nt * nj, lambda i, j: i * nj + j)
    x_mode = dict(pipeline_mode=pl.Buffered(1)) if single_buffer_x else {}
    w_cols = pl.BlockSpec((d, tf), lambda i, j: (0, j))
    w_rows = pl.BlockSpec((tf, d), lambda i, j: (j, 0))
    into_spec = [pl.BlockSpec(memory_space=pl.ANY)] if into is not None else []
    w_specs = [w_cols, w_cols, w_rows] if emit_weights else []
    w_shapes = [jax.ShapeDtypeStruct(w.shape, BF16) for w in (w_gate, w_up, w_down)]
    outs = pl.pallas_call(
        functools.partial(_ffn_kernel, n_jobs=len(jobs), has_into=into is not None,
                          emit_weights=emit_weights),
        grid=(nt, nj),
        in_specs=[
            pl.BlockSpec((tm, d), lambda i, j: (first + i, 0), **x_mode),
            pl.BlockSpec((1, d), lambda i, j: (0, 0)),
            w_cols, w_cols, w_rows,
            pl.BlockSpec((1, d), lambda i, j: (0, 0)),
        ] + into_spec + jobs.in_specs,
        out_specs=[pl.BlockSpec((tm, d), lambda i, j: (first + i, 0))] + w_specs + jobs.out_specs,
        out_shape=[jax.ShapeDtypeStruct((t, d), F32)] + w_shapes[:len(w_specs)] + jobs.out_shapes,
        input_output_aliases={6: 0} if into is not None else {},
        scratch_shapes=[pltpu.VMEM((tm, d), BF16)],
        compiler_params=pltpu.CompilerParams(
            dimension_semantics=("arbitrary", "arbitrary"),
            vmem_limit_bytes=VMEM_LIMIT_BYTES),
        name="ffn",
    )(x, pre_g.reshape(1, d), w_gate, w_up, w_down, post_g.reshape(1, d),
      *([into] if into is not None else []), *jobs.views)
    n_w = len(w_specs)
    return outs[0], list(outs[1:1 + n_w]), list(outs[1 + n_w:])


def _in_proj_kernel(*refs, n_qkv):
    x_ref, g_ref, w_ref = refs[:3]
    s5_refs = refs[3:10] + refs[12:15]
    qkv_ref, u_ref = refs[10:12]
    h_ref = refs[15]
    j = pl.program_id(1)

    def project(first):
        if first:
            x = x_ref[...]
            h_ref[...] = (x * _rms_scale(x) * g_ref[...]).astype(BF16)
        _s5_prep_groups(*s5_refs)
        return jnp.dot(h_ref[...], w_ref[...].astype(BF16), preferred_element_type=F32)

    @pl.when(j == 0)
    def _():
        qkv_ref[...] = project(True).astype(BF16)

    @pl.when((j > 0) & (j < n_qkv))
    def _():
        qkv_ref[...] = project(False).astype(BF16)

    @pl.when(j == n_qkv)
    def _():
        u_ref[...] = project(False)


def _in_proj(x, g, w_in, s5_rows, *, ssm_width, tm=1024):
    t, d = x.shape
    n = w_in.shape[1]
    tn = ssm_width
    n_qkv = (n - ssm_width) // tn
    assert t % tm == 0 and n == (n_qkv + 1) * tn
    groups, p, n2 = s5_rows[3].shape
    steps = (t // tm) * (n_qkv + 1)
    assert groups % steps == 0
    gb = groups // steps
    s5_spec = lambda *blk: pl.BlockSpec((gb,) + blk, lambda i, j: (i * (n_qkv + 1) + j, 0, 0))
    return pl.pallas_call(
        functools.partial(_in_proj_kernel, n_qkv=n_qkv),
        grid=(t // tm, n_qkv + 1),
        in_specs=[
            pl.BlockSpec((tm, d), lambda i, j: (i, 0)),
            pl.BlockSpec((1, d), lambda i, j: (0, 0)),
            pl.BlockSpec((d, tn), lambda i, j: (0, j)),
        ] + [s5_spec(1, n2)] * 3 + [s5_spec(p, n2)] * 4,
        out_specs=[
            pl.BlockSpec((tm, tn), lambda i, j: (i, jnp.minimum(j, n_qkv - 1))),
            pl.BlockSpec((tm, tn), lambda i, j: (i, 0)),
            s5_spec(GROUP_LANES, GROUP_LANES + 2 * n2),
            s5_spec(2 * n2, GROUP_LANES),
            s5_spec(2, n2),
        ],
        out_shape=[
            jax.ShapeDtypeStruct((t, n - ssm_width), BF16),
            jax.ShapeDtypeStruct((t, ssm_width), F32),
            jax.ShapeDtypeStruct((groups, GROUP_LANES, GROUP_LANES + 2 * n2), BF16),
            jax.ShapeDtypeStruct((groups, 2 * n2, GROUP_LANES), BF16),
            jax.ShapeDtypeStruct((groups, 2, n2), F32),
        ],
        scratch_shapes=[pltpu.VMEM((tm, d), BF16)],
        compiler_params=pltpu.CompilerParams(
            dimension_semantics=("parallel", "arbitrary"),
            vmem_limit_bytes=VMEM_LIMIT_BYTES),
        name="in_proj",
    )(x, g.reshape(1, d), w_in, *s5_rows)


def _attn_kernel(*refs, tq, nq, lam_init, n_jobs):
    slopes_ref, lq1_ref, lk1_ref, lq2_ref, lk2_ref, hg_ref, q_ref = refs[:7]
    k_refs = refs[7:7 + nq]
    v_refs = refs[7 + nq:7 + 2 * nq]
    n_in = 7 + 2 * nq
    job_in = refs[n_in:n_in + n_jobs]
    o_ref = refs[n_in + n_jobs]
    job_out = refs[n_in + n_jobs + 1:n_in + 2 * n_jobs + 1]
    qf_ref, kf_ref, bias_ref, s_ref, p_ref = refs[n_in + 2 * n_jobs + 1:]
    h = pl.program_id(0)
    qi = pl.program_id(1)
    b = pl.program_id(2)
    e = q_ref.shape[1]

    @pl.when(b == 0)
    def _():
        slope = slopes_ref[h]
        lane = lax.broadcasted_iota(jnp.int32, (tq, e), 1)
        row = lax.broadcasted_iota(jnp.int32, (tq, e), 0)
        row_hi = (slope * POS_SPLIT) * (row // POS_SPLIT).astype(F32)
        row_lo = slope * (row % POS_SPLIT).astype(F32)
        q_base = jnp.where(lane == 0, row_hi, jnp.where(lane == 1, row_lo,
                           jnp.where(lane < 4, 1.0, 0.0)))
        k_base = jnp.where(lane < 2, -1.0, jnp.where(lane == 2, row_hi,
                           jnp.where(lane == 3, row_lo, 0.0)))
        hi_q = (lane == 0).astype(F32)
        hi_k = (lane == 2).astype(F32)
        qf_ref[...] = (q_base + (slope * tq) * qi.astype(F32) * hi_q).astype(BF16)
        for slot in range(nq):
            chunk = lax.rem(qi + slot, nq)
            sign = jnp.where(chunk < qi, 1.0, jnp.where(chunk > qi, -1.0, 0.0))
            kf = sign * k_base + (sign * (slope * tq) * chunk.astype(F32)) * hi_k
            kf_ref[slot * tq:(slot + 1) * tq, :] = kf.astype(BF16)
        col = lax.broadcasted_iota(jnp.int32, (tq, tq), 1)
        bias_ref[...] = slope * jnp.abs(lax.broadcasted_iota(jnp.int32, (tq, tq), 0)
                                        - col).astype(F32)

    _run_cast_jobs(job_in, job_out)
    lam = (jnp.exp(jnp.sum(lq1_ref[...] * lk1_ref[...], axis=-1, keepdims=True))
           - jnp.exp(jnp.sum(lq2_ref[...] * lk2_ref[...], axis=-1, keepdims=True))
           + lam_init)

    q = q_ref[...] * jnp.asarray(ATTN_HEAD_DIM ** -0.5, BF16)
    k_aug = jnp.concatenate([jnp.concatenate([r[...] for r in k_refs], axis=0), kf_ref[...]],
                            axis=1)
    v = jnp.concatenate([r[...] for r in v_refs], axis=0)
    v_ext = jnp.concatenate([v, jnp.ones_like(v)], axis=1)
    lane = lax.broadcasted_iota(jnp.int32, q.shape, 1)
    nt = (((1,), (1,)), ((), ()))
    for c in range(2):
        in_map = (lane >= c * ATTN_HEAD_DIM) & (lane < (c + 1) * ATTN_HEAD_DIM)
        q_aug = jnp.concatenate([jnp.where(in_map, q, jnp.zeros_like(q)), qf_ref[...]], axis=1)
        s_ref[c, :, :tq] = lax.dot_general(q_aug, k_aug[:tq], nt,
                                           preferred_element_type=F32) - bias_ref[...]
        s_ref[c, :, tq:] = lax.dot_general(q_aug, k_aug[tq:], nt, preferred_element_type=F32)
        s = s_ref[c]
        p_ref[c] = jnp.exp(s - jnp.max(s, axis=-1, keepdims=True)).astype(BF16)
    outs = []
    for c in range(2):
        pv = jnp.dot(p_ref[c], v_ext, preferred_element_type=F32)
        outs.append(pv[:, :e] / pv[:, e:])
    o = outs[0] - lam * outs[1]
    o_ref[...] = (o * _rms_scale(o) * hg_ref[...] * (1.0 - lam_init)).astype(o_ref.dtype)


def _attention(qkv, slopes, lq1, lk1, lq2, lk2, head_g, *, batch, seq, lam_init, cast=(), tq=1024):
    t = qkv.shape[0]
    e = ATTN_VALUE_DIM
    nq = seq // tq
    nh = ATTN_HEADS
    assert 8 % nh == 0 and seq // POS_SPLIT <= BF16_EXACT_INTS
    assert tq % POS_SPLIT == 0 and seq % tq == 0 and nq >= 2
    vec = lambda a: a.reshape(1, -1).astype(F32)
    small = lambda n: pl.BlockSpec((1, n), lambda h, qi, b: (0, 0))
    jobs = _CastJobs(cast, nh * nq * batch, lambda h, qi, b: (h * nq + qi) * batch + b)

    def chunk_spec(slot, col0):
        return pl.BlockSpec((tq, e), lambda h, qi, b: (b * nq + lax.rem(qi + slot, nq), col0 + h))

    outs = pl.pallas_call(
        functools.partial(_attn_kernel, tq=tq, nq=nq, lam_init=lam_init, n_jobs=len(jobs)),
        grid=(nh, nq, batch),
        in_specs=[
            pl.BlockSpec(memory_space=pltpu.SMEM),
            small(ATTN_HEAD_DIM), small(ATTN_HEAD_DIM), small(ATTN_HEAD_DIM), small(ATTN_HEAD_DIM),
            small(e),
            pl.BlockSpec((tq, e), lambda h, qi, b: (b * nq + qi, h)),
        ] + [chunk_spec(s, nh) for s in range(nq)] + [chunk_spec(s, 2 * nh) for s in range(nq)]
        + jobs.in_specs,
        out_specs=[pl.BlockSpec((tq, e), lambda h, qi, b: (b * nq + qi, h))] + jobs.out_specs,
        out_shape=[jax.ShapeDtypeStruct((t, nh * e), BF16)] + jobs.out_shapes,
        scratch_shapes=[pltpu.VMEM((tq, e), BF16), pltpu.VMEM((seq, e), BF16),
                        pltpu.VMEM((tq, tq), F32), pltpu.VMEM((2, tq, seq), F32),
                        pltpu.VMEM((2, tq, seq), BF16)],
        compiler_params=pltpu.CompilerParams(
            dimension_semantics=("arbitrary", "arbitrary", "arbitrary"),
            vmem_limit_bytes=VMEM_LIMIT_BYTES),
        name="diff_attention",
    )(slopes, vec(lq1), vec(lk1), vec(lq2), vec(lk2), vec(head_g), qkv,
      *([qkv] * (2 * nq)), *jobs.views)
    return outs[0], list(outs[1:])


def _discretize(lam_re, lam_im, log_dt):
    dt = jnp.exp(log_dt)
    mag = jnp.exp(lam_re * dt)
    a_re = mag * jnp.cos(lam_im * dt)
    a_im = mag * jnp.sin(lam_im * dt)
    den = lam_re * lam_re + lam_im * lam_im
    nr = a_re - 1.0
    f_re = (nr * lam_re + a_im * lam_im) / den
    f_im = (a_im * lam_re - nr * lam_im) / den
    return a_re, a_im, f_re, f_im


def _cmul(ar, ai, br, bi):
    return ar * br - ai * bi, ar * bi + ai * br


def _split_bf16(x, parts):
    out = []
    for k in range(parts):
        piece = x.astype(BF16)
        out.append(piece)
        if k + 1 < parts:
            x = x - piece.astype(F32)
    return out


def _copy_dot(x, onehot, dims):
    return sum(lax.dot_general(piece, onehot, dims, preferred_element_type=F32)
               for piece in _split_bf16(x, 3))


def _dot_3pass(a_parts, b_parts, dims):
    a_hi, a_lo = a_parts
    b_hi, b_lo = b_parts
    dot = lambda x, y: lax.dot_general(x, y, dims, preferred_element_type=F32)
    return dot(a_hi, b_hi) + (dot(a_hi, b_lo) + dot(a_lo, b_hi))


def _int_power(a_re, a_im, e, nbits):
    shape = jnp.broadcast_shapes(a_re.shape, e.shape)
    p_re, p_im = jnp.ones(shape, F32), jnp.zeros(shape, F32)
    s_re, s_im = a_re, a_im
    for k in range(nbits):
        bit = (e & (1 << k)) != 0
        m_re, m_im = _cmul(p_re, p_im, s_re, s_im)
        p_re, p_im = jnp.where(bit, m_re, p_re), jnp.where(bit, m_im, p_im)
        if k + 1 < nbits:
            s_re, s_im = _cmul(s_re, s_im, s_re, s_im)
    return p_re, p_im


def _shift_lanes(x, s, left):
    if s == 0:
        return x
    n = x.shape[-1]
    lane = lax.broadcasted_iota(jnp.int32, x.shape, 1)
    if left:
        return jnp.where(lane < n - s, pltpu.roll(x, n - s, axis=1), 0.0)
    return jnp.where(lane >= s, pltpu.roll(x, s, axis=1), 0.0)


def _s5_prep_groups(lre_ref, lim_ref, ldt_ref, btre_ref, btim_ref, cre_ref, cim_ref,
                    w1_ref, cm_ref, al_ref):
    for i in range(w1_ref.shape[0]):
        _s5_prep_group(i, lre_ref, lim_ref, ldt_ref, btre_ref, btim_ref, cre_ref, cim_ref,
                       w1_ref, cm_ref, al_ref)


def _s5_prep_group(i, lre_ref, lim_ref, ldt_ref, btre_r_ref, btim_r_ref, cre_ref, cim_ref,
                   w1_ref, cm_ref, al_ref):
    n, p, l = SSM_STATE, SSM_GROUP, CHUNK
    nbits = l.bit_length() - 1
    contract0 = (((0,), (0,)), ((), ()))
    contract_mm = (((1,), (0,)), ((), ()))

    a_re, a_im, f_re, f_im = _discretize(lre_ref[i], lim_ref[i], ldt_ref[i])
    bt_re, bt_im = btre_r_ref[i], btim_r_ref[i]
    bb_re, bb_im = _cmul(f_re, f_im, bt_re, bt_im)
    tau = lax.broadcasted_iota(jnp.int32, (l, 2 * n), 0)
    lane = lax.broadcasted_iota(jnp.int32, (l, 2 * n), 1)
    pw_re, pw_im = _int_power(a_re, a_im, jnp.where(lane < n, l - 1 - tau, tau), nbits)
    bm_re = pw_re[:, None, :] * bb_re[None, :, :] - pw_im[:, None, :] * bb_im[None, :, :]
    bm_im = pw_re[:, None, :] * bb_im[None, :, :] + pw_im[:, None, :] * bb_re[None, :, :]
    w1_ref[i, :, GROUP_LANES:GROUP_LANES + 2 * n] = bm_re.reshape(l * p, 2 * n).astype(BF16)
    w1_ref[i, :, GROUP_LANES + 2 * n:] = bm_im.reshape(l * p, 2 * n).astype(BF16)
    al_re, al_im = a_re, a_im
    for _ in range(nbits):
        al_re, al_im = _cmul(al_re, al_im, al_re, al_im)
    al_ref[i] = jnp.concatenate([al_re, al_im], axis=0)

    quant = jnp.concatenate([a_re, a_im, jnp.zeros((SUBLANES - 2, 2 * n), F32)], axis=0)
    quant_t = quant.T
    a_re_c, a_im_c = quant_t[:, 0:1], quant_t[:, 1:2]

    lanes_p = lax.broadcasted_iota(jnp.int32, (p, GROUP_LANES), 1)
    rows_p = lax.broadcasted_iota(jnp.int32, (p, GROUP_LANES), 0)
    tile_p = (lanes_p % p == rows_p).astype(BF16)
    tile_lag = ((l - 1) - lanes_p // p == rows_p).astype(BF16)
    ac_re, ac_im = _cmul(_copy_dot(pw_re, tile_lag, contract0), _copy_dot(pw_im, tile_lag, contract0),
                         _copy_dot(cre_ref[i], tile_p, contract0),
                         _copy_dot(cim_ref[i], tile_p, contract0))
    ac_re_parts, ac_im_parts = _split_bf16(ac_re, 2), _split_bf16(ac_im, 2)
    is_fwd = lax.broadcasted_iota(jnp.int32, bb_re.shape, 1) < n
    t_rows = None
    for d in range(2):
        in_dir = is_fwd if d == 0 else ~is_fwd
        taps = (_dot_3pass(_split_bf16(jnp.where(in_dir, bb_re, 0.0), 2), ac_re_parts, contract_mm)
                - _dot_3pass(_split_bf16(jnp.where(in_dir, bb_im, 0.0), 2), ac_im_parts,
                             contract_mm))
        rows = [_shift_lanes(taps, p * (tp if d == 0 else l - 1 - tp), left=(d == 1))
                for tp in range(l)]
        rows = jnp.concatenate(rows, axis=0)
        t_rows = rows if t_rows is None else t_rows + rows
    w1_ref[i, :, 0:GROUP_LANES] = t_rows.astype(BF16)
    r_re, r_im = _cmul(a_re_c, a_im_c, ac_re, ac_im)
    cm_ref[i] = jnp.concatenate([r_re, -r_im], axis=0).astype(BF16)


def _s5_param_rows(lam_re, lam_im, log_dt, b_re, b_im, c_re, c_im):
    _, g, n = lam_re.shape
    p = b_re.shape[-1]
    row = lambda a: jnp.transpose(a, (1, 0, 2)).reshape(g, 1, 2 * n)
    ldt_r = row(jnp.broadcast_to(log_dt[:, :, None], (2, g, n)))
    bt = lambda b: jnp.transpose(b, (1, 3, 0, 2)).reshape(g, p, 2 * n)
    cr = lambda c: jnp.transpose(c, (1, 2, 0, 3)).reshape(g, p, 2 * n)
    return row(lam_re), row(lam_im), ldt_r, bt(b_re), bt(b_im), cr(c_re), cr(c_im)


def _swap_pieces(v):
    piece = lax.broadcasted_iota(jnp.int32, v[0].shape, 1) // SSM_GROUP
    v = list(v)
    for d in (4, 2, 1):
        hi = (piece & d) != 0
        for a in range(8):
            if a & d:
                continue
            va, vb = v[a], v[a + d]
            v[a] = jnp.where(hi, pltpu.roll(vb, d * SSM_GROUP, axis=1), va)
            v[a + d] = jnp.where(hi, vb, pltpu.roll(va, LANES - d * SSM_GROUP, axis=1))
    return v


def _s5_chunk_kernel(u_ref, perm_ref, w1_ref, cm_ref, al_ref, x_ref,
                     a_ref, s_ref, hf_ref, h_ref, *, nb, nc):
    gl = GROUP_LANES
    gb = GROUPS_PER_BLOCK
    n2 = 2 * SSM_STATE
    half = CHUNK // 2
    ctile = SUBLANES * CHUNK

    def token_rows(ct, h, j):
        return pl.ds(ct * ctile + h * half + j, SUBLANES, stride=CHUNK)

    rows_all = nb * nc

    def gather_body(ct, carry):
        for b in range(nb):
            for h in range(2):
                for j in range(half):
                    a_ref[j, pl.ds(h * rows_all + ct * SUBLANES * nb + b, SUBLANES, stride=nb), :] = (
                        u_ref[b, token_rows(ct, h, j), :])
        return carry

    lax.fori_loop(0, nc // SUBLANES, gather_body, 0)
    lhs = jnp.concatenate([a_ref[j] for j in range(half)], axis=1).astype(BF16)
    xp = jnp.dot(lhs, perm_ref[...], preferred_element_type=F32).astype(BF16)

    for g in range(gb):
        x = jnp.concatenate([xp[:rows_all, g * LANES:(g + 1) * LANES],
                             xp[rows_all:, g * LANES:(g + 1) * LANES]], axis=1)
        r = jnp.dot(x, w1_ref[g], preferred_element_type=F32)
        x_ref[2 * g] = r[:, :LANES]
        x_ref[2 * g + 1] = r[:, LANES:gl]
        s_ref[:, g * gl:(g + 1) * gl] = r[:, gl:]

    a_re = [al_ref[g, 0:1, :] for g in range(gb)]
    a_im = [al_ref[g, 1:2, :] for g in range(gb)]
    is_fwd = lax.broadcasted_iota(jnp.int32, (SUBLANES, n2), 1) < SSM_STATE

    def step(h, s):
        out = []
        for g in range(gb):
            hr, hi = h[:, g * gl:g * gl + n2], h[:, g * gl + n2:(g + 1) * gl]
            sr, si = s[:, g * gl:g * gl + n2], s[:, g * gl + n2:(g + 1) * gl]
            out.append(a_re[g] * hr - a_im[g] * hi + sr)
            out.append(a_re[g] * hi + a_im[g] * hr + si)
        return jnp.concatenate(out, axis=1)

    cpt = SUBLANES // nb
    n_tiles = nc // cpt
    tile_rows = lambda t: pl.ds(pl.multiple_of(t * SUBLANES, SUBLANES), SUBLANES)

    def pick(fwd_rows, bwd_rows):
        return jnp.concatenate(
            [jnp.where(is_fwd[:fwd_rows.shape[0]], fwd_rows[:, k * n2:(k + 1) * n2],
                       bwd_rows[:, k * n2:(k + 1) * n2]) for k in range(2 * gb)], axis=1)

    def sweep_body(t, h, *, second_half):
        tb = n_tiles - 1 - t
        s_f = s_ref[tile_rows(t), :]
        s_b = s_ref[tile_rows(tb), :]
        before = []
        for k in range(cpt):
            kb = cpt - 1 - k
            before.append(h)
            h = step(h, pick(s_f[k * nb:(k + 1) * nb], s_b[kb * nb:(kb + 1) * nb]))
        fwd_valid = jnp.concatenate(before, axis=0)
        bwd_valid = jnp.concatenate(before[::-1], axis=0)
        if second_half:
            h_ref[tile_rows(t), :] = pick(fwd_valid, h_ref[tile_rows(t), :])
            h_ref[tile_rows(tb), :] = pick(hf_ref[tile_rows(tb), :], bwd_valid)
        else:
            hf_ref[tile_rows(t), :] = fwd_valid
            h_ref[tile_rows(tb), :] = bwd_valid
        return h

    assert n_tiles % 2 == 0
    h_mid = lax.fori_loop(0, n_tiles // 2, functools.partial(sweep_body, second_half=False),
                          jnp.zeros((nb, gb * gl), F32))
    lax.fori_loop(n_tiles // 2, n_tiles, functools.partial(sweep_body, second_half=True), h_mid)

    for g in range(gb):
        r = jnp.dot(h_ref[:, g * gl:(g + 1) * gl].astype(BF16), cm_ref[g],
                    preferred_element_type=F32)
        x_ref[2 * g] += r[:, :LANES]
        x_ref[2 * g + 1] += r[:, LANES:]


def _s5_chunks(u, w1, cm, al):
    nb, seq, width = u.shape
    nc = seq // CHUNK
    rows = nb * nc
    gb = GROUPS_PER_BLOCK
    assert SUBLANES % nb == 0 and nc % SUBLANES == 0 and width % LANES == 0
    blk = pl.BlockSpec((nb, seq, LANES), lambda i: (0, 0, i))
    half = CHUNK // 2
    src = np.arange(half * LANES)
    j, g, p = src // LANES, (src % LANES) // SSM_GROUP, src % SSM_GROUP
    perm = np.zeros((half * LANES, half * LANES), np.float32)
    perm[src, g * LANES + j * SSM_GROUP + p] = 1.0
    return pl.pallas_call(
        functools.partial(_s5_chunk_kernel, nb=nb, nc=nc),
        grid=(width // LANES,),
        in_specs=[
            blk,
            pl.BlockSpec(perm.shape, lambda i: (0, 0), pipeline_mode=pl.Buffered(1)),
            pl.BlockSpec((gb,) + w1.shape[1:], lambda i: (i, 0, 0)),
            pl.BlockSpec((gb,) + cm.shape[1:], lambda i: (i, 0, 0)),
            pl.BlockSpec((gb,) + al.shape[1:], lambda i: (i, 0, 0)),
        ],
        out_specs=pl.BlockSpec((2 * gb, rows, LANES), lambda i: (i, 0, 0)),
        out_shape=jax.ShapeDtypeStruct((2 * width // SSM_GROUP, rows, LANES), F32),
        scratch_shapes=[pltpu.VMEM((half, 2 * rows, LANES), F32)]
        + [pltpu.VMEM((rows, gb * GROUP_LANES), F32)] * 3,
        compiler_params=pltpu.CompilerParams(
            dimension_semantics=("parallel",), vmem_limit_bytes=VMEM_LIMIT_BYTES),
        name="s5_chunks",
    )(u, jnp.asarray(perm, BF16), w1, cm, al)


def _mix_out_kernel(x_ref, a_ref, ys_ref, u_ref, d_ref, wglu_ref, bglu_ref, og_ref, wo_a_ref,
                    wo_s_ref, pg_ref, o_ref, yn_ref):
    nb, tt, d = x_ref.shape
    gb = GROUPS_PER_BLOCK
    half = CHUNK // 2
    rows = nb * tt
    for o in range(yn_ref.shape[0]):
        for b in range(nb):
            for h in range(2):
                w = [ys_ref[(o * gb + g) * 2 + h, pl.ds(b, SUBLANES, stride=nb), :]
                     for g in range(gb)]
                v = _swap_pieces(w)
                for j in range(half):
                    yn_ref[o, pl.ds(b * tt + h * half + j, SUBLANES, stride=CHUNK), :] = v[j]
    y = jnp.concatenate([yn_ref[o] for o in range(yn_ref.shape[0])], axis=1)
    y = y + d_ref[...] * u_ref[...].reshape(rows, -1)
    g = y * (0.5 * (1.0 + jnp.tanh(math.sqrt(2.0 / math.pi) * (y + 0.044715 * (y * y * y)))))
    z = jnp.dot(g.astype(BF16), wglu_ref[...], preferred_element_type=F32) + bglu_ref[...]
    s = g * _sigmoid(z)
    s = s * _rms_scale(s) * og_ref[...]
    mixed = (jnp.dot(a_ref[...].reshape(rows, -1), wo_a_ref[...], preferred_element_type=F32)
             + jnp.dot(s.astype(BF16), wo_s_ref[...], preferred_element_type=F32))
    out = x_ref[...].reshape(rows, d) + mixed * _rms_scale(mixed) * pg_ref[...]
    o_ref[...] = out.reshape(nb, tt, d)


def _mix_out(x, a, ys, u, skip_d, w_glu, b_glu, out_g, w_out, post_g):
    nb, seq, d = x.shape
    wa = a.shape[2]
    ws = u.shape[2]
    tt = SUBLANES * CHUNK
    assert seq % tt == 0 and ys.shape == (2 * ws // SSM_GROUP, nb * seq // CHUNK, LANES)
    const = lambda shape: pl.BlockSpec(shape, lambda i: (0, 0))
    tile = lambda width: pl.BlockSpec((nb, tt, width), lambda i: (0, i, 0))
    return pl.pallas_call(
        _mix_out_kernel,
        grid=(seq // tt,),
        in_specs=[
            tile(d), tile(wa),
            pl.BlockSpec((ys.shape[0], SUBLANES * nb, LANES), lambda i: (0, i, 0)),
            tile(ws), const((1, ws)),
            const((ws, ws)), const((1, ws)), const((1, ws)),
            pl.BlockSpec((wa, d), lambda i: (0, 0)),
            pl.BlockSpec((ws, d), lambda i: (wa // ws, 0)),
            const((1, d)),
        ],
        out_specs=tile(d),
        out_shape=jax.ShapeDtypeStruct(x.shape, F32),
        scratch_shapes=[pltpu.VMEM((ws // LANES, nb * tt, LANES), F32)],
        compiler_params=pltpu.CompilerParams(
            dimension_semantics=("parallel",), vmem_limit_bytes=VMEM_LIMIT_BYTES),
        name="mix_out",
    )(x, a, ys, u, skip_d.reshape(1, ws), w_glu, b_glu.reshape(1, ws), out_g.reshape(1, ws),
      w_out, w_out, post_g.reshape(1, d))


def kernel(x, ff1_pre_g, ff1_w_gate, ff1_w_up, ff1_w_down, ff1_post_g, mix_pre_g, w_in, lam_q1, lam_k1, lam_q2, lam_k2, attn_head_g, ssm_lam_re, ssm_lam_im, ssm_log_dt, ssm_b_re, ssm_b_im, ssm_c_re, ssm_c_im, ssm_d, ssm_w_glu, ssm_b_glu, ssm_out_g, w_out, mix_post_g, ff2_pre_g, ff2_w_gate, ff2_w_up, ff2_w_down, ff2_post_g):
    batch, seq, d_model = x.shape
    depth = w_in.shape[0]
    ssm_width = ssm_w_glu.shape[-1]
    slopes = jnp.asarray([2.0 ** (-8.0 * (i + 1) / ATTN_HEADS) for i in range(ATTN_HEADS)], F32)
    bf = lambda w: w.astype(BF16)

    xt = x.reshape(batch * seq, d_model)
    for l in range(depth):
        row = lambda p: p[l:l + 1]
        ff1 = (row(ff1_pre_g), ff1_w_gate[l], ff1_w_up[l], ff1_w_down[l], row(ff1_post_g))
        head, ff1_w, _ = _ffn(xt, *ff1, tm=FFN_HEAD_ROWS, tf=FFN_HEAD_COLS, single_buffer_x=True,
                              tiles=(0, 1), emit_weights=True)
        rest = (FFN_HEAD_ROWS // FFN_TILE_ROWS, (batch * seq - FFN_HEAD_ROWS) // FFN_TILE_ROWS)
        xt, _, (w_in_bf, w_glu_bf, w_out_bf) = _ffn(
            xt, ff1[0], *ff1_w, ff1[4], tiles=rest, into=head,
            cast=(w_in[l], ssm_w_glu[l], w_out[l]))

        s5_rows = _s5_param_rows(ssm_lam_re[l], ssm_lam_im[l], ssm_log_dt[l], ssm_b_re[l],
                                 ssm_b_im[l], ssm_c_re[l], ssm_c_im[l])
        qkv, u, w1, cm, al = _in_proj(xt, row(mix_pre_g), w_in_bf, s5_rows, ssm_width=ssm_width)
        lam_init = 0.8 - 0.6 * math.exp(-0.3 * l)
        a, ff2_w = _attention(qkv, slopes, row(lam_q1), row(lam_k1), row(lam_q2), row(lam_k2),
                              row(attn_head_g), batch=batch, seq=seq, lam_init=lam_init,
                              cast=(ff2_w_gate[l], ff2_w_up[l], ff2_w_down[l]))

        u3 = u.reshape(batch, seq, ssm_width)
        ys = _s5_chunks(u3, w1, cm, al)

        xt = _mix_out(xt.reshape(batch, seq, d_model), a.reshape(batch, seq, -1), ys, u3, ssm_d[l],
                      w_glu_bf, row(ssm_b_glu), row(ssm_out_g), w_out_bf, row(mix_post_g))
        xt = xt.reshape(batch * seq, d_model)

        xt, _, _ = _ffn(xt, row(ff2_pre_g), *ff2_w, row(ff2_post_g))
    return xt.reshape(batch, seq, d_model)
```

```python
import functools
import math

import jax
import jax.numpy as jnp
import numpy as np
from jax import lax
from jax.experimental import pallas as pl
from jax.experimental.pallas import tpu as pltpu

F32 = jnp.float32
BF16 = jnp.bfloat16

NORM_EPS = 1e-6
ATTN_HEADS = 8
ATTN_HEAD_DIM = 64
ATTN_VALUE_DIM = 2 * ATTN_HEAD_DIM
POS_SPLIT = 16
SSM_GROUP = 16
SSM_STATE = 64
CHUNK = 16
GROUP_LANES = CHUNK * SSM_GROUP
SUBLANES = 8
LANES = 128
BF16_ROWS = 16
GROUPS_PER_BLOCK = LANES // SSM_GROUP
PREP_GROUPS_PER_STEP = 16
FFN_HEAD_ROWS = 1024
FFN_TILE_ROWS = 512
FFN_TILE_COLS = 512
FFN_HEAD_COLS = 256
BF16_EXACT_INTS = 256

V7X_VMEM_BYTES = 64 * 1024 * 1024
VMEM_LIMIT_BYTES = V7X_VMEM_BYTES - 8 * 1024 * 1024


def _rms_scale(x):
    return lax.rsqrt(jnp.mean(x * x, axis=-1, keepdims=True) + NORM_EPS)


def _sigmoid(x):
    return 1.0 / (1.0 + jnp.exp(-x))


class _CastJobs:
    def __init__(self, arrays, n_steps, flat_index):
        self.flat_index = flat_index
        self.views, self.slabs = list(arrays), []
        for a in arrays:
            rows, cols = a.shape
            assert rows % BF16_ROWS == 0
            n = max(k for k in range(1, n_steps + 1) if (rows // BF16_ROWS) % k == 0)
            self.slabs.append((rows // n, cols, n))

    def _spec(self, slab):
        r, c, n = slab
        return pl.BlockSpec((r, c), lambda *g: (jnp.minimum(self.flat_index(*g), n - 1), 0))

    @property
    def in_specs(self):
        return [self._spec(s) for s in self.slabs]

    out_specs = in_specs

    @property
    def out_shapes(self):
        return [jax.ShapeDtypeStruct(v.shape, BF16) for v in self.views]

    def __len__(self):
        return len(self.views)


def _run_cast_jobs(in_refs, out_refs):
    for i_ref, o_ref in zip(in_refs, out_refs):
        o_ref[...] = i_ref[...].astype(BF16)


def _ffn_kernel(*refs, n_jobs, has_into, emit_weights):
    x_ref, pre_g_ref, wg_ref, wu_ref, wd_ref, post_g_ref = refs[:6]
    n_in = 6 + int(has_into)
    job_in = refs[n_in:n_in + n_jobs]
    o_ref = refs[n_in + n_jobs]
    n_out = n_in + n_jobs + 1
    w_out = refs[n_out:n_out + 3] if emit_weights else ()
    n_out += len(w_out)
    job_out = refs[n_out:n_out + n_jobs]
    h_ref = refs[n_out + n_jobs]
    j = pl.program_id(1)
    last = pl.num_programs(1) - 1

    def step(first, final):
        if first:
            x = x_ref[...]
            h_ref[...] = (x * _rms_scale(x) * pre_g_ref[...]).astype(BF16)
        _run_cast_jobs(job_in, job_out)
        wg, wu, wd = (r[...].astype(BF16) for r in (wg_ref, wu_ref, wd_ref))
        for w_ref, w in zip(w_out, (wg, wu, wd)):
            w_ref[...] = w
        h = h_ref[...]
        gate = jnp.dot(h, wg, preferred_element_type=F32)
        up = jnp.dot(h, wu, preferred_element_type=F32)
        act = (gate * _sigmoid(gate) * up).astype(BF16)
        part = jnp.dot(act, wd, preferred_element_type=F32)
        if first:
            o_ref[...] = part
        elif final:
            acc = o_ref[...] + part
            o_ref[...] = x_ref[...] + 0.5 * (acc * _rms_scale(acc) * post_g_ref[...])
        else:
            o_ref[...] += part

    pl.when(j == 0)(lambda: step(True, False))
    pl.when((j > 0) & (j < last))(lambda: step(False, False))
    pl.when(j == last)(lambda: step(False, True))


def _ffn(x, pre_g, w_gate, w_up, w_down, post_g, *, cast=(), tm=FFN_TILE_ROWS, tf=FFN_TILE_COLS,
         single_buffer_x=False, tiles=None, into=None, emit_weights=False):
    t, d = x.shape
    f = w_gate.shape[1]
    assert t % tm == 0 and f % tf == 0
    first, nt = tiles if tiles is not None else (0, t // tm)
    assert not emit_weights or nt == 1
    nj = f // tf
    assert nj >= 2
    jobs = _CastJobs(cast, nt * nj, lambda i, j: i * nj + j)
    x_mode = dict(pipeline_mode=pl.Buffered(1)) if single_buffer_x else {}
    w_cols = pl.BlockSpec((d, tf), lambda i, j: (0, j))
    w_rows = pl.BlockSpec((tf, d), lambda i, j: (j, 0))
    into_spec = [pl.BlockSpec(memory_space=pl.ANY)] if into is not None else []
    w_specs = [w_cols, w_cols, w_rows] if emit_weights else []
    w_shapes = [jax.ShapeDtypeStruct(w.shape, BF16) for w in (w_gate, w_up, w_down)]
    outs = pl.pallas_call(
        functools.partial(_ffn_kernel, n_jobs=len(jobs), has_into=into is not None,
                          emit_weights=emit_weights),
        grid=(nt, nj),
        in_specs=[
            pl.BlockSpec((tm, d), lambda i, j: (first + i, 0), **x_mode),
            pl.BlockSpec((1, d), lambda i, j: (0, 0)),
            w_cols, w_cols, w_rows,
            pl.BlockSpec((1, d), lambda i, j: (0, 0)),
        ] + into_spec + jobs.in_specs,
        out_specs=[pl.BlockSpec((tm, d), lambda i, j: (first + i, 0))] + w_specs + jobs.out_specs,
        out_shape=[jax.ShapeDtypeStruct((t, d), F32)] + w_shapes[:len(w_specs)] + jobs.out_shapes,
        input_output_aliases={6: 0} if into is not None else {},
        scratch_shapes=[pltpu.VMEM((tm, d), BF16)],
        compiler_params=pltpu.CompilerParams(
            dimension_semantics=("arbitrary", "arbitrary"),
            vmem_limit_bytes=VMEM_LIMIT_BYTES),
        name="ffn",
    )(x, pre_g.reshape(1, d), w_gate, w_up, w_down, post_g.reshape(1, d),
      *([into] if into is not None else []), *jobs.views)
    n_w = len(w_specs)
    return outs[0], list(outs[1:1 + n_w]), list(outs[1 + n_w:])


def _in_proj_kernel(x_ref, g_ref, w_ref, qkv_ref, u_ref, h_ref, *, n_qkv):
    j = pl.program_id(1)

    def project(first):
        if first:
            x = x_ref[...]
            h_ref[...] = (x * _rms_scale(x) * g_ref[...]).astype(BF16)
        return jnp.dot(h_ref[...], w_ref[...].astype(BF16), preferred_element_type=F32)

    @pl.when(j == 0)
    def _():
        qkv_ref[...] = project(True).astype(BF16)

    @pl.when((j > 0) & (j < n_qkv))
    def _():
        qkv_ref[...] = project(False).astype(BF16)

    @pl.when(j == n_qkv)
    def _():
        u_ref[...] = project(False)


def _in_proj(x, g, w_in, *, ssm_width, tm=1024):
    t, d = x.shape
    n = w_in.shape[1]
    tn = ssm_width
    n_qkv = (n - ssm_width) // tn
    assert t % tm == 0 and n == (n_qkv + 1) * tn
    return pl.pallas_call(
        functools.partial(_in_proj_kernel, n_qkv=n_qkv),
        grid=(t // tm, n_qkv + 1),
        in_specs=[
            pl.BlockSpec((tm, d), lambda i, j: (i, 0)),
            pl.BlockSpec((1, d), lambda i, j: (0, 0)),
            pl.BlockSpec((d, tn), lambda i, j: (0, j)),
        ],
        out_specs=[
            pl.BlockSpec((tm, tn), lambda i, j: (i, jnp.minimum(j, n_qkv - 1))),
            pl.BlockSpec((tm, tn), lambda i, j: (i, 0)),
        ],
        out_shape=[
            jax.ShapeDtypeStruct((t, n - ssm_width), BF16),
            jax.ShapeDtypeStruct((t, ssm_width), F32),
        ],
        scratch_shapes=[pltpu.VMEM((tm, d), BF16)],
        compiler_params=pltpu.CompilerParams(
            dimension_semantics=("parallel", "arbitrary"),
            vmem_limit_bytes=VMEM_LIMIT_BYTES),
        name="in_proj",
    )(x, g.reshape(1, d), w_in)


def _attn_kernel(*refs, tq, nq, lam_init, n_jobs):
    slopes_ref, lq1_ref, lk1_ref, lq2_ref, lk2_ref, hg_ref, q_ref = refs[:7]
    k_refs = refs[7:7 + nq]
    v_refs = refs[7 + nq:7 + 2 * nq]
    n_in = 7 + 2 * nq
    job_in = refs[n_in:n_in + n_jobs]
    o_ref = refs[n_in + n_jobs]
    job_out = refs[n_in + n_jobs + 1:n_in + 2 * n_jobs + 1]
    qf_ref, kf_ref, bias_ref, s_ref, p_ref = refs[n_in + 2 * n_jobs + 1:]
    h = pl.program_id(0)
    qi = pl.program_id(1)
    b = pl.program_id(2)
    e = q_ref.shape[1]

    @pl.when(b == 0)
    def _():
        slope = slopes_ref[h]
        lane = lax.broadcasted_iota(jnp.int32, (tq, e), 1)
        row = lax.broadcasted_iota(jnp.int32, (tq, e), 0)
        row_hi = (slope * POS_SPLIT) * (row // POS_SPLIT).astype(F32)
        row_lo = slope * (row % POS_SPLIT).astype(F32)
        q_base = jnp.where(lane == 0, row_hi, jnp.where(lane == 1, row_lo,
                           jnp.where(lane < 4, 1.0, 0.0)))
        k_base = jnp.where(lane < 2, -1.0, jnp.where(lane == 2, row_hi,
                           jnp.where(lane == 3, row_lo, 0.0)))
        hi_q = (lane == 0).astype(F32)
        hi_k = (lane == 2).astype(F32)
        qf_ref[...] = (q_base + (slope * tq) * qi.astype(F32) * hi_q).astype(BF16)
        for slot in range(nq):
            chunk = lax.rem(qi + slot, nq)
            sign = jnp.where(chunk < qi, 1.0, jnp.where(chunk > qi, -1.0, 0.0))
            kf = sign * k_base + (sign * (slope * tq) * chunk.astype(F32)) * hi_k
            kf_ref[slot * tq:(slot + 1) * tq, :] = kf.astype(BF16)
        col = lax.broadcasted_iota(jnp.int32, (tq, tq), 1)
        bias_ref[...] = slope * jnp.abs(lax.broadcasted_iota(jnp.int32, (tq, tq), 0)
                                        - col).astype(F32)

    _run_cast_jobs(job_in, job_out)
    lam = (jnp.exp(jnp.sum(lq1_ref[...] * lk1_ref[...], axis=-1, keepdims=True))
           - jnp.exp(jnp.sum(lq2_ref[...] * lk2_ref[...], axis=-1, keepdims=True))
           + lam_init)

    q = q_ref[...] * jnp.asarray(ATTN_HEAD_DIM ** -0.5, BF16)
    k_aug = jnp.concatenate([jnp.concatenate([r[...] for r in k_refs], axis=0), kf_ref[...]],
                            axis=1)
    v = jnp.concatenate([r[...] for r in v_refs], axis=0)
    v_ext = jnp.concatenate([v, jnp.ones_like(v)], axis=1)
    lane = lax.broadcasted_iota(jnp.int32, q.shape, 1)
    nt = (((1,), (1,)), ((), ()))
    for c in range(2):
        in_map = (lane >= c * ATTN_HEAD_DIM) & (lane < (c + 1) * ATTN_HEAD_DIM)
        q_aug = jnp.concatenate([jnp.where(in_map, q, jnp.zeros_like(q)), qf_ref[...]], axis=1)
        s_ref[c, :, :tq] = lax.dot_general(q_aug, k_aug[:tq], nt,
                                           preferred_element_type=F32) - bias_ref[...]
        s_ref[c, :, tq:] = lax.dot_general(q_aug, k_aug[tq:], nt, preferred_element_type=F32)
        s = s_ref[c]
        p_ref[c] = jnp.exp(s - jnp.max(s, axis=-1, keepdims=True)).astype(BF16)
    outs = []
    for c in range(2):
        pv = jnp.dot(p_ref[c], v_ext, preferred_element_type=F32)
        outs.append(pv[:, :e] / pv[:, e:])
    o = outs[0] - lam * outs[1]
    o_ref[...] = (o * _rms_scale(o) * hg_ref[...] * (1.0 - lam_init)).astype(o_ref.dtype)


def _attention(qkv, slopes, lq1, lk1, lq2, lk2, head_g, *, batch, seq, lam_init, cast=(), tq=1024):
    t = qkv.shape[0]
    e = ATTN_VALUE_DIM
    nq = seq // tq
    nh = ATTN_HEADS
    assert 8 % nh == 0 and seq // POS_SPLIT <= BF16_EXACT_INTS
    assert tq % POS_SPLIT == 0 and seq % tq == 0 and nq >= 2
    vec = lambda a: a.reshape(1, -1).astype(F32)
    small = lambda n: pl.BlockSpec((1, n), lambda h, qi, b: (0, 0))
    jobs = _CastJobs(cast, nh * nq * batch, lambda h, qi, b: (h * nq + qi) * batch + b)

    def chunk_spec(slot, col0):
        return pl.BlockSpec((tq, e), lambda h, qi, b: (b * nq + lax.rem(qi + slot, nq), col0 + h))

    outs = pl.pallas_call(
        functools.partial(_attn_kernel, tq=tq, nq=nq, lam_init=lam_init, n_jobs=len(jobs)),
        grid=(nh, nq, batch),
        in_specs=[
            pl.BlockSpec(memory_space=pltpu.SMEM),
            small(ATTN_HEAD_DIM), small(ATTN_HEAD_DIM), small(ATTN_HEAD_DIM), small(ATTN_HEAD_DIM),
            small(e),
            pl.BlockSpec((tq, e), lambda h, qi, b: (b * nq + qi, h)),
        ] + [chunk_spec(s, nh) for s in range(nq)] + [chunk_spec(s, 2 * nh) for s in range(nq)]
        + jobs.in_specs,
        out_specs=[pl.BlockSpec((tq, e), lambda h, qi, b: (b * nq + qi, h))] + jobs.out_specs,
        out_shape=[jax.ShapeDtypeStruct((t, nh * e), BF16)] + jobs.out_shapes,
        scratch_shapes=[pltpu.VMEM((tq, e), BF16), pltpu.VMEM((seq, e), BF16),
                        pltpu.VMEM((tq, tq), F32), pltpu.VMEM((2, tq, seq), F32),
                        pltpu.VMEM((2, tq, seq), BF16)],
        compiler_params=pltpu.CompilerParams(
            dimension_semantics=("arbitrary", "arbitrary", "arbitrary"),
            vmem_limit_bytes=VMEM_LIMIT_BYTES),
        name="diff_attention",
    )(slopes, vec(lq1), vec(lk1), vec(lq2), vec(lk2), vec(head_g), qkv,
      *([qkv] * (2 * nq)), *jobs.views)
    return outs[0], list(outs[1:])


def _discretize(lam_re, lam_im, log_dt):
    dt = jnp.exp(log_dt)
    mag = jnp.exp(lam_re * dt)
    a_re = mag * jnp.cos(lam_im * dt)
    a_im = mag * jnp.sin(lam_im * dt)
    den = lam_re * lam_re + lam_im * lam_im
    nr = a_re - 1.0
    f_re = (nr * lam_re + a_im * lam_im) / den
    f_im = (a_im * lam_re - nr * lam_im) / den
    return a_re, a_im, f_re, f_im


def _cmul(ar, ai, br, bi):
    return ar * br - ai * bi, ar * bi + ai * br


def _split_bf16(x, parts):
    out = []
    for k in range(parts):
        piece = x.astype(BF16)
        out.append(piece)
        if k + 1 < parts:
            x = x - piece.astype(F32)
    return out


def _copy_dot(x, onehot, dims):
    return sum(lax.dot_general(piece, onehot, dims, preferred_element_type=F32)
               for piece in _split_bf16(x, 3))


def _dot_3pass(a_parts, b_parts, dims):
    a_hi, a_lo = a_parts
    b_hi, b_lo = b_parts
    dot = lambda x, y: lax.dot_general(x, y, dims, preferred_element_type=F32)
    return dot(a_hi, b_hi) + (dot(a_hi, b_lo) + dot(a_lo, b_hi))


def _int_power(a_re, a_im, e, nbits):
    shape = jnp.broadcast_shapes(a_re.shape, e.shape)
    p_re, p_im = jnp.ones(shape, F32), jnp.zeros(shape, F32)
    s_re, s_im = a_re, a_im
    for k in range(nbits):
        bit = (e & (1 << k)) != 0
        m_re, m_im = _cmul(p_re, p_im, s_re, s_im)
        p_re, p_im = jnp.where(bit, m_re, p_re), jnp.where(bit, m_im, p_im)
        if k + 1 < nbits:
            s_re, s_im = _cmul(s_re, s_im, s_re, s_im)
    return p_re, p_im


def _shift_lanes(x, s, left):
    if s == 0:
        return x
    n = x.shape[-1]
    lane = lax.broadcasted_iota(jnp.int32, x.shape, 1)
    if left:
        return jnp.where(lane < n - s, pltpu.roll(x, n - s, axis=1), 0.0)
    return jnp.where(lane >= s, pltpu.roll(x, s, axis=1), 0.0)


def _s5_prep_kernel(lre_ref, lim_ref, ldt_ref, btre_ref, btim_ref, cre_ref, cim_ref,
                    w1_ref, cm_ref, al_ref):
    for i in range(w1_ref.shape[0]):
        _s5_prep_group(i, lre_ref, lim_ref, ldt_ref, btre_ref, btim_ref, cre_ref, cim_ref,
                       w1_ref, cm_ref, al_ref)


def _s5_prep_group(i, lre_ref, lim_ref, ldt_ref, btre_r_ref, btim_r_ref, cre_ref, cim_ref,
                   w1_ref, cm_ref, al_ref):
    n, p, l = SSM_STATE, SSM_GROUP, CHUNK
    nbits = l.bit_length() - 1
    contract0 = (((0,), (0,)), ((), ()))
    contract_mm = (((1,), (0,)), ((), ()))

    a_re, a_im, f_re, f_im = _discretize(lre_ref[i], lim_ref[i], ldt_ref[i])
    bt_re, bt_im = btre_r_ref[i], btim_r_ref[i]
    bb_re, bb_im = _cmul(f_re, f_im, bt_re, bt_im)
    tau = lax.broadcasted_iota(jnp.int32, (l, 2 * n), 0)
    lane = lax.broadcasted_iota(jnp.int32, (l, 2 * n), 1)
    pw_re, pw_im = _int_power(a_re, a_im, jnp.where(lane < n, l - 1 - tau, tau), nbits)
    bm_re = pw_re[:, None, :] * bb_re[None, :, :] - pw_im[:, None, :] * bb_im[None, :, :]
    bm_im = pw_re[:, None, :] * bb_im[None, :, :] + pw_im[:, None, :] * bb_re[None, :, :]
    w1_ref[i, :, GROUP_LANES:GROUP_LANES + 2 * n] = bm_re.reshape(l * p, 2 * n).astype(BF16)
    w1_ref[i, :, GROUP_LANES + 2 * n:] = bm_im.reshape(l * p, 2 * n).astype(BF16)
    al_re, al_im = a_re, a_im
    for _ in range(nbits):
        al_re, al_im = _cmul(al_re, al_im, al_re, al_im)
    al_ref[i] = jnp.concatenate([al_re, al_im], axis=0)

    quant = jnp.concatenate([a_re, a_im, jnp.zeros((SUBLANES - 2, 2 * n), F32)], axis=0)
    quant_t = quant.T
    a_re_c, a_im_c = quant_t[:, 0:1], quant_t[:, 1:2]

    lanes_p = lax.broadcasted_iota(jnp.int32, (p, GROUP_LANES), 1)
    rows_p = lax.broadcasted_iota(jnp.int32, (p, GROUP_LANES), 0)
    tile_p = (lanes_p % p == rows_p).astype(BF16)
    tile_lag = ((l - 1) - lanes_p // p == rows_p).astype(BF16)
    c_rows = lambda c_ref: jnp.concatenate([c_ref[0, i], c_ref[1, i]], axis=-1)
    ac_re, ac_im = _cmul(_copy_dot(pw_re, tile_lag, contract0), _copy_dot(pw_im, tile_lag, contract0),
                         _copy_dot(c_rows(cre_ref), tile_p, contract0),
                         _copy_dot(c_rows(cim_ref), tile_p, contract0))
    ac_re_parts, ac_im_parts = _split_bf16(ac_re, 2), _split_bf16(ac_im, 2)
    is_fwd = lax.broadcasted_iota(jnp.int32, bb_re.shape, 1) < n
    t_rows = None
    for d in range(2):
        in_dir = is_fwd if d == 0 else ~is_fwd
        taps = (_dot_3pass(_split_bf16(jnp.where(in_dir, bb_re, 0.0), 2), ac_re_parts, contract_mm)
                - _dot_3pass(_split_bf16(jnp.where(in_dir, bb_im, 0.0), 2), ac_im_parts,
                             contract_mm))
        rows = [_shift_lanes(taps, p * (tp if d == 0 else l - 1 - tp), left=(d == 1))
                for tp in range(l)]
        rows = jnp.concatenate(rows, axis=0)
        t_rows = rows if t_rows is None else t_rows + rows
    w1_ref[i, :, 0:GROUP_LANES] = t_rows.astype(BF16)
    r_re, r_im = _cmul(a_re_c, a_im_c, ac_re, ac_im)
    cm_ref[i] = jnp.concatenate([r_re, -r_im], axis=0).astype(BF16)


def _s5_prep(lam_re, lam_im, log_dt, b_re, b_im, c_re, c_im):
    _, g, n = lam_re.shape
    p = b_re.shape[-1]
    row = lambda a: jnp.transpose(a, (1, 0, 2)).reshape(g, 1, 2 * n)
    ldt_r = row(jnp.broadcast_to(log_dt[:, :, None], (2, g, n)))
    bt = lambda b: jnp.transpose(b, (1, 3, 0, 2)).reshape(g, p, 2 * n)

    gb = PREP_GROUPS_PER_STEP
    assert g % gb == 0
    spec_r = lambda rows: pl.BlockSpec((gb, rows, 2 * n), lambda i: (i, 0, 0))
    spec_c = pl.BlockSpec((2, gb, p, n), lambda i: (0, i, 0, 0))
    return pl.pallas_call(
        _s5_prep_kernel,
        grid=(g // gb,),
        in_specs=[spec_r(1), spec_r(1), spec_r(1), spec_r(p), spec_r(p), spec_c, spec_c],
        out_specs=[
            pl.BlockSpec((gb, GROUP_LANES, GROUP_LANES + 4 * n), lambda i: (i, 0, 0)),
            pl.BlockSpec((gb, 4 * n, GROUP_LANES), lambda i: (i, 0, 0)),
            pl.BlockSpec((gb, 2, 2 * n), lambda i: (i, 0, 0)),
        ],
        out_shape=[
            jax.ShapeDtypeStruct((g, GROUP_LANES, GROUP_LANES + 4 * n), BF16),
            jax.ShapeDtypeStruct((g, 4 * n, GROUP_LANES), BF16),
            jax.ShapeDtypeStruct((g, 2, 2 * n), F32),
        ],
        compiler_params=pltpu.CompilerParams(dimension_semantics=("parallel",)),
        name="s5_prep",
    )(row(lam_re), row(lam_im), ldt_r, bt(b_re), bt(b_im), c_re, c_im)


def _swap_pieces(v):
    piece = lax.broadcasted_iota(jnp.int32, v[0].shape, 1) // SSM_GROUP
    v = list(v)
    for d in (4, 2, 1):
        hi = (piece & d) != 0
        for a in range(8):
            if a & d:
                continue
            va, vb = v[a], v[a + d]
            v[a] = jnp.where(hi, pltpu.roll(vb, d * SSM_GROUP, axis=1), va)
            v[a + d] = jnp.where(hi, vb, pltpu.roll(va, LANES - d * SSM_GROUP, axis=1))
    return v


def _s5_chunk_kernel(u_ref, perm_ref, w1_ref, cm_ref, al_ref, x_ref,
                     a_ref, s_ref, hf_ref, h_ref, *, nb, nc):
    gl = GROUP_LANES
    gb = GROUPS_PER_BLOCK
    n2 = 2 * SSM_STATE
    half = CHUNK // 2
    ctile = SUBLANES * CHUNK

    def token_rows(ct, h, j):
        return pl.ds(ct * ctile + h * half + j, SUBLANES, stride=CHUNK)

    rows_all = nb * nc

    def gather_body(ct, carry):
        for b in range(nb):
            for h in range(2):
                for j in range(half):
                    a_ref[j, pl.ds(h * rows_all + ct * SUBLANES * nb + b, SUBLANES, stride=nb), :] = (
                        u_ref[b, token_rows(ct, h, j), :])
        return carry

    lax.fori_loop(0, nc // SUBLANES, gather_body, 0)
    lhs = jnp.concatenate([a_ref[j] for j in range(half)], axis=1).astype(BF16)
    xp = jnp.dot(lhs, perm_ref[...], preferred_element_type=F32).astype(BF16)

    for g in range(gb):
        x = jnp.concatenate([xp[:rows_all, g * LANES:(g + 1) * LANES],
                             xp[rows_all:, g * LANES:(g + 1) * LANES]], axis=1)
        r = jnp.dot(x, w1_ref[g], preferred_element_type=F32)
        x_ref[2 * g] = r[:, :LANES]
        x_ref[2 * g + 1] = r[:, LANES:gl]
        s_ref[:, g * gl:(g + 1) * gl] = r[:, gl:]

    a_re = [al_ref[g, 0:1, :] for g in range(gb)]
    a_im = [al_ref[g, 1:2, :] for g in range(gb)]
    is_fwd = lax.broadcasted_iota(jnp.int32, (SUBLANES, n2), 1) < SSM_STATE

    def step(h, s):
        out = []
        for g in range(gb):
            hr, hi = h[:, g * gl:g * gl + n2], h[:, g * gl + n2:(g + 1) * gl]
            sr, si = s[:, g * gl:g * gl + n2], s[:, g * gl + n2:(g + 1) * gl]
            out.append(a_re[g] * hr - a_im[g] * hi + sr)
            out.append(a_re[g] * hi + a_im[g] * hr + si)
        return jnp.concatenate(out, axis=1)

    cpt = SUBLANES // nb
    n_tiles = nc // cpt
    tile_rows = lambda t: pl.ds(pl.multiple_of(t * SUBLANES, SUBLANES), SUBLANES)

    def pick(fwd_rows, bwd_rows):
        return jnp.concatenate(
            [jnp.where(is_fwd[:fwd_rows.shape[0]], fwd_rows[:, k * n2:(k + 1) * n2],
                       bwd_rows[:, k * n2:(k + 1) * n2]) for k in range(2 * gb)], axis=1)

    def sweep_body(t, h, *, second_half):
        tb = n_tiles - 1 - t
        s_f = s_ref[tile_rows(t), :]
        s_b = s_ref[tile_rows(tb), :]
        before = []
        for k in range(cpt):
            kb = cpt - 1 - k
            before.append(h)
            h = step(h, pick(s_f[k * nb:(k + 1) * nb], s_b[kb * nb:(kb + 1) * nb]))
        fwd_valid = jnp.concatenate(before, axis=0)
        bwd_valid = jnp.concatenate(before[::-1], axis=0)
        if second_half:
            h_ref[tile_rows(t), :] = pick(fwd_valid, h_ref[tile_rows(t), :])
            h_ref[tile_rows(tb), :] = pick(hf_ref[tile_rows(tb), :], bwd_valid)
        else:
            hf_ref[tile_rows(t), :] = fwd_valid
            h_ref[tile_rows(tb), :] = bwd_valid
        return h

    assert n_tiles % 2 == 0
    h_mid = lax.fori_loop(0, n_tiles // 2, functools.partial(sweep_body, second_half=False),
                          jnp.zeros((nb, gb * gl), F32))
    lax.fori_loop(n_tiles // 2, n_tiles, functools.partial(sweep_body, second_half=True), h_mid)

    for g in range(gb):
        r = jnp.dot(h_ref[:, g * gl:(g + 1) * gl].astype(BF16), cm_ref[g],
                    preferred_element_type=F32)
        x_ref[2 * g] += r[:, :LANES]
        x_ref[2 * g + 1] += r[:, LANES:]


def _s5_chunks(u, w1, cm, al):
    nb, seq, width = u.shape
    nc = seq // CHUNK
    rows = nb * nc
    gb = GROUPS_PER_BLOCK
    assert SUBLANES % nb == 0 and nc % SUBLANES == 0 and width % LANES == 0
    blk = pl.BlockSpec((nb, seq, LANES), lambda i: (0, 0, i))
    half = CHUNK // 2
    src = np.arange(half * LANES)
    j, g, p = src // LANES, (src % LANES) // SSM_GROUP, src % SSM_GROUP
    perm = np.zeros((half * LANES, half * LANES), np.float32)
    perm[src, g * LANES + j * SSM_GROUP + p] = 1.0
    return pl.pallas_call(
        functools.partial(_s5_chunk_kernel, nb=nb, nc=nc),
        grid=(width // LANES,),
        in_specs=[
            blk,
            pl.BlockSpec(perm.shape, lambda i: (0, 0), pipeline_mode=pl.Buffered(1)),
            pl.BlockSpec((gb,) + w1.shape[1:], lambda i: (i, 0, 0)),
            pl.BlockSpec((gb,) + cm.shape[1:], lambda i: (i, 0, 0)),
            pl.BlockSpec((gb,) + al.shape[1:], lambda i: (i, 0, 0)),
        ],
        out_specs=pl.BlockSpec((2 * gb, rows, LANES), lambda i: (i, 0, 0)),
        out_shape=jax.ShapeDtypeStruct((2 * width // SSM_GROUP, rows, LANES), F32),
        scratch_shapes=[pltpu.VMEM((half, 2 * rows, LANES), F32)]
        + [pltpu.VMEM((rows, gb * GROUP_LANES), F32)] * 3,
        compiler_params=pltpu.CompilerParams(
            dimension_semantics=("parallel",), vmem_limit_bytes=VMEM_LIMIT_BYTES),
        name="s5_chunks",
    )(u, jnp.asarray(perm, BF16), w1, cm, al)


def _mix_out_kernel(x_ref, a_ref, ys_ref, u_ref, d_ref, wglu_ref, bglu_ref, og_ref, wo_a_ref,
                    wo_s_ref, pg_ref, o_ref, yn_ref):
    nb, tt, d = x_ref.shape
    gb = GROUPS_PER_BLOCK
    half = CHUNK // 2
    rows = nb * tt
    for o in range(yn_ref.shape[0]):
        for b in range(nb):
            for h in range(2):
                w = [ys_ref[(o * gb + g) * 2 + h, pl.ds(b, SUBLANES, stride=nb), :]
                     for g in range(gb)]
                v = _swap_pieces(w)
                for j in range(half):
                    yn_ref[o, pl.ds(b * tt + h * half + j, SUBLANES, stride=CHUNK), :] = v[j]
    y = jnp.concatenate([yn_ref[o] for o in range(yn_ref.shape[0])], axis=1)
    y = y + d_ref[...] * u_ref[...].reshape(rows, -1)
    g = y * (0.5 * (1.0 + jnp.tanh(math.sqrt(2.0 / math.pi) * (y + 0.044715 * (y * y * y)))))
    z = jnp.dot(g.astype(BF16), wglu_ref[...], preferred_element_type=F32) + bglu_ref[...]
    s = g * _sigmoid(z)
    s = s * _rms_scale(s) * og_ref[...]
    mixed = (jnp.dot(a_ref[...].reshape(rows, -1), wo_a_ref[...], preferred_element_type=F32)
             + jnp.dot(s.astype(BF16), wo_s_ref[...], preferred_element_type=F32))
    out = x_ref[...].reshape(rows, d) + mixed * _rms_scale(mixed) * pg_ref[...]
    o_ref[...] = out.reshape(nb, tt, d)


def _mix_out(x, a, ys, u, skip_d, w_glu, b_glu, out_g, w_out, post_g):
    nb, seq, d = x.shape
    wa = a.shape[2]
    ws = u.shape[2]
    tt = SUBLANES * CHUNK
    assert seq % tt == 0 and ys.shape == (2 * ws // SSM_GROUP, nb * seq // CHUNK, LANES)
    const = lambda shape: pl.BlockSpec(shape, lambda i: (0, 0))
    tile = lambda width: pl.BlockSpec((nb, tt, width), lambda i: (0, i, 0))
    return pl.pallas_call(
        _mix_out_kernel,
        grid=(seq // tt,),
        in_specs=[
            tile(d), tile(wa),
            pl.BlockSpec((ys.shape[0], SUBLANES * nb, LANES), lambda i: (0, i, 0)),
            tile(ws), const((1, ws)),
            const((ws, ws)), const((1, ws)), const((1, ws)),
            pl.BlockSpec((wa, d), lambda i: (0, 0)),
            pl.BlockSpec((ws, d), lambda i: (wa // ws, 0)),
            const((1, d)),
        ],
        out_specs=tile(d),
        out_shape=jax.ShapeDtypeStruct(x.shape, F32),
        scratch_shapes=[pltpu.VMEM((ws // LANES, nb * tt, LANES), F32)],
        compiler_params=pltpu.CompilerParams(
            dimension_semantics=("parallel",), vmem_limit_bytes=VMEM_LIMIT_BYTES),
        name="mix_out",
    )(x, a, ys, u, skip_d.reshape(1, ws), w_glu, b_glu.reshape(1, ws), out_g.reshape(1, ws),
      w_out, w_out, post_g.reshape(1, d))


def kernel(x, ff1_pre_g, ff1_w_gate, ff1_w_up, ff1_w_down, ff1_post_g, mix_pre_g, w_in, lam_q1, lam_k1, lam_q2, lam_k2, attn_head_g, ssm_lam_re, ssm_lam_im, ssm_log_dt, ssm_b_re, ssm_b_im, ssm_c_re, ssm_c_im, ssm_d, ssm_w_glu, ssm_b_glu, ssm_out_g, w_out, mix_post_g, ff2_pre_g, ff2_w_gate, ff2_w_up, ff2_w_down, ff2_post_g):
    batch, seq, d_model = x.shape
    depth = w_in.shape[0]
    ssm_width = ssm_w_glu.shape[-1]
    slopes = jnp.asarray([2.0 ** (-8.0 * (i + 1) / ATTN_HEADS) for i in range(ATTN_HEADS)], F32)
    bf = lambda w: w.astype(BF16)

    xt = x.reshape(batch * seq, d_model)
    for l in range(depth):
        row = lambda p: p[l:l + 1]
        ff1 = (row(ff1_pre_g), ff1_w_gate[l], ff1_w_up[l], ff1_w_down[l], row(ff1_post_g))
        head, ff1_w, _ = _ffn(xt, *ff1, tm=FFN_HEAD_ROWS, tf=FFN_HEAD_COLS, single_buffer_x=True,
                              tiles=(0, 1), emit_weights=True)
        rest = (FFN_HEAD_ROWS // FFN_TILE_ROWS, (batch * seq - FFN_HEAD_ROWS) // FFN_TILE_ROWS)
        xt, _, (w_in_bf, w_glu_bf, w_out_bf) = _ffn(
            xt, ff1[0], *ff1_w, ff1[4], tiles=rest, into=head,
            cast=(w_in[l], ssm_w_glu[l], w_out[l]))

        qkv, u = _in_proj(xt, row(mix_pre_g), w_in_bf, ssm_width=ssm_width)
        lam_init = 0.8 - 0.6 * math.exp(-0.3 * l)
        a, ff2_w = _attention(qkv, slopes, row(lam_q1), row(lam_k1), row(lam_q2), row(lam_k2),
                              row(attn_head_g), batch=batch, seq=seq, lam_init=lam_init,
                              cast=(ff2_w_gate[l], ff2_w_up[l], ff2_w_down[l]))

        w1, cm, al = _s5_prep(ssm_lam_re[l], ssm_lam_im[l], ssm_log_dt[l], ssm_b_re[l],
                              ssm_b_im[l], ssm_c_re[l], ssm_c_im[l])
        u3 = u.reshape(batch, seq, ssm_width)
        ys = _s5_chunks(u3, w1, cm, al)

        xt = _mix_out(xt.reshape(batch, seq, d_model), a.reshape(batch, seq, -1), ys, u3, ssm_d[l],
                      w_glu_bf, row(ssm_b_glu), row(ssm_out_g), w_out_bf, row(mix_post_g))
        xt = xt.reshape(batch * seq, d_model)

        xt, _, _ = _ffn(xt, row(ff2_pre_g), *ff2_w, row(ff2_post_g))
    return xt.reshape(batch, seq, d_model)
```
